```python
import jax, jax.numpy as jnp
from jax import lax
import numpy as np

D_MODEL = 1024
BATCH = 8
SEQ = 8192
DEPTH = 2

N_META = 16
HGRN_HEAD_V = 128
HGRN_WIDTH = D_MODEL // 2
HGRN_HEADS = HGRN_WIDTH // HGRN_HEAD_V
HGRN_EXPAND = 128
HGRN_KDIM = HGRN_HEADS * HGRN_EXPAND
CHUNK = 64
POOL_WINDOWS = (2, 4, 8, 16)
POOL_GROUPS = len(POOL_WINDOWS)
POOL_WIDTH = D_MODEL // 2
POOL_GROUP_DIM = POOL_WIDTH // POOL_GROUPS
D_FF = 2816
CONV_WIDTH = 3
EPS = 1e-6
LOG_FLOOR = 1e-30
SPLITS = (HGRN_KDIM, HGRN_KDIM, HGRN_WIDTH, HGRN_WIDTH, POOL_WIDTH, D_MODEL, D_MODEL)
IN_COLS = sum(SPLITS)

kernel_name = "hybrid_hgrn2_pool_convffn_trunk"


def rmsnorm(x, gain):
    xf = x.astype(jnp.float32)
    y = xf * lax.rsqrt(jnp.mean(xf * xf, axis=-1, keepdims=True) + EPS)
    return (y * gain.astype(jnp.float32)).astype(x.dtype)


def _hgrn2_chunk_step(state, xs):
    q, k, v, lf = xs
    c = q.shape[2]
    b = jnp.cumsum(lf, axis=2)
    o_inter = jnp.einsum('bhtk,bhkv->bhtv', q * jnp.exp(b), state)
    diff = b[:, :, :, None, :] - b[:, :, None, :, :]
    causal = (jnp.arange(c)[:, None] >= jnp.arange(c)[None, :])[None, None, :, :, None]
    decay = jnp.where(causal, jnp.exp(jnp.where(causal, diff, 0.0)), 0.0)
    scores = jnp.einsum('bhtk,bhtsk,bhsk->bhts', q, decay, k)
    o_intra = jnp.einsum('bhts,bhsv->bhtv', scores, v)
    b_last = b[:, :, -1, :]
    k_dec = k * jnp.exp(b_last[:, :, None, :] - b)
    new_state = jnp.exp(b_last)[..., None] * state + jnp.einsum('bhsk,bhsv->bhkv', k_dec, v)
    return new_state, o_inter + o_intra


def hgrn2(q, f_logit, i, g, lb, out_gain):
    bsz, L, _ = q.shape
    lb = lb.astype(jnp.float32)
    fl = f_logit.astype(jnp.float32)
    f = lb + (1.0 - lb) * jax.nn.sigmoid(fl)
    log_f = jnp.log(jnp.maximum(f, LOG_FLOOR))
    k = (1.0 - lb) * jax.nn.sigmoid(-fl)
    pad = CHUNK - N_META
    def prep(t, d):
        t = jnp.pad(t.astype(jnp.float32), ((0, 0), (pad, 0), (0, 0)))
        n = t.shape[1] // CHUNK
        return t.reshape(bsz, n, CHUNK, HGRN_HEADS, d).transpose(1, 0, 3, 2, 4)
    xs = (prep(q, HGRN_EXPAND), prep(k, HGRN_EXPAND), prep(i, HGRN_HEAD_V), prep(log_f, HGRN_EXPAND))
    state0 = jnp.zeros((bsz, HGRN_HEADS, HGRN_EXPAND, HGRN_HEAD_V), jnp.float32)
    _, o = lax.scan(_hgrn2_chunk_step, state0, xs)
    o = o.transpose(1, 0, 3, 2, 4).reshape(bsz, -1, HGRN_HEADS, HGRN_HEAD_V)[:, pad:]
    o = o * lax.rsqrt(jnp.mean(o * o, axis=-1, keepdims=True) + EPS)
    o = o.reshape(bsz, L, HGRN_WIDTH) * out_gain.astype(jnp.float32)
    return o * jax.nn.sigmoid(g.astype(jnp.float32))


def multiscale_pool(v, proj, scale):
    bsz, L, _ = v.shape
    vg = v.astype(jnp.float32).reshape(bsz, L, POOL_GROUPS, POOL_GROUP_DIM)
    cs0 = jnp.pad(jnp.cumsum(vg, axis=1), ((0, 0), (1, 0), (0, 0), (0, 0)))
    t1 = jnp.arange(1, L + 1)
    outs = []
    for gi, w in enumerate(POOL_WINDOWS):
        hi = cs0[:, 1:, gi]
        lo = jnp.pad(cs0[:, :L + 1 - w, gi], ((0, 0), (w - 1, 0), (0, 0)))
        cnt = jnp.minimum(t1, w).astype(jnp.float32)[None, :, None]
        outs.append((hi - lo) / cnt - vg[:, :, gi])
    pooled = jnp.stack(outs, axis=2)
    y = jnp.einsum('blgc,gcd->blgd', pooled, proj.astype(jnp.float32))
    return y.reshape(bsz, L, POOL_WIDTH) * scale.astype(jnp.float32)


def causal_dwconv(x, w, b):
    L = x.shape[1]
    xp = jnp.pad(x, ((0, 0), (CONV_WIDTH - 1, 0), (0, 0)))
    y = b
    for j in range(CONV_WIDTH):
        y = y + xp[:, j:j + L] * w[j]
    return y


def _fwd_setup_inputs(seed: int = 0) -> dict:
    key = jax.random.key(seed)
    ks = jax.random.split(key, 24)
    nrm = lambda k, shape, s: jax.random.normal(k, shape, jnp.float32) * s
    gain = lambda k, shape: 1.0 + nrm(k, shape, 0.05)
    return {
        "x": nrm(ks[0], (BATCH, SEQ, D_MODEL), 1.0),
        "meta_tokens": nrm(ks[1], (N_META, D_MODEL), 1.0),
        "mix_norm_pre": gain(ks[2], (DEPTH, D_MODEL)),
        "mix_norm_post": gain(ks[3], (DEPTH, D_MODEL)),
        "w_in": nrm(ks[4], (DEPTH, D_MODEL, IN_COLS), D_MODEL ** -0.5),
        "hgrn_lower_bounds": nrm(ks[5], (DEPTH, HGRN_KDIM), 0.5),
        "hgrn_out_norm": gain(ks[6], (DEPTH, HGRN_WIDTH)),
        "w_branch_hgrn": nrm(ks[7], (DEPTH, HGRN_WIDTH, D_MODEL), HGRN_WIDTH ** -0.5),
        "pool_proj": nrm(ks[8], (DEPTH, POOL_GROUPS, POOL_GROUP_DIM, POOL_GROUP_DIM), POOL_GROUP_DIM ** -0.5),
        "pool_scale": gain(ks[9], (DEPTH, POOL_WIDTH)),
        "w_branch_pool": nrm(ks[10], (DEPTH, POOL_WIDTH, D_MODEL), POOL_WIDTH ** -0.5),
        "w_out": nrm(ks[11], (DEPTH, D_MODEL, D_MODEL), D_MODEL ** -0.5),
        "ffn_norm_pre": gain(ks[12], (DEPTH, D_MODEL)),
        "ffn_norm_post": gain(ks[13], (DEPTH, D_MODEL)),
        "ffn_w_gate": nrm(ks[14], (DEPTH, D_MODEL, D_FF), D_MODEL ** -0.5),
        "ffn_w_up": nrm(ks[15], (DEPTH, D_MODEL, D_FF), D_MODEL ** -0.5),
        "ffn_conv_w": nrm(ks[16], (DEPTH, CONV_WIDTH, D_FF), CONV_WIDTH ** -0.5),
        "ffn_conv_b": nrm(ks[17], (DEPTH, D_FF), 0.02),
        "ffn_w_down": nrm(ks[18], (DEPTH, D_FF, D_MODEL), D_FF ** -0.5),
    }


def _fwd_reference(x, meta_tokens, mix_norm_pre, mix_norm_post, w_in, hgrn_lower_bounds, hgrn_out_norm,
              w_branch_hgrn, pool_proj, pool_scale, w_branch_pool, w_out, ffn_norm_pre, ffn_norm_post,
              ffn_w_gate, ffn_w_up, ffn_conv_w, ffn_conv_b, ffn_w_down):
    bsz = x.shape[0]
    meta = jnp.broadcast_to(meta_tokens[None].astype(x.dtype), (bsz, N_META, D_MODEL))
    h = jnp.concatenate([meta, x], axis=1)
    gam = jax.nn.softmax(hgrn_lower_bounds.astype(jnp.float32), axis=0)
    lbs = jnp.clip(jnp.cumsum(gam, axis=0) - gam[0], 0.0, 1.0)
    cuts = list(np.cumsum(SPLITS)[:-1])
    for l in range(DEPTH):
        u = rmsnorm(h, mix_norm_pre[l])
        proj = u @ w_in[l]
        q, f_logit, i_in, g_out, v_pool, gate_a, gate_b = jnp.split(proj, cuts, axis=-1)
        a = hgrn2(q, f_logit, i_in, g_out, lbs[l], hgrn_out_norm[l])
        p = multiscale_pool(v_pool, pool_proj[l], pool_scale[l])
        z = (jax.nn.sigmoid(gate_a.astype(jnp.float32)) * (a @ w_branch_hgrn[l])
             + jax.nn.sigmoid(gate_b.astype(jnp.float32)) * (p @ w_branch_pool[l]))
        h = h + rmsnorm(z @ w_out[l], mix_norm_post[l])
        u = rmsnorm(h, ffn_norm_pre[l])
        gt = causal_dwconv(u @ ffn_w_gate[l], ffn_conv_w[l], ffn_conv_b[l])
        y = (jax.nn.gelu(gt, approximate=True) * (u @ ffn_w_up[l])) @ ffn_w_down[l]
        h = h + rmsnorm(y, ffn_norm_post[l])
    return h[:, N_META:]


import jax as _jax
import jax.numpy as _jnp

TWIN_FORMAT = 'train_step'
FWD_PARAMS = ['x', 'meta_tokens', 'mix_norm_pre', 'mix_norm_post', 'w_in', 'hgrn_lower_bounds', 'hgrn_out_norm', 'w_branch_hgrn', 'pool_proj', 'pool_scale', 'w_branch_pool', 'w_out', 'ffn_norm_pre', 'ffn_norm_post', 'ffn_w_gate', 'ffn_w_up', 'ffn_conv_w', 'ffn_conv_b', 'ffn_w_down']
TWIN_WEIGHTS = ['meta_tokens', 'mix_norm_pre', 'mix_norm_post', 'w_in', 'hgrn_lower_bounds', 'hgrn_out_norm', 'w_branch_hgrn', 'pool_proj', 'pool_scale', 'w_branch_pool', 'w_out', 'ffn_norm_pre', 'ffn_norm_post', 'ffn_w_gate', 'ffn_w_up', 'ffn_conv_w', 'ffn_conv_b', 'ffn_w_down']
TWIN_DIFF_INPUT = 'x'
TWIN_INPUTS = ['x', 'meta_tokens', 'mix_norm_pre', 'mix_norm_post', 'w_in', 'hgrn_lower_bounds', 'hgrn_out_norm', 'w_branch_hgrn', 'pool_proj', 'pool_scale', 'w_branch_pool', 'w_out', 'ffn_norm_pre', 'ffn_norm_post', 'ffn_w_gate', 'ffn_w_up', 'ffn_conv_w', 'ffn_conv_b', 'ffn_w_down', 'loss_target', 'm_meta_tokens', 'm_mix_norm_pre', 'm_mix_norm_post', 'm_w_in', 'm_hgrn_lower_bounds', 'm_hgrn_out_norm', 'm_w_branch_hgrn', 'm_pool_proj', 'm_pool_scale', 'm_w_branch_pool', 'm_w_out', 'm_ffn_norm_pre', 'm_ffn_norm_post', 'm_ffn_w_gate', 'm_ffn_w_up', 'm_ffn_conv_w', 'm_ffn_conv_b', 'm_ffn_w_down', 'v_meta_tokens', 'v_mix_norm_pre', 'v_mix_norm_post', 'v_w_in', 'v_hgrn_lower_bounds', 'v_hgrn_out_norm', 'v_w_branch_hgrn', 'v_pool_proj', 'v_pool_scale', 'v_w_branch_pool', 'v_w_out', 'v_ffn_norm_pre', 'v_ffn_norm_post', 'v_ffn_w_gate', 'v_ffn_w_up', 'v_ffn_conv_w', 'v_ffn_conv_b', 'v_ffn_w_down']
TWIN_OUTPUTS = ['loss', 'grad_x', 'grad_meta_tokens', 'grad_mix_norm_pre', 'grad_mix_norm_post', 'grad_w_in', 'grad_hgrn_lower_bounds', 'grad_hgrn_out_norm', 'grad_w_branch_hgrn', 'grad_pool_proj', 'grad_pool_scale', 'grad_w_branch_pool', 'grad_w_out', 'grad_ffn_norm_pre', 'grad_ffn_norm_post', 'grad_ffn_w_gate', 'grad_ffn_w_up', 'grad_ffn_conv_w', 'grad_ffn_conv_b', 'grad_ffn_w_down', 'delta_meta_tokens', 'delta_mix_norm_pre', 'delta_mix_norm_post', 'delta_w_in', 'delta_hgrn_lower_bounds', 'delta_hgrn_out_norm', 'delta_w_branch_hgrn', 'delta_pool_proj', 'delta_pool_scale', 'delta_w_branch_pool', 'delta_w_out', 'delta_ffn_norm_pre', 'delta_ffn_norm_post', 'delta_ffn_w_gate', 'delta_ffn_w_up', 'delta_ffn_conv_w', 'delta_ffn_conv_b', 'delta_ffn_w_down', 'new_m_meta_tokens', 'new_m_mix_norm_pre', 'new_m_mix_norm_post', 'new_m_w_in', 'new_m_hgrn_lower_bounds', 'new_m_hgrn_out_norm', 'new_m_w_branch_hgrn', 'new_m_pool_proj', 'new_m_pool_scale', 'new_m_w_branch_pool', 'new_m_w_out', 'new_m_ffn_norm_pre', 'new_m_ffn_norm_post', 'new_m_ffn_w_gate', 'new_m_ffn_w_up', 'new_m_ffn_conv_w', 'new_m_ffn_conv_b', 'new_m_ffn_w_down', 'new_v_meta_tokens', 'new_v_mix_norm_pre', 'new_v_mix_norm_post', 'new_v_w_in', 'new_v_hgrn_lower_bounds', 'new_v_hgrn_out_norm', 'new_v_w_branch_hgrn', 'new_v_pool_proj', 'new_v_pool_scale', 'new_v_w_branch_pool', 'new_v_w_out', 'new_v_ffn_norm_pre', 'new_v_ffn_norm_post', 'new_v_ffn_w_gate', 'new_v_ffn_w_up', 'new_v_ffn_conv_w', 'new_v_ffn_conv_b', 'new_v_ffn_w_down']
TWIN_LEAF_KINDS = {'loss': 'loss', 'grad_x': 'grad_x', 'grad_meta_tokens': 'grad_w', 'grad_mix_norm_pre': 'grad_w', 'grad_mix_norm_post': 'grad_w', 'grad_w_in': 'grad_w', 'grad_hgrn_lower_bounds': 'grad_w', 'grad_hgrn_out_norm': 'grad_w', 'grad_w_branch_hgrn': 'grad_w', 'grad_pool_proj': 'grad_w', 'grad_pool_scale': 'grad_w', 'grad_w_branch_pool': 'grad_w', 'grad_w_out': 'grad_w', 'grad_ffn_norm_pre': 'grad_w', 'grad_ffn_norm_post': 'grad_w', 'grad_ffn_w_gate': 'grad_w', 'grad_ffn_w_up': 'grad_w', 'grad_ffn_conv_w': 'grad_w', 'grad_ffn_conv_b': 'grad_w', 'grad_ffn_w_down': 'grad_w', 'delta_meta_tokens': 'delta_w', 'delta_mix_norm_pre': 'delta_w', 'delta_mix_norm_post': 'delta_w', 'delta_w_in': 'delta_w', 'delta_hgrn_lower_bounds': 'delta_w', 'delta_hgrn_out_norm': 'delta_w', 'delta_w_branch_hgrn': 'delta_w', 'delta_pool_proj': 'delta_w', 'delta_pool_scale': 'delta_w', 'delta_w_branch_pool': 'delta_w', 'delta_w_out': 'delta_w', 'delta_ffn_norm_pre': 'delta_w', 'delta_ffn_norm_post': 'delta_w', 'delta_ffn_w_gate': 'delta_w', 'delta_ffn_w_up': 'delta_w', 'delta_ffn_conv_w': 'delta_w', 'delta_ffn_conv_b': 'delta_w', 'delta_ffn_w_down': 'delta_w', 'new_m_meta_tokens': 'new_m', 'new_m_mix_norm_pre': 'new_m', 'new_m_mix_norm_post': 'new_m', 'new_m_w_in': 'new_m', 'new_m_hgrn_lower_bounds': 'new_m', 'new_m_hgrn_out_norm': 'new_m', 'new_m_w_branch_hgrn': 'new_m', 'new_m_pool_proj': 'new_m', 'new_m_pool_scale': 'new_m', 'new_m_w_branch_pool': 'new_m', 'new_m_w_out': 'new_m', 'new_m_ffn_norm_pre': 'new_m', 'new_m_ffn_norm_post': 'new_m', 'new_m_ffn_w_gate': 'new_m', 'new_m_ffn_w_up': 'new_m', 'new_m_ffn_conv_w': 'new_m', 'new_m_ffn_conv_b': 'new_m', 'new_m_ffn_w_down': 'new_m', 'new_v_meta_tokens': 'new_v', 'new_v_mix_norm_pre': 'new_v', 'new_v_mix_norm_post': 'new_v', 'new_v_w_in': 'new_v', 'new_v_hgrn_lower_bounds': 'new_v', 'new_v_hgrn_out_norm': 'new_v', 'new_v_w_branch_hgrn': 'new_v', 'new_v_pool_proj': 'new_v', 'new_v_pool_scale': 'new_v', 'new_v_w_branch_pool': 'new_v', 'new_v_w_out': 'new_v', 'new_v_ffn_norm_pre': 'new_v', 'new_v_ffn_norm_post': 'new_v', 'new_v_ffn_w_gate': 'new_v', 'new_v_ffn_w_up': 'new_v', 'new_v_ffn_conv_w': 'new_v', 'new_v_ffn_conv_b': 'new_v', 'new_v_ffn_w_down': 'new_v'}


def _forward(args):
    return _fwd_reference(*[args[k] for k in FWD_PARAMS])


def _output_shape():
    def fwd():
        inp = _fwd_setup_inputs(0)
        return _fwd_reference(*[inp[k] for k in FWD_PARAMS])
    out = _jax.eval_shape(fwd)
    return out.shape, out.dtype

N_MICROBATCH = 1
ADAM_LR = 0.001
ADAM_B1 = 0.9
ADAM_B2 = 0.999
ADAM_EPS = 1e-08
ADAM_WD = 0.01
ADAM_STEP = 10
PER_EXAMPLE_BATCH_AXIS = {'x': 0, 'loss_target': 0}
SHARED_INPUTS = []
_WEIGHT_DTYPES = {'meta_tokens': _jnp.float32, 'mix_norm_pre': _jnp.float32, 'mix_norm_post': _jnp.float32, 'w_in': _jnp.float32, 'hgrn_lower_bounds': _jnp.float32, 'hgrn_out_norm': _jnp.float32, 'w_branch_hgrn': _jnp.float32, 'pool_proj': _jnp.float32, 'pool_scale': _jnp.float32, 'w_branch_pool': _jnp.float32, 'w_out': _jnp.float32, 'ffn_norm_pre': _jnp.float32, 'ffn_norm_post': _jnp.float32, 'ffn_w_gate': _jnp.float32, 'ffn_w_up': _jnp.float32, 'ffn_conv_w': _jnp.float32, 'ffn_conv_b': _jnp.float32, 'ffn_w_down': _jnp.float32}
MOMENT_SCALE = {'meta_tokens': 2.992614e-02, 'mix_norm_pre': 2.112473e+00, 'mix_norm_post': 6.374394e+01, 'w_in': 1.019342e+00, 'hgrn_lower_bounds': 3.143468e-01, 'hgrn_out_norm': 8.678431e-01, 'w_branch_hgrn': 5.979382e-01, 'pool_proj': 3.188667e+00, 'pool_scale': 3.499753e+00, 'w_branch_pool': 2.438753e+00, 'w_out': 2.622098e+00, 'ffn_norm_pre': 1.647901e+00, 'ffn_norm_post': 6.416087e+01, 'ffn_w_gate': 4.639159e-01, 'ffn_w_up': 1.036344e+00, 'ffn_conv_w': 8.255688e-01, 'ffn_conv_b': 1.897891e+00, 'ffn_w_down': 1.814631e+00}


def _to_microbatches(a, axis):
    t = _jnp.moveaxis(a, axis, 0)
    t = t.reshape((N_MICROBATCH, t.shape[0] // N_MICROBATCH) + t.shape[1:])
    return _jnp.moveaxis(t, 1, axis + 1)


def setup_inputs(seed: int = 0) -> dict:
    inp = _fwd_setup_inputs(seed)
    key = _jax.random.fold_in(_jax.random.key(seed), 7919)
    shape, _ = _output_shape()
    out = dict(inp)
    out["loss_target"] = _jax.random.normal(_jax.random.fold_in(key, 0), shape, _jnp.float32)
    for i, name in enumerate(TWIN_WEIGHTS):
        w = inp[name].astype(_jnp.float32)
        if MOMENT_SCALE is None:
            s = _jnp.sqrt(_jnp.mean(_jnp.square(w)) + 1e-30)
        else:
            s = MOMENT_SCALE[name]
        km, kv = _jax.random.split(_jax.random.fold_in(key, i + 1))
        out[name] = w
        out["m_" + name] = s * _jax.random.normal(km, w.shape, _jnp.float32)
        out["v_" + name] = (s * s) * _jax.random.uniform(kv, w.shape, _jnp.float32, 0.5, 1.5)
    if N_MICROBATCH > 1:
        for name, axis in PER_EXAMPLE_BATCH_AXIS.items():
            out[name] = _to_microbatches(out[name], axis)
    return {'x': out['x'], 'meta_tokens': out['meta_tokens'], 'mix_norm_pre': out['mix_norm_pre'], 'mix_norm_post': out['mix_norm_post'], 'w_in': out['w_in'], 'hgrn_lower_bounds': out['hgrn_lower_bounds'], 'hgrn_out_norm': out['hgrn_out_norm'], 'w_branch_hgrn': out['w_branch_hgrn'], 'pool_proj': out['pool_proj'], 'pool_scale': out['pool_scale'], 'w_branch_pool': out['w_branch_pool'], 'w_out': out['w_out'], 'ffn_norm_pre': out['ffn_norm_pre'], 'ffn_norm_post': out['ffn_norm_post'], 'ffn_w_gate': out['ffn_w_gate'], 'ffn_w_up': out['ffn_w_up'], 'ffn_conv_w': out['ffn_conv_w'], 'ffn_conv_b': out['ffn_conv_b'], 'ffn_w_down': out['ffn_w_down'], 'loss_target': out['loss_target'], 'm_meta_tokens': out['m_meta_tokens'], 'm_mix_norm_pre': out['m_mix_norm_pre'], 'm_mix_norm_post': out['m_mix_norm_post'], 'm_w_in': out['m_w_in'], 'm_hgrn_lower_bounds': out['m_hgrn_lower_bounds'], 'm_hgrn_out_norm': out['m_hgrn_out_norm'], 'm_w_branch_hgrn': out['m_w_branch_hgrn'], 'm_pool_proj': out['m_pool_proj'], 'm_pool_scale': out['m_pool_scale'], 'm_w_branch_pool': out['m_w_branch_pool'], 'm_w_out': out['m_w_out'], 'm_ffn_norm_pre': out['m_ffn_norm_pre'], 'm_ffn_norm_post': out['m_ffn_norm_post'], 'm_ffn_w_gate': out['m_ffn_w_gate'], 'm_ffn_w_up': out['m_ffn_w_up'], 'm_ffn_conv_w': out['m_ffn_conv_w'], 'm_ffn_conv_b': out['m_ffn_conv_b'], 'm_ffn_w_down': out['m_ffn_w_down'], 'v_meta_tokens': out['v_meta_tokens'], 'v_mix_norm_pre': out['v_mix_norm_pre'], 'v_mix_norm_post': out['v_mix_norm_post'], 'v_w_in': out['v_w_in'], 'v_hgrn_lower_bounds': out['v_hgrn_lower_bounds'], 'v_hgrn_out_norm': out['v_hgrn_out_norm'], 'v_w_branch_hgrn': out['v_w_branch_hgrn'], 'v_pool_proj': out['v_pool_proj'], 'v_pool_scale': out['v_pool_scale'], 'v_w_branch_pool': out['v_w_branch_pool'], 'v_w_out': out['v_w_out'], 'v_ffn_norm_pre': out['v_ffn_norm_pre'], 'v_ffn_norm_post': out['v_ffn_norm_post'], 'v_ffn_w_gate': out['v_ffn_w_gate'], 'v_ffn_w_up': out['v_ffn_w_up'], 'v_ffn_conv_w': out['v_ffn_conv_w'], 'v_ffn_conv_b': out['v_ffn_conv_b'], 'v_ffn_w_down': out['v_ffn_w_down']}


def _loss(weights, diff, rest, loss_target):
    with _jax.named_scope("forward"):
        args = {**rest, TWIN_DIFF_INPUT: diff, **{k: w.astype(_WEIGHT_DTYPES[k]) for k, w in weights.items()}}
        y = _forward(args)
    with _jax.named_scope("loss_head"):
        err = _jnp.square(y.astype(_jnp.float32) - loss_target)
        return 0.5 * _jnp.sum(_jnp.mean(err, axis=-1)) if err.ndim else 0.5 * err


def _adamw(w, g, m, v):
    m = ADAM_B1 * m + (1.0 - ADAM_B1) * g
    v = ADAM_B2 * v + (1.0 - ADAM_B2) * _jnp.square(g)
    m_hat = m / (1.0 - ADAM_B1 ** ADAM_STEP)
    v_hat = v / (1.0 - ADAM_B2 ** ADAM_STEP)
    delta = -ADAM_LR * (m_hat / (_jnp.sqrt(v_hat) + ADAM_EPS) + ADAM_WD * w)
    return delta, m, v


def reference(x, meta_tokens, mix_norm_pre, mix_norm_post, w_in, hgrn_lower_bounds, hgrn_out_norm, w_branch_hgrn, pool_proj, pool_scale, w_branch_pool, w_out, ffn_norm_pre, ffn_norm_post, ffn_w_gate, ffn_w_up, ffn_conv_w, ffn_conv_b, ffn_w_down, loss_target, m_meta_tokens, m_mix_norm_pre, m_mix_norm_post, m_w_in, m_hgrn_lower_bounds, m_hgrn_out_norm, m_w_branch_hgrn, m_pool_proj, m_pool_scale, m_w_branch_pool, m_w_out, m_ffn_norm_pre, m_ffn_norm_post, m_ffn_w_gate, m_ffn_w_up, m_ffn_conv_w, m_ffn_conv_b, m_ffn_w_down, v_meta_tokens, v_mix_norm_pre, v_mix_norm_post, v_w_in, v_hgrn_lower_bounds, v_hgrn_out_norm, v_w_branch_hgrn, v_pool_proj, v_pool_scale, v_w_branch_pool, v_w_out, v_ffn_norm_pre, v_ffn_norm_post, v_ffn_w_gate, v_ffn_w_up, v_ffn_conv_w, v_ffn_conv_b, v_ffn_w_down):
    given = dict(x=x, meta_tokens=meta_tokens, mix_norm_pre=mix_norm_pre, mix_norm_post=mix_norm_post, w_in=w_in, hgrn_lower_bounds=hgrn_lower_bounds, hgrn_out_norm=hgrn_out_norm, w_branch_hgrn=w_branch_hgrn, pool_proj=pool_proj, pool_scale=pool_scale, w_branch_pool=w_branch_pool, w_out=w_out, ffn_norm_pre=ffn_norm_pre, ffn_norm_post=ffn_norm_post, ffn_w_gate=ffn_w_gate, ffn_w_up=ffn_w_up, ffn_conv_w=ffn_conv_w, ffn_conv_b=ffn_conv_b, ffn_w_down=ffn_w_down, loss_target=loss_target, m_meta_tokens=m_meta_tokens, m_mix_norm_pre=m_mix_norm_pre, m_mix_norm_post=m_mix_norm_post, m_w_in=m_w_in, m_hgrn_lower_bounds=m_hgrn_lower_bounds, m_hgrn_out_norm=m_hgrn_out_norm, m_w_branch_hgrn=m_w_branch_hgrn, m_pool_proj=m_pool_proj, m_pool_scale=m_pool_scale, m_w_branch_pool=m_w_branch_pool, m_w_out=m_w_out, m_ffn_norm_pre=m_ffn_norm_pre, m_ffn_norm_post=m_ffn_norm_post, m_ffn_w_gate=m_ffn_w_gate, m_ffn_w_up=m_ffn_w_up, m_ffn_conv_w=m_ffn_conv_w, m_ffn_conv_b=m_ffn_conv_b, m_ffn_w_down=m_ffn_w_down, v_meta_tokens=v_meta_tokens, v_mix_norm_pre=v_mix_norm_pre, v_mix_norm_post=v_mix_norm_post, v_w_in=v_w_in, v_hgrn_lower_bounds=v_hgrn_lower_bounds, v_hgrn_out_norm=v_hgrn_out_norm, v_w_branch_hgrn=v_w_branch_hgrn, v_pool_proj=v_pool_proj, v_pool_scale=v_pool_scale, v_w_branch_pool=v_w_branch_pool, v_w_out=v_w_out, v_ffn_norm_pre=v_ffn_norm_pre, v_ffn_norm_post=v_ffn_norm_post, v_ffn_w_gate=v_ffn_w_gate, v_ffn_w_up=v_ffn_w_up, v_ffn_conv_w=v_ffn_conv_w, v_ffn_conv_b=v_ffn_conv_b, v_ffn_w_down=v_ffn_w_down)
    weights = {n: given[n] for n in TWIN_WEIGHTS}
    shared = {n: given[n] for n in SHARED_INPUTS}
    per_example = {n: given[n] for n in ['x']}
    grad_fn = _jax.value_and_grad(_loss, argnums=(0, 1))

    def one_microbatch(ex, loss_target):
        ex = dict(ex)
        diff = ex.pop(TWIN_DIFF_INPUT)
        return grad_fn(weights, diff, {**shared, **ex}, loss_target)

    if N_MICROBATCH == 1:
        loss, (grad_w, grad_x) = one_microbatch(per_example, given["loss_target"])
    else:
        def body(carry, xs):
            loss_sum, grad_sum = carry
            l_k, (gw_k, gx_k) = one_microbatch(xs[0], xs[1])
            with _jax.named_scope("update"):
                return (loss_sum + l_k, _jax.tree.map(_jnp.add, grad_sum, gw_k)), gx_k

        init = (_jnp.zeros((), _jnp.float32), _jax.tree.map(_jnp.zeros_like, weights))
        (loss, grad_w), grad_x = _jax.lax.scan(body, init, (per_example, given["loss_target"]))
    with _jax.named_scope("update"):
        delta_w, new_m, new_v = {}, {}, {}
        for n in TWIN_WEIGHTS:
            delta_w[n], new_m[n], new_v[n] = _adamw(weights[n], grad_w[n], given["m_" + n], given["v_" + n])
    return (loss, grad_x, *[grad_w[n] for n in TWIN_WEIGHTS], *[delta_w[n] for n in TWIN_WEIGHTS],
            *[new_m[n] for n in TWIN_WEIGHTS], *[new_v[n] for n in TWIN_WEIGHTS])
```

```python
import jax
import jax.numpy as jnp
from jax import lax
from jax.experimental import pallas as pl
from jax.experimental.pallas import tpu as pltpu

F32 = jnp.float32
BF = jnp.bfloat16

D_MODEL = 1024
N_META = 16
DEPTH = 2
HEADS = 4
HEAD_DIM = 128
HGRN_W = 512
POOL_W = 512
POOL_WINDOWS = (2, 4, 8, 16)
D_FF = 2816
IN_COLS = 4608
EPS = 1e-6
LOG_FLOOR = 1e-30
N_DEV = 8

ADAM_LR = 0.001
ADAM_B1 = 0.9
ADAM_B2 = 0.999
ADAM_EPS = 1e-08
ADAM_WD = 0.01
ADAM_STEP = 10

SUB = 16
CHUNK = 48
EXP_CLAMP = 80.0
ROW_TILE = 432
FFN_TILE = 144
HALO = 16
VMEM_LIMIT = 56 * 1024 * 1024
MESH = pl.DeviceIdType.MESH

WEIGHT_NAMES = ['meta_tokens', 'mix_norm_pre', 'mix_norm_post', 'w_in', 'hgrn_lower_bounds', 'hgrn_out_norm',
                'w_branch_hgrn', 'pool_proj', 'pool_scale', 'w_branch_pool', 'w_out', 'ffn_norm_pre', 'ffn_norm_post',
                'ffn_w_gate', 'ffn_w_up', 'ffn_conv_w', 'ffn_conv_b', 'ffn_w_down']
SHARD_AXIS = {'meta_tokens': 1, 'w_in': 2, 'w_branch_hgrn': 2, 'w_branch_pool': 2, 'w_out': 1,
              'ffn_w_gate': 2, 'ffn_w_up': 2, 'ffn_conv_w': 2, 'ffn_w_down': 1}
MATMUL_WEIGHTS = ['w_in', 'w_branch_hgrn', 'w_branch_pool', 'w_out', 'ffn_w_gate', 'ffn_w_up', 'ffn_w_down']
F32_GATHERED = ['meta_tokens', 'ffn_conv_w']
SHARDED = [n for n in WEIGHT_NAMES if n in SHARD_AXIS]
REPLICATED = [n for n in WEIGHT_NAMES if n not in SHARD_AXIS]


def _params(sem=None):
    return pltpu.CompilerParams(dimension_semantics=sem, vmem_limit_bytes=VMEM_LIMIT)


def _tile(total, pref, mult=16):
    best = None
    for t in range(mult, min(total, pref) + 1, mult):
        if total % t == 0:
            best = t
    assert best is not None, (total, pref, mult)
    return best


def _whole(shape):
    return pl.BlockSpec(shape, lambda *_: (0,) * len(shape))


def _cols(tm, width, j):
    return pl.BlockSpec((tm, width), lambda i, j=j: (i, j))


def _dot(a, b):
    return jnp.dot(a, b, preferred_element_type=F32)


def _dot_nt(a, b):
    return lax.dot_general(a, b, (((1,), (1,)), ((), ())), preferred_element_type=F32)


def _dot_tn(a, b):
    return lax.dot_general(a, b, (((0,), (0,)), ((), ())), preferred_element_type=F32)


def _rms_fwd(x, g):
    r = lax.rsqrt(jnp.mean(x * x, axis=-1, keepdims=True) + EPS)
    return x * r * g


def _rms_bwd(dy, x, g):
    r = lax.rsqrt(jnp.mean(x * x, axis=-1, keepdims=True) + EPS)
    xh = x * r
    dyg = dy * g
    dx = r * (dyg - xh * jnp.mean(dyg * xh, axis=-1, keepdims=True))
    return dx, jnp.sum(dy * xh, axis=0, keepdims=True)


_GELU_C = 0.7978845608028654
_GELU_A = 0.044715


def _gelu(x):
    t = jnp.tanh(_GELU_C * (x + _GELU_A * x * x * x))
    return 0.5 * x * (1.0 + t), t


def _gelu_grad(x, t):
    return 0.5 * (1.0 + t) + 0.5 * x * (1.0 - t * t) * _GELU_C * (1.0 + 3.0 * _GELU_A * x * x)


def _split3(x):
    x1 = x.astype(BF)
    r1 = x - x1.astype(F32)
    x2 = r1.astype(BF)
    x3 = (r1 - x2.astype(F32)).astype(BF)
    return x1, x2, x3


def _tri_mm(tri, x):
    x1, x2, x3 = _split3(x)
    return _dot(tri, x1) + _dot(tri, x2) + _dot(tri, x3)


def _softmax2(lb_ref):
    l0 = lb_ref[0:1, :]
    l1 = lb_ref[1:2, :]
    m = jnp.maximum(l0, l1)
    e0 = jnp.exp(l0 - m)
    e1 = jnp.exp(l1 - m)
    return e0 / (e0 + e1), e1 / (e0 + e1)


def _layer_lower_bound(lb_ref, layer):
    g0, g1 = _softmax2(lb_ref)
    if layer == 0:
        return jnp.clip(g0 - g0, 0.0, 1.0)
    return jnp.clip((g0 + g1) - g0, 0.0, 1.0)


def _all_gather(x, name):
    n, w = x.shape

    def body(x_ref, out_ref, send_sems, recv_sems, local_sem):
        mx, my, mc = lax.axis_index("x"), lax.axis_index("y"), lax.axis_index("c")
        me, sibling = (mx, my, mc), (mx, my, 1 - mc)
        chips = [(1 - mx, my), (mx, 1 - my), (1 - mx, 1 - my)]

        def rows(px, py, pc):
            return out_ref.at[4 * px + 2 * py + pc]

        def copy(k, block, to, src=None):
            return pltpu.make_async_remote_copy(
                src_ref=rows(*block) if src is None else src, dst_ref=rows(*block),
                send_sem=send_sems.at[k], recv_sem=recv_sems.at[k], device_id=to, device_id_type=MESH)

        mine = pltpu.make_async_copy(x_ref, rows(*me), local_sem)
        mine.start()
        first = [copy(0, me, sibling, src=x_ref)]
        first += [copy(1 + j, me, (*chip, mc), src=x_ref) for j, chip in enumerate(chips)]
        for cp in first:
            cp.start()
        passed = [copy(4 + j, (*chip, mc), sibling) for j, chip in enumerate(chips)]
        for j, chip in enumerate(chips):
            copy(1 + j, (*chip, mc), me).wait_recv()
            passed[j].start()
        copy(0, sibling, me).wait_recv()
        for j, chip in enumerate(chips):
            copy(4 + j, (*chip, 1 - mc), me).wait_recv()
        for cp in first + passed:
            cp.wait_send()
        mine.wait()

    return pl.pallas_call(
        body, name=name,
        out_shape=jax.ShapeDtypeStruct((N_DEV, n, w), x.dtype),
        in_specs=[pl.BlockSpec(memory_space=pl.ANY)],
        out_specs=pl.BlockSpec(memory_space=pl.ANY),
        scratch_shapes=[pltpu.SemaphoreType.DMA((7,)), pltpu.SemaphoreType.DMA((7,)), pltpu.SemaphoreType.DMA],
    )(x)


def _exchange(g, name):
    _, n, w = g.shape

    def body(g_ref, out_ref, send_sems, recv_sems, local_sem):
        mx, my, mc = lax.axis_index("x"), lax.axis_index("y"), lax.axis_index("c")
        me = 4 * mx + 2 * my + mc
        mine = pltpu.make_async_copy(g_ref.at[me], out_ref.at[0], local_sem)
        mine.start()
        copies = []
        for d in range(1, N_DEV):
            px = (1 - mx) if (d >> 2) & 1 else mx
            py = (1 - my) if (d >> 1) & 1 else my
            pc = (1 - mc) if d & 1 else mc
            copies.append(pltpu.make_async_remote_copy(
                src_ref=g_ref.at[4 * px + 2 * py + pc], dst_ref=out_ref.at[d],
                send_sem=send_sems.at[d - 1], recv_sem=recv_sems.at[d - 1],
                device_id=(px, py, pc), device_id_type=MESH))
        for cp in copies:
            cp.start()
        for cp in copies:
            cp.wait_recv()
        for cp in copies:
            cp.wait_send()
        mine.wait()

    return pl.pallas_call(
        body, name=name,
        out_shape=jax.ShapeDtypeStruct((N_DEV, n, w), g.dtype),
        in_specs=[pl.BlockSpec(memory_space=pl.ANY)],
        out_specs=pl.BlockSpec(memory_space=pl.ANY),
        scratch_shapes=[pltpu.SemaphoreType.DMA((7,)), pltpu.SemaphoreType.DMA((7,)), pltpu.SemaphoreType.DMA],
    )(g)


def _sum8(parts, name):
    _, n, w = parts.shape
    tn = _tile(n, 1024, 8)

    def body(p_ref, o_ref):
        acc = p_ref[0]
        for d in range(1, N_DEV):
            acc = acc + p_ref[d]
        o_ref[...] = acc

    return pl.pallas_call(
        body, name=name, out_shape=jax.ShapeDtypeStruct((n, w), F32), grid=(n // tn,),
        in_specs=[pl.BlockSpec((N_DEV, tn, w), lambda i: (0, i, 0))],
        out_specs=pl.BlockSpec((tn, w), lambda i: (i, 0)),
        compiler_params=_params(("parallel",)),
    )(parts)


def _pack(arrays, dtype, row_mult):
    flat = jnp.concatenate([a.astype(dtype).reshape(-1) for a in arrays])
    pad = (-flat.shape[0]) % (128 * row_mult)
    if pad:
        flat = jnp.concatenate([flat, jnp.zeros((pad,), dtype)])
    return flat.reshape(-1, 128)


def _pack_blocks(arrays, row_mult):
    flat = jnp.concatenate([a.astype(F32).reshape(N_DEV, -1) for a in arrays], axis=1)
    pad = (-flat.shape[1]) % (128 * row_mult)
    if pad:
        flat = jnp.concatenate([flat, jnp.zeros((N_DEV, pad), F32)], axis=1)
    return flat.reshape(N_DEV, -1, 128)


def _unpack(flat2d, shapes):
    lead = flat2d.shape[:-2]
    flat = flat2d.reshape(lead + (-1,))
    out, off = [], 0
    for s in shapes:
        size = 1
        for d in s:
            size *= d
        out.append(flat[..., off:off + size].reshape(lead + tuple(s)))
        off += size
    return out


def _blocks_to_full(blocks, axis):
    moved = jnp.moveaxis(blocks, 0, axis)
    shape = list(moved.shape)
    shape[axis:axis + 2] = [shape[axis] * shape[axis + 1]]
    return moved.reshape(shape)


def _full_to_blocks(full, axis):
    shape = list(full.shape)
    shape[axis:axis + 1] = [N_DEV, shape[axis] // N_DEV]
    return jnp.moveaxis(full.reshape(shape), axis, 0)


def _norm_matmul(h, gain, w, tn, name):
    t_rows, d = h.shape
    n = w.shape[1]
    tm = _tile(t_rows, ROW_TILE)

    def body(h_ref, g_ref, w_ref, o_ref):
        u = _rms_fwd(h_ref[...], g_ref[...])
        o_ref[...] = _dot(u.astype(BF), w_ref[...])

    return pl.pallas_call(
        body, name=name, out_shape=jax.ShapeDtypeStruct((t_rows, n), F32), grid=(n // tn, t_rows // tm),
        in_specs=[pl.BlockSpec((tm, d), lambda j, i: (i, 0)), pl.BlockSpec((1, d), lambda j, i: (0, 0)),
                  pl.BlockSpec((d, tn), lambda j, i: (0, j))],
        out_specs=pl.BlockSpec((tm, tn), lambda j, i: (i, j)),
        compiler_params=_params(("parallel", "parallel")),
    )(h, gain, w)


def _chunk_masks():
    row = lax.broadcasted_iota(jnp.int32, (CHUNK, CHUNK), 0)
    col = lax.broadcasted_iota(jnp.int32, (CHUNK, CHUNK), 1)
    tri_lo = (col <= row).astype(BF)
    tri_up = (col >= row).astype(BF)
    rb = jnp.right_shift(row, 4)
    cb = jnp.right_shift(col, 4)
    diag = (rb == cb) & (col <= row)
    off = [(rb == i) & (col < SUB * i) for i in range(1, CHUNK // SUB)]
    return tri_lo, tri_up, diag, off


def _chunk_gates(fl, lb):
    sg = jax.nn.sigmoid(fl)
    s2 = jax.nn.sigmoid(-fl)
    f = lb + (1.0 - lb) * sg
    lf = jnp.log(jnp.maximum(f, LOG_FLOOR))
    k = (1.0 - lb) * s2
    return sg, s2, f, lf, k


def _chunk_factors(q, k, b):
    nb = CHUNK // SUB
    rq = jnp.concatenate([jnp.broadcast_to(b[SUB * i:SUB * i + 1], (SUB, HEAD_DIM)) for i in range(nb)], axis=0)
    mq = jnp.concatenate([jnp.broadcast_to(b[SUB * i + SUB // 2:SUB * i + SUB // 2 + 1], (SUB, HEAD_DIM))
                          for i in range(nb)], axis=0)
    e_qs = jnp.exp(b - rq)
    e_qm = jnp.exp(jnp.minimum(b - mq, EXP_CLAMP))
    e_km = jnp.exp(jnp.minimum(mq - b, EXP_CLAMP))
    e_ks = [jnp.exp(jnp.minimum(b[SUB * i:SUB * i + 1] - b, 0.0)) for i in range(1, nb)]
    qs = (q * e_qs).astype(BF)
    qm = (q * e_qm).astype(BF)
    km = (k * e_km).astype(BF)
    ks = [(k * e).astype(BF) for e in e_ks]
    return e_qs, e_qm, e_km, e_ks, qs, qm, km, ks


def _chunk_scores(qs, qm, km, ks, diag, off):
    a = jnp.where(diag, _dot_nt(qm, km), 0.0)
    for m, kk in zip(off, ks):
        a = jnp.where(m, _dot_nt(qs, kk), a)
    return a


def _hgrn_fwd(proj, lower_bounds, layer, name):
    t_rows = proj.shape[0]
    tm = _tile(t_rows, ROW_TILE, CHUNK)
    nct = tm // CHUNK

    def body(q_ref, f_ref, i_ref, lb_ref, o_ref, sall_ref, st_ref):
        @pl.when(pl.program_id(0) == 0)
        def _():
            st_ref[...] = jnp.zeros_like(st_ref)

        lbs = _layer_lower_bound(lb_ref, layer)
        tri_lo, _, diag, off = _chunk_masks()

        def chunk(c, carry):
            r0 = pl.multiple_of(c * CHUNK, SUB)
            for hh in range(HEADS):
                cs = slice(hh * HEAD_DIM, (hh + 1) * HEAD_DIM)
                q = q_ref[pl.ds(r0, CHUNK), cs]
                v = i_ref[pl.ds(r0, CHUNK), cs]
                _, _, _, lf, k = _chunk_gates(f_ref[pl.ds(r0, CHUNK), cs], lbs[:, cs])
                b = _tri_mm(tri_lo, lf)
                _, _, _, _, qs, qm, km, ks = _chunk_factors(q, k, b)
                a = _chunk_scores(qs, qm, km, ks, diag, off)
                st = st_ref[hh]
                sall_ref[c, hh] = st
                vb = v.astype(BF)
                qc = (q * jnp.exp(b)).astype(BF)
                o_ref[pl.ds(r0, CHUNK), cs] = _dot_nt(qc, st.astype(BF)) + _dot(a.astype(BF), vb)
                b_last = b[CHUNK - 1:CHUNK]
                kdec = (k * jnp.exp(b_last - b)).astype(BF)
                st_ref[hh] = st * jnp.exp(b_last) + _dot_tn(vb, kdec)
            return carry

        lax.fori_loop(0, nct, chunk, 0)

    return pl.pallas_call(
        body, name=name,
        out_shape=(jax.ShapeDtypeStruct((t_rows, HGRN_W), F32),
                   jax.ShapeDtypeStruct((t_rows // CHUNK, HEADS, HEAD_DIM, HEAD_DIM), F32)),
        grid=(t_rows // tm,),
        in_specs=[_cols(tm, 512, 0), _cols(tm, 512, 1), _cols(tm, 512, 2), _whole((DEPTH, HGRN_W))],
        out_specs=(pl.BlockSpec((tm, HGRN_W), lambda i: (i, 0)),
                   pl.BlockSpec((nct, HEADS, HEAD_DIM, HEAD_DIM), lambda i: (i, 0, 0, 0))),
        scratch_shapes=[pltpu.VMEM((HEADS, HEAD_DIM, HEAD_DIM), F32)],
        compiler_params=_params(("arbitrary",)),
    )(proj, proj, proj, lower_bounds)


def _head_norm(o):
    outs, rs = [], []
    for hh in range(HEADS):
        oh = o[:, hh * HEAD_DIM:(hh + 1) * HEAD_DIM]
        r = lax.rsqrt(jnp.mean(oh * oh, axis=-1, keepdims=True) + EPS)
        outs.append(oh * r)
        rs.append(r)
    return outs, rs


def _window_counts(row0, rows):
    t1 = (row0 + lax.broadcasted_iota(jnp.int32, (rows, 1), 0) + 1).astype(F32)
    return [1.0 / jnp.minimum(t1, float(w)) for w in POOL_WINDOWS]


def _pool_fwd(v, halo, row0):
    rows = v.shape[0]
    inv = _window_counts(row0, rows)
    outs = []
    for gi, w in enumerate(POOL_WINDOWS):
        cs = slice(gi * HEAD_DIM, (gi + 1) * HEAD_DIM)
        s = jnp.concatenate([halo[:, cs], v[:, cs]], axis=0)
        step = 1
        while step < w:
            s = s + pltpu.roll(s, step, 0)
            step *= 2
        outs.append(s[HALO:] * inv[gi] - v[:, cs])
    return outs


def _pool_bwd(dpooled, halo, row0):
    rows = dpooled[0].shape[0]
    inv = _window_counts(row0, rows)
    inv_h = _window_counts(row0 + rows, HALO)
    outs = []
    for gi, w in enumerate(POOL_WINDOWS):
        s = jnp.concatenate([dpooled[gi] * inv[gi], halo[gi] * inv_h[gi]], axis=0)
        n_ext = s.shape[0]
        step = 1
        while step < w:
            s = s + pltpu.roll(s, n_ext - step, 0)
            step *= 2
        outs.append(s[:rows] - dpooled[gi])
    return outs


def _pool_project(pooled, pp_ref, scale):
    y = jnp.concatenate([_dot(pooled[gi].astype(BF), pp_ref[gi].astype(BF)) for gi in range(4)], axis=1)
    return y, y * scale


def _mix_merge(proj, o, h, out_gain, pool_proj, pool_scale, w_bh, w_bp, w_out, gain_post, name):
    t_rows = h.shape[0]
    tm = _tile(t_rows, ROW_TILE)
    hpt = tm // HALO

    def body(g_ref, v_ref, vh_ref, ga0_ref, ga1_ref, gb0_ref, gb1_ref, o_ref, h_ref, og_ref, pp_ref, ps_ref,
             wbh_ref, wbp_ref, wo_ref, gp_ref, hn_ref, r_ref, a_ref, p_ref):
        i = pl.program_id(0)
        on, _ = _head_norm(o_ref[...])
        a = jnp.concatenate(on, axis=1) * og_ref[...] * jax.nn.sigmoid(g_ref[...])
        halo = jnp.where(i == 0, 0.0, vh_ref[...])
        pooled = _pool_fwd(v_ref[...], halo, i * tm)
        _, p = _pool_project(pooled, pp_ref, ps_ref[...])
        ab, pb = a.astype(BF), p.astype(BF)
        sa = jax.nn.sigmoid(jnp.concatenate([ga0_ref[...], ga1_ref[...]], axis=1))
        sb = jax.nn.sigmoid(jnp.concatenate([gb0_ref[...], gb1_ref[...]], axis=1))
        z = sa * _dot(ab, wbh_ref[...]) + sb * _dot(pb, wbp_ref[...])
        r = _dot(z.astype(BF), wo_ref[...])
        hn_ref[...] = h_ref[...] + _rms_fwd(r, gp_ref[...])
        r_ref[...] = r
        a_ref[...] = ab
        p_ref[...] = pb

    rows = lambda width: pl.BlockSpec((tm, width), lambda i: (i, 0))
    return pl.pallas_call(
        body, name=name,
        out_shape=(jax.ShapeDtypeStruct((t_rows, D_MODEL), F32), jax.ShapeDtypeStruct((t_rows, D_MODEL), F32),
                   jax.ShapeDtypeStruct((t_rows, HGRN_W), BF), jax.ShapeDtypeStruct((t_rows, POOL_W), BF)),
        grid=(t_rows // tm,),
        in_specs=[_cols(tm, 512, 3), _cols(tm, 512, 4),
                  pl.BlockSpec((HALO, 512), lambda i: (jnp.maximum(i * hpt - 1, 0), 4)),
                  _cols(tm, 512, 5), _cols(tm, 512, 6), _cols(tm, 512, 7), _cols(tm, 512, 8),
                  rows(HGRN_W), rows(D_MODEL), _whole((1, HGRN_W)), _whole((4, HEAD_DIM, HEAD_DIM)),
                  _whole((1, POOL_W)), _whole((HGRN_W, D_MODEL)), _whole((POOL_W, D_MODEL)),
                  _whole((D_MODEL, D_MODEL)), _whole((1, D_MODEL))],
        out_specs=(rows(D_MODEL), rows(D_MODEL), rows(HGRN_W), rows(POOL_W)),
        compiler_params=_params(("parallel",)),
    )(proj, proj, proj, proj, proj, proj, proj, o, h, out_gain, pool_proj, pool_scale, w_bh, w_bp, w_out, gain_post)


def _conv_fwd(g, halo, cw):
    ext = jnp.concatenate([halo, g], axis=0)
    return (cw[0:1] * pltpu.roll(ext, 2, 0)[8:] + cw[1:2] * pltpu.roll(ext, 1, 0)[8:] + cw[2:3] * g)


def _ffn_down(gu, h, conv_w, conv_b, w_down, gain_post, name):
    t_rows = h.shape[0]
    tm = _tile(t_rows, FFN_TILE)
    hpt = tm // 8

    def body(g_ref, gh_ref, up_ref, h_ref, cw_ref, cb_ref, wd_ref, gp_ref, hn_ref, y_ref):
        i = pl.program_id(0)
        halo = jnp.where(i == 0, 0.0, gh_ref[...])
        gt = _conv_fwd(g_ref[...], halo, cw_ref[...]) + cb_ref[...]
        act = _gelu(gt)[0] * up_ref[...]
        y = _dot(act.astype(BF), wd_ref[...])
        hn_ref[...] = h_ref[...] + _rms_fwd(y, gp_ref[...])
        y_ref[...] = y

    rows = lambda width: pl.BlockSpec((tm, width), lambda i: (i, 0))
    return pl.pallas_call(
        body, name=name,
        out_shape=(jax.ShapeDtypeStruct((t_rows, D_MODEL), F32), jax.ShapeDtypeStruct((t_rows, D_MODEL), F32)),
        grid=(t_rows // tm,),
        in_specs=[_cols(tm, D_FF, 0), pl.BlockSpec((8, D_FF), lambda i: (jnp.maximum(i * hpt - 1, 0), 0)),
                  _cols(tm, D_FF, 1), rows(D_MODEL), _whole((3, D_FF)), _whole((1, D_FF)),
                  _whole((D_FF, D_MODEL)), _whole((1, D_MODEL))],
        out_specs=(rows(D_MODEL), rows(D_MODEL)),
        compiler_params=_params(("parallel",)),
    )(gu, gu, gu, h, conv_w, conv_b, w_down, gain_post)


def _loss_grad(h, target, name):
    t_rows, d = h.shape
    tm = _tile(t_rows, ROW_TILE)

    def body(h_ref, t_ref, dh_ref, l_ref):
        i = pl.program_id(0)

        @pl.when(i == 0)
        def _():
            l_ref[...] = jnp.zeros_like(l_ref)

        row = i * tm + lax.broadcasted_iota(jnp.int32, (tm, 1), 0)
        err = jnp.where(row >= N_META, h_ref[...] - t_ref[...], 0.0)
        dh_ref[...] = err * (1.0 / d)
        l_ref[...] += jnp.sum(err * err, axis=0, keepdims=True) * (0.5 / d)

    rows = pl.BlockSpec((tm, d), lambda i: (i, 0))
    return pl.pallas_call(
        body, name=name,
        out_shape=(jax.ShapeDtypeStruct((t_rows, d), F32), jax.ShapeDtypeStruct((1, d), F32)),
        grid=(t_rows // tm,), in_specs=[rows, rows], out_specs=(rows, _whole((1, d))),
        compiler_params=_params(("arbitrary",)),
    )(h, target)


def _dw(a, b, tn, name, gain=None):
    t_rows, k = a.shape
    n = b.shape[1]
    tm = _tile(t_rows, ROW_TILE)

    def body(*refs):
        if gain is None:
            a_ref, b_ref, o_ref = refs
            av = a_ref[...]
        else:
            a_ref, g_ref, b_ref, o_ref = refs
            av = _rms_fwd(a_ref[...], g_ref[...])

        @pl.when(pl.program_id(1) == 0)
        def _():
            o_ref[...] = jnp.zeros_like(o_ref)

        o_ref[...] += _dot_tn(av.astype(BF), b_ref[...].astype(BF))

    in_specs = [pl.BlockSpec((tm, k), lambda j, i: (i, 0))]
    args = [a]
    if gain is not None:
        in_specs.append(pl.BlockSpec((1, k), lambda j, i: (0, 0)))
        args.append(gain)
    in_specs.append(pl.BlockSpec((tm, tn), lambda j, i: (i, j)))
    args.append(b)
    return pl.pallas_call(
        body, name=name, out_shape=jax.ShapeDtypeStruct((k, n), F32), grid=(n // tn, t_rows // tm),
        in_specs=in_specs, out_specs=pl.BlockSpec((k, tn), lambda j, i: (0, j)),
        compiler_params=_params(("parallel", "arbitrary")),
    )(*args)


def _ffn_bwd_down(dh, y, gain_post, gu, conv_w, conv_b, w_down, name):
    t_rows = dh.shape[0]
    tm = _tile(t_rows, FFN_TILE)
    hpt = tm // 8

    def body(dh_ref, y_ref, gp_ref, g_ref, gh_ref, up_ref, cw_ref, cb_ref, wd_ref,
             dy_ref, act_ref, dgt_ref, dup_ref, dgp_ref, dcw_ref, dcb_ref):
        i = pl.program_id(0)

        @pl.when(i == 0)
        def _():
            dgp_ref[...] = jnp.zeros_like(dgp_ref)
            dcw_ref[...] = jnp.zeros_like(dcw_ref)
            dcb_ref[...] = jnp.zeros_like(dcb_ref)

        dy, dgp = _rms_bwd(dh_ref[...], y_ref[...], gp_ref[...])
        dgp_ref[...] += dgp
        dyb = dy.astype(BF)
        dy_ref[...] = dyb
        g = g_ref[...]
        up = up_ref[...]
        ext = jnp.concatenate([jnp.where(i == 0, 0.0, gh_ref[...]), g], axis=0)
        g2 = pltpu.roll(ext, 2, 0)[8:]
        g1 = pltpu.roll(ext, 1, 0)[8:]
        cw = cw_ref[...]
        gt = cw[0:1] * g2 + cw[1:2] * g1 + cw[2:3] * g + cb_ref[...]
        gl, th = _gelu(gt)
        act_ref[...] = (gl * up).astype(BF)
        dact = _dot_nt(dyb, wd_ref[...])
        dup_ref[...] = dact * gl
        dgt = dact * up * _gelu_grad(gt, th)
        dgt_ref[...] = dgt
        dcb_ref[...] += jnp.sum(dgt, axis=0, keepdims=True)
        dcw_ref[...] += jnp.concatenate([jnp.sum(dgt * g2, axis=0, keepdims=True),
                                         jnp.sum(dgt * g1, axis=0, keepdims=True),
                                         jnp.sum(dgt * g, axis=0, keepdims=True)], axis=0)

    rows = lambda width: pl.BlockSpec((tm, width), lambda i: (i, 0))
    return pl.pallas_call(
        body, name=name,
        out_shape=(jax.ShapeDtypeStruct((t_rows, D_MODEL), BF), jax.ShapeDtypeStruct((t_rows, D_FF), BF),
                   jax.ShapeDtypeStruct((t_rows, D_FF), F32), jax.ShapeDtypeStruct((t_rows, D_FF), F32),
                   jax.ShapeDtypeStruct((1, D_MODEL), F32), jax.ShapeDtypeStruct((3, D_FF), F32),
                   jax.ShapeDtypeStruct((1, D_FF), F32)),
        grid=(t_rows // tm,),
        in_specs=[rows(D_MODEL), rows(D_MODEL), _whole((1, D_MODEL)), _cols(tm, D_FF, 0),
                  pl.BlockSpec((8, D_FF), lambda i: (jnp.maximum(i * hpt - 1, 0), 0)), _cols(tm, D_FF, 1),
                  _whole((3, D_FF)), _whole((1, D_FF)), _whole((D_FF, D_MODEL))],
        out_specs=(rows(D_MODEL), rows(D_FF), rows(D_FF), rows(D_FF),
                   _whole((1, D_MODEL)), _whole((3, D_FF)), _whole((1, D_FF))),
        compiler_params=_params(("arbitrary",)),
    )(dh, y, gain_post, gu, gu, gu, conv_w, conv_b, w_down)


def _ffn_bwd_up(dgt, dup, conv_w, w_gu, h, gain_pre, dh_out, name):
    t_rows = h.shape[0]
    tm = _tile(t_rows, FFN_TILE)
    hpt = tm // 8
    last = t_rows // tm - 1

    def body(dgt_ref, dgn_ref, dup_ref, cw_ref, w_ref, h_ref, gp_ref, dho_ref, dh_ref, dgu_ref, dg_ref):
        i = pl.program_id(0)

        @pl.when(i == 0)
        def _():
            dg_ref[...] = jnp.zeros_like(dg_ref)

        dgt = dgt_ref[...]
        ext = jnp.concatenate([dgt, jnp.where(i == last, 0.0, dgn_ref[...])], axis=0)
        n_ext = tm + 8
        cw = cw_ref[...]
        dg = (cw[2:3] * dgt + cw[1:2] * pltpu.roll(ext, n_ext - 1, 0)[:tm]
              + cw[0:1] * pltpu.roll(ext, n_ext - 2, 0)[:tm])
        dgb = dg.astype(BF)
        dub = dup_ref[...].astype(BF)
        dgu_ref[:, 0:D_FF] = dgb
        dgu_ref[:, D_FF:2 * D_FF] = dub
        du = _dot_nt(dgb, w_ref[:, 0:D_FF]) + _dot_nt(dub, w_ref[:, D_FF:2 * D_FF])
        dx, dgain = _rms_bwd(du, h_ref[...], gp_ref[...])
        dh_ref[...] = dho_ref[...] + dx
        dg_ref[...] += dgain

    rows = lambda width: pl.BlockSpec((tm, width), lambda i: (i, 0))
    return pl.pallas_call(
        body, name=name,
        out_shape=(jax.ShapeDtypeStruct((t_rows, D_MODEL), F32), jax.ShapeDtypeStruct((t_rows, 2 * D_FF), BF),
                   jax.ShapeDtypeStruct((1, D_MODEL), F32)),
        grid=(t_rows // tm,),
        in_specs=[rows(D_FF), pl.BlockSpec((8, D_FF), lambda i: (jnp.minimum((i + 1) * hpt, (last + 1) * hpt - 1), 0)),
                  rows(D_FF), _whole((3, D_FF)), _whole((D_MODEL, 2 * D_FF)), rows(D_MODEL), _whole((1, D_MODEL)),
                  rows(D_MODEL)],
        out_specs=(rows(D_MODEL), rows(2 * D_FF), _whole((1, D_MODEL))),
        compiler_params=_params(("arbitrary",)),
    )(dgt, dgt, dup, conv_w, w_gu, h, gain_pre, dh_out)


def _mix_bwd_a(dh, r, gain_post, proj, a, p, w_bh, w_bp, w_out, name):
    t_rows = dh.shape[0]
    tm = _tile(t_rows, ROW_TILE)

    def body(dh_ref, r_ref, gp_ref, ga0_ref, ga1_ref, gb0_ref, gb1_ref, a_ref, p_ref, wbh_ref, wbp_ref, wo_ref,
             dr_ref, z_ref, dya_ref, dyp_ref, da_ref, dp_ref, dgab_ref, dg_ref):
        @pl.when(pl.program_id(0) == 0)
        def _():
            dg_ref[...] = jnp.zeros_like(dg_ref)

        dr, dgain = _rms_bwd(dh_ref[...], r_ref[...], gp_ref[...])
        dg_ref[...] += dgain
        drb = dr.astype(BF)
        dr_ref[...] = drb
        dz = _dot_nt(drb, wo_ref[...])
        ya = _dot(a_ref[...], wbh_ref[...])
        yp = _dot(p_ref[...], wbp_ref[...])
        sa = jax.nn.sigmoid(jnp.concatenate([ga0_ref[...], ga1_ref[...]], axis=1))
        sb = jax.nn.sigmoid(jnp.concatenate([gb0_ref[...], gb1_ref[...]], axis=1))
        z_ref[...] = (sa * ya + sb * yp).astype(BF)
        dgab_ref[:, 0:D_MODEL] = (dz * ya * sa * (1.0 - sa)).astype(BF)
        dgab_ref[:, D_MODEL:2 * D_MODEL] = (dz * yp * sb * (1.0 - sb)).astype(BF)
        dya = (dz * sa).astype(BF)
        dyp = (dz * sb).astype(BF)
        dya_ref[...] = dya
        dyp_ref[...] = dyp
        da_ref[...] = _dot_nt(dya, wbh_ref[...])
        dp_ref[...] = _dot_nt(dyp, wbp_ref[...])

    rows = lambda width: pl.BlockSpec((tm, width), lambda i: (i, 0))
    bf = lambda width: jax.ShapeDtypeStruct((t_rows, width), BF)
    f32 = lambda width: jax.ShapeDtypeStruct((t_rows, width), F32)
    return pl.pallas_call(
        body, name=name,
        out_shape=(bf(D_MODEL), bf(D_MODEL), bf(D_MODEL), bf(D_MODEL), f32(HGRN_W), f32(POOL_W), bf(2 * D_MODEL),
                   jax.ShapeDtypeStruct((1, D_MODEL), F32)),
        grid=(t_rows // tm,),
        in_specs=[rows(D_MODEL), rows(D_MODEL), _whole((1, D_MODEL)),
                  _cols(tm, 512, 5), _cols(tm, 512, 6), _cols(tm, 512, 7), _cols(tm, 512, 8),
                  rows(HGRN_W), rows(POOL_W), _whole((HGRN_W, D_MODEL)), _whole((POOL_W, D_MODEL)),
                  _whole((D_MODEL, D_MODEL))],
        out_specs=(rows(D_MODEL), rows(D_MODEL), rows(D_MODEL), rows(D_MODEL), rows(HGRN_W), rows(POOL_W),
                   rows(2 * D_MODEL), _whole((1, D_MODEL))),
        compiler_params=_params(("arbitrary",)),
    )(dh, r, gain_post, proj, proj, proj, proj, a, p, w_bh, w_bp, w_out)


def _mix_bwd_b(da, dp, proj, o, out_gain, pool_proj, pool_scale, name):
    t_rows = da.shape[0]
    tm = _tile(t_rows, ROW_TILE)
    hpt = tm // HALO
    last = t_rows // tm - 1

    def body(da_ref, dp_ref, dpn_ref, g_ref, v_ref, vh_ref, o_ref, og_ref, pp_ref, ps_ref,
             do_ref, dgv_ref, dog_ref, dpp_ref, dps_ref):
        i = pl.program_id(0)

        @pl.when(i == 0)
        def _():
            dog_ref[...] = jnp.zeros_like(dog_ref)
            dpp_ref[...] = jnp.zeros_like(dpp_ref)
            dps_ref[...] = jnp.zeros_like(dps_ref)

        da = da_ref[...]
        og = og_ref[...]
        o = o_ref[...]
        on, rs = _head_norm(o)
        onc = jnp.concatenate(on, axis=1)
        sg = jax.nn.sigmoid(g_ref[...])
        dog_ref[...] += jnp.sum(da * onc * sg, axis=0, keepdims=True)
        dgv_ref[:, 0:HGRN_W] = (da * onc * og * sg * (1.0 - sg)).astype(BF)
        don = da * og * sg
        for hh in range(HEADS):
            cs = slice(hh * HEAD_DIM, (hh + 1) * HEAD_DIM)
            d = don[:, cs]
            do_ref[:, cs] = rs[hh] * (d - on[hh] * jnp.mean(d * on[hh], axis=-1, keepdims=True))

        scale = ps_ref[...]
        halo = jnp.where(i == 0, 0.0, vh_ref[...])
        pooled = _pool_fwd(v_ref[...], halo, i * tm)
        y, _ = _pool_project(pooled, pp_ref, scale)
        dp = dp_ref[...]
        dps_ref[...] += jnp.sum(dp * y, axis=0, keepdims=True)
        dy = dp * scale
        dyn = jnp.where(i == last, 0.0, dpn_ref[...]) * scale
        dpooled, dhalo = [], []
        for gi in range(4):
            cs = slice(gi * HEAD_DIM, (gi + 1) * HEAD_DIM)
            ppb = pp_ref[gi].astype(BF)
            dyb = dy[:, cs].astype(BF)
            dpooled.append(_dot_nt(dyb, ppb))
            dhalo.append(_dot_nt(dyn[:, cs].astype(BF), ppb))
            dpp_ref[gi] += _dot_tn(pooled[gi].astype(BF), dyb)
        dv = _pool_bwd(dpooled, dhalo, i * tm)
        dgv_ref[:, HGRN_W:HGRN_W + POOL_W] = jnp.concatenate(dv, axis=1).astype(BF)

    rows = lambda width: pl.BlockSpec((tm, width), lambda i: (i, 0))
    return pl.pallas_call(
        body, name=name,
        out_shape=(jax.ShapeDtypeStruct((t_rows, HGRN_W), F32), jax.ShapeDtypeStruct((t_rows, HGRN_W + POOL_W), BF),
                   jax.ShapeDtypeStruct((1, HGRN_W), F32), jax.ShapeDtypeStruct((4, HEAD_DIM, HEAD_DIM), F32),
                   jax.ShapeDtypeStruct((1, POOL_W), F32)),
        grid=(t_rows // tm,),
        in_specs=[rows(HGRN_W), rows(POOL_W),
                  pl.BlockSpec((HALO, POOL_W), lambda i: (jnp.minimum((i + 1) * hpt, (last + 1) * hpt - 1), 0)),
                  _cols(tm, 512, 3), _cols(tm, 512, 4),
                  pl.BlockSpec((HALO, 512), lambda i: (jnp.maximum(i * hpt - 1, 0), 4)),
                  rows(HGRN_W), _whole((1, HGRN_W)), _whole((4, HEAD_DIM, HEAD_DIM)), _whole((1, POOL_W))],
        out_specs=(rows(HGRN_W), rows(HGRN_W + POOL_W), _whole((1, HGRN_W)), _whole((4, HEAD_DIM, HEAD_DIM)),
                   _whole((1, POOL_W))),
        compiler_params=_params(("arbitrary",)),
    )(da, dp, dp, proj, proj, proj, o, out_gain, pool_proj, pool_scale)


def _hgrn_bwd(proj, lower_bounds, layer, states, do, name):
    t_rows = proj.shape[0]
    tm = _tile(t_rows, ROW_TILE, CHUNK)
    nct = tm // CHUNK
    n_tiles = t_rows // tm

    def body(q_ref, f_ref, i_ref, lb_ref, sall_ref, do_ref, dqfi_ref, dlb_ref, dst_ref):
        @pl.when(pl.program_id(0) == 0)
        def _():
            dst_ref[...] = jnp.zeros_like(dst_ref)
            dlb_ref[...] = jnp.zeros_like(dlb_ref)

        lbs = _layer_lower_bound(lb_ref, layer)
        tri_lo, tri_up, diag, off = _chunk_masks()
        is_last_row = lax.broadcasted_iota(jnp.int32, (CHUNK, 1), 0) == CHUNK - 1

        def chunk(ci, carry):
            c = nct - 1 - ci
            r0 = pl.multiple_of(c * CHUNK, SUB)
            for hh in range(HEADS):
                cs = slice(hh * HEAD_DIM, (hh + 1) * HEAD_DIM)
                lb = lbs[:, cs]
                q = q_ref[pl.ds(r0, CHUNK), cs]
                v = i_ref[pl.ds(r0, CHUNK), cs]
                sg, s2, f, lf, k = _chunk_gates(f_ref[pl.ds(r0, CHUNK), cs], lb)
                b = _tri_mm(tri_lo, lf)
                e_qs, e_qm, e_km, e_ks, qs, qm, km, ks = _chunk_factors(q, k, b)
                a = _chunk_scores(qs, qm, km, ks, diag, off)
                st = sall_ref[c, hh]
                dst = dst_ref[hh]
                stb, dstb = st.astype(BF), dst.astype(BF)
                vb = v.astype(BF)
                dob = do_ref[pl.ds(r0, CHUNK), cs].astype(BF)
                b_last = b[CHUNK - 1:CHUNK]
                e_b = jnp.exp(b)
                e_last = jnp.exp(b_last)
                e_kl = jnp.exp(b_last - b)
                qc = (q * e_b).astype(BF)
                kdec = (k * e_kl).astype(BF)

                da_full = _dot_nt(dob, vb)
                dv = _dot_tn(a.astype(BF), dob) + _dot_nt(kdec, dstb)
                dq = e_b * _dot(dob, stb)
                da_d = jnp.where(diag, da_full, 0.0).astype(BF)
                dq = dq + e_qm * _dot(da_d, km)
                dk = e_km * _dot_tn(da_d, qm) + e_kl * _dot(vb, dstb)
                for m, kk, ek in zip(off, ks, e_ks):
                    da_i = jnp.where(m, da_full, 0.0).astype(BF)
                    dq = dq + e_qs * _dot(da_i, kk)
                    dk = dk + ek * _dot_tn(da_i, qs)

                st_new = st * e_last + _dot_tn(vb, kdec)
                db = q * dq - k * dk
                db = db + jnp.where(is_last_row, jnp.sum(st_new * dst, axis=0, keepdims=True), 0.0)
                dlf = _tri_mm(tri_up, db)
                df = jnp.where(f > LOG_FLOOR, dlf / f, 0.0)
                dfl = df * (1.0 - lb) * sg * (1.0 - sg) - dk * (1.0 - lb) * s2 * (1.0 - s2)
                dlb_ref[:, cs] += jnp.sum(df * (1.0 - sg) - dk * s2, axis=0, keepdims=True)
                dst_ref[hh] = dst * e_last + _dot_tn(dob, qc)

                dqfi_ref[pl.ds(r0, CHUNK), cs] = dq.astype(BF)
                dqfi_ref[pl.ds(r0, CHUNK), pl.ds(HGRN_W + hh * HEAD_DIM, HEAD_DIM)] = dfl.astype(BF)
                dqfi_ref[pl.ds(r0, CHUNK), pl.ds(2 * HGRN_W + hh * HEAD_DIM, HEAD_DIM)] = dv.astype(BF)
            return carry

        lax.fori_loop(0, nct, chunk, 0)

    rev = lambda width, j: pl.BlockSpec((tm, width), lambda i, j=j: (n_tiles - 1 - i, j))
    return pl.pallas_call(
        body, name=name,
        out_shape=(jax.ShapeDtypeStruct((t_rows, 3 * HGRN_W), BF), jax.ShapeDtypeStruct((1, HGRN_W), F32)),
        grid=(n_tiles,),
        in_specs=[rev(512, 0), rev(512, 1), rev(512, 2), _whole((DEPTH, HGRN_W)),
                  pl.BlockSpec((nct, HEADS, HEAD_DIM, HEAD_DIM), lambda i: (n_tiles - 1 - i, 0, 0, 0)),
                  rev(HGRN_W, 0)],
        out_specs=(rev(3 * HGRN_W, 0), _whole((1, HGRN_W))),
        scratch_shapes=[pltpu.VMEM((HEADS, HEAD_DIM, HEAD_DIM), F32)],
        compiler_params=_params(("arbitrary",)),
    )(proj, proj, proj, lower_bounds, states, do)


def _in_bwd(dqfi, dgv, dgab, w_in, h, gain_pre, dh_out, name):
    t_rows = h.shape[0]
    tm = _tile(t_rows, ROW_TILE)
    c1 = 3 * HGRN_W
    c2 = c1 + HGRN_W + POOL_W

    def body(d1_ref, d2_ref, d3_ref, w_ref, h_ref, gp_ref, dho_ref, dh_ref, dg_ref):
        @pl.when(pl.program_id(0) == 0)
        def _():
            dg_ref[...] = jnp.zeros_like(dg_ref)

        du = (_dot_nt(d1_ref[...], w_ref[:, 0:c1]) + _dot_nt(d2_ref[...], w_ref[:, c1:c2])
              + _dot_nt(d3_ref[...], w_ref[:, c2:IN_COLS]))
        dx, dgain = _rms_bwd(du, h_ref[...], gp_ref[...])
        dh_ref[...] = dho_ref[...] + dx
        dg_ref[...] += dgain

    rows = lambda width: pl.BlockSpec((tm, width), lambda i: (i, 0))
    return pl.pallas_call(
        body, name=name,
        out_shape=(jax.ShapeDtypeStruct((t_rows, D_MODEL), F32), jax.ShapeDtypeStruct((1, D_MODEL), F32)),
        grid=(t_rows // tm,),
        in_specs=[rows(c1), rows(c2 - c1), rows(IN_COLS - c2), _whole((D_MODEL, IN_COLS)), rows(D_MODEL),
                  _whole((1, D_MODEL)), rows(D_MODEL)],
        out_specs=(rows(D_MODEL), _whole((1, D_MODEL))),
        compiler_params=_params(("arbitrary",)),
    )(dqfi, dgv, dgab, w_in, h, gain_pre, dh_out)


def _lower_bound_grad(lower_bounds, dlbs, name):
    def body(lb_ref, d_ref, o_ref):
        g0, g1 = _softmax2(lb_ref)
        bound = (g0 + g1) - g0
        inside = (bound > 0.0) & (bound < 1.0)
        dg1 = jnp.where(inside, d_ref[1:2, :], 0.0)
        inner = g1 * dg1
        o_ref[0:1, :] = g0 * (0.0 - inner)
        o_ref[1:2, :] = g1 * (dg1 - inner)

    return pl.pallas_call(body, name=name, out_shape=jax.ShapeDtypeStruct(lower_bounds.shape, F32))(lower_bounds, dlbs)


def _adamw(w, g, m, v, name):
    r, c = w.shape
    tr = r if (r % 8 or r <= 512) else _tile(r, 512, 8)
    c1 = 1.0 - ADAM_B1 ** ADAM_STEP
    c2 = 1.0 - ADAM_B2 ** ADAM_STEP

    def body(w_ref, g_ref, m_ref, v_ref, d_ref, nm_ref, nv_ref):
        gg = g_ref[...]
        nm = ADAM_B1 * m_ref[...] + (1.0 - ADAM_B1) * gg
        nv = ADAM_B2 * v_ref[...] + (1.0 - ADAM_B2) * (gg * gg)
        d_ref[...] = -ADAM_LR * ((nm / c1) / (jnp.sqrt(nv / c2) + ADAM_EPS) + ADAM_WD * w_ref[...])
        nm_ref[...] = nm
        nv_ref[...] = nv

    blk = pl.BlockSpec((tr, c), lambda i: (i, 0))
    shp = jax.ShapeDtypeStruct((r, c), F32)
    return pl.pallas_call(
        body, name=name, out_shape=(shp, shp, shp), grid=(r // tr,),
        in_specs=[blk, blk, blk, blk], out_specs=(blk, blk, blk),
        compiler_params=_params(("parallel",)),
    )(w, g, m, v)


def _as2d(a):
    return a.reshape(-1, a.shape[-1])


def _train_step(x, target, w, m, v):
    mat_shapes = [w[n].shape for n in MATMUL_WEIGHTS]
    gathered = _all_gather(_pack([w[n] for n in MATMUL_WEIGHTS], BF, 16), "gather_weights")
    full = {n: _blocks_to_full(blk, SHARD_AXIS[n]) for n, blk in zip(MATMUL_WEIGHTS, _unpack(gathered, mat_shapes))}
    f32_shapes = [w[n].shape for n in F32_GATHERED]
    gathered32 = _all_gather(_pack([w[n] for n in F32_GATHERED], F32, 8), "gather_meta_conv")
    full.update({n: _blocks_to_full(blk, SHARD_AXIS[n])
                 for n, blk in zip(F32_GATHERED, _unpack(gathered32, f32_shapes))})
    w_gu = jnp.concatenate([full['ffn_w_gate'], full['ffn_w_up']], axis=2)
    row = lambda name, l: w[name][l][None]

    h = jnp.concatenate([full['meta_tokens'], x], axis=0)
    saved = []
    for l in range(DEPTH):
        s = {'h_in': h}
        s['proj'] = _norm_matmul(h, row('mix_norm_pre', l), full['w_in'][l], IN_COLS // 2, f"in_proj_{l}")
        s['o'], s['states'] = _hgrn_fwd(s['proj'], w['hgrn_lower_bounds'], l, f"hgrn_fwd_{l}")
        h, s['r'], s['a'], s['p'] = _mix_merge(
            s['proj'], s['o'], h, row('hgrn_out_norm', l), w['pool_proj'][l], row('pool_scale', l),
            full['w_branch_hgrn'][l], full['w_branch_pool'][l], full['w_out'][l], row('mix_norm_post', l),
            f"mix_merge_{l}")
        s['h_mid'] = h
        s['gu'] = _norm_matmul(h, row('ffn_norm_pre', l), w_gu[l], D_FF, f"ffn_proj_{l}")
        h, s['y'] = _ffn_down(s['gu'], h, full['ffn_conv_w'][l], row('ffn_conv_b', l), full['ffn_w_down'][l],
                              row('ffn_norm_post', l), f"ffn_down_{l}")
        saved.append(s)

    dh, loss_cols = _loss_grad(h, jnp.pad(target, ((N_META, 0), (0, 0))), "loss_grad")
    loss = jnp.sum(loss_cols)

    gf = {n: [None] * DEPTH for n in WEIGHT_NAMES}
    for l in reversed(range(DEPTH)):
        s = saved[l]
        dy, act, dgt, dup, gf['ffn_norm_post'][l], gf['ffn_conv_w'][l], gf['ffn_conv_b'][l] = _ffn_bwd_down(
            dh, s['y'], row('ffn_norm_post', l), s['gu'], full['ffn_conv_w'][l], row('ffn_conv_b', l),
            full['ffn_w_down'][l], f"ffn_bwd_down_{l}")
        gf['ffn_w_down'][l] = _dw(act, dy, D_MODEL // 2, f"dw_down_{l}")
        dh, dgu, gf['ffn_norm_pre'][l] = _ffn_bwd_up(
            dgt, dup, full['ffn_conv_w'][l], w_gu[l], s['h_mid'], row('ffn_norm_pre', l), dh, f"ffn_bwd_up_{l}")
        dwgu = _dw(s['h_mid'], dgu, D_FF // 2, f"dw_gate_up_{l}", gain=row('ffn_norm_pre', l))
        gf['ffn_w_gate'][l], gf['ffn_w_up'][l] = dwgu[:, :D_FF], dwgu[:, D_FF:]

        dr, z, dya, dyp, da, dp, dgab, gf['mix_norm_post'][l] = _mix_bwd_a(
            dh, s['r'], row('mix_norm_post', l), s['proj'], s['a'], s['p'], full['w_branch_hgrn'][l],
            full['w_branch_pool'][l], full['w_out'][l], f"mix_bwd_a_{l}")
        gf['w_out'][l] = _dw(z, dr, D_MODEL, f"dw_out_{l}")
        gf['w_branch_hgrn'][l] = _dw(s['a'], dya, D_MODEL, f"dw_branch_hgrn_{l}")
        gf['w_branch_pool'][l] = _dw(s['p'], dyp, D_MODEL, f"dw_branch_pool_{l}")
        do, dgv, gf['hgrn_out_norm'][l], gf['pool_proj'][l], gf['pool_scale'][l] = _mix_bwd_b(
            da, dp, s['proj'], s['o'], row('hgrn_out_norm', l), w['pool_proj'][l], row('pool_scale', l),
            f"mix_bwd_b_{l}")
        dqfi, gf['hgrn_lower_bounds'][l] = _hgrn_bwd(s['proj'], w['hgrn_lower_bounds'], l, s['states'], do,
                                                     f"hgrn_bwd_{l}")
        dh, gf['mix_norm_pre'][l] = _in_bwd(dqfi, dgv, dgab, full['w_in'][l], s['h_in'], row('mix_norm_pre', l), dh,
                                            f"in_bwd_{l}")
        gain = row('mix_norm_pre', l)
        gf['w_in'][l] = jnp.concatenate(
            [_dw(s['h_in'], dqfi, 3 * HGRN_W, f"dw_in_qfi_{l}", gain=gain),
             _dw(s['h_in'], dgv, HGRN_W + POOL_W, f"dw_in_gv_{l}", gain=gain),
             _dw(s['h_in'], dgab, D_MODEL, f"dw_in_gates_{l}", gain=gain)], axis=1)

    grad_x = dh[N_META:]
    stack = lambda n: jnp.stack(gf[n]) if gf[n][0].shape[0] != 1 else jnp.concatenate(gf[n], axis=0)
    gfull = {n: stack(n) for n in WEIGHT_NAMES if n != 'meta_tokens'}
    gfull['meta_tokens'] = dh[:N_META]

    shard_shapes = [w[n].shape for n in SHARDED]
    blocks = _pack_blocks([_full_to_blocks(gfull[n], SHARD_AXIS[n]) for n in SHARDED], 8)
    received = _exchange(blocks, "exchange_grads")
    grads = dict(zip(SHARDED, _unpack(_sum8(received, "sum_grads"), shard_shapes)))

    rep_shapes = [w[n].shape for n in REPLICATED]
    partials = _all_gather(_pack([gfull[n] for n in REPLICATED], F32, 8), "gather_small_grads")
    rep = dict(zip(REPLICATED, _unpack(_sum8(partials, "sum_small_grads"), rep_shapes)))
    rep['hgrn_lower_bounds'] = _lower_bound_grad(w['hgrn_lower_bounds'], rep['hgrn_lower_bounds'], "lower_bound_grad")
    grads.update(rep)

    delta, new_m, new_v = {}, {}, {}
    for n in WEIGHT_NAMES:
        shape = w[n].shape
        d2, m2, v2 = _adamw(_as2d(w[n]), _as2d(grads[n]), _as2d(m[n]), _as2d(v[n]), f"adamw_{n}")
        delta[n], new_m[n], new_v[n] = d2.reshape(shape), m2.reshape(shape), v2.reshape(shape)
    return loss, grad_x, grads, delta, new_m, new_v


def kernel(x, meta_tokens, mix_norm_pre, mix_norm_post, w_in, hgrn_lower_bounds, hgrn_out_norm, w_branch_hgrn, pool_proj, pool_scale, w_branch_pool, w_out, ffn_norm_pre, ffn_norm_post, ffn_w_gate, ffn_w_up, ffn_conv_w, ffn_conv_b, ffn_w_down, loss_target, m_meta_tokens, m_mix_norm_pre, m_mix_norm_post, m_w_in, m_hgrn_lower_bounds, m_hgrn_out_norm, m_w_branch_hgrn, m_pool_proj, m_pool_scale, m_w_branch_pool, m_w_out, m_ffn_norm_pre, m_ffn_norm_post, m_ffn_w_gate, m_ffn_w_up, m_ffn_conv_w, m_ffn_conv_b, m_ffn_w_down, v_meta_tokens, v_mix_norm_pre, v_mix_norm_post, v_w_in, v_hgrn_lower_bounds, v_hgrn_out_norm, v_w_branch_hgrn, v_pool_proj, v_pool_scale, v_w_branch_pool, v_w_out, v_ffn_norm_pre, v_ffn_norm_post, v_ffn_w_gate, v_ffn_w_up, v_ffn_conv_w, v_ffn_conv_b, v_ffn_w_down):
    w = dict(zip(WEIGHT_NAMES, (meta_tokens, mix_norm_pre, mix_norm_post, w_in, hgrn_lower_bounds, hgrn_out_norm,
                                w_branch_hgrn, pool_proj, pool_scale, w_branch_pool, w_out, ffn_norm_pre,
                                ffn_norm_post, ffn_w_gate, ffn_w_up, ffn_conv_w, ffn_conv_b, ffn_w_down)))
    m = dict(zip(WEIGHT_NAMES, (m_meta_tokens, m_mix_norm_pre, m_mix_norm_post, m_w_in, m_hgrn_lower_bounds,
                                m_hgrn_out_norm, m_w_branch_hgrn, m_pool_proj, m_pool_scale, m_w_branch_pool, m_w_out,
                                m_ffn_norm_pre, m_ffn_norm_post, m_ffn_w_gate, m_ffn_w_up, m_ffn_conv_w,
                                m_ffn_conv_b, m_ffn_w_down)))
    v = dict(zip(WEIGHT_NAMES, (v_meta_tokens, v_mix_norm_pre, v_mix_norm_post, v_w_in, v_hgrn_lower_bounds,
                                v_hgrn_out_norm, v_w_branch_hgrn, v_pool_proj, v_pool_scale, v_w_branch_pool, v_w_out,
                                v_ffn_norm_pre, v_ffn_norm_post, v_ffn_w_gate, v_ffn_w_up, v_ffn_conv_w,
                                v_ffn_conv_b, v_ffn_w_down)))
    loss_local, grad_x, grads, delta, new_m, new_v = _train_step(x[0], loss_target[0], w, m, v)
    loss = lax.psum(loss_local, ("x", "y", "c"))
    return (loss, grad_x[None], *[grads[n] for n in WEIGHT_NAMES], *[delta[n] for n in WEIGHT_NAMES],
            *[new_m[n] for n in WEIGHT_NAMES], *[new_v[n] for n in WEIGHT_NAMES])
```

```python
import jax
import jax.numpy as jnp
from jax import lax
from jax.experimental import pallas as pl
from jax.experimental.pallas import tpu as pltpu

F32 = jnp.float32
BF = jnp.bfloat16

D_MODEL = 1024
N_META = 16
DEPTH = 2
HEADS = 4
HEAD_DIM = 128
HGRN_W = 512
POOL_W = 512
POOL_WINDOWS = (2, 4, 8, 16)
D_FF = 2816
IN_COLS = 4608
EPS = 1e-6
LOG_FLOOR = 1e-30
N_DEV = 8

ADAM_LR = 0.001
ADAM_B1 = 0.9
ADAM_B2 = 0.999
ADAM_EPS = 1e-08
ADAM_WD = 0.01
ADAM_STEP = 10

SUB = 16
CHUNK = 48
EXP_CLAMP = 80.0
ROW_TILE = 432
FFN_TILE = 144
HALO = 16
VMEM_LIMIT = 56 * 1024 * 1024
MESH = pl.DeviceIdType.MESH

WEIGHT_NAMES = ['meta_tokens', 'mix_norm_pre', 'mix_norm_post', 'w_in', 'hgrn_lower_bounds', 'hgrn_out_norm',
                'w_branch_hgrn', 'pool_proj', 'pool_scale', 'w_branch_pool', 'w_out', 'ffn_norm_pre', 'ffn_norm_post',
                'ffn_w_gate', 'ffn_w_up', 'ffn_conv_w', 'ffn_conv_b', 'ffn_w_down']
SHARD_AXIS = {'meta_tokens': 1, 'w_in': 2, 'w_branch_hgrn': 2, 'w_branch_pool': 2, 'w_out': 1,
              'ffn_w_gate': 2, 'ffn_w_up': 2, 'ffn_conv_w': 2, 'ffn_w_down': 1}
F32_GATHERED = ['meta_tokens', 'ffn_conv_w']
REPLICATED = [n for n in WEIGHT_NAMES if n not in SHARD_AXIS]

W_ROWS = {'w_in': (0, 576),
          'wb': (576, 128),
          'w_out': (704, 128),
          'wg': (832, 352),
          'wu': (1184, 352),
          'wd': (1536, 352)}
SLAB_ROWS = 1888


def _params(sem=None):
    return pltpu.CompilerParams(dimension_semantics=sem, vmem_limit_bytes=VMEM_LIMIT)


def _tile(total, pref, mult=16):
    best = None
    for t in range(mult, min(total, pref) + 1, mult):
        if total % t == 0:
            best = t
    assert best is not None, (total, pref, mult)
    return best


def _whole(shape):
    return pl.BlockSpec(shape, lambda *_: (0,) * len(shape))


def _cols(tm, width, j):
    return pl.BlockSpec((tm, width), lambda i, j=j: (i, j))


def _dot(a, b):
    return jnp.dot(a, b, preferred_element_type=F32)


def _dot_nt(a, b):
    return lax.dot_general(a, b, (((1,), (1,)), ((), ())), preferred_element_type=F32)


def _dot_tn(a, b):
    return lax.dot_general(a, b, (((0,), (0,)), ((), ())), preferred_element_type=F32)


def _rms_fwd(x, g):
    r = lax.rsqrt(jnp.mean(x * x, axis=-1, keepdims=True) + EPS)
    return x * r * g


def _rms_bwd(dy, x, g):
    r = lax.rsqrt(jnp.mean(x * x, axis=-1, keepdims=True) + EPS)
    xh = x * r
    dyg = dy * g
    dx = r * (dyg - xh * jnp.mean(dyg * xh, axis=-1, keepdims=True))
    return dx, jnp.sum(dy * xh, axis=0, keepdims=True)


_GELU_C = 0.7978845608028654
_GELU_A = 0.044715


def _gelu(x):
    t = jnp.tanh(_GELU_C * (x + _GELU_A * x * x * x))
    return 0.5 * x * (1.0 + t), t


def _gelu_grad(x, t):
    return 0.5 * (1.0 + t) + 0.5 * x * (1.0 - t * t) * _GELU_C * (1.0 + 3.0 * _GELU_A * x * x)


def _split3(x):
    x1 = x.astype(BF)
    r1 = x - x1.astype(F32)
    x2 = r1.astype(BF)
    x3 = (r1 - x2.astype(F32)).astype(BF)
    return x1, x2, x3


def _tri_mm(tri, x):
    x1, x2, x3 = _split3(x)
    return _dot(tri, x1) + _dot(tri, x2) + _dot(tri, x3)


def _softmax2(lb_ref):
    l0 = lb_ref[0:1, :]
    l1 = lb_ref[1:2, :]
    m = jnp.maximum(l0, l1)
    e0 = jnp.exp(l0 - m)
    e1 = jnp.exp(l1 - m)
    return e0 / (e0 + e1), e1 / (e0 + e1)


def _layer_lower_bound(lb_ref, layer):
    g0, g1 = _softmax2(lb_ref)
    if layer == 0:
        return jnp.clip(g0 - g0, 0.0, 1.0)
    return jnp.clip((g0 + g1) - g0, 0.0, 1.0)


def _all_gather(x, name):
    n, w = x.shape

    def body(x_ref, out_ref, send_sems, recv_sems, local_sem):
        mx, my, mc = lax.axis_index("x"), lax.axis_index("y"), lax.axis_index("c")
        me, sibling = (mx, my, mc), (mx, my, 1 - mc)
        chips = [(1 - mx, my), (mx, 1 - my), (1 - mx, 1 - my)]

        def rows(px, py, pc):
            return out_ref.at[4 * px + 2 * py + pc]

        def copy(k, block, to, src=None):
            return pltpu.make_async_remote_copy(
                src_ref=rows(*block) if src is None else src, dst_ref=rows(*block),
                send_sem=send_sems.at[k], recv_sem=recv_sems.at[k], device_id=to, device_id_type=MESH)

        mine = pltpu.make_async_copy(x_ref, rows(*me), local_sem)
        mine.start()
        first = [copy(0, me, sibling, src=x_ref)]
        first += [copy(1 + j, me, (*chip, mc), src=x_ref) for j, chip in enumerate(chips)]
        for cp in first:
            cp.start()
        passed = [copy(4 + j, (*chip, mc), sibling) for j, chip in enumerate(chips)]
        for j, chip in enumerate(chips):
            copy(1 + j, (*chip, mc), me).wait_recv()
            passed[j].start()
        copy(0, sibling, me).wait_recv()
        for j, chip in enumerate(chips):
            copy(4 + j, (*chip, 1 - mc), me).wait_recv()
        for cp in first + passed:
            cp.wait_send()
        mine.wait()

    return pl.pallas_call(
        body, name=name,
        out_shape=jax.ShapeDtypeStruct((N_DEV, n, w), x.dtype),
        in_specs=[pl.BlockSpec(memory_space=pl.ANY)],
        out_specs=pl.BlockSpec(memory_space=pl.ANY),
        scratch_shapes=[pltpu.SemaphoreType.DMA((7,)), pltpu.SemaphoreType.DMA((7,)), pltpu.SemaphoreType.DMA],
    )(x)


GRAD_PIECES = [(0, 0, 576, 0), (1, 0, 128, 576), (2, 0, 128, 704), (3, 0, 352, 832), (3, D_FF, 352, 1184),
               (4, 0, 352, 1536)]


def _exchange(d_in, d_b, d_out, d_gu, d_down, name):
    def body(a0, a1, a2, a3, a4, out_ref, send_sems, recv_sems, local_sems):
        arrays = (a0, a1, a2, a3, a4)
        mx, my, mc = lax.axis_index("x"), lax.axis_index("y"), lax.axis_index("c")

        def pieces(dev, slot):
            for ai, base, rows, off in GRAD_PIECES:
                src = arrays[ai].at[pl.ds(pl.multiple_of(base + rows * dev, 8), rows)]
                yield src, out_ref.at[slot, pl.ds(off, rows)]

        local = [pltpu.make_async_copy(src, dst, local_sems.at[k])
                 for k, (src, dst) in enumerate(pieces(4 * mx + 2 * my + mc, 0))]
        for cp in local:
            cp.start()
        for d in range(1, N_DEV):
            px = (1 - mx) if (d >> 2) & 1 else mx
            py = (1 - my) if (d >> 1) & 1 else my
            pc = (1 - mc) if d & 1 else mc
            for src, dst in pieces(4 * px + 2 * py + pc, d):
                pltpu.make_async_remote_copy(
                    src_ref=src, dst_ref=dst, send_sem=send_sems.at[d - 1], recv_sem=recv_sems.at[d - 1],
                    device_id=(px, py, pc), device_id_type=MESH).start()
        for d in range(1, N_DEV):
            whole = pltpu.make_async_remote_copy(
                src_ref=out_ref.at[d], dst_ref=out_ref.at[d], send_sem=send_sems.at[d - 1],
                recv_sem=recv_sems.at[d - 1], device_id=(mx, my, mc), device_id_type=MESH)
            whole.wait_recv()
            whole.wait_send()
        for cp in local:
            cp.wait()

    return pl.pallas_call(
        body, name=name,
        out_shape=jax.ShapeDtypeStruct((N_DEV, SLAB_ROWS, D_MODEL), F32),
        in_specs=[pl.BlockSpec(memory_space=pl.ANY)] * 5,
        out_specs=pl.BlockSpec(memory_space=pl.ANY),
        scratch_shapes=[pltpu.SemaphoreType.DMA((7,)), pltpu.SemaphoreType.DMA((7,)),
                        pltpu.SemaphoreType.DMA((len(GRAD_PIECES),))],
    )(d_in, d_b, d_out, d_gu, d_down)


def _sum8(parts, name):
    _, n, w = parts.shape
    tn = _tile(n, 256 if w > 128 else 1024, 8)

    def body(p_ref, o_ref):
        acc = p_ref[0]
        for d in range(1, N_DEV):
            acc = acc + p_ref[d]
        o_ref[...] = acc

    return pl.pallas_call(
        body, name=name, out_shape=jax.ShapeDtypeStruct((n, w), F32), grid=(n // tn,),
        in_specs=[pl.BlockSpec((N_DEV, tn, w), lambda i: (0, i, 0))],
        out_specs=pl.BlockSpec((tn, w), lambda i: (i, 0)),
        compiler_params=_params(("parallel",)),
    )(parts)


def _pack(arrays, dtype, row_mult):
    flat = jnp.concatenate([a.astype(dtype).reshape(-1) for a in arrays])
    pad = (-flat.shape[0]) % (128 * row_mult)
    if pad:
        flat = jnp.concatenate([flat, jnp.zeros((pad,), dtype)])
    return flat.reshape(-1, 128)


def _unpack(flat2d, shapes):
    lead = flat2d.shape[:-2]
    flat = flat2d.reshape(lead + (-1,))
    out, off = [], 0
    for s in shapes:
        size = 1
        for d in s:
            size *= d
        out.append(flat[..., off:off + size].reshape(lead + tuple(s)))
        off += size
    return out


def _blocks_to_full(blocks, axis):
    moved = jnp.moveaxis(blocks, 0, axis)
    shape = list(moved.shape)
    shape[axis:axis + 2] = [shape[axis] * shape[axis + 1]]
    return moved.reshape(shape)


def _weight_scratch(keys):
    return ([pltpu.VMEM((N_DEV * W_ROWS[k][1], D_MODEL), BF) for k in keys]
            + [pltpu.SemaphoreType.DMA((N_DEV * len(keys),))])


def _fetch_weights(slabs_ref, keys, bufs, sems):
    @pl.when(pl.program_id(0) == 0)
    def _():
        copies = []
        for ki, (key, buf) in enumerate(zip(keys, bufs)):
            row0, rows = W_ROWS[key]
            for j in range(N_DEV):
                copies.append(pltpu.make_async_copy(slabs_ref.at[j, pl.ds(row0, rows)],
                                                    buf.at[pl.ds(j * rows, rows)], sems.at[ki * N_DEV + j]))
        for cp in copies:
            cp.start()
        for cp in copies:
            cp.wait()


HBM_SPEC = pl.BlockSpec(memory_space=pl.ANY)


def _norm_matmul(h, gain, slabs, keys, name):
    t_rows, d = h.shape
    widths = [N_DEV * W_ROWS[k][1] for k in keys]
    tm = _tile(t_rows, ROW_TILE)

    def body(h_ref, g_ref, slabs_ref, o_ref, *scratch):
        bufs, sems = scratch[:-1], scratch[-1]
        _fetch_weights(slabs_ref, keys, bufs, sems)
        u = _rms_fwd(h_ref[...], g_ref[...]).astype(BF)
        off = 0
        for buf, n in zip(bufs, widths):
            o_ref[:, off:off + n] = _dot_nt(u, buf[...])
            off += n

    return pl.pallas_call(
        body, name=name, out_shape=jax.ShapeDtypeStruct((t_rows, sum(widths)), F32), grid=(t_rows // tm,),
        in_specs=[pl.BlockSpec((tm, d), lambda i: (i, 0)), _whole((1, d)), HBM_SPEC],
        out_specs=pl.BlockSpec((tm, sum(widths)), lambda i: (i, 0)),
        scratch_shapes=_weight_scratch(keys),
        compiler_params=_params(("arbitrary",)),
    )(h, gain, slabs)


def _chunk_masks():
    row = lax.broadcasted_iota(jnp.int32, (CHUNK, CHUNK), 0)
    col = lax.broadcasted_iota(jnp.int32, (CHUNK, CHUNK), 1)
    tri_lo = (col <= row).astype(BF)
    tri_up = (col >= row).astype(BF)
    rb = jnp.right_shift(row, 4)
    cb = jnp.right_shift(col, 4)
    diag = (rb == cb) & (col <= row)
    off = [(rb == i) & (col < SUB * i) for i in range(1, CHUNK // SUB)]
    return tri_lo, tri_up, diag, off


def _chunk_gates(fl, lb):
    sg = jax.nn.sigmoid(fl)
    s2 = jax.nn.sigmoid(-fl)
    f = lb + (1.0 - lb) * sg
    lf = jnp.log(jnp.maximum(f, LOG_FLOOR))
    k = (1.0 - lb) * s2
    return sg, s2, f, lf, k


def _chunk_factors(q, k, b):
    nb = CHUNK // SUB
    rq = jnp.concatenate([jnp.broadcast_to(b[SUB * i:SUB * i + 1], (SUB, HEAD_DIM)) for i in range(nb)], axis=0)
    mq = jnp.concatenate([jnp.broadcast_to(b[SUB * i + SUB // 2:SUB * i + SUB // 2 + 1], (SUB, HEAD_DIM))
                          for i in range(nb)], axis=0)
    e_qs = jnp.exp(b - rq)
    e_qm = jnp.exp(jnp.minimum(b - mq, EXP_CLAMP))
    e_km = jnp.exp(jnp.minimum(mq - b, EXP_CLAMP))
    e_ks = [jnp.exp(jnp.minimum(b[SUB * i:SUB * i + 1] - b, 0.0)) for i in range(1, nb)]
    qs = (q * e_qs).astype(BF)
    qm = (q * e_qm).astype(BF)
    km = (k * e_km).astype(BF)
    ks = [(k * e).astype(BF) for e in e_ks]
    return e_qs, e_qm, e_km, e_ks, qs, qm, km, ks


def _chunk_scores(qs, qm, km, ks, diag, off):
    a = jnp.where(diag, _dot_nt(qm, km), 0.0)
    for m, kk in zip(off, ks):
        a = jnp.where(m, _dot_nt(qs, kk), a)
    return a


def _hgrn_fwd(proj, lower_bounds, layer, name):
    t_rows = proj.shape[0]
    tm = _tile(t_rows, ROW_TILE, CHUNK)
    nct = tm // CHUNK

    def body(q_ref, f_ref, i_ref, lb_ref, o_ref, sall_ref, st_ref):
        @pl.when(pl.program_id(0) == 0)
        def _():
            st_ref[...] = jnp.zeros_like(st_ref)

        lbs = _layer_lower_bound(lb_ref, layer)
        tri_lo, _, diag, off = _chunk_masks()

        def chunk(c, carry):
            r0 = pl.multiple_of(c * CHUNK, SUB)
            for hh in range(HEADS):
                cs = slice(hh * HEAD_DIM, (hh + 1) * HEAD_DIM)
                q = q_ref[pl.ds(r0, CHUNK), cs]
                v = i_ref[pl.ds(r0, CHUNK), cs]
                _, _, _, lf, k = _chunk_gates(f_ref[pl.ds(r0, CHUNK), cs], lbs[:, cs])
                b = _tri_mm(tri_lo, lf)
                _, _, _, _, qs, qm, km, ks = _chunk_factors(q, k, b)
                a = _chunk_scores(qs, qm, km, ks, diag, off)
                st = st_ref[hh]
                sall_ref[c, hh] = st
                vb = v.astype(BF)
                qc = (q * jnp.exp(b)).astype(BF)
                o_ref[pl.ds(r0, CHUNK), cs] = _dot_nt(qc, st.astype(BF)) + _dot(a.astype(BF), vb)
                b_last = b[CHUNK - 1:CHUNK]
                kdec = (k * jnp.exp(b_last - b)).astype(BF)
                st_ref[hh] = st * jnp.exp(b_last) + _dot_tn(vb, kdec)
            return carry

        lax.fori_loop(0, nct, chunk, 0)

    return pl.pallas_call(
        body, name=name,
        out_shape=(jax.ShapeDtypeStruct((t_rows, HGRN_W), F32),
                   jax.ShapeDtypeStruct((t_rows // CHUNK, HEADS, HEAD_DIM, HEAD_DIM), F32)),
        grid=(t_rows // tm,),
        in_specs=[_cols(tm, 512, 0), _cols(tm, 512, 1), _cols(tm, 512, 2), _whole((DEPTH, HGRN_W))],
        out_specs=(pl.BlockSpec((tm, HGRN_W), lambda i: (i, 0)),
                   pl.BlockSpec((nct, HEADS, HEAD_DIM, HEAD_DIM), lambda i: (i, 0, 0, 0))),
        scratch_shapes=[pltpu.VMEM((HEADS, HEAD_DIM, HEAD_DIM), F32)],
        compiler_params=_params(("arbitrary",)),
    )(proj, proj, proj, lower_bounds)


def _head_norm(o):
    outs, rs = [], []
    for hh in range(HEADS):
        oh = o[:, hh * HEAD_DIM:(hh + 1) * HEAD_DIM]
        r = lax.rsqrt(jnp.mean(oh * oh, axis=-1, keepdims=True) + EPS)
        outs.append(oh * r)
        rs.append(r)
    return outs, rs


def _window_counts(row0, rows):
    t1 = (row0 + lax.broadcasted_iota(jnp.int32, (rows, 1), 0) + 1).astype(F32)
    return [1.0 / jnp.minimum(t1, float(w)) for w in POOL_WINDOWS]


def _pool_fwd(v, halo, row0):
    rows = v.shape[0]
    inv = _window_counts(row0, rows)
    outs = []
    for gi, w in enumerate(POOL_WINDOWS):
        cs = slice(gi * HEAD_DIM, (gi + 1) * HEAD_DIM)
        s = jnp.concatenate([halo[:, cs], v[:, cs]], axis=0)
        step = 1
        while step < w:
            s = s + pltpu.roll(s, step, 0)
            step *= 2
        outs.append(s[HALO:] * inv[gi] - v[:, cs])
    return outs


def _pool_bwd(dpooled, halo, row0):
    rows = dpooled[0].shape[0]
    inv = _window_counts(row0, rows)
    inv_h = _window_counts(row0 + rows, HALO)
    outs = []
    for gi, w in enumerate(POOL_WINDOWS):
        s = jnp.concatenate([dpooled[gi] * inv[gi], halo[gi] * inv_h[gi]], axis=0)
        n_ext = s.shape[0]
        step = 1
        while step < w:
            s = s + pltpu.roll(s, n_ext - step, 0)
            step *= 2
        outs.append(s[:rows] - dpooled[gi])
    return outs


def _pool_project(pooled, pp_ref, scale):
    y = jnp.concatenate([_dot(pooled[gi].astype(BF), pp_ref[gi].astype(BF)) for gi in range(4)], axis=1)
    return y, y * scale


def _mix_merge(proj, o, h, out_gain, pool_proj, pool_scale, slabs, gain_post, name):
    t_rows = h.shape[0]
    tm = _tile(t_rows, ROW_TILE)
    hpt = tm // HALO
    keys = ['wb', 'w_out']

    def body(g_ref, v_ref, vh_ref, ga0_ref, ga1_ref, gb0_ref, gb1_ref, o_ref, h_ref, og_ref, pp_ref, ps_ref,
             slabs_ref, gp_ref, hn_ref, r_ref, a_ref, p_ref, wb_ref, wo_ref, sems):
        _fetch_weights(slabs_ref, keys, (wb_ref, wo_ref), sems)
        i = pl.program_id(0)
        on, _ = _head_norm(o_ref[...])
        a = jnp.concatenate(on, axis=1) * og_ref[...] * jax.nn.sigmoid(g_ref[...])
        halo = jnp.where(i == 0, 0.0, vh_ref[...])
        pooled = _pool_fwd(v_ref[...], halo, i * tm)
        _, p = _pool_project(pooled, pp_ref, ps_ref[...])
        ab, pb = a.astype(BF), p.astype(BF)
        sa = jax.nn.sigmoid(jnp.concatenate([ga0_ref[...], ga1_ref[...]], axis=1))
        sb = jax.nn.sigmoid(jnp.concatenate([gb0_ref[...], gb1_ref[...]], axis=1))
        z = sa * _dot_nt(ab, wb_ref[:, 0:HGRN_W]) + sb * _dot_nt(pb, wb_ref[:, HGRN_W:HGRN_W + POOL_W])
        r = _dot(z.astype(BF), wo_ref[...])
        hn_ref[...] = h_ref[...] + _rms_fwd(r, gp_ref[...])
        r_ref[...] = r
        a_ref[...] = ab
        p_ref[...] = pb

    rows = lambda width: pl.BlockSpec((tm, width), lambda i: (i, 0))
    return pl.pallas_call(
        body, name=name,
        out_shape=(jax.ShapeDtypeStruct((t_rows, D_MODEL), F32), jax.ShapeDtypeStruct((t_rows, D_MODEL), F32),
                   jax.ShapeDtypeStruct((t_rows, HGRN_W), BF), jax.ShapeDtypeStruct((t_rows, POOL_W), BF)),
        grid=(t_rows // tm,),
        in_specs=[_cols(tm, 512, 3), _cols(tm, 512, 4),
                  pl.BlockSpec((HALO, 512), lambda i: (jnp.maximum(i * hpt - 1, 0), 4)),
                  _cols(tm, 512, 5), _cols(tm, 512, 6), _cols(tm, 512, 7), _cols(tm, 512, 8),
                  rows(HGRN_W), rows(D_MODEL), _whole((1, HGRN_W)), _whole((4, HEAD_DIM, HEAD_DIM)),
                  _whole((1, POOL_W)), HBM_SPEC, _whole((1, D_MODEL))],
        out_specs=(rows(D_MODEL), rows(D_MODEL), rows(HGRN_W), rows(POOL_W)),
        scratch_shapes=_weight_scratch(keys),
        compiler_params=_params(("arbitrary",)),
    )(proj, proj, proj, proj, proj, proj, proj, o, h, out_gain, pool_proj, pool_scale, slabs, gain_post)


def _conv_fwd(g, halo, cw):
    ext = jnp.concatenate([halo, g], axis=0)
    return (cw[0:1] * pltpu.roll(ext, 2, 0)[8:] + cw[1:2] * pltpu.roll(ext, 1, 0)[8:] + cw[2:3] * g)


def _ffn_down(gu, h, conv_w, conv_b, slabs, gain_post, name):
    t_rows = h.shape[0]
    tm = _tile(t_rows, FFN_TILE)
    hpt = tm // 8

    def body(g_ref, gh_ref, up_ref, h_ref, cw_ref, cb_ref, slabs_ref, gp_ref, hn_ref, y_ref, wd_ref, sems):
        _fetch_weights(slabs_ref, ['wd'], (wd_ref,), sems)
        i = pl.program_id(0)
        halo = jnp.where(i == 0, 0.0, gh_ref[...])
        gt = _conv_fwd(g_ref[...], halo, cw_ref[...]) + cb_ref[...]
        act = _gelu(gt)[0] * up_ref[...]
        y = _dot(act.astype(BF), wd_ref[...])
        hn_ref[...] = h_ref[...] + _rms_fwd(y, gp_ref[...])
        y_ref[...] = y

    rows = lambda width: pl.BlockSpec((tm, width), lambda i: (i, 0))
    return pl.pallas_call(
        body, name=name,
        out_shape=(jax.ShapeDtypeStruct((t_rows, D_MODEL), F32), jax.ShapeDtypeStruct((t_rows, D_MODEL), F32)),
        grid=(t_rows // tm,),
        in_specs=[_cols(tm, D_FF, 0), pl.BlockSpec((8, D_FF), lambda i: (jnp.maximum(i * hpt - 1, 0), 0)),
                  _cols(tm, D_FF, 1), rows(D_MODEL), _whole((3, D_FF)), _whole((1, D_FF)),
                  HBM_SPEC, _whole((1, D_MODEL))],
        out_specs=(rows(D_MODEL), rows(D_MODEL)),
        scratch_shapes=_weight_scratch(['wd']),
        compiler_params=_params(("arbitrary",)),
    )(gu, gu, gu, h, conv_w, conv_b, slabs, gain_post)


def _loss_grad(h, target, name):
    t_rows, d = h.shape
    tm = _tile(t_rows, ROW_TILE)

    def body(h_ref, t_ref, dh_ref, l_ref):
        i = pl.program_id(0)

        @pl.when(i == 0)
        def _():
            l_ref[...] = jnp.zeros_like(l_ref)

        row = i * tm + lax.broadcasted_iota(jnp.int32, (tm, 1), 0)
        err = jnp.where(row >= N_META, h_ref[...] - t_ref[...], 0.0)
        dh_ref[...] = err * (1.0 / d)
        l_ref[...] += jnp.sum(err * err, axis=0, keepdims=True) * (0.5 / d)

    rows = pl.BlockSpec((tm, d), lambda i: (i, 0))
    return pl.pallas_call(
        body, name=name,
        out_shape=(jax.ShapeDtypeStruct((t_rows, d), F32), jax.ShapeDtypeStruct((1, d), F32)),
        grid=(t_rows // tm,), in_specs=[rows, rows], out_specs=(rows, _whole((1, d))),
        compiler_params=_params(("arbitrary",)),
    )(h, target)


def _dw(lhs, rhs, name, gain=None):
    t_rows = lhs[0].shape[0]
    tm = _tile(t_rows, ROW_TILE)
    paired = len(rhs) > 1
    n_rows = lhs[0].shape[1] if paired else sum(x.shape[1] for x in lhs)
    n_cols = sum(x.shape[1] for x in rhs)
    last = t_rows // tm - 1

    def body(*refs):
        lhs_refs = refs[:len(lhs)]
        rhs_refs = refs[len(lhs):len(lhs) + len(rhs)]
        rest = refs[len(lhs) + len(rhs):]
        if gain is not None:
            g_ref, o_ref, acc = rest
            rv = [_rms_fwd(rhs_refs[0][...], g_ref[...]).astype(BF)]
        else:
            o_ref, acc = rest
            rv = [r[...] for r in rhs_refs]
        i = pl.program_id(0)

        @pl.when(i == 0)
        def _():
            acc[...] = jnp.zeros_like(acc)

        r0, c0 = 0, 0
        for p, l_ref in enumerate(lhs_refs):
            r = rv[p] if paired else rv[0]
            n = l_ref.shape[1]
            step = 512 if n % 512 == 0 else 256
            for s in range(0, n, step):
                acc[r0 + s:r0 + s + step, c0:c0 + r.shape[1]] += _dot_tn(l_ref[:, s:s + step], r)
            if paired:
                c0 += r.shape[1]
            else:
                r0 += n

        @pl.when(i == last)
        def _():
            pltpu.sync_copy(acc, o_ref)

    rows = lambda x: pl.BlockSpec((tm, x.shape[1]), lambda i: (i, 0))
    in_specs = [rows(x) for x in lhs] + [rows(x) for x in rhs]
    args = list(lhs) + list(rhs)
    if gain is not None:
        in_specs.append(_whole(gain.shape))
        args.append(gain)
    return pl.pallas_call(
        body, name=name, out_shape=jax.ShapeDtypeStruct((n_rows, n_cols), F32), grid=(t_rows // tm,),
        in_specs=in_specs, out_specs=HBM_SPEC,
        scratch_shapes=[pltpu.VMEM((n_rows, n_cols), F32)],
        compiler_params=_params(("arbitrary",)),
    )(*args)


def _ffn_bwd_down(dh, y, gain_post, gu, conv_w, conv_b, slabs, name):
    t_rows = dh.shape[0]
    tm = _tile(t_rows, FFN_TILE)
    hpt = tm // 8

    def body(dh_ref, y_ref, gp_ref, g_ref, gh_ref, up_ref, cw_ref, cb_ref, slabs_ref,
             dy_ref, act_ref, dgt_ref, dup_ref, dgp_ref, dcw_ref, dcb_ref, wd_ref, sems):
        _fetch_weights(slabs_ref, ['wd'], (wd_ref,), sems)
        i = pl.program_id(0)

        @pl.when(i == 0)
        def _():
            dgp_ref[...] = jnp.zeros_like(dgp_ref)
            dcw_ref[...] = jnp.zeros_like(dcw_ref)
            dcb_ref[...] = jnp.zeros_like(dcb_ref)

        dy, dgp = _rms_bwd(dh_ref[...], y_ref[...], gp_ref[...])
        dgp_ref[...] += dgp
        dyb = dy.astype(BF)
        dy_ref[...] = dyb
        g = g_ref[...]
        up = up_ref[...]
        ext = jnp.concatenate([jnp.where(i == 0, 0.0, gh_ref[...]), g], axis=0)
        g2 = pltpu.roll(ext, 2, 0)[8:]
        g1 = pltpu.roll(ext, 1, 0)[8:]
        cw = cw_ref[...]
        gt = cw[0:1] * g2 + cw[1:2] * g1 + cw[2:3] * g + cb_ref[...]
        gl, th = _gelu(gt)
        act_ref[...] = (gl * up).astype(BF)
        dact = _dot_nt(dyb, wd_ref[...])
        dup_ref[...] = dact * gl
        dgt = dact * up * _gelu_grad(gt, th)
        dgt_ref[...] = dgt
        dcb_ref[...] += jnp.sum(dgt, axis=0, keepdims=True)
        dcw_ref[...] += jnp.concatenate([jnp.sum(dgt * g2, axis=0, keepdims=True),
                                         jnp.sum(dgt * g1, axis=0, keepdims=True),
                                         jnp.sum(dgt * g, axis=0, keepdims=True)], axis=0)

    rows = lambda width: pl.BlockSpec((tm, width), lambda i: (i, 0))
    return pl.pallas_call(
        body, name=name,
        out_shape=(jax.ShapeDtypeStruct((t_rows, D_MODEL), BF), jax.ShapeDtypeStruct((t_rows, D_FF), BF),
                   jax.ShapeDtypeStruct((t_rows, D_FF), F32), jax.ShapeDtypeStruct((t_rows, D_FF), F32),
                   jax.ShapeDtypeStruct((1, D_MODEL), F32), jax.ShapeDtypeStruct((3, D_FF), F32),
                   jax.ShapeDtypeStruct((1, D_FF), F32)),
        grid=(t_rows // tm,),
        in_specs=[rows(D_MODEL), rows(D_MODEL), _whole((1, D_MODEL)), _cols(tm, D_FF, 0),
                  pl.BlockSpec((8, D_FF), lambda i: (jnp.maximum(i * hpt - 1, 0), 0)), _cols(tm, D_FF, 1),
                  _whole((3, D_FF)), _whole((1, D_FF)), HBM_SPEC],
        out_specs=(rows(D_MODEL), rows(D_FF), rows(D_FF), rows(D_FF),
                   _whole((1, D_MODEL)), _whole((3, D_FF)), _whole((1, D_FF))),
        scratch_shapes=_weight_scratch(['wd']),
        compiler_params=_params(("arbitrary",)),
    )(dh, y, gain_post, gu, gu, gu, conv_w, conv_b, slabs)


def _ffn_bwd_up(dgt, dup, conv_w, slabs, h, gain_pre, dh_out, name):
    t_rows = h.shape[0]
    tm = _tile(t_rows, FFN_TILE)
    hpt = tm // 8
    last = t_rows // tm - 1

    def body(dgt_ref, dgn_ref, dup_ref, cw_ref, slabs_ref, h_ref, gp_ref, dho_ref, dh_ref, dgu_ref, dg_ref,
             wg_ref, wu_ref, sems):
        _fetch_weights(slabs_ref, ['wg', 'wu'], (wg_ref, wu_ref), sems)
        i = pl.program_id(0)

        @pl.when(i == 0)
        def _():
            dg_ref[...] = jnp.zeros_like(dg_ref)

        dgt = dgt_ref[...]
        ext = jnp.concatenate([dgt, jnp.where(i == last, 0.0, dgn_ref[...])], axis=0)
        n_ext = tm + 8
        cw = cw_ref[...]
        dg = (cw[2:3] * dgt + cw[1:2] * pltpu.roll(ext, n_ext - 1, 0)[:tm]
              + cw[0:1] * pltpu.roll(ext, n_ext - 2, 0)[:tm])
        dgb = dg.astype(BF)
        dub = dup_ref[...].astype(BF)
        dgu_ref[:, 0:D_FF] = dgb
        dgu_ref[:, D_FF:2 * D_FF] = dub
        du = _dot(dgb, wg_ref[...]) + _dot(dub, wu_ref[...])
        dx, dgain = _rms_bwd(du, h_ref[...], gp_ref[...])
        dh_ref[...] = dho_ref[...] + dx
        dg_ref[...] += dgain

    rows = lambda width: pl.BlockSpec((tm, width), lambda i: (i, 0))
    return pl.pallas_call(
        body, name=name,
        out_shape=(jax.ShapeDtypeStruct((t_rows, D_MODEL), F32), jax.ShapeDtypeStruct((t_rows, 2 * D_FF), BF),
                   jax.ShapeDtypeStruct((1, D_MODEL), F32)),
        grid=(t_rows // tm,),
        in_specs=[rows(D_FF), pl.BlockSpec((8, D_FF), lambda i: (jnp.minimum((i + 1) * hpt, (last + 1) * hpt - 1), 0)),
                  rows(D_FF), _whole((3, D_FF)), HBM_SPEC, rows(D_MODEL), _whole((1, D_MODEL)),
                  rows(D_MODEL)],
        out_specs=(rows(D_MODEL), rows(2 * D_FF), _whole((1, D_MODEL))),
        scratch_shapes=_weight_scratch(['wg', 'wu']),
        compiler_params=_params(("arbitrary",)),
    )(dgt, dgt, dup, conv_w, slabs, h, gain_pre, dh_out)


def _mix_bwd_a(dh, r, gain_post, proj, a, p, slabs, name):
    t_rows = dh.shape[0]
    tm = _tile(t_rows, ROW_TILE)

    def body(dh_ref, r_ref, gp_ref, ga0_ref, ga1_ref, gb0_ref, gb1_ref, a_ref, p_ref, slabs_ref,
             dr_ref, z_ref, dya_ref, dyp_ref, da_ref, dp_ref, dgab_ref, dg_ref, wb_ref, wo_ref, sems):
        _fetch_weights(slabs_ref, ['wb', 'w_out'], (wb_ref, wo_ref), sems)

        @pl.when(pl.program_id(0) == 0)
        def _():
            dg_ref[...] = jnp.zeros_like(dg_ref)

        dr, dgain = _rms_bwd(dh_ref[...], r_ref[...], gp_ref[...])
        dg_ref[...] += dgain
        drb = dr.astype(BF)
        dr_ref[...] = drb
        dz = _dot_nt(drb, wo_ref[...])
        ya = _dot_nt(a_ref[...], wb_ref[:, 0:HGRN_W])
        yp = _dot_nt(p_ref[...], wb_ref[:, HGRN_W:HGRN_W + POOL_W])
        sa = jax.nn.sigmoid(jnp.concatenate([ga0_ref[...], ga1_ref[...]], axis=1))
        sb = jax.nn.sigmoid(jnp.concatenate([gb0_ref[...], gb1_ref[...]], axis=1))
        z_ref[...] = (sa * ya + sb * yp).astype(BF)
        dgab_ref[:, 0:D_MODEL] = (dz * ya * sa * (1.0 - sa)).astype(BF)
        dgab_ref[:, D_MODEL:2 * D_MODEL] = (dz * yp * sb * (1.0 - sb)).astype(BF)
        dya = (dz * sa).astype(BF)
        dyp = (dz * sb).astype(BF)
        dya_ref[...] = dya
        dyp_ref[...] = dyp
        da_ref[...] = _dot(dya, wb_ref[:, 0:HGRN_W])
        dp_ref[...] = _dot(dyp, wb_ref[:, HGRN_W:HGRN_W + POOL_W])

    rows = lambda width: pl.BlockSpec((tm, width), lambda i: (i, 0))
    bf = lambda width: jax.ShapeDtypeStruct((t_rows, width), BF)
    f32 = lambda width: jax.ShapeDtypeStruct((t_rows, width), F32)
    return pl.pallas_call(
        body, name=name,
        out_shape=(bf(D_MODEL), bf(D_MODEL), bf(D_MODEL), bf(D_MODEL), f32(HGRN_W), f32(POOL_W), bf(2 * D_MODEL),
                   jax.ShapeDtypeStruct((1, D_MODEL), F32)),
        grid=(t_rows // tm,),
        in_specs=[rows(D_MODEL), rows(D_MODEL), _whole((1, D_MODEL)),
                  _cols(tm, 512, 5), _cols(tm, 512, 6), _cols(tm, 512, 7), _cols(tm, 512, 8),
                  rows(HGRN_W), rows(POOL_W), HBM_SPEC],
        out_specs=(rows(D_MODEL), rows(D_MODEL), rows(D_MODEL), rows(D_MODEL), rows(HGRN_W), rows(POOL_W),
                   rows(2 * D_MODEL), _whole((1, D_MODEL))),
        scratch_shapes=_weight_scratch(['wb', 'w_out']),
        compiler_params=_params(("arbitrary",)),
    )(dh, r, gain_post, proj, proj, proj, proj, a, p, slabs)


def _mix_bwd_b(da, dp, proj, o, out_gain, pool_proj, pool_scale, name):
    t_rows = da.shape[0]
    tm = _tile(t_rows, ROW_TILE)
    hpt = tm // HALO
    last = t_rows // tm - 1

    def body(da_ref, dp_ref, dpn_ref, g_ref, v_ref, vh_ref, o_ref, og_ref, pp_ref, ps_ref,
             do_ref, dgv_ref, dog_ref, dpp_ref, dps_ref):
        i = pl.program_id(0)

        @pl.when(i == 0)
        def _():
            dog_ref[...] = jnp.zeros_like(dog_ref)
            dpp_ref[...] = jnp.zeros_like(dpp_ref)
            dps_ref[...] = jnp.zeros_like(dps_ref)

        da = da_ref[...]
        og = og_ref[...]
        o = o_ref[...]
        on, rs = _head_norm(o)
        onc = jnp.concatenate(on, axis=1)
        sg = jax.nn.sigmoid(g_ref[...])
        dog_ref[...] += jnp.sum(da * onc * sg, axis=0, keepdims=True)
        dgv_ref[:, 0:HGRN_W] = (da * onc * og * sg * (1.0 - sg)).astype(BF)
        don = da * og * sg
        for hh in range(HEADS):
            cs = slice(hh * HEAD_DIM, (hh + 1) * HEAD_DIM)
            d = don[:, cs]
            do_ref[:, cs] = rs[hh] * (d - on[hh] * jnp.mean(d * on[hh], axis=-1, keepdims=True))

        scale = ps_ref[...]
        halo = jnp.where(i == 0, 0.0, vh_ref[...])
        pooled = _pool_fwd(v_ref[...], halo, i * tm)
        y, _ = _pool_project(pooled, pp_ref, scale)
        dp = dp_ref[...]
        dps_ref[...] += jnp.sum(dp * y, axis=0, keepdims=True)
        dy = dp * scale
        dyn = jnp.where(i == last, 0.0, dpn_ref[...]) * scale
        dpooled, dhalo = [], []
        for gi in range(4):
            cs = slice(gi * HEAD_DIM, (gi + 1) * HEAD_DIM)
            ppb = pp_ref[gi].astype(BF)
            dyb = dy[:, cs].astype(BF)
            dpooled.append(_dot_nt(dyb, ppb))
            dhalo.append(_dot_nt(dyn[:, cs].astype(BF), ppb))
            dpp_ref[gi] += _dot_tn(pooled[gi].astype(BF), dyb)
        dv = _pool_bwd(dpooled, dhalo, i * tm)
        dgv_ref[:, HGRN_W:HGRN_W + POOL_W] = jnp.concatenate(dv, axis=1).astype(BF)

    rows = lambda width: pl.BlockSpec((tm, width), lambda i: (i, 0))
    return pl.pallas_call(
        body, name=name,
        out_shape=(jax.ShapeDtypeStruct((t_rows, HGRN_W), F32), jax.ShapeDtypeStruct((t_rows, HGRN_W + POOL_W), BF),
                   jax.ShapeDtypeStruct((1, HGRN_W), F32), jax.ShapeDtypeStruct((4, HEAD_DIM, HEAD_DIM), F32),
                   jax.ShapeDtypeStruct((1, POOL_W), F32)),
        grid=(t_rows // tm,),
        in_specs=[rows(HGRN_W), rows(POOL_W),
                  pl.BlockSpec((HALO, POOL_W), lambda i: (jnp.minimum((i + 1) * hpt, (last + 1) * hpt - 1), 0)),
                  _cols(tm, 512, 3), _cols(tm, 512, 4),
                  pl.BlockSpec((HALO, 512), lambda i: (jnp.maximum(i * hpt - 1, 0), 4)),
                  rows(HGRN_W), _whole((1, HGRN_W)), _whole((4, HEAD_DIM, HEAD_DIM)), _whole((1, POOL_W))],
        out_specs=(rows(HGRN_W), rows(HGRN_W + POOL_W), _whole((1, HGRN_W)), _whole((4, HEAD_DIM, HEAD_DIM)),
                   _whole((1, POOL_W))),
        compiler_params=_params(("arbitrary",)),
    )(da, dp, dp, proj, proj, proj, o, out_gain, pool_proj, pool_scale)


def _hgrn_bwd(proj, lower_bounds, layer, states, do, name):
    t_rows = proj.shape[0]
    tm = _tile(t_rows, ROW_TILE, CHUNK)
    nct = tm // CHUNK
    n_tiles = t_rows // tm

    def body(q_ref, f_ref, i_ref, lb_ref, sall_ref, do_ref, dqfi_ref, dlb_ref, dst_ref):
        @pl.when(pl.program_id(0) == 0)
        def _():
            dst_ref[...] = jnp.zeros_like(dst_ref)
            dlb_ref[...] = jnp.zeros_like(dlb_ref)

        lbs = _layer_lower_bound(lb_ref, layer)
        tri_lo, tri_up, diag, off = _chunk_masks()
        is_last_row = lax.broadcasted_iota(jnp.int32, (CHUNK, 1), 0) == CHUNK - 1

        def chunk(ci, carry):
            c = nct - 1 - ci
            r0 = pl.multiple_of(c * CHUNK, SUB)
            for hh in range(HEADS):
                cs = slice(hh * HEAD_DIM, (hh + 1) * HEAD_DIM)
                lb = lbs[:, cs]
                q = q_ref[pl.ds(r0, CHUNK), cs]
                v = i_ref[pl.ds(r0, CHUNK), cs]
                sg, s2, f, lf, k = _chunk_gates(f_ref[pl.ds(r0, CHUNK), cs], lb)
                b = _tri_mm(tri_lo, lf)
                e_qs, e_qm, e_km, e_ks, qs, qm, km, ks = _chunk_factors(q, k, b)
                a = _chunk_scores(qs, qm, km, ks, diag, off)
                st = sall_ref[c, hh]
                dst = dst_ref[hh]
                stb, dstb = st.astype(BF), dst.astype(BF)
                vb = v.astype(BF)
                dob = do_ref[pl.ds(r0, CHUNK), cs].astype(BF)
                b_last = b[CHUNK - 1:CHUNK]
                e_b = jnp.exp(b)
                e_last = jnp.exp(b_last)
                e_kl = jnp.exp(b_last - b)
                qc = (q * e_b).astype(BF)
                kdec = (k * e_kl).astype(BF)

                da_full = _dot_nt(dob, vb)
                dv = _dot_tn(a.astype(BF), dob) + _dot_nt(kdec, dstb)
                dq = e_b * _dot(dob, stb)
                da_d = jnp.where(diag, da_full, 0.0).astype(BF)
                dq = dq + e_qm * _dot(da_d, km)
                dk = e_km * _dot_tn(da_d, qm) + e_kl * _dot(vb, dstb)
                for m, kk, ek in zip(off, ks, e_ks):
                    da_i = jnp.where(m, da_full, 0.0).astype(BF)
                    dq = dq + e_qs * _dot(da_i, kk)
                    dk = dk + ek * _dot_tn(da_i, qs)

                st_new = st * e_last + _dot_tn(vb, kdec)
                db = q * dq - k * dk
                db = db + jnp.where(is_last_row, jnp.sum(st_new * dst, axis=0, keepdims=True), 0.0)
                dlf = _tri_mm(tri_up, db)
                df = jnp.where(f > LOG_FLOOR, dlf / f, 0.0)
                dfl = df * (1.0 - lb) * sg * (1.0 - sg) - dk * (1.0 - lb) * s2 * (1.0 - s2)
                dlb_ref[:, cs] += jnp.sum(df * (1.0 - sg) - dk * s2, axis=0, keepdims=True)
                dst_ref[hh] = dst * e_last + _dot_tn(dob, qc)

                dqfi_ref[pl.ds(r0, CHUNK), cs] = dq.astype(BF)
                dqfi_ref[pl.ds(r0, CHUNK), pl.ds(HGRN_W + hh * HEAD_DIM, HEAD_DIM)] = dfl.astype(BF)
                dqfi_ref[pl.ds(r0, CHUNK), pl.ds(2 * HGRN_W + hh * HEAD_DIM, HEAD_DIM)] = dv.astype(BF)
            return carry

        lax.fori_loop(0, nct, chunk, 0)

    rev = lambda width, j: pl.BlockSpec((tm, width), lambda i, j=j: (n_tiles - 1 - i, j))
    return pl.pallas_call(
        body, name=name,
        out_shape=(jax.ShapeDtypeStruct((t_rows, 3 * HGRN_W), BF), jax.ShapeDtypeStruct((1, HGRN_W), F32)),
        grid=(n_tiles,),
        in_specs=[rev(512, 0), rev(512, 1), rev(512, 2), _whole((DEPTH, HGRN_W)),
                  pl.BlockSpec((nct, HEADS, HEAD_DIM, HEAD_DIM), lambda i: (n_tiles - 1 - i, 0, 0, 0)),
                  rev(HGRN_W, 0)],
        out_specs=(rev(3 * HGRN_W, 0), _whole((1, HGRN_W))),
        scratch_shapes=[pltpu.VMEM((HEADS, HEAD_DIM, HEAD_DIM), F32)],
        compiler_params=_params(("arbitrary",)),
    )(proj, proj, proj, lower_bounds, states, do)


def _in_bwd(dqfi, dgv, dgab, slabs, h, gain_pre, dh_out, name):
    t_rows = h.shape[0]
    tm = _tile(t_rows, ROW_TILE)
    c1 = 3 * HGRN_W
    c2 = c1 + HGRN_W + POOL_W

    def body(d1_ref, d2_ref, d3_ref, slabs_ref, h_ref, gp_ref, dho_ref, dh_ref, dg_ref, w_ref, sems):
        _fetch_weights(slabs_ref, ['w_in'], (w_ref,), sems)

        @pl.when(pl.program_id(0) == 0)
        def _():
            dg_ref[...] = jnp.zeros_like(dg_ref)

        du = (_dot(d1_ref[...], w_ref[0:c1, :]) + _dot(d2_ref[...], w_ref[c1:c2, :])
              + _dot(d3_ref[...], w_ref[c2:IN_COLS, :]))
        dx, dgain = _rms_bwd(du, h_ref[...], gp_ref[...])
        dh_ref[...] = dho_ref[...] + dx
        dg_ref[...] += dgain

    rows = lambda width: pl.BlockSpec((tm, width), lambda i: (i, 0))
    return pl.pallas_call(
        body, name=name,
        out_shape=(jax.ShapeDtypeStruct((t_rows, D_MODEL), F32), jax.ShapeDtypeStruct((1, D_MODEL), F32)),
        grid=(t_rows // tm,),
        in_specs=[rows(c1), rows(c2 - c1), rows(IN_COLS - c2), HBM_SPEC, rows(D_MODEL),
                  _whole((1, D_MODEL)), rows(D_MODEL)],
        out_specs=(rows(D_MODEL), _whole((1, D_MODEL))),
        scratch_shapes=_weight_scratch(['w_in']),
        compiler_params=_params(("arbitrary",)),
    )(dqfi, dgv, dgab, slabs, h, gain_pre, dh_out)


def _lower_bound_grad(lower_bounds, dlbs, name):
    def body(lb_ref, d_ref, o_ref):
        g0, g1 = _softmax2(lb_ref)
        bound = (g0 + g1) - g0
        inside = (bound > 0.0) & (bound < 1.0)
        dg1 = jnp.where(inside, d_ref[1:2, :], 0.0)
        inner = g1 * dg1
        o_ref[0:1, :] = g0 * (0.0 - inner)
        o_ref[1:2, :] = g1 * (dg1 - inner)

    return pl.pallas_call(body, name=name, out_shape=jax.ShapeDtypeStruct(lower_bounds.shape, F32))(lower_bounds, dlbs)


def _adamw(w, g, m, v, name):
    r, c = w.shape
    tr = r if (r % 8 or r <= 512) else _tile(r, 512, 8)
    c1 = 1.0 - ADAM_B1 ** ADAM_STEP
    c2 = 1.0 - ADAM_B2 ** ADAM_STEP

    def body(w_ref, g_ref, m_ref, v_ref, d_ref, nm_ref, nv_ref):
        gg = g_ref[...]
        nm = ADAM_B1 * m_ref[...] + (1.0 - ADAM_B1) * gg
        nv = ADAM_B2 * v_ref[...] + (1.0 - ADAM_B2) * (gg * gg)
        d_ref[...] = -ADAM_LR * ((nm / c1) / (jnp.sqrt(nv / c2) + ADAM_EPS) + ADAM_WD * w_ref[...])
        nm_ref[...] = nm
        nv_ref[...] = nv

    blk = pl.BlockSpec((tr, c), lambda i: (i, 0))
    shp = jax.ShapeDtypeStruct((r, c), F32)
    return pl.pallas_call(
        body, name=name, out_shape=(shp, shp, shp), grid=(r // tr,),
        in_specs=[blk, blk, blk, blk], out_specs=(blk, blk, blk),
        compiler_params=_params(("parallel",)),
    )(w, g, m, v)


def _as2d(a):
    return a.reshape(-1, a.shape[-1])


def _layer_slab(w, l):
    t = lambda a: jnp.swapaxes(a, 0, 1)
    parts = [t(w['w_in'][l]), jnp.concatenate([t(w['w_branch_hgrn'][l]), t(w['w_branch_pool'][l])], axis=1),
             w['w_out'][l], t(w['ffn_w_gate'][l]), t(w['ffn_w_up'][l]), w['ffn_w_down'][l]]
    return jnp.concatenate(parts, axis=0).astype(BF)


def _slab_grads(gs):
    t = lambda a: jnp.swapaxes(a, 0, 1)
    cut = lambda key: gs[W_ROWS[key][0]:W_ROWS[key][0] + W_ROWS[key][1]]
    wb = cut('wb')
    return {'w_in': t(cut('w_in')), 'w_branch_hgrn': t(wb[:, :HGRN_W]), 'w_branch_pool': t(wb[:, HGRN_W:]),
            'w_out': cut('w_out'), 'ffn_w_gate': t(cut('wg')), 'ffn_w_up': t(cut('wu')), 'ffn_w_down': cut('wd')}


def _train_step(x, target, w, m, v):
    slabs = [_all_gather(_layer_slab(w, l), f"gather_weights_{l}") for l in range(DEPTH)]
    f32_shapes = [w[n].shape for n in F32_GATHERED]
    gathered32 = _all_gather(_pack([w[n] for n in F32_GATHERED], F32, 8), "gather_meta_conv")
    full = {n: _blocks_to_full(blk, SHARD_AXIS[n]) for n, blk in zip(F32_GATHERED, _unpack(gathered32, f32_shapes))}
    row = lambda name, l: w[name][l][None]

    h = jnp.concatenate([full['meta_tokens'], x], axis=0)
    saved = []
    for l in range(DEPTH):
        s = {'h_in': h}
        s['proj'] = _norm_matmul(h, row('mix_norm_pre', l), slabs[l], ['w_in'], f"in_proj_{l}")
        s['o'], s['states'] = _hgrn_fwd(s['proj'], w['hgrn_lower_bounds'], l, f"hgrn_fwd_{l}")
        h, s['r'], s['a'], s['p'] = _mix_merge(
            s['proj'], s['o'], h, row('hgrn_out_norm', l), w['pool_proj'][l], row('pool_scale', l), slabs[l],
            row('mix_norm_post', l), f"mix_merge_{l}")
        s['h_mid'] = h
        s['gu'] = _norm_matmul(h, row('ffn_norm_pre', l), slabs[l], ['wg', 'wu'], f"ffn_proj_{l}")
        h, s['y'] = _ffn_down(s['gu'], h, full['ffn_conv_w'][l], row('ffn_conv_b', l), slabs[l],
                              row('ffn_norm_post', l), f"ffn_down_{l}")
        saved.append(s)

    dh, loss_cols = _loss_grad(h, jnp.pad(target, ((N_META, 0), (0, 0))), "loss_grad")
    loss = jnp.sum(loss_cols)

    small = {n: [None] * DEPTH for n in REPLICATED + ['ffn_conv_w']}
    received = [None] * DEPTH
    for l in reversed(range(DEPTH)):
        s = saved[l]
        dy, act, dgt, dup, small['ffn_norm_post'][l], small['ffn_conv_w'][l], small['ffn_conv_b'][l] = _ffn_bwd_down(
            dh, s['y'], row('ffn_norm_post', l), s['gu'], full['ffn_conv_w'][l], row('ffn_conv_b', l), slabs[l],
            f"ffn_bwd_down_{l}")
        d_down = _dw([act], [dy], f"dw_down_{l}")
        dh, dgu, small['ffn_norm_pre'][l] = _ffn_bwd_up(
            dgt, dup, full['ffn_conv_w'][l], slabs[l], s['h_mid'], row('ffn_norm_pre', l), dh, f"ffn_bwd_up_{l}")
        d_gu = _dw([dgu], [s['h_mid']], f"dw_gate_up_{l}", gain=row('ffn_norm_pre', l))

        dr, z, dya, dyp, da, dp, dgab, small['mix_norm_post'][l] = _mix_bwd_a(
            dh, s['r'], row('mix_norm_post', l), s['proj'], s['a'], s['p'], slabs[l], f"mix_bwd_a_{l}")
        d_out = _dw([z], [dr], f"dw_out_{l}")
        d_b = _dw([dya, dyp], [s['a'], s['p']], f"dw_branch_{l}")
        do, dgv, small['hgrn_out_norm'][l], small['pool_proj'][l], small['pool_scale'][l] = _mix_bwd_b(
            da, dp, s['proj'], s['o'], row('hgrn_out_norm', l), w['pool_proj'][l], row('pool_scale', l),
            f"mix_bwd_b_{l}")
        dqfi, small['hgrn_lower_bounds'][l] = _hgrn_bwd(s['proj'], w['hgrn_lower_bounds'], l, s['states'], do,
                                                        f"hgrn_bwd_{l}")
        dh, small['mix_norm_pre'][l] = _in_bwd(dqfi, dgv, dgab, slabs[l], s['h_in'], row('mix_norm_pre', l), dh,
                                               f"in_bwd_{l}")
        d_in = _dw([dqfi, dgv, dgab], [s['h_in']], f"dw_in_{l}", gain=row('mix_norm_pre', l))
        received[l] = _exchange(d_in, d_b, d_out, d_gu, d_down, f"exchange_grads_{l}")

    grad_x = dh[N_META:]

    per_layer = [_slab_grads(_sum8(received[l], f"sum_grads_{l}")) for l in range(DEPTH)]
    grads = {n: jnp.stack([per_layer[l][n] for l in range(DEPTH)]) for n in per_layer[0]}

    stack = lambda n: jnp.stack(small[n]) if small[n][0].shape[0] != 1 else jnp.concatenate(small[n], axis=0)
    names = list(small) + ['meta_tokens']
    partial = {n: stack(n) for n in small}
    partial['meta_tokens'] = dh[:N_META]
    gathered = _all_gather(_pack([partial[n] for n in names], F32, 8), "gather_small_grads")
    rep = dict(zip(names, _unpack(_sum8(gathered, "sum_small_grads"), [partial[n].shape for n in names])))
    rep['hgrn_lower_bounds'] = _lower_bound_grad(w['hgrn_lower_bounds'], rep['hgrn_lower_bounds'], "lower_bound_grad")
    me = 4 * lax.axis_index("x") + 2 * lax.axis_index("y") + lax.axis_index("c")
    for n in F32_GATHERED:
        width = w[n].shape[SHARD_AXIS[n]]
        rep[n] = lax.dynamic_slice_in_dim(rep[n], me * width, width, axis=SHARD_AXIS[n])
    grads.update(rep)

    delta, new_m, new_v = {}, {}, {}
    for n in WEIGHT_NAMES:
        shape = w[n].shape
        d2, m2, v2 = _adamw(_as2d(w[n]), _as2d(grads[n]), _as2d(m[n]), _as2d(v[n]), f"adamw_{n}")
        delta[n], new_m[n], new_v[n] = d2.reshape(shape), m2.reshape(shape), v2.reshape(shape)
    return loss, grad_x, grads, delta, new_m, new_v


def kernel(x, meta_tokens, mix_norm_pre, mix_norm_post, w_in, hgrn_lower_bounds, hgrn_out_norm, w_branch_hgrn, pool_proj, pool_scale, w_branch_pool, w_out, ffn_norm_pre, ffn_norm_post, ffn_w_gate, ffn_w_up, ffn_conv_w, ffn_conv_b, ffn_w_down, loss_target, m_meta_tokens, m_mix_norm_pre, m_mix_norm_post, m_w_in, m_hgrn_lower_bounds, m_hgrn_out_norm, m_w_branch_hgrn, m_pool_proj, m_pool_scale, m_w_branch_pool, m_w_out, m_ffn_norm_pre, m_ffn_norm_post, m_ffn_w_gate, m_ffn_w_up, m_ffn_conv_w, m_ffn_conv_b, m_ffn_w_down, v_meta_tokens, v_mix_norm_pre, v_mix_norm_post, v_w_in, v_hgrn_lower_bounds, v_hgrn_out_norm, v_w_branch_hgrn, v_pool_proj, v_pool_scale, v_w_branch_pool, v_w_out, v_ffn_norm_pre, v_ffn_norm_post, v_ffn_w_gate, v_ffn_w_up, v_ffn_conv_w, v_ffn_conv_b, v_ffn_w_down):
    w = dict(zip(WEIGHT_NAMES, (meta_tokens, mix_norm_pre, mix_norm_post, w_in, hgrn_lower_bounds, hgrn_out_norm,
                                w_branch_hgrn, pool_proj, pool_scale, w_branch_pool, w_out, ffn_norm_pre,
                                ffn_norm_post, ffn_w_gate, ffn_w_up, ffn_conv_w, ffn_conv_b, ffn_w_down)))
    m = dict(zip(WEIGHT_NAMES, (m_meta_tokens, m_mix_norm_pre, m_mix_norm_post, m_w_in, m_hgrn_lower_bounds,
                                m_hgrn_out_norm, m_w_branch_hgrn, m_pool_proj, m_pool_scale, m_w_branch_pool, m_w_out,
                                m_ffn_norm_pre, m_ffn_norm_post, m_ffn_w_gate, m_ffn_w_up, m_ffn_conv_w,
                                m_ffn_conv_b, m_ffn_w_down)))
    v = dict(zip(WEIGHT_NAMES, (v_meta_tokens, v_mix_norm_pre, v_mix_norm_post, v_w_in, v_hgrn_lower_bounds,
                                v_hgrn_out_norm, v_w_branch_hgrn, v_pool_proj, v_pool_scale, v_w_branch_pool, v_w_out,
                                v_ffn_norm_pre, v_ffn_norm_post, v_ffn_w_gate, v_ffn_w_up, v_ffn_conv_w,
                                v_ffn_conv_b, v_ffn_w_down)))
    loss_local, grad_x, grads, delta, new_m, new_v = _train_step(x[0], loss_target[0], w, m, v)
    loss = lax.psum(loss_local, ("x", "y", "c"))
    return (loss, grad_x[None], *[grads[n] for n in WEIGHT_NAMES], *[delta[n] for n in WEIGHT_NAMES],
            *[new_m[n] for n in WEIGHT_NAMES], *[new_v[n] for n in WEIGHT_NAMES])
```

```python
import jax
import jax.numpy as jnp
from jax import lax
from jax.experimental import pallas as pl
from jax.experimental.pallas import tpu as pltpu

F32 = jnp.float32
BF = jnp.bfloat16

D_MODEL = 1024
N_META = 16
DEPTH = 2
HEADS = 4
HEAD_DIM = 128
HGRN_W = 512
POOL_W = 512
POOL_WINDOWS = (2, 4, 8, 16)
D_FF = 2816
IN_COLS = 4608
EPS = 1e-6
LOG_FLOOR = 1e-30
N_DEV = 8

ADAM_LR = 0.001
ADAM_B1 = 0.9
ADAM_B2 = 0.999
ADAM_EPS = 1e-08
ADAM_WD = 0.01
ADAM_STEP = 10

SUB = 16
CHUNK = 48
HGRN_UNROLL = 3
EXP_CLAMP = 80.0
ROW_TILE = 432
FFN_TILE = 144
HALO = 16
VMEM_LIMIT = 56 * 1024 * 1024
MESH = pl.DeviceIdType.MESH

WEIGHT_NAMES = ['meta_tokens', 'mix_norm_pre', 'mix_norm_post', 'w_in', 'hgrn_lower_bounds', 'hgrn_out_norm',
                'w_branch_hgrn', 'pool_proj', 'pool_scale', 'w_branch_pool', 'w_out', 'ffn_norm_pre', 'ffn_norm_post',
                'ffn_w_gate', 'ffn_w_up', 'ffn_conv_w', 'ffn_conv_b', 'ffn_w_down']
SHARD_AXIS = {'meta_tokens': 1, 'w_in': 2, 'w_branch_hgrn': 2, 'w_branch_pool': 2, 'w_out': 1,
              'ffn_w_gate': 2, 'ffn_w_up': 2, 'ffn_conv_w': 2, 'ffn_w_down': 1}
F32_GATHERED = ['meta_tokens', 'ffn_conv_w']
REPLICATED = [n for n in WEIGHT_NAMES if n not in SHARD_AXIS]

W_ROWS = {'w_in': (0, 576),
          'wb': (576, 128),
          'w_out': (704, 128),
          'wg': (832, 352),
          'wu': (1184, 352),
          'wd': (1536, 352)}
SLAB_ROWS = 1888
REST_KEYS = ['wb', 'w_out', 'wg', 'wu', 'wd']


def _params(sem=None):
    return pltpu.CompilerParams(dimension_semantics=sem, vmem_limit_bytes=VMEM_LIMIT)


def _tile(total, pref, mult=16):
    best = None
    for t in range(mult, min(total, pref) + 1, mult):
        if total % t == 0:
            best = t
    assert best is not None, (total, pref, mult)
    return best


def _whole(shape):
    return pl.BlockSpec(shape, lambda *_: (0,) * len(shape))


def _cols(tm, width, j):
    return pl.BlockSpec((tm, width), lambda i, j=j: (i, j))


def _dot(a, b):
    return jnp.dot(a, b, preferred_element_type=F32)


def _dot_nt(a, b):
    return lax.dot_general(a, b, (((1,), (1,)), ((), ())), preferred_element_type=F32)


def _dot_tn(a, b):
    return lax.dot_general(a, b, (((0,), (0,)), ((), ())), preferred_element_type=F32)


def _rms_fwd(x, g):
    r = lax.rsqrt(jnp.mean(x * x, axis=-1, keepdims=True) + EPS)
    return x * r * g


def _rms_bwd(dy, x, g):
    r = lax.rsqrt(jnp.mean(x * x, axis=-1, keepdims=True) + EPS)
    xh = x * r
    dyg = dy * g
    dx = r * (dyg - xh * jnp.mean(dyg * xh, axis=-1, keepdims=True))
    return dx, jnp.sum(dy * xh, axis=0, keepdims=True)


_GELU_C = 0.7978845608028654
_GELU_A = 0.044715


def _gelu(x):
    t = jnp.tanh(_GELU_C * (x + _GELU_A * x * x * x))
    return 0.5 * x * (1.0 + t), t


def _gelu_grad(x, t):
    return 0.5 * (1.0 + t) + 0.5 * x * (1.0 - t * t) * _GELU_C * (1.0 + 3.0 * _GELU_A * x * x)


def _split3(x):
    x1 = x.astype(BF)
    r1 = x - x1.astype(F32)
    x2 = r1.astype(BF)
    x3 = (r1 - x2.astype(F32)).astype(BF)
    return x1, x2, x3


def _tri_mm(tri, x):
    x1, x2, x3 = _split3(x)
    return _dot(tri, x1) + _dot(tri, x2) + _dot(tri, x3)


def _softmax2(lb_ref):
    l0 = lb_ref[0:1, :]
    l1 = lb_ref[1:2, :]
    m = jnp.maximum(l0, l1)
    e0 = jnp.exp(l0 - m)
    e1 = jnp.exp(l1 - m)
    return e0 / (e0 + e1), e1 / (e0 + e1)


def _layer_lower_bound(lb_ref, layer):
    g0, g1 = _softmax2(lb_ref)
    if layer == 0:
        return jnp.clip(g0 - g0, 0.0, 1.0)
    return jnp.clip((g0 + g1) - g0, 0.0, 1.0)


def _all_gather(x, name):
    n, w = x.shape

    def body(x_ref, out_ref, send_sems, recv_sems, local_sem):
        mx, my, mc = lax.axis_index("x"), lax.axis_index("y"), lax.axis_index("c")
        me, sibling = (mx, my, mc), (mx, my, 1 - mc)
        chips = [(1 - mx, my), (mx, 1 - my), (1 - mx, 1 - my)]

        def rows(px, py, pc):
            return out_ref.at[4 * px + 2 * py + pc]

        def copy(k, block, to, src=None):
            return pltpu.make_async_remote_copy(
                src_ref=rows(*block) if src is None else src, dst_ref=rows(*block),
                send_sem=send_sems.at[k], recv_sem=recv_sems.at[k], device_id=to, device_id_type=MESH)

        mine = pltpu.make_async_copy(x_ref, rows(*me), local_sem)
        mine.start()
        first = [copy(0, me, sibling, src=x_ref)]
        first += [copy(1 + j, me, (*chip, mc), src=x_ref) for j, chip in enumerate(chips)]
        for cp in first:
            cp.start()
        passed = [copy(4 + j, (*chip, mc), sibling) for j, chip in enumerate(chips)]
        for j, chip in enumerate(chips):
            copy(1 + j, (*chip, mc), me).wait_recv()
            passed[j].start()
        copy(0, sibling, me).wait_recv()
        for j, chip in enumerate(chips):
            copy(4 + j, (*chip, 1 - mc), me).wait_recv()
        for cp in first + passed:
            cp.wait_send()
        mine.wait()

    return pl.pallas_call(
        body, name=name,
        out_shape=jax.ShapeDtypeStruct((N_DEV, n, w), x.dtype),
        in_specs=[pl.BlockSpec(memory_space=pl.ANY)],
        out_specs=pl.BlockSpec(memory_space=pl.ANY),
        scratch_shapes=[pltpu.SemaphoreType.DMA((7,)), pltpu.SemaphoreType.DMA((7,)), pltpu.SemaphoreType.DMA],
    )(x)


GRAD_SRC = {'w_in': ('d_in', 0), 'wb': ('d_b', 0), 'w_out': ('d_out', 0), 'wg': ('d_gu', 0), 'wu': ('d_gu', D_FF),
            'wd': ('d_down', 0)}


class _GradExchange:
    def __init__(self, keys, grads):
        self.names = sorted({GRAD_SRC[k][0] for k in keys})
        self.arrays = [grads[n][0] for n in self.names] + [grads[n][1] for n in self.names]
        self.pieces, off = [], 0
        for k in keys:
            name, base = GRAD_SRC[k]
            rows = W_ROWS[k][1]
            self.pieces.append((self.names.index(name), base, rows, off))
            off += rows
        self.keys, self.rows = keys, off
        self.out_shape = (jax.ShapeDtypeStruct((N_DEV - 1, off, D_MODEL), BF), jax.ShapeDtypeStruct((off, D_MODEL), F32))
        self.scratch = [pltpu.SemaphoreType.DMA((N_DEV - 1,)), pltpu.SemaphoreType.DMA((N_DEV - 1,)),
                        pltpu.SemaphoreType.DMA((len(self.pieces),))]

    def _local(self, ins, outs, scr):
        mx, my, mc = lax.axis_index("x"), lax.axis_index("y"), lax.axis_index("c")
        me = 4 * mx + 2 * my + mc
        return [pltpu.make_async_copy(ins[ai].at[pl.ds(pl.multiple_of(base + rows * me, 8), rows)],
                                      outs[1].at[pl.ds(off, rows)], scr[2].at[k])
                for k, (ai, base, rows, off) in enumerate(self.pieces)]

    def start(self, ins, outs, scr):
        mx, my, mc = lax.axis_index("x"), lax.axis_index("y"), lax.axis_index("c")
        bf16 = ins[len(self.names):]
        for cp in self._local(ins, outs, scr):
            cp.start()
        for d in range(1, N_DEV):
            px = (1 - mx) if (d >> 2) & 1 else mx
            py = (1 - my) if (d >> 1) & 1 else my
            pc = (1 - mc) if d & 1 else mc
            peer = 4 * px + 2 * py + pc
            for ai, base, rows, off in self.pieces:
                pltpu.make_async_remote_copy(
                    src_ref=bf16[ai].at[pl.ds(pl.multiple_of(base + rows * peer, 16), rows)],
                    dst_ref=outs[0].at[d - 1, pl.ds(off, rows)], send_sem=scr[0].at[d - 1], recv_sem=scr[1].at[d - 1],
                    device_id=(px, py, pc), device_id_type=MESH).start()

    def wait(self, ins, outs, scr):
        me = (lax.axis_index("x"), lax.axis_index("y"), lax.axis_index("c"))
        for d in range(1, N_DEV):
            slot = pltpu.make_async_remote_copy(
                src_ref=outs[0].at[d - 1], dst_ref=outs[0].at[d - 1], send_sem=scr[0].at[d - 1],
                recv_sem=scr[1].at[d - 1], device_id=me, device_id_type=MESH)
            slot.wait_recv()
            slot.wait_send()
        for cp in self._local(ins, outs, scr):
            cp.wait()


def _hosted_call(body, comm, name, grid, in_specs, args, out_shape, out_specs, scratch_shapes):
    if comm is None:
        return pl.pallas_call(body, name=name, grid=grid, in_specs=in_specs, out_specs=out_specs, out_shape=out_shape,
                              scratch_shapes=scratch_shapes, compiler_params=_params(("arbitrary",)))(*args), None
    n_in, n_out, n_scr, n_c = len(in_specs), len(out_shape), len(scratch_shapes), len(comm.arrays)
    steps = grid[0]

    def hosted(*refs):
        ins, cins = refs[:n_in], refs[n_in:n_in + n_c]
        outs = refs[n_in + n_c:n_in + n_c + n_out]
        couts = refs[n_in + n_c + n_out:n_in + n_c + n_out + 2]
        scr = refs[n_in + n_c + n_out + 2:n_in + n_c + n_out + 2 + n_scr]
        cscr = refs[n_in + n_c + n_out + 2 + n_scr:]

        @pl.when(pl.program_id(0) == 0)
        def _():
            comm.start(cins, couts, cscr)

        body(*ins, *outs, *scr)

        @pl.when(pl.program_id(0) == steps - 1)
        def _():
            comm.wait(cins, couts, cscr)

    res = pl.pallas_call(
        hosted, name=name, grid=grid, in_specs=list(in_specs) + [HBM_SPEC] * n_c,
        out_specs=tuple(out_specs) + (HBM_SPEC, HBM_SPEC), out_shape=tuple(out_shape) + comm.out_shape,
        scratch_shapes=list(scratch_shapes) + comm.scratch, compiler_params=_params(("arbitrary",)),
    )(*args, *comm.arrays)
    return res[:n_out], res[n_out:]


def _sum_parts(own, recv, name):
    n, w = own.shape
    tn = _tile(n, 256, 16)

    def body(own_ref, p_ref, o_ref):
        acc = own_ref[...]
        for d in range(N_DEV - 1):
            acc = acc + p_ref[d].astype(F32)
        o_ref[...] = acc

    return pl.pallas_call(
        body, name=name, out_shape=jax.ShapeDtypeStruct((n, w), F32), grid=(n // tn,),
        in_specs=[pl.BlockSpec((tn, w), lambda i: (i, 0)), pl.BlockSpec((N_DEV - 1, tn, w), lambda i: (0, i, 0))],
        out_specs=pl.BlockSpec((tn, w), lambda i: (i, 0)),
        compiler_params=_params(("parallel",)),
    )(own, recv)


def _sum8(parts, name):
    _, n, w = parts.shape
    tn = _tile(n, 256 if w > 128 else 1024, 8)

    def body(p_ref, o_ref):
        acc = p_ref[0]
        for d in range(1, N_DEV):
            acc = acc + p_ref[d]
        o_ref[...] = acc

    return pl.pallas_call(
        body, name=name, out_shape=jax.ShapeDtypeStruct((n, w), F32), grid=(n // tn,),
        in_specs=[pl.BlockSpec((N_DEV, tn, w), lambda i: (0, i, 0))],
        out_specs=pl.BlockSpec((tn, w), lambda i: (i, 0)),
        compiler_params=_params(("parallel",)),
    )(parts)


def _pack(arrays, dtype, row_mult):
    flat = jnp.concatenate([a.astype(dtype).reshape(-1) for a in arrays])
    pad = (-flat.shape[0]) % (128 * row_mult)
    if pad:
        flat = jnp.concatenate([flat, jnp.zeros((pad,), dtype)])
    return flat.reshape(-1, 128)


def _unpack(flat2d, shapes):
    lead = flat2d.shape[:-2]
    flat = flat2d.reshape(lead + (-1,))
    out, off = [], 0
    for s in shapes:
        size = 1
        for d in s:
            size *= d
        out.append(flat[..., off:off + size].reshape(lead + tuple(s)))
        off += size
    return out


def _blocks_to_full(blocks, axis):
    moved = jnp.moveaxis(blocks, 0, axis)
    shape = list(moved.shape)
    shape[axis:axis + 2] = [shape[axis] * shape[axis + 1]]
    return moved.reshape(shape)


def _weight_scratch(keys):
    return ([pltpu.VMEM((N_DEV * W_ROWS[k][1], D_MODEL), BF) for k in keys]
            + [pltpu.SemaphoreType.DMA((N_DEV * len(keys),))])


def _fetch_weights(slabs_ref, keys, bufs, sems):
    @pl.when(pl.program_id(0) == 0)
    def _():
        copies = []
        for ki, (key, buf) in enumerate(zip(keys, bufs)):
            row0, rows = W_ROWS[key]
            for j in range(N_DEV):
                copies.append(pltpu.make_async_copy(slabs_ref.at[j, pl.ds(row0, rows)],
                                                    buf.at[pl.ds(j * rows, rows)], sems.at[ki * N_DEV + j]))
        for cp in copies:
            cp.start()
        for cp in copies:
            cp.wait()


HBM_SPEC = pl.BlockSpec(memory_space=pl.ANY)


def _norm_matmul(h, gain, slabs, keys, out_dtype, name):
    t_rows, d = h.shape
    widths = [N_DEV * W_ROWS[k][1] for k in keys]
    tm = _tile(t_rows, ROW_TILE)

    def body(h_ref, g_ref, slabs_ref, o_ref, *scratch):
        bufs, sems = scratch[:-1], scratch[-1]
        _fetch_weights(slabs_ref, keys, bufs, sems)
        u = _rms_fwd(h_ref[...], g_ref[...]).astype(BF)
        off = 0
        for buf, n in zip(bufs, widths):
            o_ref[:, off:off + n] = _dot_nt(u, buf[...]).astype(out_dtype)
            off += n

    return pl.pallas_call(
        body, name=name, out_shape=jax.ShapeDtypeStruct((t_rows, sum(widths)), out_dtype), grid=(t_rows // tm,),
        in_specs=[pl.BlockSpec((tm, d), lambda i: (i, 0)), _whole((1, d)), HBM_SPEC],
        out_specs=pl.BlockSpec((tm, sum(widths)), lambda i: (i, 0)),
        scratch_shapes=_weight_scratch(keys),
        compiler_params=_params(("arbitrary",)),
    )(h, gain, slabs)


def _chunk_masks():
    row = lax.broadcasted_iota(jnp.int32, (CHUNK, CHUNK), 0)
    col = lax.broadcasted_iota(jnp.int32, (CHUNK, CHUNK), 1)
    tri_lo = (col <= row).astype(BF)
    tri_up = (col >= row).astype(BF)
    rb = jnp.right_shift(row, 4)
    cb = jnp.right_shift(col, 4)
    diag = (rb == cb) & (col <= row)
    off = [(rb == i) & (col < SUB * i) for i in range(1, CHUNK // SUB)]
    return tri_lo, tri_up, diag, off


def _chunk_gates(fl, lb):
    sg = jax.nn.sigmoid(fl)
    s2 = jax.nn.sigmoid(-fl)
    f = lb + (1.0 - lb) * sg
    lf = jnp.log(jnp.maximum(f, LOG_FLOOR))
    k = (1.0 - lb) * s2
    return sg, s2, f, lf, k


def _chunk_factors(q, k, b):
    nb = CHUNK // SUB
    rq = jnp.concatenate([jnp.broadcast_to(b[SUB * i:SUB * i + 1], (SUB, HEAD_DIM)) for i in range(nb)], axis=0)
    mq = jnp.concatenate([jnp.broadcast_to(b[SUB * i + SUB // 2:SUB * i + SUB // 2 + 1], (SUB, HEAD_DIM))
                          for i in range(nb)], axis=0)
    e_qs = jnp.exp(b - rq)
    e_qm = jnp.exp(jnp.minimum(b - mq, EXP_CLAMP))
    e_km = jnp.exp(jnp.minimum(mq - b, EXP_CLAMP))
    e_ks = [jnp.exp(jnp.minimum(b[SUB * i:SUB * i + 1] - b, 0.0)) for i in range(1, nb)]
    qs = (q * e_qs).astype(BF)
    qm = (q * e_qm).astype(BF)
    km = (k * e_km).astype(BF)
    ks = [(k * e).astype(BF) for e in e_ks]
    return e_qs, e_qm, e_km, e_ks, qs, qm, km, ks


def _chunk_scores(qs, qm, km, ks, diag, off):
    a = jnp.where(diag, _dot_nt(qm, km), 0.0)
    for m, kk in zip(off, ks):
        a = jnp.where(m, _dot_nt(qs, kk), a)
    return a


def _hgrn_fwd(proj, lower_bounds, layer, name):
    t_rows = proj.shape[0]
    tm = _tile(t_rows, ROW_TILE, CHUNK)
    nct = tm // CHUNK

    def body(q_ref, f_ref, i_ref, lb_ref, o_ref, sall_ref, st_ref):
        @pl.when(pl.program_id(0) == 0)
        def _():
            st_ref[...] = jnp.zeros_like(st_ref)

        lbs = _layer_lower_bound(lb_ref, layer)
        tri_lo, _, diag, off = _chunk_masks()

        def chunk(c, carry):
            r0 = pl.multiple_of(c * CHUNK, SUB)
            for hh in range(HEADS):
                cs = slice(hh * HEAD_DIM, (hh + 1) * HEAD_DIM)
                q = q_ref[pl.ds(r0, CHUNK), cs]
                v = i_ref[pl.ds(r0, CHUNK), cs]
                _, _, _, lf, k = _chunk_gates(f_ref[pl.ds(r0, CHUNK), cs], lbs[:, cs])
                b = _tri_mm(tri_lo, lf)
                _, _, _, _, qs, qm, km, ks = _chunk_factors(q, k, b)
                a = _chunk_scores(qs, qm, km, ks, diag, off)
                st = st_ref[hh]
                sall_ref[c, hh] = st
                vb = v.astype(BF)
                qc = (q * jnp.exp(b)).astype(BF)
                o_ref[pl.ds(r0, CHUNK), cs] = _dot_nt(qc, st.astype(BF)) + _dot(a.astype(BF), vb)
                b_last = b[CHUNK - 1:CHUNK]
                kdec = (k * jnp.exp(b_last - b)).astype(BF)
                st_ref[hh] = st * jnp.exp(b_last) + _dot_tn(vb, kdec)
            return carry

        lax.fori_loop(0, nct, chunk, 0, unroll=HGRN_UNROLL if nct % HGRN_UNROLL == 0 else 1)

    return pl.pallas_call(
        body, name=name,
        out_shape=(jax.ShapeDtypeStruct((t_rows, HGRN_W), F32),
                   jax.ShapeDtypeStruct((t_rows // CHUNK, HEADS, HEAD_DIM, HEAD_DIM), F32)),
        grid=(t_rows // tm,),
        in_specs=[_cols(tm, 512, 0), _cols(tm, 512, 1), _cols(tm, 512, 2), _whole((DEPTH, HGRN_W))],
        out_specs=(pl.BlockSpec((tm, HGRN_W), lambda i: (i, 0)),
                   pl.BlockSpec((nct, HEADS, HEAD_DIM, HEAD_DIM), lambda i: (i, 0, 0, 0))),
        scratch_shapes=[pltpu.VMEM((HEADS, HEAD_DIM, HEAD_DIM), F32)],
        compiler_params=_params(("arbitrary",)),
    )(proj, proj, proj, lower_bounds)


def _head_norm(o):
    outs, rs = [], []
    for hh in range(HEADS):
        oh = o[:, hh * HEAD_DIM:(hh + 1) * HEAD_DIM]
        r = lax.rsqrt(jnp.mean(oh * oh, axis=-1, keepdims=True) + EPS)
        outs.append(oh * r)
        rs.append(r)
    return outs, rs


def _window_counts(row0, rows):
    t1 = (row0 + lax.broadcasted_iota(jnp.int32, (rows, 1), 0) + 1).astype(F32)
    return [1.0 / jnp.minimum(t1, float(w)) for w in POOL_WINDOWS]


def _pool_fwd(v, halo, row0):
    rows = v.shape[0]
    inv = _window_counts(row0, rows)
    outs = []
    for gi, w in enumerate(POOL_WINDOWS):
        cs = slice(gi * HEAD_DIM, (gi + 1) * HEAD_DIM)
        s = jnp.concatenate([halo[:, cs], v[:, cs]], axis=0)
        step = 1
        while step < w:
            s = s + pltpu.roll(s, step, 0)
            step *= 2
        outs.append(s[HALO:] * inv[gi] - v[:, cs])
    return outs


def _pool_bwd(dpooled, halo, row0):
    rows = dpooled[0].shape[0]
    inv = _window_counts(row0, rows)
    inv_h = _window_counts(row0 + rows, HALO)
    outs = []
    for gi, w in enumerate(POOL_WINDOWS):
        s = jnp.concatenate([dpooled[gi] * inv[gi], halo[gi] * inv_h[gi]], axis=0)
        n_ext = s.shape[0]
        step = 1
        while step < w:
            s = s + pltpu.roll(s, n_ext - step, 0)
            step *= 2
        outs.append(s[:rows] - dpooled[gi])
    return outs


def _pool_project(pooled, pp_ref, scale):
    y = jnp.concatenate([_dot(pooled[gi].astype(BF), pp_ref[gi].astype(BF)) for gi in range(4)], axis=1)
    return y, y * scale


def _mix_merge(proj, o, h, out_gain, pool_proj, pool_scale, slabs, gain_post, name):
    t_rows = h.shape[0]
    tm = _tile(t_rows, ROW_TILE)
    hpt = tm // HALO
    keys = ['wb', 'w_out']

    def body(g_ref, v_ref, vh_ref, ga0_ref, ga1_ref, gb0_ref, gb1_ref, o_ref, h_ref, og_ref, pp_ref, ps_ref,
             slabs_ref, gp_ref, hn_ref, r_ref, a_ref, p_ref, wb_ref, wo_ref, sems):
        _fetch_weights(slabs_ref, keys, (wb_ref, wo_ref), sems)
        i = pl.program_id(0)
        on, _ = _head_norm(o_ref[...])
        a = jnp.concatenate(on, axis=1) * og_ref[...] * jax.nn.sigmoid(g_ref[...])
        halo = jnp.where(i == 0, 0.0, vh_ref[...])
        pooled = _pool_fwd(v_ref[...], halo, i * tm)
        _, p = _pool_project(pooled, pp_ref, ps_ref[...])
        ab, pb = a.astype(BF), p.astype(BF)
        sa = jax.nn.sigmoid(jnp.concatenate([ga0_ref[...], ga1_ref[...]], axis=1))
        sb = jax.nn.sigmoid(jnp.concatenate([gb0_ref[...], gb1_ref[...]], axis=1))
        z = sa * _dot_nt(ab, wb_ref[:, 0:HGRN_W]) + sb * _dot_nt(pb, wb_ref[:, HGRN_W:HGRN_W + POOL_W])
        r = _dot(z.astype(BF), wo_ref[...])
        hn_ref[...] = h_ref[...] + _rms_fwd(r, gp_ref[...])
        r_ref[...] = r
        a_ref[...] = ab
        p_ref[...] = pb

    rows = lambda width: pl.BlockSpec((tm, width), lambda i: (i, 0))
    return pl.pallas_call(
        body, name=name,
        out_shape=(jax.ShapeDtypeStruct((t_rows, D_MODEL), F32), jax.ShapeDtypeStruct((t_rows, D_MODEL), F32),
                   jax.ShapeDtypeStruct((t_rows, HGRN_W), BF), jax.ShapeDtypeStruct((t_rows, POOL_W), BF)),
        grid=(t_rows // tm,),
        in_specs=[_cols(tm, 512, 3), _cols(tm, 512, 4),
                  pl.BlockSpec((HALO, 512), lambda i: (jnp.maximum(i * hpt - 1, 0), 4)),
                  _cols(tm, 512, 5), _cols(tm, 512, 6), _cols(tm, 512, 7), _cols(tm, 512, 8),
                  rows(HGRN_W), rows(D_MODEL), _whole((1, HGRN_W)), _whole((4, HEAD_DIM, HEAD_DIM)),
                  _whole((1, POOL_W)), HBM_SPEC, _whole((1, D_MODEL))],
        out_specs=(rows(D_MODEL), rows(D_MODEL), rows(HGRN_W), rows(POOL_W)),
        scratch_shapes=_weight_scratch(keys),
        compiler_params=_params(("arbitrary",)),
    )(proj, proj, proj, proj, proj, proj, proj, o, h, out_gain, pool_proj, pool_scale, slabs, gain_post)


def _conv_fwd(g, halo, cw):
    ext = jnp.concatenate([halo, g], axis=0)
    return (cw[0:1] * pltpu.roll(ext, 2, 0)[8:] + cw[1:2] * pltpu.roll(ext, 1, 0)[8:] + cw[2:3] * g)


def _ffn_down(gu, h, conv_w, conv_b, slabs, gain_post, name):
    t_rows = h.shape[0]
    tm = _tile(t_rows, FFN_TILE)
    hpt = tm // HALO

    def body(g_ref, gh_ref, up_ref, h_ref, cw_ref, cb_ref, slabs_ref, gp_ref, hn_ref, y_ref, wd_ref, sems):
        _fetch_weights(slabs_ref, ['wd'], (wd_ref,), sems)
        i = pl.program_id(0)
        halo = jnp.where(i == 0, 0.0, gh_ref[8:HALO, :].astype(F32))
        gt = _conv_fwd(g_ref[...].astype(F32), halo, cw_ref[...]) + cb_ref[...]
        act = _gelu(gt)[0] * up_ref[...].astype(F32)
        y = _dot(act.astype(BF), wd_ref[...])
        hn_ref[...] = h_ref[...] + _rms_fwd(y, gp_ref[...])
        y_ref[...] = y

    rows = lambda width: pl.BlockSpec((tm, width), lambda i: (i, 0))
    return pl.pallas_call(
        body, name=name,
        out_shape=(jax.ShapeDtypeStruct((t_rows, D_MODEL), F32), jax.ShapeDtypeStruct((t_rows, D_MODEL), F32)),
        grid=(t_rows // tm,),
        in_specs=[_cols(tm, D_FF, 0), pl.BlockSpec((HALO, D_FF), lambda i: (jnp.maximum(i * hpt - 1, 0), 0)),
                  _cols(tm, D_FF, 1), rows(D_MODEL), _whole((3, D_FF)), _whole((1, D_FF)),
                  HBM_SPEC, _whole((1, D_MODEL))],
        out_specs=(rows(D_MODEL), rows(D_MODEL)),
        scratch_shapes=_weight_scratch(['wd']),
        compiler_params=_params(("arbitrary",)),
    )(gu, gu, gu, h, conv_w, conv_b, slabs, gain_post)


def _loss_grad(h, target, name):
    t_rows, d = h.shape
    tm = _tile(t_rows, ROW_TILE)

    def body(h_ref, t_ref, dh_ref, l_ref):
        i = pl.program_id(0)

        @pl.when(i == 0)
        def _():
            l_ref[...] = jnp.zeros_like(l_ref)

        row = i * tm + lax.broadcasted_iota(jnp.int32, (tm, 1), 0)
        err = jnp.where(row >= N_META, h_ref[...] - t_ref[...], 0.0)
        dh_ref[...] = err * (1.0 / d)
        l_ref[...] += jnp.sum(err * err, axis=0, keepdims=True) * (0.5 / d)

    rows = pl.BlockSpec((tm, d), lambda i: (i, 0))
    return pl.pallas_call(
        body, name=name,
        out_shape=(jax.ShapeDtypeStruct((t_rows, d), F32), jax.ShapeDtypeStruct((1, d), F32)),
        grid=(t_rows // tm,), in_specs=[rows, rows], out_specs=(rows, _whole((1, d))),
        compiler_params=_params(("arbitrary",)),
    )(h, target)


def _dw(lhs, rhs, name, gain=None):
    t_rows = lhs[0].shape[0]
    tm = _tile(t_rows, ROW_TILE)
    paired = len(rhs) > 1
    n_rows = lhs[0].shape[1] if paired else sum(x.shape[1] for x in lhs)
    n_cols = sum(x.shape[1] for x in rhs)
    last = t_rows // tm - 1

    def body(*refs):
        lhs_refs = refs[:len(lhs)]
        rhs_refs = refs[len(lhs):len(lhs) + len(rhs)]
        rest = refs[len(lhs) + len(rhs):]
        if gain is not None:
            g_ref, o_ref, o16_ref, acc, stage = rest
            rv = [_rms_fwd(rhs_refs[0][...], g_ref[...]).astype(BF)]
        else:
            o_ref, o16_ref, acc, stage = rest
            rv = [r[...] for r in rhs_refs]
        i = pl.program_id(0)

        @pl.when(i == 0)
        def _():
            acc[...] = jnp.zeros_like(acc)

        r0, c0 = 0, 0
        for p, l_ref in enumerate(lhs_refs):
            r = rv[p] if paired else rv[0]
            n = l_ref.shape[1]
            step = 512 if n % 512 == 0 else 256
            for s in range(0, n, step):
                acc[r0 + s:r0 + s + step, c0:c0 + r.shape[1]] += _dot_tn(l_ref[:, s:s + step], r)
            if paired:
                c0 += r.shape[1]
            else:
                r0 += n

        @pl.when(i == last)
        def _():
            pltpu.sync_copy(acc, o_ref)
            for s in range(0, n_rows, 256):
                stage[...] = acc[s:s + 256, :].astype(BF)
                pltpu.sync_copy(stage, o16_ref.at[pl.ds(s, 256)])

    rows = lambda x: pl.BlockSpec((tm, x.shape[1]), lambda i: (i, 0))
    in_specs = [rows(x) for x in lhs] + [rows(x) for x in rhs]
    args = list(lhs) + list(rhs)
    if gain is not None:
        in_specs.append(_whole(gain.shape))
        args.append(gain)
    return pl.pallas_call(
        body, name=name,
        out_shape=(jax.ShapeDtypeStruct((n_rows, n_cols), F32), jax.ShapeDtypeStruct((n_rows, n_cols), BF)),
        grid=(t_rows // tm,), in_specs=in_specs, out_specs=(HBM_SPEC, HBM_SPEC),
        scratch_shapes=[pltpu.VMEM((n_rows, n_cols), F32), pltpu.VMEM((256, n_cols), BF)],
        compiler_params=_params(("arbitrary",)),
    )(*args)


def _ffn_bwd_down(dh, y, gain_post, gu, conv_w, conv_b, slabs, name, comm=None):
    t_rows = dh.shape[0]
    tm = _tile(t_rows, FFN_TILE)
    hpt = tm // HALO

    def body(dh_ref, y_ref, gp_ref, g_ref, gh_ref, up_ref, cw_ref, cb_ref, slabs_ref,
             dy_ref, act_ref, dgt_ref, dup_ref, dgp_ref, dcw_ref, dcb_ref, wd_ref, sems):
        _fetch_weights(slabs_ref, ['wd'], (wd_ref,), sems)
        i = pl.program_id(0)

        @pl.when(i == 0)
        def _():
            dgp_ref[...] = jnp.zeros_like(dgp_ref)
            dcw_ref[...] = jnp.zeros_like(dcw_ref)
            dcb_ref[...] = jnp.zeros_like(dcb_ref)

        dy, dgp = _rms_bwd(dh_ref[...], y_ref[...], gp_ref[...])
        dgp_ref[...] += dgp
        dyb = dy.astype(BF)
        dy_ref[...] = dyb
        g = g_ref[...].astype(F32)
        up = up_ref[...].astype(F32)
        ext = jnp.concatenate([jnp.where(i == 0, 0.0, gh_ref[8:HALO, :].astype(F32)), g], axis=0)
        g2 = pltpu.roll(ext, 2, 0)[8:]
        g1 = pltpu.roll(ext, 1, 0)[8:]
        cw = cw_ref[...]
        gt = cw[0:1] * g2 + cw[1:2] * g1 + cw[2:3] * g + cb_ref[...]
        gl, th = _gelu(gt)
        act_ref[...] = (gl * up).astype(BF)
        dact = _dot_nt(dyb, wd_ref[...])
        dup_ref[...] = (dact * gl).astype(BF)
        dgt = dact * up * _gelu_grad(gt, th)
        dgt_ref[...] = dgt.astype(BF)
        dcb_ref[...] += jnp.sum(dgt, axis=0, keepdims=True)
        dcw_ref[...] += jnp.concatenate([jnp.sum(dgt * g2, axis=0, keepdims=True),
                                         jnp.sum(dgt * g1, axis=0, keepdims=True),
                                         jnp.sum(dgt * g, axis=0, keepdims=True)], axis=0)

    rows = lambda width: pl.BlockSpec((tm, width), lambda i: (i, 0))
    bf = lambda width: jax.ShapeDtypeStruct((t_rows, width), BF)
    outs, exchanged = _hosted_call(
        body, comm, name, (t_rows // tm,),
        [rows(D_MODEL), rows(D_MODEL), _whole((1, D_MODEL)), _cols(tm, D_FF, 0),
         pl.BlockSpec((HALO, D_FF), lambda i: (jnp.maximum(i * hpt - 1, 0), 0)), _cols(tm, D_FF, 1),
         _whole((3, D_FF)), _whole((1, D_FF)), HBM_SPEC],
        (dh, y, gain_post, gu, gu, gu, conv_w, conv_b, slabs),
        (bf(D_MODEL), bf(D_FF), bf(D_FF), bf(D_FF), jax.ShapeDtypeStruct((1, D_MODEL), F32),
         jax.ShapeDtypeStruct((3, D_FF), F32), jax.ShapeDtypeStruct((1, D_FF), F32)),
        (rows(D_MODEL), rows(D_FF), rows(D_FF), rows(D_FF), _whole((1, D_MODEL)), _whole((3, D_FF)),
         _whole((1, D_FF))),
        _weight_scratch(['wd']))
    return (*outs, exchanged)


def _ffn_bwd_up(dgt, dup, conv_w, slabs, h, gain_pre, dh_out, name):
    t_rows = h.shape[0]
    tm = _tile(t_rows, FFN_TILE)
    hpt = tm // HALO
    last = t_rows // tm - 1

    def body(dgt_ref, dgn_ref, dup_ref, cw_ref, slabs_ref, h_ref, gp_ref, dho_ref, dh_ref, dgu_ref, dg_ref,
             wg_ref, wu_ref, sems):
        _fetch_weights(slabs_ref, ['wg', 'wu'], (wg_ref, wu_ref), sems)
        i = pl.program_id(0)

        @pl.when(i == 0)
        def _():
            dg_ref[...] = jnp.zeros_like(dg_ref)

        dgt = dgt_ref[...].astype(F32)
        ext = jnp.concatenate([dgt, jnp.where(i == last, 0.0, dgn_ref[0:8, :].astype(F32))], axis=0)
        n_ext = tm + 8
        cw = cw_ref[...]
        dg = (cw[2:3] * dgt + cw[1:2] * pltpu.roll(ext, n_ext - 1, 0)[:tm]
              + cw[0:1] * pltpu.roll(ext, n_ext - 2, 0)[:tm])
        dgb = dg.astype(BF)
        dub = dup_ref[...]
        dgu_ref[:, 0:D_FF] = dgb
        dgu_ref[:, D_FF:2 * D_FF] = dub
        du = _dot(dgb, wg_ref[...]) + _dot(dub, wu_ref[...])
        dx, dgain = _rms_bwd(du, h_ref[...], gp_ref[...])
        dh_ref[...] = dho_ref[...] + dx
        dg_ref[...] += dgain

    rows = lambda width: pl.BlockSpec((tm, width), lambda i: (i, 0))
    return pl.pallas_call(
        body, name=name,
        out_shape=(jax.ShapeDtypeStruct((t_rows, D_MODEL), F32), jax.ShapeDtypeStruct((t_rows, 2 * D_FF), BF),
                   jax.ShapeDtypeStruct((1, D_MODEL), F32)),
        grid=(t_rows // tm,),
        in_specs=[rows(D_FF), pl.BlockSpec((HALO, D_FF), lambda i: (jnp.minimum((i + 1) * hpt, (last + 1) * hpt - 1), 0)),
                  rows(D_FF), _whole((3, D_FF)), HBM_SPEC, rows(D_MODEL), _whole((1, D_MODEL)),
                  rows(D_MODEL)],
        out_specs=(rows(D_MODEL), rows(2 * D_FF), _whole((1, D_MODEL))),
        scratch_shapes=_weight_scratch(['wg', 'wu']),
        compiler_params=_params(("arbitrary",)),
    )(dgt, dgt, dup, conv_w, slabs, h, gain_pre, dh_out)


def _mix_bwd_a(dh, r, gain_post, proj, a, p, slabs, name):
    t_rows = dh.shape[0]
    tm = _tile(t_rows, ROW_TILE)

    def body(dh_ref, r_ref, gp_ref, ga0_ref, ga1_ref, gb0_ref, gb1_ref, a_ref, p_ref, slabs_ref,
             dr_ref, z_ref, dya_ref, dyp_ref, da_ref, dp_ref, dgab_ref, dg_ref, wb_ref, wo_ref, sems):
        _fetch_weights(slabs_ref, ['wb', 'w_out'], (wb_ref, wo_ref), sems)

        @pl.when(pl.program_id(0) == 0)
        def _():
            dg_ref[...] = jnp.zeros_like(dg_ref)

        dr, dgain = _rms_bwd(dh_ref[...], r_ref[...], gp_ref[...])
        dg_ref[...] += dgain
        drb = dr.astype(BF)
        dr_ref[...] = drb
        dz = _dot_nt(drb, wo_ref[...])
        ya = _dot_nt(a_ref[...], wb_ref[:, 0:HGRN_W])
        yp = _dot_nt(p_ref[...], wb_ref[:, HGRN_W:HGRN_W + POOL_W])
        sa = jax.nn.sigmoid(jnp.concatenate([ga0_ref[...], ga1_ref[...]], axis=1))
        sb = jax.nn.sigmoid(jnp.concatenate([gb0_ref[...], gb1_ref[...]], axis=1))
        z_ref[...] = (sa * ya + sb * yp).astype(BF)
        dgab_ref[:, 0:D_MODEL] = (dz * ya * sa * (1.0 - sa)).astype(BF)
        dgab_ref[:, D_MODEL:2 * D_MODEL] = (dz * yp * sb * (1.0 - sb)).astype(BF)
        dya = (dz * sa).astype(BF)
        dyp = (dz * sb).astype(BF)
        dya_ref[...] = dya
        dyp_ref[...] = dyp
        da_ref[...] = _dot(dya, wb_ref[:, 0:HGRN_W])
        dp_ref[...] = _dot(dyp, wb_ref[:, HGRN_W:HGRN_W + POOL_W])

    rows = lambda width: pl.BlockSpec((tm, width), lambda i: (i, 0))
    bf = lambda width: jax.ShapeDtypeStruct((t_rows, width), BF)
    f32 = lambda width: jax.ShapeDtypeStruct((t_rows, width), F32)
    return pl.pallas_call(
        body, name=name,
        out_shape=(bf(D_MODEL), bf(D_MODEL), bf(D_MODEL), bf(D_MODEL), f32(HGRN_W), f32(POOL_W), bf(2 * D_MODEL),
                   jax.ShapeDtypeStruct((1, D_MODEL), F32)),
        grid=(t_rows // tm,),
        in_specs=[rows(D_MODEL), rows(D_MODEL), _whole((1, D_MODEL)),
                  _cols(tm, 512, 5), _cols(tm, 512, 6), _cols(tm, 512, 7), _cols(tm, 512, 8),
                  rows(HGRN_W), rows(POOL_W), HBM_SPEC],
        out_specs=(rows(D_MODEL), rows(D_MODEL), rows(D_MODEL), rows(D_MODEL), rows(HGRN_W), rows(POOL_W),
                   rows(2 * D_MODEL), _whole((1, D_MODEL))),
        scratch_shapes=_weight_scratch(['wb', 'w_out']),
        compiler_params=_params(("arbitrary",)),
    )(dh, r, gain_post, proj, proj, proj, proj, a, p, slabs)


def _mix_bwd_b(da, dp, proj, o, out_gain, pool_proj, pool_scale, name):
    t_rows = da.shape[0]
    tm = _tile(t_rows, ROW_TILE)
    hpt = tm // HALO
    last = t_rows // tm - 1

    def body(da_ref, dp_ref, dpn_ref, g_ref, v_ref, vh_ref, o_ref, og_ref, pp_ref, ps_ref,
             do_ref, dgv_ref, dog_ref, dpp_ref, dps_ref):
        i = pl.program_id(0)

        @pl.when(i == 0)
        def _():
            dog_ref[...] = jnp.zeros_like(dog_ref)
            dpp_ref[...] = jnp.zeros_like(dpp_ref)
            dps_ref[...] = jnp.zeros_like(dps_ref)

        da = da_ref[...]
        og = og_ref[...]
        o = o_ref[...]
        on, rs = _head_norm(o)
        onc = jnp.concatenate(on, axis=1)
        sg = jax.nn.sigmoid(g_ref[...])
        dog_ref[...] += jnp.sum(da * onc * sg, axis=0, keepdims=True)
        dgv_ref[:, 0:HGRN_W] = (da * onc * og * sg * (1.0 - sg)).astype(BF)
        don = da * og * sg
        for hh in range(HEADS):
            cs = slice(hh * HEAD_DIM, (hh + 1) * HEAD_DIM)
            d = don[:, cs]
            do_ref[:, cs] = rs[hh] * (d - on[hh] * jnp.mean(d * on[hh], axis=-1, keepdims=True))

        scale = ps_ref[...]
        halo = jnp.where(i == 0, 0.0, vh_ref[...])
        pooled = _pool_fwd(v_ref[...], halo, i * tm)
        y, _ = _pool_project(pooled, pp_ref, scale)
        dp = dp_ref[...]
        dps_ref[...] += jnp.sum(dp * y, axis=0, keepdims=True)
        dy = dp * scale
        dyn = jnp.where(i == last, 0.0, dpn_ref[...]) * scale
        dpooled, dhalo = [], []
        for gi in range(4):
            cs = slice(gi * HEAD_DIM, (gi + 1) * HEAD_DIM)
            ppb = pp_ref[gi].astype(BF)
            dyb = dy[:, cs].astype(BF)
            dpooled.append(_dot_nt(dyb, ppb))
            dhalo.append(_dot_nt(dyn[:, cs].astype(BF), ppb))
            dpp_ref[gi] += _dot_tn(pooled[gi].astype(BF), dyb)
        dv = _pool_bwd(dpooled, dhalo, i * tm)
        dgv_ref[:, HGRN_W:HGRN_W + POOL_W] = jnp.concatenate(dv, axis=1).astype(BF)

    rows = lambda width: pl.BlockSpec((tm, width), lambda i: (i, 0))
    return pl.pallas_call(
        body, name=name,
        out_shape=(jax.ShapeDtypeStruct((t_rows, HGRN_W), F32), jax.ShapeDtypeStruct((t_rows, HGRN_W + POOL_W), BF),
                   jax.ShapeDtypeStruct((1, HGRN_W), F32), jax.ShapeDtypeStruct((4, HEAD_DIM, HEAD_DIM), F32),
                   jax.ShapeDtypeStruct((1, POOL_W), F32)),
        grid=(t_rows // tm,),
        in_specs=[rows(HGRN_W), rows(POOL_W),
                  pl.BlockSpec((HALO, POOL_W), lambda i: (jnp.minimum((i + 1) * hpt, (last + 1) * hpt - 1), 0)),
                  _cols(tm, 512, 3), _cols(tm, 512, 4),
                  pl.BlockSpec((HALO, 512), lambda i: (jnp.maximum(i * hpt - 1, 0), 4)),
                  rows(HGRN_W), _whole((1, HGRN_W)), _whole((4, HEAD_DIM, HEAD_DIM)), _whole((1, POOL_W))],
        out_specs=(rows(HGRN_W), rows(HGRN_W + POOL_W), _whole((1, HGRN_W)), _whole((4, HEAD_DIM, HEAD_DIM)),
                   _whole((1, POOL_W))),
        compiler_params=_params(("arbitrary",)),
    )(da, dp, dp, proj, proj, proj, o, out_gain, pool_proj, pool_scale)


def _hgrn_bwd(proj, lower_bounds, layer, states, do, name, comm=None):
    t_rows = proj.shape[0]
    tm = _tile(t_rows, ROW_TILE, CHUNK)
    nct = tm // CHUNK
    n_tiles = t_rows // tm

    def body(q_ref, f_ref, i_ref, lb_ref, sall_ref, do_ref, dqfi_ref, dlb_ref, dst_ref):
        @pl.when(pl.program_id(0) == 0)
        def _():
            dst_ref[...] = jnp.zeros_like(dst_ref)
            dlb_ref[...] = jnp.zeros_like(dlb_ref)

        lbs = _layer_lower_bound(lb_ref, layer)
        tri_lo, tri_up, diag, off = _chunk_masks()
        is_last_row = lax.broadcasted_iota(jnp.int32, (CHUNK, 1), 0) == CHUNK - 1

        def chunk(ci, carry):
            c = nct - 1 - ci
            r0 = pl.multiple_of(c * CHUNK, SUB)
            for hh in range(HEADS):
                cs = slice(hh * HEAD_DIM, (hh + 1) * HEAD_DIM)
                lb = lbs[:, cs]
                q = q_ref[pl.ds(r0, CHUNK), cs]
                v = i_ref[pl.ds(r0, CHUNK), cs]
                sg, s2, f, lf, k = _chunk_gates(f_ref[pl.ds(r0, CHUNK), cs], lb)
                b = _tri_mm(tri_lo, lf)
                e_qs, e_qm, e_km, e_ks, qs, qm, km, ks = _chunk_factors(q, k, b)
                a = _chunk_scores(qs, qm, km, ks, diag, off)
                st = sall_ref[c, hh]
                dst = dst_ref[hh]
                stb, dstb = st.astype(BF), dst.astype(BF)
                vb = v.astype(BF)
                dob = do_ref[pl.ds(r0, CHUNK), cs].astype(BF)
                b_last = b[CHUNK - 1:CHUNK]
                e_b = jnp.exp(b)
                e_last = jnp.exp(b_last)
                e_kl = jnp.exp(b_last - b)
                qc = (q * e_b).astype(BF)
                kdec = (k * e_kl).astype(BF)

                da_full = _dot_nt(dob, vb)
                dv = _dot_tn(a.astype(BF), dob) + _dot_nt(kdec, dstb)
                dq = e_b * _dot(dob, stb)
                da_d = jnp.where(diag, da_full, 0.0).astype(BF)
                dq = dq + e_qm * _dot(da_d, km)
                dk = e_km * _dot_tn(da_d, qm) + e_kl * _dot(vb, dstb)
                for m, kk, ek in zip(off, ks, e_ks):
                    da_i = jnp.where(m, da_full, 0.0).astype(BF)
                    dq = dq + e_qs * _dot(da_i, kk)
                    dk = dk + ek * _dot_tn(da_i, qs)

                st_new = st * e_last + _dot_tn(vb, kdec)
                db = q * dq - k * dk
                db = db + jnp.where(is_last_row, jnp.sum(st_new * dst, axis=0, keepdims=True), 0.0)
                dlf = _tri_mm(tri_up, db)
                df = jnp.where(f > LOG_FLOOR, dlf / f, 0.0)
                dfl = df * (1.0 - lb) * sg * (1.0 - sg) - dk * (1.0 - lb) * s2 * (1.0 - s2)
                dlb_ref[:, cs] += jnp.sum(df * (1.0 - sg) - dk * s2, axis=0, keepdims=True)
                dst_ref[hh] = dst * e_last + _dot_tn(dob, qc)

                dqfi_ref[pl.ds(r0, CHUNK), cs] = dq.astype(BF)
                dqfi_ref[pl.ds(r0, CHUNK), pl.ds(HGRN_W + hh * HEAD_DIM, HEAD_DIM)] = dfl.astype(BF)
                dqfi_ref[pl.ds(r0, CHUNK), pl.ds(2 * HGRN_W + hh * HEAD_DIM, HEAD_DIM)] = dv.astype(BF)
            return carry

        lax.fori_loop(0, nct, chunk, 0, unroll=HGRN_UNROLL if nct % HGRN_UNROLL == 0 else 1)

    rev = lambda width, j: pl.BlockSpec((tm, width), lambda i, j=j: (n_tiles - 1 - i, j))
    outs, exchanged = _hosted_call(
        body, comm, name, (n_tiles,),
        [rev(512, 0), rev(512, 1), rev(512, 2), _whole((DEPTH, HGRN_W)),
         pl.BlockSpec((nct, HEADS, HEAD_DIM, HEAD_DIM), lambda i: (n_tiles - 1 - i, 0, 0, 0)), rev(HGRN_W, 0)],
        (proj, proj, proj, lower_bounds, states, do),
        (jax.ShapeDtypeStruct((t_rows, 3 * HGRN_W), BF), jax.ShapeDtypeStruct((1, HGRN_W), F32)),
        (rev(3 * HGRN_W, 0), _whole((1, HGRN_W))),
        [pltpu.VMEM((HEADS, HEAD_DIM, HEAD_DIM), F32)])
    return (*outs, exchanged)


def _in_bwd(dqfi, dgv, dgab, slabs, h, gain_pre, dh_out, name, comm=None):
    t_rows = h.shape[0]
    tm = _tile(t_rows, ROW_TILE)
    c1 = 3 * HGRN_W
    c2 = c1 + HGRN_W + POOL_W

    def body(d1_ref, d2_ref, d3_ref, slabs_ref, h_ref, gp_ref, dho_ref, dh_ref, dg_ref, w_ref, sems):
        _fetch_weights(slabs_ref, ['w_in'], (w_ref,), sems)

        @pl.when(pl.program_id(0) == 0)
        def _():
            dg_ref[...] = jnp.zeros_like(dg_ref)

        du = (_dot(d1_ref[...], w_ref[0:c1, :]) + _dot(d2_ref[...], w_ref[c1:c2, :])
              + _dot(d3_ref[...], w_ref[c2:IN_COLS, :]))
        dx, dgain = _rms_bwd(du, h_ref[...], gp_ref[...])
        dh_ref[...] = dho_ref[...] + dx
        dg_ref[...] += dgain

    rows = lambda width: pl.BlockSpec((tm, width), lambda i: (i, 0))
    outs, exchanged = _hosted_call(
        body, comm, name, (t_rows // tm,),
        [rows(c1), rows(c2 - c1), rows(IN_COLS - c2), HBM_SPEC, rows(D_MODEL), _whole((1, D_MODEL)), rows(D_MODEL)],
        (dqfi, dgv, dgab, slabs, h, gain_pre, dh_out),
        (jax.ShapeDtypeStruct((t_rows, D_MODEL), F32), jax.ShapeDtypeStruct((1, D_MODEL), F32)),
        (rows(D_MODEL), _whole((1, D_MODEL))),
        _weight_scratch(['w_in']))
    return (*outs, exchanged)


def _lower_bound_grad(lower_bounds, dlbs, name):
    def body(lb_ref, d_ref, o_ref):
        g0, g1 = _softmax2(lb_ref)
        bound = (g0 + g1) - g0
        inside = (bound > 0.0) & (bound < 1.0)
        dg1 = jnp.where(inside, d_ref[1:2, :], 0.0)
        inner = g1 * dg1
        o_ref[0:1, :] = g0 * (0.0 - inner)
        o_ref[1:2, :] = g1 * (dg1 - inner)

    return pl.pallas_call(body, name=name, out_shape=jax.ShapeDtypeStruct(lower_bounds.shape, F32))(lower_bounds, dlbs)


def _adamw(w, g, m, v, name):
    r, c = w.shape
    tr = r if (r % 8 or r <= 512) else _tile(r, 512, 8)
    c1 = 1.0 - ADAM_B1 ** ADAM_STEP
    c2 = 1.0 - ADAM_B2 ** ADAM_STEP

    def body(w_ref, g_ref, m_ref, v_ref, d_ref, nm_ref, nv_ref):
        gg = g_ref[...]
        nm = ADAM_B1 * m_ref[...] + (1.0 - ADAM_B1) * gg
        nv = ADAM_B2 * v_ref[...] + (1.0 - ADAM_B2) * (gg * gg)
        d_ref[...] = -ADAM_LR * ((nm / c1) / (jnp.sqrt(nv / c2) + ADAM_EPS) + ADAM_WD * w_ref[...])
        nm_ref[...] = nm
        nv_ref[...] = nv

    blk = pl.BlockSpec((tr, c), lambda i: (i, 0))
    shp = jax.ShapeDtypeStruct((r, c), F32)
    return pl.pallas_call(
        body, name=name, out_shape=(shp, shp, shp), grid=(r // tr,),
        in_specs=[blk, blk, blk, blk], out_specs=(blk, blk, blk),
        compiler_params=_params(("parallel",)),
    )(w, g, m, v)


def _as2d(a):
    return a.reshape(-1, a.shape[-1])


def _layer_slab(w, l):
    t = lambda a: jnp.swapaxes(a, 0, 1)
    parts = [t(w['w_in'][l]), jnp.concatenate([t(w['w_branch_hgrn'][l]), t(w['w_branch_pool'][l])], axis=1),
             w['w_out'][l], t(w['ffn_w_gate'][l]), t(w['ffn_w_up'][l]), w['ffn_w_down'][l]]
    return jnp.concatenate(parts, axis=0).astype(BF)


def _slab_grads(sums):
    t = lambda a: jnp.swapaxes(a, 0, 1)
    wb = sums['wb']
    return {'w_in': t(sums['w_in']), 'w_branch_hgrn': t(wb[:, :HGRN_W]), 'w_branch_pool': t(wb[:, HGRN_W:]),
            'w_out': sums['w_out'], 'ffn_w_gate': t(sums['wg']), 'ffn_w_up': t(sums['wu']), 'ffn_w_down': sums['wd']}


def _train_step(x, target, w, m, v):
    slabs = [_all_gather(_layer_slab(w, l), f"gather_weights_{l}") for l in range(DEPTH)]
    f32_shapes = [w[n].shape for n in F32_GATHERED]
    gathered32 = _all_gather(_pack([w[n] for n in F32_GATHERED], F32, 8), "gather_meta_conv")
    full = {n: _blocks_to_full(blk, SHARD_AXIS[n]) for n, blk in zip(F32_GATHERED, _unpack(gathered32, f32_shapes))}
    row = lambda name, l: w[name][l][None]

    h = jnp.concatenate([full['meta_tokens'], x], axis=0)
    saved = []
    for l in range(DEPTH):
        s = {'h_in': h}
        s['proj'] = _norm_matmul(h, row('mix_norm_pre', l), slabs[l], ['w_in'], F32, f"in_proj_{l}")
        s['o'], s['states'] = _hgrn_fwd(s['proj'], w['hgrn_lower_bounds'], l, f"hgrn_fwd_{l}")
        h, s['r'], s['a'], s['p'] = _mix_merge(
            s['proj'], s['o'], h, row('hgrn_out_norm', l), w['pool_proj'][l], row('pool_scale', l), slabs[l],
            row('mix_norm_post', l), f"mix_merge_{l}")
        s['h_mid'] = h
        s['gu'] = _norm_matmul(h, row('ffn_norm_pre', l), slabs[l], ['wg', 'wu'], BF, f"ffn_proj_{l}")
        h, s['y'] = _ffn_down(s['gu'], h, full['ffn_conv_w'][l], row('ffn_conv_b', l), slabs[l],
                              row('ffn_norm_post', l), f"ffn_down_{l}")
        saved.append(s)

    dh, loss_cols = _loss_grad(h, jnp.pad(target, ((N_META, 0), (0, 0))), "loss_grad")
    loss = jnp.sum(loss_cols)

    small = {n: [None] * DEPTH for n in REPLICATED + ['ffn_conv_w']}
    exchanged = [{} for _ in range(DEPTH)]
    pending = None
    for l in reversed(range(DEPTH)):
        s = saved[l]
        g = {}
        (dy, act, dgt, dup, small['ffn_norm_post'][l], small['ffn_conv_w'][l], small['ffn_conv_b'][l],
         done) = _ffn_bwd_down(dh, s['y'], row('ffn_norm_post', l), s['gu'], full['ffn_conv_w'][l],
                               row('ffn_conv_b', l), slabs[l], f"ffn_bwd_down_{l}", comm=pending)
        if pending is not None:
            exchanged[l + 1]['in'] = done
        g['d_down'] = _dw([act], [dy], f"dw_down_{l}")
        dh, dgu, small['ffn_norm_pre'][l] = _ffn_bwd_up(
            dgt, dup, full['ffn_conv_w'][l], slabs[l], s['h_mid'], row('ffn_norm_pre', l), dh, f"ffn_bwd_up_{l}")
        g['d_gu'] = _dw([dgu], [s['h_mid']], f"dw_gate_up_{l}", gain=row('ffn_norm_pre', l))

        dr, z, dya, dyp, da, dp, dgab, small['mix_norm_post'][l] = _mix_bwd_a(
            dh, s['r'], row('mix_norm_post', l), s['proj'], s['a'], s['p'], slabs[l], f"mix_bwd_a_{l}")
        g['d_out'] = _dw([z], [dr], f"dw_out_{l}")
        g['d_b'] = _dw([dya, dyp], [s['a'], s['p']], f"dw_branch_{l}")
        do, dgv, small['hgrn_out_norm'][l], small['pool_proj'][l], small['pool_scale'][l] = _mix_bwd_b(
            da, dp, s['proj'], s['o'], row('hgrn_out_norm', l), w['pool_proj'][l], row('pool_scale', l),
            f"mix_bwd_b_{l}")
        dqfi, small['hgrn_lower_bounds'][l], exchanged[l]['rest'] = _hgrn_bwd(
            s['proj'], w['hgrn_lower_bounds'], l, s['states'], do, f"hgrn_bwd_{l}",
            comm=_GradExchange(REST_KEYS, g))
        g['d_in'] = _dw([dqfi, dgv, dgab], [s['h_in']], f"dw_in_{l}", gain=row('mix_norm_pre', l))
        pending = _GradExchange(['w_in'], g)
        if l == 0:
            dh, small['mix_norm_pre'][l], exchanged[l]['in'] = _in_bwd(
                dqfi, dgv, dgab, slabs[l], s['h_in'], row('mix_norm_pre', l), dh, f"in_bwd_{l}", comm=pending)
        else:
            dh, small['mix_norm_pre'][l], _ = _in_bwd(
                dqfi, dgv, dgab, slabs[l], s['h_in'], row('mix_norm_pre', l), dh, f"in_bwd_{l}")

    grad_x = dh[N_META:]

    per_layer = []
    for l in range(DEPTH):
        sums = {}
        for part, keys in (('in', ['w_in']), ('rest', REST_KEYS)):
            recv, own = exchanged[l][part]
            total = _sum_parts(own, recv, f"sum_grads_{part}_{l}")
            off = 0
            for k in keys:
                sums[k] = total[off:off + W_ROWS[k][1]]
                off += W_ROWS[k][1]
        per_layer.append(_slab_grads(sums))
    grads = {n: jnp.stack([per_layer[l][n] for l in range(DEPTH)]) for n in per_layer[0]}

    stack = lambda n: jnp.stack(small[n]) if small[n][0].shape[0] != 1 else jnp.concatenate(small[n], axis=0)
    names = list(small) + ['meta_tokens']
    partial = {n: stack(n) for n in small}
    partial['meta_tokens'] = dh[:N_META]
    gathered = _all_gather(_pack([partial[n] for n in names], F32, 8), "gather_small_grads")
    rep = dict(zip(names, _unpack(_sum8(gathered, "sum_small_grads"), [partial[n].shape for n in names])))
    rep['hgrn_lower_bounds'] = _lower_bound_grad(w['hgrn_lower_bounds'], rep['hgrn_lower_bounds'], "lower_bound_grad")
    me = 4 * lax.axis_index("x") + 2 * lax.axis_index("y") + lax.axis_index("c")
    for n in F32_GATHERED:
        width = w[n].shape[SHARD_AXIS[n]]
        rep[n] = lax.dynamic_slice_in_dim(rep[n], me * width, width, axis=SHARD_AXIS[n])
    grads.update(rep)

    delta, new_m, new_v = {}, {}, {}
    for n in WEIGHT_NAMES:
        shape = w[n].shape
        d2, m2, v2 = _adamw(_as2d(w[n]), _as2d(grads[n]), _as2d(m[n]), _as2d(v[n]), f"adamw_{n}")
        delta[n], new_m[n], new_v[n] = d2.reshape(shape), m2.reshape(shape), v2.reshape(shape)
    return loss, grad_x, grads, delta, new_m, new_v


def kernel(x, meta_tokens, mix_norm_pre, mix_norm_post, w_in, hgrn_lower_bounds, hgrn_out_norm, w_branch_hgrn, pool_proj, pool_scale, w_branch_pool, w_out, ffn_norm_pre, ffn_norm_post, ffn_w_gate, ffn_w_up, ffn_conv_w, ffn_conv_b, ffn_w_down, loss_target, m_meta_tokens, m_mix_norm_pre, m_mix_norm_post, m_w_in, m_hgrn_lower_bounds, m_hgrn_out_norm, m_w_branch_hgrn, m_pool_proj, m_pool_scale, m_w_branch_pool, m_w_out, m_ffn_norm_pre, m_ffn_norm_post, m_ffn_w_gate, m_ffn_w_up, m_ffn_conv_w, m_ffn_conv_b, m_ffn_w_down, v_meta_tokens, v_mix_norm_pre, v_mix_norm_post, v_w_in, v_hgrn_lower_bounds, v_hgrn_out_norm, v_w_branch_hgrn, v_pool_proj, v_pool_scale, v_w_branch_pool, v_w_out, v_ffn_norm_pre, v_ffn_norm_post, v_ffn_w_gate, v_ffn_w_up, v_ffn_conv_w, v_ffn_conv_b, v_ffn_w_down):
    w = dict(zip(WEIGHT_NAMES, (meta_tokens, mix_norm_pre, mix_norm_post, w_in, hgrn_lower_bounds, hgrn_out_norm,
                                w_branch_hgrn, pool_proj, pool_scale, w_branch_pool, w_out, ffn_norm_pre,
                                ffn_norm_post, ffn_w_gate, ffn_w_up, ffn_conv_w, ffn_conv_b, ffn_w_down)))
    m = dict(zip(WEIGHT_NAMES, (m_meta_tokens, m_mix_norm_pre, m_mix_norm_post, m_w_in, m_hgrn_lower_bounds,
                                m_hgrn_out_norm, m_w_branch_hgrn, m_pool_proj, m_pool_scale, m_w_branch_pool, m_w_out,
                                m_ffn_norm_pre, m_ffn_norm_post, m_ffn_w_gate, m_ffn_w_up, m_ffn_conv_w,
                                m_ffn_conv_b, m_ffn_w_down)))
    v = dict(zip(WEIGHT_NAMES, (v_meta_tokens, v_mix_norm_pre, v_mix_norm_post, v_w_in, v_hgrn_lower_bounds,
                                v_hgrn_out_norm, v_w_branch_hgrn, v_pool_proj, v_pool_scale, v_w_branch_pool, v_w_out,
                                v_ffn_norm_pre, v_ffn_norm_post, v_ffn_w_gate, v_ffn_w_up, v_ffn_conv_w,
                                v_ffn_conv_b, v_ffn_w_down)))
    loss_local, grad_x, grads, delta, new_m, new_v = _train_step(x[0], loss_target[0], w, m, v)
    loss = lax.psum(loss_local, ("x", "y", "c"))
    return (loss, grad_x[None], *[grads[n] for n in WEIGHT_NAMES], *[delta[n] for n in WEIGHT_NAMES],
            *[new_m[n] for n in WEIGHT_NAMES], *[new_v[n] for n in WEIGHT_NAMES])
```

```python
import jax
import jax.numpy as jnp
from jax import lax
from jax.experimental import pallas as pl
from jax.experimental.pallas import tpu as pltpu

F32 = jnp.float32
BF = jnp.bfloat16

D_MODEL = 1024
N_META = 16
DEPTH = 2
HEADS = 4
HEAD_DIM = 128
HGRN_W = 512
POOL_W = 512
POOL_WINDOWS = (2, 4, 8, 16)
D_FF = 2816
IN_COLS = 4608
EPS = 1e-6
LOG_FLOOR = 1e-30
N_DEV = 8

ADAM_LR = 0.001
ADAM_B1 = 0.9
ADAM_B2 = 0.999
ADAM_EPS = 1e-08
ADAM_WD = 0.01
ADAM_STEP = 10

SUB = 16
CHUNK = 48
HGRN_UNROLL = 3
EXP_CLAMP = 80.0
ROW_TILE = 432
FFN_TILE = 144
HALO = 16
VMEM_LIMIT = 56 * 1024 * 1024
MESH = pl.DeviceIdType.MESH

WEIGHT_NAMES = ['meta_tokens', 'mix_norm_pre', 'mix_norm_post', 'w_in', 'hgrn_lower_bounds', 'hgrn_out_norm',
                'w_branch_hgrn', 'pool_proj', 'pool_scale', 'w_branch_pool', 'w_out', 'ffn_norm_pre', 'ffn_norm_post',
                'ffn_w_gate', 'ffn_w_up', 'ffn_conv_w', 'ffn_conv_b', 'ffn_w_down']
SHARD_AXIS = {'meta_tokens': 1, 'w_in': 2, 'w_branch_hgrn': 2, 'w_branch_pool': 2, 'w_out': 1,
              'ffn_w_gate': 2, 'ffn_w_up': 2, 'ffn_conv_w': 2, 'ffn_w_down': 1}
F32_GATHERED = ['meta_tokens', 'ffn_conv_w']
REPLICATED = [n for n in WEIGHT_NAMES if n not in SHARD_AXIS]

W_ROWS = {'w_in': (0, 576),
          'wb': (576, 128),
          'w_out': (704, 128),
          'wg': (832, 352),
          'wu': (1184, 352),
          'wd': (1536, 352)}
SLAB_ROWS = 1888
REST_KEYS = ['wb', 'w_out', 'wg', 'wu', 'wd']


def _params(sem=None):
    return pltpu.CompilerParams(dimension_semantics=sem, vmem_limit_bytes=VMEM_LIMIT)


def _tile(total, pref, mult=16):
    best = None
    for t in range(mult, min(total, pref) + 1, mult):
        if total % t == 0:
            best = t
    assert best is not None, (total, pref, mult)
    return best


def _whole(shape):
    return pl.BlockSpec(shape, lambda *_: (0,) * len(shape))


def _cols(tm, width, j):
    return pl.BlockSpec((tm, width), lambda i, j=j: (i, j))


def _dot(a, b):
    return jnp.dot(a, b, preferred_element_type=F32)


def _dot_nt(a, b):
    return lax.dot_general(a, b, (((1,), (1,)), ((), ())), preferred_element_type=F32)


def _dot_tn(a, b):
    return lax.dot_general(a, b, (((0,), (0,)), ((), ())), preferred_element_type=F32)


def _rms_fwd(x, g):
    r = lax.rsqrt(jnp.mean(x * x, axis=-1, keepdims=True) + EPS)
    return x * r * g


def _rms_bwd(dy, x, g):
    r = lax.rsqrt(jnp.mean(x * x, axis=-1, keepdims=True) + EPS)
    xh = x * r
    dyg = dy * g
    dx = r * (dyg - xh * jnp.mean(dyg * xh, axis=-1, keepdims=True))
    return dx, jnp.sum(dy * xh, axis=0, keepdims=True)


_GELU_C = 0.7978845608028654
_GELU_A = 0.044715


def _gelu_and_grad(x):
    x2 = x * x
    t = jnp.tanh(x * (_GELU_C + (_GELU_C * _GELU_A) * x2))
    u = 1.0 + t
    hx = 0.5 * x
    return hx * u, 0.5 * u + (hx * (1.0 - t * t)) * (_GELU_C + (3.0 * _GELU_C * _GELU_A) * x2)


def _split3(x):
    x1 = x.astype(BF)
    r1 = x - x1.astype(F32)
    x2 = r1.astype(BF)
    x3 = (r1 - x2.astype(F32)).astype(BF)
    return x1, x2, x3


def _tri_mm(tri, x):
    x1, x2, x3 = _split3(x)
    return _dot(tri, x1) + _dot(tri, x2) + _dot(tri, x3)


def _softmax2(lb_ref):
    l0 = lb_ref[0:1, :]
    l1 = lb_ref[1:2, :]
    m = jnp.maximum(l0, l1)
    e0 = jnp.exp(l0 - m)
    e1 = jnp.exp(l1 - m)
    return e0 / (e0 + e1), e1 / (e0 + e1)


def _layer_lower_bound(lb_ref, layer):
    g0, g1 = _softmax2(lb_ref)
    if layer == 0:
        return jnp.clip(g0 - g0, 0.0, 1.0)
    return jnp.clip((g0 + g1) - g0, 0.0, 1.0)


def _all_gather(x, name):
    n, w = x.shape

    def body(x_ref, out_ref, send_sems, recv_sems, local_sem):
        mx, my, mc = lax.axis_index("x"), lax.axis_index("y"), lax.axis_index("c")
        me, sibling = (mx, my, mc), (mx, my, 1 - mc)
        chips = [(1 - mx, my), (mx, 1 - my), (1 - mx, 1 - my)]

        def rows(px, py, pc):
            return out_ref.at[4 * px + 2 * py + pc]

        def copy(k, block, to, src=None):
            return pltpu.make_async_remote_copy(
                src_ref=rows(*block) if src is None else src, dst_ref=rows(*block),
                send_sem=send_sems.at[k], recv_sem=recv_sems.at[k], device_id=to, device_id_type=MESH)

        mine = pltpu.make_async_copy(x_ref, rows(*me), local_sem)
        mine.start()
        first = [copy(0, me, sibling, src=x_ref)]
        first += [copy(1 + j, me, (*chip, mc), src=x_ref) for j, chip in enumerate(chips)]
        for cp in first:
            cp.start()
        passed = [copy(4 + j, (*chip, mc), sibling) for j, chip in enumerate(chips)]
        for j, chip in enumerate(chips):
            copy(1 + j, (*chip, mc), me).wait_recv()
            passed[j].start()
        copy(0, sibling, me).wait_recv()
        for j, chip in enumerate(chips):
            copy(4 + j, (*chip, 1 - mc), me).wait_recv()
        for cp in first + passed:
            cp.wait_send()
        mine.wait()

    return pl.pallas_call(
        body, name=name,
        out_shape=jax.ShapeDtypeStruct((N_DEV, n, w), x.dtype),
        in_specs=[pl.BlockSpec(memory_space=pl.ANY)],
        out_specs=pl.BlockSpec(memory_space=pl.ANY),
        scratch_shapes=[pltpu.SemaphoreType.DMA((7,)), pltpu.SemaphoreType.DMA((7,)), pltpu.SemaphoreType.DMA],
    )(x)


GRAD_SRC = {'w_in': ('d_in', 0), 'wb': ('d_b', 0), 'w_out': ('d_out', 0), 'wg': ('d_gu', 0), 'wu': ('d_gu', D_FF),
            'wd': ('d_down', 0)}


class _GradExchange:
    def __init__(self, keys, grads):
        self.names = sorted({GRAD_SRC[k][0] for k in keys})
        self.arrays = [grads[n][0] for n in self.names] + [grads[n][1] for n in self.names]
        self.pieces, off = [], 0
        for k in keys:
            name, base = GRAD_SRC[k]
            rows = W_ROWS[k][1]
            self.pieces.append((self.names.index(name), base, rows, off))
            off += rows
        self.keys, self.rows = keys, off
        self.out_shape = (jax.ShapeDtypeStruct((N_DEV - 1, off, D_MODEL), BF), jax.ShapeDtypeStruct((off, D_MODEL), F32))
        self.scratch = [pltpu.SemaphoreType.DMA((N_DEV - 1,)), pltpu.SemaphoreType.DMA((N_DEV - 1,)),
                        pltpu.SemaphoreType.DMA((len(self.pieces),))]

    def _local(self, ins, outs, scr):
        mx, my, mc = lax.axis_index("x"), lax.axis_index("y"), lax.axis_index("c")
        me = 4 * mx + 2 * my + mc
        return [pltpu.make_async_copy(ins[ai].at[pl.ds(pl.multiple_of(base + rows * me, 8), rows)],
                                      outs[1].at[pl.ds(off, rows)], scr[2].at[k])
                for k, (ai, base, rows, off) in enumerate(self.pieces)]

    def start(self, ins, outs, scr):
        mx, my, mc = lax.axis_index("x"), lax.axis_index("y"), lax.axis_index("c")
        bf16 = ins[len(self.names):]
        for cp in self._local(ins, outs, scr):
            cp.start()
        for d in range(1, N_DEV):
            px = (1 - mx) if (d >> 2) & 1 else mx
            py = (1 - my) if (d >> 1) & 1 else my
            pc = (1 - mc) if d & 1 else mc
            peer = 4 * px + 2 * py + pc
            for ai, base, rows, off in self.pieces:
                pltpu.make_async_remote_copy(
                    src_ref=bf16[ai].at[pl.ds(pl.multiple_of(base + rows * peer, 16), rows)],
                    dst_ref=outs[0].at[d - 1, pl.ds(off, rows)], send_sem=scr[0].at[d - 1], recv_sem=scr[1].at[d - 1],
                    device_id=(px, py, pc), device_id_type=MESH).start()

    def wait(self, ins, outs, scr):
        me = (lax.axis_index("x"), lax.axis_index("y"), lax.axis_index("c"))
        for d in range(1, N_DEV):
            slot = pltpu.make_async_remote_copy(
                src_ref=outs[0].at[d - 1], dst_ref=outs[0].at[d - 1], send_sem=scr[0].at[d - 1],
                recv_sem=scr[1].at[d - 1], device_id=me, device_id_type=MESH)
            slot.wait_recv()
            slot.wait_send()
        for cp in self._local(ins, outs, scr):
            cp.wait()


class _SlabGather:
    def __init__(self, slab, row0, rows):
        self.arrays = [slab]
        self.row0, self.rows = row0, rows
        self.out_shape = (jax.ShapeDtypeStruct((N_DEV, rows, D_MODEL), slab.dtype),)
        self.scratch = [pltpu.SemaphoreType.DMA((7,)), pltpu.SemaphoreType.DMA((7,)), pltpu.SemaphoreType.DMA]

    def _parts(self, ins, outs, scr):
        mx, my, mc = lax.axis_index("x"), lax.axis_index("y"), lax.axis_index("c")
        me, sibling = (mx, my, mc), (mx, my, 1 - mc)
        chips = [(1 - mx, my), (mx, 1 - my), (1 - mx, 1 - my)]
        mine_ref = ins[0].at[pl.ds(self.row0, self.rows)]
        block = lambda px, py, pc: outs[0].at[4 * px + 2 * py + pc]

        def copy(k, blk, to, src=None):
            return pltpu.make_async_remote_copy(
                src_ref=block(*blk) if src is None else src, dst_ref=block(*blk),
                send_sem=scr[0].at[k], recv_sem=scr[1].at[k], device_id=to, device_id_type=MESH)

        local = pltpu.make_async_copy(mine_ref, block(*me), scr[2])
        first = [copy(0, me, sibling, src=mine_ref)]
        first += [copy(1 + j, me, (*chip, mc), src=mine_ref) for j, chip in enumerate(chips)]
        passed = [copy(4 + j, (*chip, mc), sibling) for j, chip in enumerate(chips)]
        landed = [copy(1 + j, (*chip, mc), me) for j, chip in enumerate(chips)]
        from_sibling = [copy(0, sibling, me)] + [copy(4 + j, (*chip, 1 - mc), me) for j, chip in enumerate(chips)]
        return local, first, passed, landed, from_sibling

    def start(self, ins, outs, scr):
        local, first, _, _, _ = self._parts(ins, outs, scr)
        local.start()
        for cp in first:
            cp.start()

    def wait(self, ins, outs, scr):
        local, first, passed, landed, from_sibling = self._parts(ins, outs, scr)
        for arrived, forward in zip(landed, passed):
            arrived.wait_recv()
            forward.start()
        for cp in from_sibling:
            cp.wait_recv()
        for cp in first + passed:
            cp.wait_send()
        local.wait()


def _hosted_call(body, comm, name, grid, in_specs, args, out_shape, out_specs, scratch_shapes):
    if comm is None:
        return pl.pallas_call(body, name=name, grid=grid, in_specs=in_specs, out_specs=out_specs, out_shape=out_shape,
                              scratch_shapes=scratch_shapes, compiler_params=_params(("arbitrary",)))(*args), None
    n_in, n_out, n_scr, n_c = len(in_specs), len(out_shape), len(scratch_shapes), len(comm.arrays)
    n_co = len(comm.out_shape)
    steps = grid[0]

    def hosted(*refs):
        ins, cins = refs[:n_in], refs[n_in:n_in + n_c]
        outs = refs[n_in + n_c:n_in + n_c + n_out]
        couts = refs[n_in + n_c + n_out:n_in + n_c + n_out + n_co]
        scr = refs[n_in + n_c + n_out + n_co:n_in + n_c + n_out + n_co + n_scr]
        cscr = refs[n_in + n_c + n_out + n_co + n_scr:]

        @pl.when(pl.program_id(0) == 0)
        def _():
            comm.start(cins, couts, cscr)

        body(*ins, *outs, *scr)

        @pl.when(pl.program_id(0) == steps - 1)
        def _():
            comm.wait(cins, couts, cscr)

    res = pl.pallas_call(
        hosted, name=name, grid=grid, in_specs=list(in_specs) + [HBM_SPEC] * n_c,
        out_specs=tuple(out_specs) + (HBM_SPEC,) * n_co, out_shape=tuple(out_shape) + tuple(comm.out_shape),
        scratch_shapes=list(scratch_shapes) + comm.scratch, compiler_params=_params(("arbitrary",)),
    )(*args, *comm.arrays)
    return res[:n_out], res[n_out:]


def _sum_parts(own, recv, name):
    n, w = own.shape
    tn = _tile(n, 256, 16)

    def body(own_ref, p_ref, o_ref):
        acc = own_ref[...]
        for d in range(N_DEV - 1):
            acc = acc + p_ref[d].astype(F32)
        o_ref[...] = acc

    return pl.pallas_call(
        body, name=name, out_shape=jax.ShapeDtypeStruct((n, w), F32), grid=(n // tn,),
        in_specs=[pl.BlockSpec((tn, w), lambda i: (i, 0)), pl.BlockSpec((N_DEV - 1, tn, w), lambda i: (0, i, 0))],
        out_specs=pl.BlockSpec((tn, w), lambda i: (i, 0)),
        compiler_params=_params(("parallel",)),
    )(own, recv)


def _sum8(parts, name):
    _, n, w = parts.shape
    tn = _tile(n, 256 if w > 128 else 1024, 8)

    def body(p_ref, o_ref):
        acc = p_ref[0]
        for d in range(1, N_DEV):
            acc = acc + p_ref[d]
        o_ref[...] = acc

    return pl.pallas_call(
        body, name=name, out_shape=jax.ShapeDtypeStruct((n, w), F32), grid=(n // tn,),
        in_specs=[pl.BlockSpec((N_DEV, tn, w), lambda i: (0, i, 0))],
        out_specs=pl.BlockSpec((tn, w), lambda i: (i, 0)),
        compiler_params=_params(("parallel",)),
    )(parts)


def _pack(arrays, dtype, row_mult):
    flat = jnp.concatenate([a.astype(dtype).reshape(-1) for a in arrays])
    pad = (-flat.shape[0]) % (128 * row_mult)
    if pad:
        flat = jnp.concatenate([flat, jnp.zeros((pad,), dtype)])
    return flat.reshape(-1, 128)


def _unpack(flat2d, shapes):
    lead = flat2d.shape[:-2]
    flat = flat2d.reshape(lead + (-1,))
    out, off = [], 0
    for s in shapes:
        size = 1
        for d in s:
            size *= d
        out.append(flat[..., off:off + size].reshape(lead + tuple(s)))
        off += size
    return out


def _blocks_to_full(blocks, axis):
    moved = jnp.moveaxis(blocks, 0, axis)
    shape = list(moved.shape)
    shape[axis:axis + 2] = [shape[axis] * shape[axis + 1]]
    return moved.reshape(shape)


def _weight_scratch(keys):
    return ([pltpu.VMEM((N_DEV * W_ROWS[k][1], D_MODEL), BF) for k in keys]
            + [pltpu.SemaphoreType.DMA((N_DEV * len(keys),))])


def _fetch_weights(slabs_ref, keys, bufs, sems, base=0):
    @pl.when(pl.program_id(0) == 0)
    def _():
        copies = []
        for ki, (key, buf) in enumerate(zip(keys, bufs)):
            row0, rows = W_ROWS[key][0] - base, W_ROWS[key][1]
            for j in range(N_DEV):
                copies.append(pltpu.make_async_copy(slabs_ref.at[j, pl.ds(row0, rows)],
                                                    buf.at[pl.ds(j * rows, rows)], sems.at[ki * N_DEV + j]))
        for cp in copies:
            cp.start()
        for cp in copies:
            cp.wait()


HBM_SPEC = pl.BlockSpec(memory_space=pl.ANY)


def _norm_matmul(h, gain, slabs, keys, out_dtype, name, comm=None):
    t_rows, d = h.shape
    widths = [N_DEV * W_ROWS[k][1] for k in keys]
    tm = _tile(t_rows, ROW_TILE)

    def body(h_ref, g_ref, slabs_ref, o_ref, *scratch):
        bufs, sems = scratch[:-1], scratch[-1]
        _fetch_weights(slabs_ref, keys, bufs, sems, slabs[1])
        u = _rms_fwd(h_ref[...], g_ref[...]).astype(BF)
        off = 0
        for buf, n in zip(bufs, widths):
            o_ref[:, off:off + n] = _dot_nt(u, buf[...]).astype(out_dtype)
            off += n

    outs, moved = _hosted_call(
        body, comm, name, (t_rows // tm,),
        [pl.BlockSpec((tm, d), lambda i: (i, 0)), _whole((1, d)), HBM_SPEC], (h, gain, slabs[0]),
        (jax.ShapeDtypeStruct((t_rows, sum(widths)), out_dtype),),
        (pl.BlockSpec((tm, sum(widths)), lambda i: (i, 0)),), _weight_scratch(keys))
    return outs[0], moved


def _chunk_masks():
    row = lax.broadcasted_iota(jnp.int32, (CHUNK, CHUNK), 0)
    col = lax.broadcasted_iota(jnp.int32, (CHUNK, CHUNK), 1)
    tri_lo = (col <= row).astype(BF)
    tri_up = (col >= row).astype(BF)
    rb = jnp.right_shift(row, 4)
    cb = jnp.right_shift(col, 4)
    diag = (rb == cb) & (col <= row)
    off = [(rb == i) & (col < SUB * i) for i in range(1, CHUNK // SUB)]
    return tri_lo, tri_up, diag, off


def _chunk_gates(fl, lb):
    sg = jax.nn.sigmoid(fl)
    s2 = jax.nn.sigmoid(-fl)
    f = lb + (1.0 - lb) * sg
    lf = jnp.log(jnp.maximum(f, LOG_FLOOR))
    k = (1.0 - lb) * s2
    return sg, s2, f, lf, k


def _chunk_factors(q, k, b):
    nb = CHUNK // SUB
    rq = jnp.concatenate([jnp.broadcast_to(b[SUB * i:SUB * i + 1], (SUB, HEAD_DIM)) for i in range(nb)], axis=0)
    mq = jnp.concatenate([jnp.broadcast_to(b[SUB * i + SUB // 2:SUB * i + SUB // 2 + 1], (SUB, HEAD_DIM))
                          for i in range(nb)], axis=0)
    e_qs = jnp.exp(b - rq)
    e_qm = jnp.exp(jnp.minimum(b - mq, EXP_CLAMP))
    e_km = jnp.exp(jnp.minimum(mq - b, EXP_CLAMP))
    e_ks = [jnp.exp(jnp.minimum(b[SUB * i:SUB * i + 1] - b, 0.0)) for i in range(1, nb)]
    qs = (q * e_qs).astype(BF)
    qm = (q * e_qm).astype(BF)
    km = (k * e_km).astype(BF)
    ks = [(k * e).astype(BF) for e in e_ks]
    return e_qs, e_qm, e_km, e_ks, qs, qm, km, ks


def _chunk_scores(qs, qm, km, ks, diag, off):
    a = jnp.where(diag, _dot_nt(qm, km), 0.0)
    for m, kk in zip(off, ks):
        a = jnp.where(m, _dot_nt(qs, kk), a)
    return a


def _hgrn_fwd(proj, lower_bounds, layer, name, comm=None):
    t_rows = proj.shape[0]
    tm = _tile(t_rows, ROW_TILE, CHUNK)
    nct = tm // CHUNK

    def body(q_ref, f_ref, i_ref, lb_ref, o_ref, sall_ref, st_ref):
        @pl.when(pl.program_id(0) == 0)
        def _():
            st_ref[...] = jnp.zeros_like(st_ref)

        lbs = _layer_lower_bound(lb_ref, layer)
        tri_lo, _, diag, off = _chunk_masks()

        group = HGRN_UNROLL if nct % HGRN_UNROLL == 0 else 1

        def chunk_group(gi, carry):
            pairs = [(u, hh) for u in range(group) for hh in range(HEADS)]
            rows = [pl.ds(pl.multiple_of((gi * group + u) * CHUNK, SUB), CHUNK) for u in range(group)]
            cols = [slice(hh * HEAD_DIM, (hh + 1) * HEAD_DIM) for hh in range(HEADS)]
            q = {p: q_ref[rows[p[0]], cols[p[1]]] for p in pairs}
            vb = {p: i_ref[rows[p[0]], cols[p[1]]].astype(BF) for p in pairs}
            gates = {p: _chunk_gates(f_ref[rows[p[0]], cols[p[1]]], lbs[:, cols[p[1]]]) for p in pairs}
            b = {p: _tri_mm(tri_lo, gates[p][3]) for p in pairs}
            fac = {p: _chunk_factors(q[p], gates[p][4], b[p]) for p in pairs}
            a = {p: _chunk_scores(*fac[p][4:], diag, off) for p in pairs}
            o_intra = {p: _dot(a[p].astype(BF), vb[p]) for p in pairs}
            upd = {p: _dot_tn(vb[p], (gates[p][4] * jnp.exp(b[p][CHUNK - 1:CHUNK] - b[p])).astype(BF)) for p in pairs}
            qc = {p: (q[p] * jnp.exp(b[p])).astype(BF) for p in pairs}
            for hh in range(HEADS):
                st = st_ref[hh]
                for u in range(group):
                    p = (u, hh)
                    sall_ref[gi * group + u, hh] = st
                    o_ref[rows[u], cols[hh]] = _dot_nt(qc[p], st.astype(BF)) + o_intra[p]
                    st = st * jnp.exp(b[p][CHUNK - 1:CHUNK]) + upd[p]
                st_ref[hh] = st
            return carry

        lax.fori_loop(0, nct // group, chunk_group, 0)

    outs, moved = _hosted_call(
        body, comm, name, (t_rows // tm,),
        [_cols(tm, 512, 0), _cols(tm, 512, 1), _cols(tm, 512, 2), _whole((DEPTH, HGRN_W))],
        (proj, proj, proj, lower_bounds),
        (jax.ShapeDtypeStruct((t_rows, HGRN_W), F32),
         jax.ShapeDtypeStruct((t_rows // CHUNK, HEADS, HEAD_DIM, HEAD_DIM), F32)),
        (pl.BlockSpec((tm, HGRN_W), lambda i: (i, 0)),
         pl.BlockSpec((nct, HEADS, HEAD_DIM, HEAD_DIM), lambda i: (i, 0, 0, 0))),
        [pltpu.VMEM((HEADS, HEAD_DIM, HEAD_DIM), F32)])
    return (*outs, moved)


def _head_norm(o):
    outs, rs = [], []
    for hh in range(HEADS):
        oh = o[:, hh * HEAD_DIM:(hh + 1) * HEAD_DIM]
        r = lax.rsqrt(jnp.mean(oh * oh, axis=-1, keepdims=True) + EPS)
        outs.append(oh * r)
        rs.append(r)
    return outs, rs


def _window_counts(row0, rows):
    t1 = (row0 + lax.broadcasted_iota(jnp.int32, (rows, 1), 0) + 1).astype(F32)
    return [1.0 / jnp.minimum(t1, float(w)) for w in POOL_WINDOWS]


def _pool_fwd(v, halo, row0):
    rows = v.shape[0]
    inv = _window_counts(row0, rows)
    outs = []
    for gi, w in enumerate(POOL_WINDOWS):
        cs = slice(gi * HEAD_DIM, (gi + 1) * HEAD_DIM)
        s = jnp.concatenate([halo[:, cs], v[:, cs]], axis=0)
        step = 1
        while step < w:
            s = s + pltpu.roll(s, step, 0)
            step *= 2
        outs.append(s[HALO:] * inv[gi] - v[:, cs])
    return outs


def _pool_bwd(dpooled, halo, row0):
    rows = dpooled[0].shape[0]
    inv = _window_counts(row0, rows)
    inv_h = _window_counts(row0 + rows, HALO)
    outs = []
    for gi, w in enumerate(POOL_WINDOWS):
        s = jnp.concatenate([dpooled[gi] * inv[gi], halo[gi] * inv_h[gi]], axis=0)
        n_ext = s.shape[0]
        step = 1
        while step < w:
            s = s + pltpu.roll(s, n_ext - step, 0)
            step *= 2
        outs.append(s[:rows] - dpooled[gi])
    return outs


def _pool_project(pooled, pp_ref, scale):
    y = jnp.concatenate([_dot(pooled[gi].astype(BF), pp_ref[gi].astype(BF)) for gi in range(4)], axis=1)
    return y, y * scale


def _mix_merge(proj, o, h, out_gain, pool_proj, pool_scale, slabs, gain_post, name):
    t_rows = h.shape[0]
    tm = _tile(t_rows, ROW_TILE)
    hpt = tm // HALO
    keys = ['wb', 'w_out']

    def body(g_ref, v_ref, vh_ref, ga0_ref, ga1_ref, gb0_ref, gb1_ref, o_ref, h_ref, og_ref, pp_ref, ps_ref,
             slabs_ref, gp_ref, hn_ref, r_ref, a_ref, p_ref, wb_ref, wo_ref, sems):
        _fetch_weights(slabs_ref, keys, (wb_ref, wo_ref), sems, slabs[1])
        i = pl.program_id(0)
        on, _ = _head_norm(o_ref[...])
        a = jnp.concatenate(on, axis=1) * og_ref[...] * jax.nn.sigmoid(g_ref[...])
        halo = jnp.where(i == 0, 0.0, vh_ref[...])
        pooled = _pool_fwd(v_ref[...], halo, i * tm)
        _, p = _pool_project(pooled, pp_ref, ps_ref[...])
        ab, pb = a.astype(BF), p.astype(BF)
        sa = jax.nn.sigmoid(jnp.concatenate([ga0_ref[...], ga1_ref[...]], axis=1))
        sb = jax.nn.sigmoid(jnp.concatenate([gb0_ref[...], gb1_ref[...]], axis=1))
        z = sa * _dot_nt(ab, wb_ref[:, 0:HGRN_W]) + sb * _dot_nt(pb, wb_ref[:, HGRN_W:HGRN_W + POOL_W])
        r = _dot(z.astype(BF), wo_ref[...])
        hn_ref[...] = h_ref[...] + _rms_fwd(r, gp_ref[...])
        r_ref[...] = r
        a_ref[...] = ab
        p_ref[...] = pb

    rows = lambda width: pl.BlockSpec((tm, width), lambda i: (i, 0))
    return pl.pallas_call(
        body, name=name,
        out_shape=(jax.ShapeDtypeStruct((t_rows, D_MODEL), F32), jax.ShapeDtypeStruct((t_rows, D_MODEL), F32),
                   jax.ShapeDtypeStruct((t_rows, HGRN_W), BF), jax.ShapeDtypeStruct((t_rows, POOL_W), BF)),
        grid=(t_rows // tm,),
        in_specs=[_cols(tm, 512, 3), _cols(tm, 512, 4),
                  pl.BlockSpec((HALO, 512), lambda i: (jnp.maximum(i * hpt - 1, 0), 4)),
                  _cols(tm, 512, 5), _cols(tm, 512, 6), _cols(tm, 512, 7), _cols(tm, 512, 8),
                  rows(HGRN_W), rows(D_MODEL), _whole((1, HGRN_W)), _whole((4, HEAD_DIM, HEAD_DIM)),
                  _whole((1, POOL_W)), HBM_SPEC, _whole((1, D_MODEL))],
        out_specs=(rows(D_MODEL), rows(D_MODEL), rows(HGRN_W), rows(POOL_W)),
        scratch_shapes=_weight_scratch(keys),
        compiler_params=_params(("arbitrary",)),
    )(proj, proj, proj, proj, proj, proj, proj, o, h, out_gain, pool_proj, pool_scale, slabs[0], gain_post)


def _conv_fwd(g, halo, cw):
    ext = jnp.concatenate([halo, g], axis=0)
    return (cw[0:1] * pltpu.roll(ext, 2, 0)[8:] + cw[1:2] * pltpu.roll(ext, 1, 0)[8:] + cw[2:3] * g)


def _ffn_down(gu, h, conv_w, conv_b, slabs, gain_post, name):
    t_rows = h.shape[0]
    tm = _tile(t_rows, FFN_TILE)
    hpt = tm // HALO

    def body(g_ref, gh_ref, up_ref, h_ref, cw_ref, cb_ref, slabs_ref, gp_ref, hn_ref, y_ref, gl_ref, gg_ref,
             wd_ref, sems):
        _fetch_weights(slabs_ref, ['wd'], (wd_ref,), sems, slabs[1])
        i = pl.program_id(0)
        halo = jnp.where(i == 0, 0.0, gh_ref[8:HALO, :].astype(F32))
        gt = _conv_fwd(g_ref[...].astype(F32), halo, cw_ref[...]) + cb_ref[...]
        gl, gg = _gelu_and_grad(gt)
        gl_ref[...] = gl.astype(BF)
        gg_ref[...] = gg.astype(BF)
        act = gl * up_ref[...].astype(F32)
        y = _dot(act.astype(BF), wd_ref[...])
        hn_ref[...] = h_ref[...] + _rms_fwd(y, gp_ref[...])
        y_ref[...] = y

    rows = lambda width: pl.BlockSpec((tm, width), lambda i: (i, 0))
    return pl.pallas_call(
        body, name=name,
        out_shape=(jax.ShapeDtypeStruct((t_rows, D_MODEL), F32), jax.ShapeDtypeStruct((t_rows, D_MODEL), F32),
                   jax.ShapeDtypeStruct((t_rows, D_FF), BF), jax.ShapeDtypeStruct((t_rows, D_FF), BF)),
        grid=(t_rows // tm,),
        in_specs=[_cols(tm, D_FF, 0), pl.BlockSpec((HALO, D_FF), lambda i: (jnp.maximum(i * hpt - 1, 0), 0)),
                  _cols(tm, D_FF, 1), rows(D_MODEL), _whole((3, D_FF)), _whole((1, D_FF)),
                  HBM_SPEC, _whole((1, D_MODEL))],
        out_specs=(rows(D_MODEL), rows(D_MODEL), rows(D_FF), rows(D_FF)),
        scratch_shapes=_weight_scratch(['wd']),
        compiler_params=_params(("arbitrary",)),
    )(gu, gu, gu, h, conv_w, conv_b, slabs[0], gain_post)


def _loss_grad(h, target, name):
    t_rows, d = h.shape
    tm = _tile(t_rows, ROW_TILE)

    def body(h_ref, t_ref, dh_ref, l_ref):
        i = pl.program_id(0)

        @pl.when(i == 0)
        def _():
            l_ref[...] = jnp.zeros_like(l_ref)

        row = i * tm + lax.broadcasted_iota(jnp.int32, (tm, 1), 0)
        err = jnp.where(row >= N_META, h_ref[...] - t_ref[...], 0.0)
        dh_ref[...] = err * (1.0 / d)
        l_ref[...] += jnp.sum(err * err, axis=0, keepdims=True) * (0.5 / d)

    rows = pl.BlockSpec((tm, d), lambda i: (i, 0))
    return pl.pallas_call(
        body, name=name,
        out_shape=(jax.ShapeDtypeStruct((t_rows, d), F32), jax.ShapeDtypeStruct((1, d), F32)),
        grid=(t_rows // tm,), in_specs=[rows, rows], out_specs=(rows, _whole((1, d))),
        compiler_params=_params(("arbitrary",)),
    )(h, target)


def _dw(lhs, rhs, name, gain=None):
    t_rows = lhs[0].shape[0]
    tm = _tile(t_rows, ROW_TILE)
    paired = len(rhs) > 1
    n_rows = lhs[0].shape[1] if paired else sum(x.shape[1] for x in lhs)
    n_cols = sum(x.shape[1] for x in rhs)
    last = t_rows // tm - 1

    def body(*refs):
        lhs_refs = refs[:len(lhs)]
        rhs_refs = refs[len(lhs):len(lhs) + len(rhs)]
        rest = refs[len(lhs) + len(rhs):]
        if gain is not None:
            g_ref, o_ref, o16_ref, acc, stage = rest
            rv = [_rms_fwd(rhs_refs[0][...], g_ref[...]).astype(BF)]
        else:
            o_ref, o16_ref, acc, stage = rest
            rv = [r[...] for r in rhs_refs]
        i = pl.program_id(0)

        @pl.when(i == 0)
        def _():
            acc[...] = jnp.zeros_like(acc)

        r0, c0 = 0, 0
        for p, l_ref in enumerate(lhs_refs):
            r = rv[p] if paired else rv[0]
            n = l_ref.shape[1]
            step = 512 if n % 512 == 0 else 256
            for s in range(0, n, step):
                acc[r0 + s:r0 + s + step, c0:c0 + r.shape[1]] += _dot_tn(l_ref[:, s:s + step], r)
            if paired:
                c0 += r.shape[1]
            else:
                r0 += n

        @pl.when(i == last)
        def _():
            pltpu.sync_copy(acc, o_ref)
            for s in range(0, n_rows, 256):
                stage[...] = acc[s:s + 256, :].astype(BF)
                pltpu.sync_copy(stage, o16_ref.at[pl.ds(s, 256)])

    rows = lambda x: pl.BlockSpec((tm, x.shape[1]), lambda i: (i, 0))
    in_specs = [rows(x) for x in lhs] + [rows(x) for x in rhs]
    args = list(lhs) + list(rhs)
    if gain is not None:
        in_specs.append(_whole(gain.shape))
        args.append(gain)
    return pl.pallas_call(
        body, name=name,
        out_shape=(jax.ShapeDtypeStruct((n_rows, n_cols), F32), jax.ShapeDtypeStruct((n_rows, n_cols), BF)),
        grid=(t_rows // tm,), in_specs=in_specs, out_specs=(HBM_SPEC, HBM_SPEC),
        scratch_shapes=[pltpu.VMEM((n_rows, n_cols), F32), pltpu.VMEM((256, n_cols), BF)],
        compiler_params=_params(("arbitrary",)),
    )(*args)


def _ffn_bwd_down(dh, y, gain_post, gu, gl, gg, slabs, name, comm=None):
    t_rows = dh.shape[0]
    tm = _tile(t_rows, FFN_TILE)
    hpt = tm // HALO

    def body(dh_ref, y_ref, gp_ref, g_ref, gh_ref, up_ref, gl_ref, gg_ref, slabs_ref,
             dy_ref, act_ref, dgt_ref, dup_ref, dgp_ref, dcw_ref, dcb_ref, wd_ref, sems):
        _fetch_weights(slabs_ref, ['wd'], (wd_ref,), sems, slabs[1])
        i = pl.program_id(0)

        @pl.when(i == 0)
        def _():
            dgp_ref[...] = jnp.zeros_like(dgp_ref)
            dcw_ref[...] = jnp.zeros_like(dcw_ref)
            dcb_ref[...] = jnp.zeros_like(dcb_ref)

        dy, dgp = _rms_bwd(dh_ref[...], y_ref[...], gp_ref[...])
        dgp_ref[...] += dgp
        dyb = dy.astype(BF)
        dy_ref[...] = dyb
        g = g_ref[...].astype(F32)
        up = up_ref[...].astype(F32)
        ext = jnp.concatenate([jnp.where(i == 0, 0.0, gh_ref[8:HALO, :].astype(F32)), g], axis=0)
        g2 = pltpu.roll(ext, 2, 0)[8:]
        g1 = pltpu.roll(ext, 1, 0)[8:]
        gl = gl_ref[...].astype(F32)
        act_ref[...] = (gl * up).astype(BF)
        dact = _dot_nt(dyb, wd_ref[...])
        dup_ref[...] = (dact * gl).astype(BF)
        dgt = dact * up * gg_ref[...].astype(F32)
        dgt_ref[...] = dgt.astype(BF)
        dcb_ref[...] += jnp.sum(dgt, axis=0, keepdims=True)
        dcw_ref[...] += jnp.concatenate([jnp.sum(dgt * g2, axis=0, keepdims=True),
                                         jnp.sum(dgt * g1, axis=0, keepdims=True),
                                         jnp.sum(dgt * g, axis=0, keepdims=True)], axis=0)

    rows = lambda width: pl.BlockSpec((tm, width), lambda i: (i, 0))
    bf = lambda width: jax.ShapeDtypeStruct((t_rows, width), BF)
    outs, exchanged = _hosted_call(
        body, comm, name, (t_rows // tm,),
        [rows(D_MODEL), rows(D_MODEL), _whole((1, D_MODEL)), _cols(tm, D_FF, 0),
         pl.BlockSpec((HALO, D_FF), lambda i: (jnp.maximum(i * hpt - 1, 0), 0)), _cols(tm, D_FF, 1),
         rows(D_FF), rows(D_FF), HBM_SPEC],
        (dh, y, gain_post, gu, gu, gu, gl, gg, slabs[0]),
        (bf(D_MODEL), bf(D_FF), bf(D_FF), bf(D_FF), jax.ShapeDtypeStruct((1, D_MODEL), F32),
         jax.ShapeDtypeStruct((3, D_FF), F32), jax.ShapeDtypeStruct((1, D_FF), F32)),
        (rows(D_MODEL), rows(D_FF), rows(D_FF), rows(D_FF), _whole((1, D_MODEL)), _whole((3, D_FF)),
         _whole((1, D_FF))),
        _weight_scratch(['wd']))
    return (*outs, exchanged)


def _ffn_bwd_up(dgt, dup, conv_w, slabs, h, gain_pre, dh_out, name):
    t_rows = h.shape[0]
    tm = _tile(t_rows, FFN_TILE)
    hpt = tm // HALO
    last = t_rows // tm - 1

    def body(dgt_ref, dgn_ref, dup_ref, cw_ref, slabs_ref, h_ref, gp_ref, dho_ref, dh_ref, dgu_ref, dg_ref,
             wg_ref, wu_ref, sems):
        _fetch_weights(slabs_ref, ['wg', 'wu'], (wg_ref, wu_ref), sems, slabs[1])
        i = pl.program_id(0)

        @pl.when(i == 0)
        def _():
            dg_ref[...] = jnp.zeros_like(dg_ref)

        dgt = dgt_ref[...].astype(F32)
        ext = jnp.concatenate([dgt, jnp.where(i == last, 0.0, dgn_ref[0:8, :].astype(F32))], axis=0)
        n_ext = tm + 8
        cw = cw_ref[...]
        dg = (cw[2:3] * dgt + cw[1:2] * pltpu.roll(ext, n_ext - 1, 0)[:tm]
              + cw[0:1] * pltpu.roll(ext, n_ext - 2, 0)[:tm])
        dgb = dg.astype(BF)
        dub = dup_ref[...]
        dgu_ref[:, 0:D_FF] = dgb
        dgu_ref[:, D_FF:2 * D_FF] = dub
        du = _dot(dgb, wg_ref[...]) + _dot(dub, wu_ref[...])
        dx, dgain = _rms_bwd(du, h_ref[...], gp_ref[...])
        dh_ref[...] = dho_ref[...] + dx
        dg_ref[...] += dgain

    rows = lambda width: pl.BlockSpec((tm, width), lambda i: (i, 0))
    return pl.pallas_call(
        body, name=name,
        out_shape=(jax.ShapeDtypeStruct((t_rows, D_MODEL), F32), jax.ShapeDtypeStruct((t_rows, 2 * D_FF), BF),
                   jax.ShapeDtypeStruct((1, D_MODEL), F32)),
        grid=(t_rows // tm,),
        in_specs=[rows(D_FF), pl.BlockSpec((HALO, D_FF), lambda i: (jnp.minimum((i + 1) * hpt, (last + 1) * hpt - 1), 0)),
                  rows(D_FF), _whole((3, D_FF)), HBM_SPEC, rows(D_MODEL), _whole((1, D_MODEL)),
                  rows(D_MODEL)],
        out_specs=(rows(D_MODEL), rows(2 * D_FF), _whole((1, D_MODEL))),
        scratch_shapes=_weight_scratch(['wg', 'wu']),
        compiler_params=_params(("arbitrary",)),
    )(dgt, dgt, dup, conv_w, slabs[0], h, gain_pre, dh_out)


def _mix_bwd_a(dh, r, gain_post, proj, a, p, slabs, name):
    t_rows = dh.shape[0]
    tm = _tile(t_rows, ROW_TILE)

    def body(dh_ref, r_ref, gp_ref, ga0_ref, ga1_ref, gb0_ref, gb1_ref, a_ref, p_ref, slabs_ref,
             dr_ref, z_ref, dya_ref, dyp_ref, da_ref, dp_ref, dgab_ref, dg_ref, wb_ref, wo_ref, sems):
        _fetch_weights(slabs_ref, ['wb', 'w_out'], (wb_ref, wo_ref), sems, slabs[1])

        @pl.when(pl.program_id(0) == 0)
        def _():
            dg_ref[...] = jnp.zeros_like(dg_ref)

        dr, dgain = _rms_bwd(dh_ref[...], r_ref[...], gp_ref[...])
        dg_ref[...] += dgain
        drb = dr.astype(BF)
        dr_ref[...] = drb
        dz = _dot_nt(drb, wo_ref[...])
        ya = _dot_nt(a_ref[...], wb_ref[:, 0:HGRN_W])
        yp = _dot_nt(p_ref[...], wb_ref[:, HGRN_W:HGRN_W + POOL_W])
        sa = jax.nn.sigmoid(jnp.concatenate([ga0_ref[...], ga1_ref[...]], axis=1))
        sb = jax.nn.sigmoid(jnp.concatenate([gb0_ref[...], gb1_ref[...]], axis=1))
        z_ref[...] = (sa * ya + sb * yp).astype(BF)
        dgab_ref[:, 0:D_MODEL] = (dz * ya * sa * (1.0 - sa)).astype(BF)
        dgab_ref[:, D_MODEL:2 * D_MODEL] = (dz * yp * sb * (1.0 - sb)).astype(BF)
        dya = (dz * sa).astype(BF)
        dyp = (dz * sb).astype(BF)
        dya_ref[...] = dya
        dyp_ref[...] = dyp
        da_ref[...] = _dot(dya, wb_ref[:, 0:HGRN_W])
        dp_ref[...] = _dot(dyp, wb_ref[:, HGRN_W:HGRN_W + POOL_W])

    rows = lambda width: pl.BlockSpec((tm, width), lambda i: (i, 0))
    bf = lambda width: jax.ShapeDtypeStruct((t_rows, width), BF)
    f32 = lambda width: jax.ShapeDtypeStruct((t_rows, width), F32)
    return pl.pallas_call(
        body, name=name,
        out_shape=(bf(D_MODEL), bf(D_MODEL), bf(D_MODEL), bf(D_MODEL), f32(HGRN_W), f32(POOL_W), bf(2 * D_MODEL),
                   jax.ShapeDtypeStruct((1, D_MODEL), F32)),
        grid=(t_rows // tm,),
        in_specs=[rows(D_MODEL), rows(D_MODEL), _whole((1, D_MODEL)),
                  _cols(tm, 512, 5), _cols(tm, 512, 6), _cols(tm, 512, 7), _cols(tm, 512, 8),
                  rows(HGRN_W), rows(POOL_W), HBM_SPEC],
        out_specs=(rows(D_MODEL), rows(D_MODEL), rows(D_MODEL), rows(D_MODEL), rows(HGRN_W), rows(POOL_W),
                   rows(2 * D_MODEL), _whole((1, D_MODEL))),
        scratch_shapes=_weight_scratch(['wb', 'w_out']),
        compiler_params=_params(("arbitrary",)),
    )(dh, r, gain_post, proj, proj, proj, proj, a, p, slabs[0])


def _mix_bwd_b(da, dp, proj, o, out_gain, pool_proj, pool_scale, name):
    t_rows = da.shape[0]
    tm = _tile(t_rows, ROW_TILE)
    hpt = tm // HALO
    last = t_rows // tm - 1

    def body(da_ref, dp_ref, dpn_ref, g_ref, v_ref, vh_ref, o_ref, og_ref, pp_ref, ps_ref,
             do_ref, dgv_ref, dog_ref, dpp_ref, dps_ref):
        i = pl.program_id(0)

        @pl.when(i == 0)
        def _():
            dog_ref[...] = jnp.zeros_like(dog_ref)
            dpp_ref[...] = jnp.zeros_like(dpp_ref)
            dps_ref[...] = jnp.zeros_like(dps_ref)

        da = da_ref[...]
        og = og_ref[...]
        o = o_ref[...]
        on, rs = _head_norm(o)
        onc = jnp.concatenate(on, axis=1)
        sg = jax.nn.sigmoid(g_ref[...])
        dog_ref[...] += jnp.sum(da * onc * sg, axis=0, keepdims=True)
        dgv_ref[:, 0:HGRN_W] = (da * onc * og * sg * (1.0 - sg)).astype(BF)
        don = da * og * sg
        for hh in range(HEADS):
            cs = slice(hh * HEAD_DIM, (hh + 1) * HEAD_DIM)
            d = don[:, cs]
            do_ref[:, cs] = rs[hh] * (d - on[hh] * jnp.mean(d * on[hh], axis=-1, keepdims=True))

        scale = ps_ref[...]
        halo = jnp.where(i == 0, 0.0, vh_ref[...])
        pooled = _pool_fwd(v_ref[...], halo, i * tm)
        y, _ = _pool_project(pooled, pp_ref, scale)
        dp = dp_ref[...]
        dps_ref[...] += jnp.sum(dp * y, axis=0, keepdims=True)
        dy = dp * scale
        dyn = jnp.where(i == last, 0.0, dpn_ref[...]) * scale
        dpooled, dhalo = [], []
        for gi in range(4):
            cs = slice(gi * HEAD_DIM, (gi + 1) * HEAD_DIM)
            ppb = pp_ref[gi].astype(BF)
            dyb = dy[:, cs].astype(BF)
            dpooled.append(_dot_nt(dyb, ppb))
            dhalo.append(_dot_nt(dyn[:, cs].astype(BF), ppb))
            dpp_ref[gi] += _dot_tn(pooled[gi].astype(BF), dyb)
        dv = _pool_bwd(dpooled, dhalo, i * tm)
        dgv_ref[:, HGRN_W:HGRN_W + POOL_W] = jnp.concatenate(dv, axis=1).astype(BF)

    rows = lambda width: pl.BlockSpec((tm, width), lambda i: (i, 0))
    return pl.pallas_call(
        body, name=name,
        out_shape=(jax.ShapeDtypeStruct((t_rows, HGRN_W), F32), jax.ShapeDtypeStruct((t_rows, HGRN_W + POOL_W), BF),
                   jax.ShapeDtypeStruct((1, HGRN_W), F32), jax.ShapeDtypeStruct((4, HEAD_DIM, HEAD_DIM), F32),
                   jax.ShapeDtypeStruct((1, POOL_W), F32)),
        grid=(t_rows // tm,),
        in_specs=[rows(HGRN_W), rows(POOL_W),
                  pl.BlockSpec((HALO, POOL_W), lambda i: (jnp.minimum((i + 1) * hpt, (last + 1) * hpt - 1), 0)),
                  _cols(tm, 512, 3), _cols(tm, 512, 4),
                  pl.BlockSpec((HALO, 512), lambda i: (jnp.maximum(i * hpt - 1, 0), 4)),
                  rows(HGRN_W), _whole((1, HGRN_W)), _whole((4, HEAD_DIM, HEAD_DIM)), _whole((1, POOL_W))],
        out_specs=(rows(HGRN_W), rows(HGRN_W + POOL_W), _whole((1, HGRN_W)), _whole((4, HEAD_DIM, HEAD_DIM)),
                   _whole((1, POOL_W))),
        compiler_params=_params(("arbitrary",)),
    )(da, dp, dp, proj, proj, proj, o, out_gain, pool_proj, pool_scale)


def _hgrn_bwd(proj, lower_bounds, layer, states, do, name, comm=None):
    t_rows = proj.shape[0]
    tm = _tile(t_rows, ROW_TILE, CHUNK)
    nct = tm // CHUNK
    n_tiles = t_rows // tm

    def body(q_ref, f_ref, i_ref, lb_ref, sall_ref, do_ref, dqfi_ref, dlb_ref, dst_ref):
        @pl.when(pl.program_id(0) == 0)
        def _():
            dst_ref[...] = jnp.zeros_like(dst_ref)
            dlb_ref[...] = jnp.zeros_like(dlb_ref)

        lbs = _layer_lower_bound(lb_ref, layer)
        tri_lo, tri_up, diag, off = _chunk_masks()
        is_last_row = lax.broadcasted_iota(jnp.int32, (CHUNK, 1), 0) == CHUNK - 1

        group = HGRN_UNROLL if nct % HGRN_UNROLL == 0 else 1

        def chunk_group(gi, carry):
            pairs = [(u, hh) for u in range(group) for hh in range(HEADS)]
            cidx = [nct - 1 - (gi * group + u) for u in range(group)]
            rows = [pl.ds(pl.multiple_of(c * CHUNK, SUB), CHUNK) for c in cidx]
            cols = [slice(hh * HEAD_DIM, (hh + 1) * HEAD_DIM) for hh in range(HEADS)]
            lb = {p: lbs[:, cols[p[1]]] for p in pairs}
            q = {p: q_ref[rows[p[0]], cols[p[1]]] for p in pairs}
            vb = {p: i_ref[rows[p[0]], cols[p[1]]].astype(BF) for p in pairs}
            dob = {p: do_ref[rows[p[0]], cols[p[1]]].astype(BF) for p in pairs}
            gates = {p: _chunk_gates(f_ref[rows[p[0]], cols[p[1]]], lb[p]) for p in pairs}
            k = {p: gates[p][4] for p in pairs}
            b = {p: _tri_mm(tri_lo, gates[p][3]) for p in pairs}
            fac = {p: _chunk_factors(q[p], k[p], b[p]) for p in pairs}
            e_b = {p: jnp.exp(b[p]) for p in pairs}
            e_last = {p: jnp.exp(b[p][CHUNK - 1:CHUNK]) for p in pairs}
            e_kl = {p: jnp.exp(b[p][CHUNK - 1:CHUNK] - b[p]) for p in pairs}
            qc = {p: (q[p] * e_b[p]).astype(BF) for p in pairs}
            kdec = {p: (k[p] * e_kl[p]).astype(BF) for p in pairs}
            a = {p: _chunk_scores(*fac[p][4:], diag, off) for p in pairs}
            da_full = {p: _dot_nt(dob[p], vb[p]) for p in pairs}
            dv, dq, dk, upd, dupd = {}, {}, {}, {}, {}
            for p in pairs:
                e_qs, e_qm, e_km, e_ks, qs, qm, km, ks = fac[p]
                da_d = jnp.where(diag, da_full[p], 0.0).astype(BF)
                dv[p] = _dot_tn(a[p].astype(BF), dob[p])
                dq[p] = e_qm * _dot(da_d, km)
                dk[p] = e_km * _dot_tn(da_d, qm)
                for m, kk, ek in zip(off, ks, e_ks):
                    da_i = jnp.where(m, da_full[p], 0.0).astype(BF)
                    dq[p] = dq[p] + e_qs * _dot(da_i, kk)
                    dk[p] = dk[p] + ek * _dot_tn(da_i, qs)
                upd[p] = _dot_tn(vb[p], kdec[p])
                dupd[p] = _dot_tn(dob[p], qc[p])
            db = {}
            for hh in range(HEADS):
                dst = dst_ref[hh]
                for u in range(group):
                    p = (u, hh)
                    st = sall_ref[cidx[u], hh]
                    dstb = dst.astype(BF)
                    dv[p] = dv[p] + _dot_nt(kdec[p], dstb)
                    dq[p] = dq[p] + e_b[p] * _dot(dob[p], st.astype(BF))
                    dk[p] = dk[p] + e_kl[p] * _dot(vb[p], dstb)
                    st_new = st * e_last[p] + upd[p]
                    db[p] = (q[p] * dq[p] - k[p] * dk[p]
                             + jnp.where(is_last_row, jnp.sum(st_new * dst, axis=0, keepdims=True), 0.0))
                    dst = dst * e_last[p] + dupd[p]
                dst_ref[hh] = dst
            for p in pairs:
                u, hh = p
                sg, s2, f = gates[p][:3]
                dlf = _tri_mm(tri_up, db[p])
                df = jnp.where(f > LOG_FLOOR, dlf / f, 0.0)
                dfl = df * (1.0 - lb[p]) * sg * (1.0 - sg) - dk[p] * (1.0 - lb[p]) * s2 * (1.0 - s2)
                dlb_ref[:, cols[hh]] += jnp.sum(df * (1.0 - sg) - dk[p] * s2, axis=0, keepdims=True)
                dqfi_ref[rows[u], cols[hh]] = dq[p].astype(BF)
                dqfi_ref[rows[u], pl.ds(HGRN_W + hh * HEAD_DIM, HEAD_DIM)] = dfl.astype(BF)
                dqfi_ref[rows[u], pl.ds(2 * HGRN_W + hh * HEAD_DIM, HEAD_DIM)] = dv[p].astype(BF)
            return carry

        lax.fori_loop(0, nct // group, chunk_group, 0)

    rev = lambda width, j: pl.BlockSpec((tm, width), lambda i, j=j: (n_tiles - 1 - i, j))
    outs, exchanged = _hosted_call(
        body, comm, name, (n_tiles,),
        [rev(512, 0), rev(512, 1), rev(512, 2), _whole((DEPTH, HGRN_W)),
         pl.BlockSpec((nct, HEADS, HEAD_DIM, HEAD_DIM), lambda i: (n_tiles - 1 - i, 0, 0, 0)), rev(HGRN_W, 0)],
        (proj, proj, proj, lower_bounds, states, do),
        (jax.ShapeDtypeStruct((t_rows, 3 * HGRN_W), BF), jax.ShapeDtypeStruct((1, HGRN_W), F32)),
        (rev(3 * HGRN_W, 0), _whole((1, HGRN_W))),
        [pltpu.VMEM((HEADS, HEAD_DIM, HEAD_DIM), F32)])
    return (*outs, exchanged)


def _in_bwd(dqfi, dgv, dgab, slabs, h, gain_pre, dh_out, name, comm=None):
    t_rows = h.shape[0]
    tm = _tile(t_rows, ROW_TILE)
    c1 = 3 * HGRN_W
    c2 = c1 + HGRN_W + POOL_W

    def body(d1_ref, d2_ref, d3_ref, slabs_ref, h_ref, gp_ref, dho_ref, dh_ref, dg_ref, w_ref, sems):
        _fetch_weights(slabs_ref, ['w_in'], (w_ref,), sems, slabs[1])

        @pl.when(pl.program_id(0) == 0)
        def _():
            dg_ref[...] = jnp.zeros_like(dg_ref)

        du = (_dot(d1_ref[...], w_ref[0:c1, :]) + _dot(d2_ref[...], w_ref[c1:c2, :])
              + _dot(d3_ref[...], w_ref[c2:IN_COLS, :]))
        dx, dgain = _rms_bwd(du, h_ref[...], gp_ref[...])
        dh_ref[...] = dho_ref[...] + dx
        dg_ref[...] += dgain

    rows = lambda width: pl.BlockSpec((tm, width), lambda i: (i, 0))
    outs, exchanged = _hosted_call(
        body, comm, name, (t_rows // tm,),
        [rows(c1), rows(c2 - c1), rows(IN_COLS - c2), HBM_SPEC, rows(D_MODEL), _whole((1, D_MODEL)), rows(D_MODEL)],
        (dqfi, dgv, dgab, slabs[0], h, gain_pre, dh_out),
        (jax.ShapeDtypeStruct((t_rows, D_MODEL), F32), jax.ShapeDtypeStruct((1, D_MODEL), F32)),
        (rows(D_MODEL), _whole((1, D_MODEL))),
        _weight_scratch(['w_in']))
    return (*outs, exchanged)


def _lower_bound_grad(lower_bounds, dlbs, name):
    def body(lb_ref, d_ref, o_ref):
        g0, g1 = _softmax2(lb_ref)
        bound = (g0 + g1) - g0
        inside = (bound > 0.0) & (bound < 1.0)
        dg1 = jnp.where(inside, d_ref[1:2, :], 0.0)
        inner = g1 * dg1
        o_ref[0:1, :] = g0 * (0.0 - inner)
        o_ref[1:2, :] = g1 * (dg1 - inner)

    return pl.pallas_call(body, name=name, out_shape=jax.ShapeDtypeStruct(lower_bounds.shape, F32))(lower_bounds, dlbs)


def _adamw(w, g, m, v, name):
    r, c = w.shape
    tr = r if (r % 8 or r <= 512) else _tile(r, 512, 8)
    c1 = 1.0 - ADAM_B1 ** ADAM_STEP
    c2 = 1.0 - ADAM_B2 ** ADAM_STEP

    def body(w_ref, g_ref, m_ref, v_ref, d_ref, nm_ref, nv_ref):
        gg = g_ref[...]
        nm = ADAM_B1 * m_ref[...] + (1.0 - ADAM_B1) * gg
        nv = ADAM_B2 * v_ref[...] + (1.0 - ADAM_B2) * (gg * gg)
        d_ref[...] = -ADAM_LR * ((nm / c1) / (jnp.sqrt(nv / c2) + ADAM_EPS) + ADAM_WD * w_ref[...])
        nm_ref[...] = nm
        nv_ref[...] = nv

    blk = pl.BlockSpec((tr, c), lambda i: (i, 0))
    shp = jax.ShapeDtypeStruct((r, c), F32)
    return pl.pallas_call(
        body, name=name, out_shape=(shp, shp, shp), grid=(r // tr,),
        in_specs=[blk, blk, blk, blk], out_specs=(blk, blk, blk),
        compiler_params=_params(("parallel",)),
    )(w, g, m, v)


def _as2d(a):
    return a.reshape(-1, a.shape[-1])


def _layer_slab(w, l):
    t = lambda a: jnp.swapaxes(a, 0, 1)
    parts = [t(w['w_in'][l]), jnp.concatenate([t(w['w_branch_hgrn'][l]), t(w['w_branch_pool'][l])], axis=1),
             w['w_out'][l], t(w['ffn_w_gate'][l]), t(w['ffn_w_up'][l]), w['ffn_w_down'][l]]
    return jnp.concatenate(parts, axis=0).astype(BF)


def _slab_grads(sums):
    t = lambda a: jnp.swapaxes(a, 0, 1)
    wb = sums['wb']
    return {'w_in': t(sums['w_in']), 'w_branch_hgrn': t(wb[:, :HGRN_W]), 'w_branch_pool': t(wb[:, HGRN_W:]),
            'w_out': sums['w_out'], 'ffn_w_gate': t(sums['wg']), 'ffn_w_up': t(sums['wu']), 'ffn_w_down': sums['wd']}


def _train_step(x, target, w, m, v):
    my_slabs = [_layer_slab(w, l) for l in range(DEPTH)]
    n_in = W_ROWS['w_in'][1]
    slabs = [{} for _ in range(DEPTH)]
    slabs[0]['w_in'] = (_all_gather(my_slabs[0][:n_in], "gather_w_in_0"), 0)
    f32_shapes = [w[n].shape for n in F32_GATHERED]
    gathered32 = _all_gather(_pack([w[n] for n in F32_GATHERED], F32, 8), "gather_meta_conv")
    full = {n: _blocks_to_full(blk, SHARD_AXIS[n]) for n, blk in zip(F32_GATHERED, _unpack(gathered32, f32_shapes))}
    row = lambda name, l: w[name][l][None]

    h = jnp.concatenate([full['meta_tokens'], x], axis=0)
    saved = []
    for l in range(DEPTH):
        s = {'h_in': h}
        if l == 0:
            s['proj'], moved = _norm_matmul(h, row('mix_norm_pre', l), slabs[l]['w_in'], ['w_in'], F32, f"in_proj_{l}",
                                            comm=_SlabGather(my_slabs[0], n_in, SLAB_ROWS - n_in))
            slabs[0].update({k: (moved[0], n_in) for k in REST_KEYS})
        else:
            s['proj'], _ = _norm_matmul(h, row('mix_norm_pre', l), slabs[l]['w_in'], ['w_in'], F32, f"in_proj_{l}")
        if l + 1 < DEPTH:
            s['o'], s['states'], moved = _hgrn_fwd(s['proj'], w['hgrn_lower_bounds'], l, f"hgrn_fwd_{l}",
                                                   comm=_SlabGather(my_slabs[l + 1], 0, SLAB_ROWS))
            slabs[l + 1] = {k: (moved[0], 0) for k in W_ROWS}
        else:
            s['o'], s['states'], _ = _hgrn_fwd(s['proj'], w['hgrn_lower_bounds'], l, f"hgrn_fwd_{l}")
        h, s['r'], s['a'], s['p'] = _mix_merge(
            s['proj'], s['o'], h, row('hgrn_out_norm', l), w['pool_proj'][l], row('pool_scale', l), slabs[l]['wb'],
            row('mix_norm_post', l), f"mix_merge_{l}")
        s['h_mid'] = h
        s['gu'], _ = _norm_matmul(h, row('ffn_norm_pre', l), slabs[l]['wg'], ['wg', 'wu'], BF, f"ffn_proj_{l}")
        h, s['y'], s['gl'], s['gg'] = _ffn_down(s['gu'], h, full['ffn_conv_w'][l], row('ffn_conv_b', l),
                                                slabs[l]['wd'], row('ffn_norm_post', l), f"ffn_down_{l}")
        saved.append(s)

    dh, loss_cols = _loss_grad(h, jnp.pad(target, ((N_META, 0), (0, 0))), "loss_grad")
    loss = jnp.sum(loss_cols)

    small = {n: [None] * DEPTH for n in REPLICATED + ['ffn_conv_w']}
    exchanged = [{} for _ in range(DEPTH)]
    pending = None
    for l in reversed(range(DEPTH)):
        s = saved[l]
        g = {}
        (dy, act, dgt, dup, small['ffn_norm_post'][l], small['ffn_conv_w'][l], small['ffn_conv_b'][l],
         done) = _ffn_bwd_down(dh, s['y'], row('ffn_norm_post', l), s['gu'], s['gl'], s['gg'], slabs[l]['wd'],
                               f"ffn_bwd_down_{l}", comm=pending)
        if pending is not None:
            exchanged[l + 1]['in'] = done
        g['d_down'] = _dw([act], [dy], f"dw_down_{l}")
        dh, dgu, small['ffn_norm_pre'][l] = _ffn_bwd_up(
            dgt, dup, full['ffn_conv_w'][l], slabs[l]['wg'], s['h_mid'], row('ffn_norm_pre', l), dh,
            f"ffn_bwd_up_{l}")
        g['d_gu'] = _dw([dgu], [s['h_mid']], f"dw_gate_up_{l}", gain=row('ffn_norm_pre', l))

        dr, z, dya, dyp, da, dp, dgab, small['mix_norm_post'][l] = _mix_bwd_a(
            dh, s['r'], row('mix_norm_post', l), s['proj'], s['a'], s['p'], slabs[l]['wb'], f"mix_bwd_a_{l}")
        g['d_out'] = _dw([z], [dr], f"dw_out_{l}")
        g['d_b'] = _dw([dya, dyp], [s['a'], s['p']], f"dw_branch_{l}")
        do, dgv, small['hgrn_out_norm'][l], small['pool_proj'][l], small['pool_scale'][l] = _mix_bwd_b(
            da, dp, s['proj'], s['o'], row('hgrn_out_norm', l), w['pool_proj'][l], row('pool_scale', l),
            f"mix_bwd_b_{l}")
        dqfi, small['hgrn_lower_bounds'][l], exchanged[l]['rest'] = _hgrn_bwd(
            s['proj'], w['hgrn_lower_bounds'], l, s['states'], do, f"hgrn_bwd_{l}",
            comm=_GradExchange(REST_KEYS, g))
        g['d_in'] = _dw([dqfi, dgv, dgab], [s['h_in']], f"dw_in_{l}", gain=row('mix_norm_pre', l))
        pending = _GradExchange(['w_in'], g)
        if l == 0:
            dh, small['mix_norm_pre'][l], exchanged[l]['in'] = _in_bwd(
                dqfi, dgv, dgab, slabs[l]['w_in'], s['h_in'], row('mix_norm_pre', l), dh, f"in_bwd_{l}", comm=pending)
        else:
            dh, small['mix_norm_pre'][l], _ = _in_bwd(
                dqfi, dgv, dgab, slabs[l]['w_in'], s['h_in'], row('mix_norm_pre', l), dh, f"in_bwd_{l}")

    grad_x = dh[N_META:]

    per_layer = []
    for l in range(DEPTH):
        sums = {}
        for part, keys in (('in', ['w_in']), ('rest', REST_KEYS)):
            recv, own = exchanged[l][part]
            total = _sum_parts(own, recv, f"sum_grads_{part}_{l}")
            off = 0
            for k in keys:
                sums[k] = total[off:off + W_ROWS[k][1]]
                off += W_ROWS[k][1]
        per_layer.append(_slab_grads(sums))
    grads = {n: jnp.stack([per_layer[l][n] for l in range(DEPTH)]) for n in per_layer[0]}

    stack = lambda n: jnp.stack(small[n]) if small[n][0].shape[0] != 1 else jnp.concatenate(small[n], axis=0)
    names = list(small) + ['meta_tokens']
    partial = {n: stack(n) for n in small}
    partial['meta_tokens'] = dh[:N_META]
    gathered = _all_gather(_pack([partial[n] for n in names], F32, 8), "gather_small_grads")
    rep = dict(zip(names, _unpack(_sum8(gathered, "sum_small_grads"), [partial[n].shape for n in names])))
    rep['hgrn_lower_bounds'] = _lower_bound_grad(w['hgrn_lower_bounds'], rep['hgrn_lower_bounds'], "lower_bound_grad")
    me = 4 * lax.axis_index("x") + 2 * lax.axis_index("y") + lax.axis_index("c")
    for n in F32_GATHERED:
        width = w[n].shape[SHARD_AXIS[n]]
        rep[n] = lax.dynamic_slice_in_dim(rep[n], me * width, width, axis=SHARD_AXIS[n])
    grads.update(rep)

    delta, new_m, new_v = {}, {}, {}
    for n in WEIGHT_NAMES:
        shape = w[n].shape
        d2, m2, v2 = _adamw(_as2d(w[n]), _as2d(grads[n]), _as2d(m[n]), _as2d(v[n]), f"adamw_{n}")
        delta[n], new_m[n], new_v[n] = d2.reshape(shape), m2.reshape(shape), v2.reshape(shape)
    return loss, grad_x, grads, delta, new_m, new_v


def kernel(x, meta_tokens, mix_norm_pre, mix_norm_post, w_in, hgrn_lower_bounds, hgrn_out_norm, w_branch_hgrn, pool_proj, pool_scale, w_branch_pool, w_out, ffn_norm_pre, ffn_norm_post, ffn_w_gate, ffn_w_up, ffn_conv_w, ffn_conv_b, ffn_w_down, loss_target, m_meta_tokens, m_mix_norm_pre, m_mix_norm_post, m_w_in, m_hgrn_lower_bounds, m_hgrn_out_norm, m_w_branch_hgrn, m_pool_proj, m_pool_scale, m_w_branch_pool, m_w_out, m_ffn_norm_pre, m_ffn_norm_post, m_ffn_w_gate, m_ffn_w_up, m_ffn_conv_w, m_ffn_conv_b, m_ffn_w_down, v_meta_tokens, v_mix_norm_pre, v_mix_norm_post, v_w_in, v_hgrn_lower_bounds, v_hgrn_out_norm, v_w_branch_hgrn, v_pool_proj, v_pool_scale, v_w_branch_pool, v_w_out, v_ffn_norm_pre, v_ffn_norm_post, v_ffn_w_gate, v_ffn_w_up, v_ffn_conv_w, v_ffn_conv_b, v_ffn_w_down):
    w = dict(zip(WEIGHT_NAMES, (meta_tokens, mix_norm_pre, mix_norm_post, w_in, hgrn_lower_bounds, hgrn_out_norm,
                                w_branch_hgrn, pool_proj, pool_scale, w_branch_pool, w_out, ffn_norm_pre,
                                ffn_norm_post, ffn_w_gate, ffn_w_up, ffn_conv_w, ffn_conv_b, ffn_w_down)))
    m = dict(zip(WEIGHT_NAMES, (m_meta_tokens, m_mix_norm_pre, m_mix_norm_post, m_w_in, m_hgrn_lower_bounds,
                                m_hgrn_out_norm, m_w_branch_hgrn, m_pool_proj, m_pool_scale, m_w_branch_pool, m_w_out,
                                m_ffn_norm_pre, m_ffn_norm_post, m_ffn_w_gate, m_ffn_w_up, m_ffn_conv_w,
                                m_ffn_conv_b, m_ffn_w_down)))
    v = dict(zip(WEIGHT_NAMES, (v_meta_tokens, v_mix_norm_pre, v_mix_norm_post, v_w_in, v_hgrn_lower_bounds,
                                v_hgrn_out_norm, v_w_branch_hgrn, v_pool_proj, v_pool_scale, v_w_branch_pool, v_w_out,
                                v_ffn_norm_pre, v_ffn_norm_post, v_ffn_w_gate, v_ffn_w_up, v_ffn_conv_w,
                                v_ffn_conv_b, v_ffn_w_down)))
    loss_local, grad_x, grads, delta, new_m, new_v = _train_step(x[0], loss_target[0], w, m, v)
    loss = lax.psum(loss_local, ("x", "y", "c"))
    return (loss, grad_x[None], *[grads[n] for n in WEIGHT_NAMES], *[delta[n] for n in WEIGHT_NAMES],
            *[new_m[n] for n in WEIGHT_NAMES], *[new_v[n] for n in WEIGHT_NAMES])
```

```python
import jax
import jax.numpy as jnp
from jax import lax
from jax.experimental import pallas as pl
from jax.experimental.pallas import tpu as pltpu

F32 = jnp.float32
BF = jnp.bfloat16

D_MODEL = 1024
N_META = 16
DEPTH = 2
HEADS = 4
HEAD_DIM = 128
HGRN_W = 512
POOL_W = 512
POOL_WINDOWS = (2, 4, 8, 16)
D_FF = 2816
IN_COLS = 4608
EPS = 1e-6
LOG_FLOOR = 1e-30
N_DEV = 8

ADAM_LR = 0.001
ADAM_B1 = 0.9
ADAM_B2 = 0.999
ADAM_EPS = 1e-08
ADAM_WD = 0.01
ADAM_STEP = 10

SUB = 16
CHUNK = 48
HGRN_UNROLL = 3
EXP_CLAMP = 80.0
ROW_TILE = 432
FFN_TILE = 144
HALO = 16
VMEM_LIMIT = 56 * 1024 * 1024
DW_BUFFER_BYTES = 44 * 1024 * 1024
MESH = pl.DeviceIdType.MESH

WEIGHT_NAMES = ['meta_tokens', 'mix_norm_pre', 'mix_norm_post', 'w_in', 'hgrn_lower_bounds', 'hgrn_out_norm',
                'w_branch_hgrn', 'pool_proj', 'pool_scale', 'w_branch_pool', 'w_out', 'ffn_norm_pre', 'ffn_norm_post',
                'ffn_w_gate', 'ffn_w_up', 'ffn_conv_w', 'ffn_conv_b', 'ffn_w_down']
SHARD_AXIS = {'meta_tokens': 1, 'w_in': 2, 'w_branch_hgrn': 2, 'w_branch_pool': 2, 'w_out': 1,
              'ffn_w_gate': 2, 'ffn_w_up': 2, 'ffn_conv_w': 2, 'ffn_w_down': 1}
F32_GATHERED = ['meta_tokens', 'ffn_conv_w']
REPLICATED = [n for n in WEIGHT_NAMES if n not in SHARD_AXIS]

W_ROWS = {'w_in': (0, 576),
          'wb': (576, 128),
          'w_out': (704, 128),
          'wg': (832, 352),
          'wu': (1184, 352),
          'wd': (1536, 352)}
SLAB_ROWS = 1888
EXCHANGE_GROUPS = {'wd': ['wd'], 'wg': ['wg'], 'rest': ['wu', 'wb', 'w_out'], 'in': ['w_in']}


def _params(sem=None):
    return pltpu.CompilerParams(dimension_semantics=sem, vmem_limit_bytes=VMEM_LIMIT)


def _tile(total, pref, mult=16):
    best = None
    for t in range(mult, min(total, pref) + 1, mult):
        if total % t == 0:
            best = t
    assert best is not None, (total, pref, mult)
    return best


def _whole(shape):
    return pl.BlockSpec(shape, lambda *_: (0,) * len(shape))


def _cols(tm, width, j):
    return pl.BlockSpec((tm, width), lambda i, j=j: (i, j))


def _dot(a, b):
    return jnp.dot(a, b, preferred_element_type=F32)


def _dot_nt(a, b):
    return lax.dot_general(a, b, (((1,), (1,)), ((), ())), preferred_element_type=F32)


def _dot_tn(a, b):
    return lax.dot_general(a, b, (((0,), (0,)), ((), ())), preferred_element_type=F32)


def _rms_fwd(x, g):
    r = lax.rsqrt(jnp.mean(x * x, axis=-1, keepdims=True) + EPS)
    return x * r * g


def _rms_bwd(dy, x, g):
    r = lax.rsqrt(jnp.mean(x * x, axis=-1, keepdims=True) + EPS)
    xh = x * r
    dyg = dy * g
    dx = r * (dyg - xh * jnp.mean(dyg * xh, axis=-1, keepdims=True))
    return dx, jnp.sum(dy * xh, axis=0, keepdims=True)


_GELU_C = 0.7978845608028654
_GELU_A = 0.044715


def _gelu_and_grad(x):
    x2 = x * x
    t = jnp.tanh(x * (_GELU_C + (_GELU_C * _GELU_A) * x2))
    u = 1.0 + t
    hx = 0.5 * x
    return hx * u, 0.5 * u + (hx * (1.0 - t * t)) * (_GELU_C + (3.0 * _GELU_C * _GELU_A) * x2)


def _split3(x):
    x1 = x.astype(BF)
    r1 = x - x1.astype(F32)
    x2 = r1.astype(BF)
    x3 = (r1 - x2.astype(F32)).astype(BF)
    return x1, x2, x3


def _tri_mm(tri, x):
    x1, x2, x3 = _split3(x)
    return _dot(tri, x1) + _dot(tri, x2) + _dot(tri, x3)


def _softmax2(lb_ref):
    l0 = lb_ref[0:1, :]
    l1 = lb_ref[1:2, :]
    m = jnp.maximum(l0, l1)
    e0 = jnp.exp(l0 - m)
    e1 = jnp.exp(l1 - m)
    return e0 / (e0 + e1), e1 / (e0 + e1)


def _layer_lower_bound(lb_ref, layer):
    g0, g1 = _softmax2(lb_ref)
    if layer == 0:
        return jnp.clip(g0 - g0, 0.0, 1.0)
    return jnp.clip((g0 + g1) - g0, 0.0, 1.0)


def _all_gather(x, name):
    n, w = x.shape

    def body(x_ref, out_ref, send_sems, recv_sems, local_sem):
        mx, my, mc = lax.axis_index("x"), lax.axis_index("y"), lax.axis_index("c")
        me, sibling = (mx, my, mc), (mx, my, 1 - mc)
        chips = [(1 - mx, my), (mx, 1 - my), (1 - mx, 1 - my)]

        def rows(px, py, pc):
            return out_ref.at[4 * px + 2 * py + pc]

        def copy(k, block, to, src=None):
            return pltpu.make_async_remote_copy(
                src_ref=rows(*block) if src is None else src, dst_ref=rows(*block),
                send_sem=send_sems.at[k], recv_sem=recv_sems.at[k], device_id=to, device_id_type=MESH)

        mine = pltpu.make_async_copy(x_ref, rows(*me), local_sem)
        mine.start()
        first = [copy(0, me, sibling, src=x_ref)]
        first += [copy(1 + j, me, (*chip, mc), src=x_ref) for j, chip in enumerate(chips)]
        for cp in first:
            cp.start()
        passed = [copy(4 + j, (*chip, mc), sibling) for j, chip in enumerate(chips)]
        for j, chip in enumerate(chips):
            copy(1 + j, (*chip, mc), me).wait_recv()
            passed[j].start()
        copy(0, sibling, me).wait_recv()
        for j, chip in enumerate(chips):
            copy(4 + j, (*chip, 1 - mc), me).wait_recv()
        for cp in first + passed:
            cp.wait_send()
        mine.wait()

    return pl.pallas_call(
        body, name=name,
        out_shape=jax.ShapeDtypeStruct((N_DEV, n, w), x.dtype),
        in_specs=[pl.BlockSpec(memory_space=pl.ANY)],
        out_specs=pl.BlockSpec(memory_space=pl.ANY),
        scratch_shapes=[pltpu.SemaphoreType.DMA((7,)), pltpu.SemaphoreType.DMA((7,)), pltpu.SemaphoreType.DMA],
    )(x)


GRAD_SRC = {'w_in': ('d_in', 0), 'wb': ('d_b', 0), 'w_out': ('d_out', 0), 'wg': ('d_gu', 0), 'wu': ('d_gu', D_FF),
            'wd': ('d_down', 0)}


class _GradExchange:
    def __init__(self, keys, grads):
        self.names = sorted({GRAD_SRC[k][0] for k in keys})
        self.arrays = [grads[n][0] for n in self.names] + [grads[n][1] for n in self.names]
        self.pieces, off = [], 0
        for k in keys:
            name, base = GRAD_SRC[k]
            rows = W_ROWS[k][1]
            self.pieces.append((self.names.index(name), base, rows, off))
            off += rows
        self.keys, self.rows = keys, off
        self.out_shape = (jax.ShapeDtypeStruct((N_DEV - 1, off, D_MODEL), BF), jax.ShapeDtypeStruct((off, D_MODEL), F32))
        self.scratch = [pltpu.SemaphoreType.DMA((N_DEV - 1,)), pltpu.SemaphoreType.DMA((N_DEV - 1,)),
                        pltpu.SemaphoreType.DMA((len(self.pieces),))]

    def _local(self, ins, outs, scr):
        mx, my, mc = lax.axis_index("x"), lax.axis_index("y"), lax.axis_index("c")
        me = 4 * mx + 2 * my + mc
        return [pltpu.make_async_copy(ins[ai].at[pl.ds(pl.multiple_of(base + rows * me, 8), rows)],
                                      outs[1].at[pl.ds(off, rows)], scr[2].at[k])
                for k, (ai, base, rows, off) in enumerate(self.pieces)]

    def start(self, ins, outs, scr):
        mx, my, mc = lax.axis_index("x"), lax.axis_index("y"), lax.axis_index("c")
        bf16 = ins[len(self.names):]
        for cp in self._local(ins, outs, scr):
            cp.start()
        for d in range(1, N_DEV):
            px = (1 - mx) if (d >> 2) & 1 else mx
            py = (1 - my) if (d >> 1) & 1 else my
            pc = (1 - mc) if d & 1 else mc
            peer = 4 * px + 2 * py + pc
            for ai, base, rows, off in self.pieces:
                pltpu.make_async_remote_copy(
                    src_ref=bf16[ai].at[pl.ds(pl.multiple_of(base + rows * peer, 16), rows)],
                    dst_ref=outs[0].at[d - 1, pl.ds(off, rows)], send_sem=scr[0].at[d - 1], recv_sem=scr[1].at[d - 1],
                    device_id=(px, py, pc), device_id_type=MESH).start()

    def wait(self, ins, outs, scr):
        me = (lax.axis_index("x"), lax.axis_index("y"), lax.axis_index("c"))
        for d in range(1, N_DEV):
            slot = pltpu.make_async_remote_copy(
                src_ref=outs[0].at[d - 1], dst_ref=outs[0].at[d - 1], send_sem=scr[0].at[d - 1],
                recv_sem=scr[1].at[d - 1], device_id=me, device_id_type=MESH)
            slot.wait_recv()
            slot.wait_send()
        for cp in self._local(ins, outs, scr):
            cp.wait()


class _SlabGather:
    def __init__(self, slab, row0, rows):
        self.arrays = [slab]
        self.row0, self.rows = row0, rows
        self.out_shape = (jax.ShapeDtypeStruct((N_DEV, rows, D_MODEL), slab.dtype),)
        self.scratch = [pltpu.SemaphoreType.DMA((7,)), pltpu.SemaphoreType.DMA((7,)), pltpu.SemaphoreType.DMA]

    def _parts(self, ins, outs, scr):
        mx, my, mc = lax.axis_index("x"), lax.axis_index("y"), lax.axis_index("c")
        me, sibling = (mx, my, mc), (mx, my, 1 - mc)
        chips = [(1 - mx, my), (mx, 1 - my), (1 - mx, 1 - my)]
        mine_ref = ins[0].at[pl.ds(self.row0, self.rows)]
        block = lambda px, py, pc: outs[0].at[4 * px + 2 * py + pc]

        def copy(k, blk, to, src=None):
            return pltpu.make_async_remote_copy(
                src_ref=block(*blk) if src is None else src, dst_ref=block(*blk),
                send_sem=scr[0].at[k], recv_sem=scr[1].at[k], device_id=to, device_id_type=MESH)

        local = pltpu.make_async_copy(mine_ref, block(*me), scr[2])
        first = [copy(0, me, sibling, src=mine_ref)]
        first += [copy(1 + j, me, (*chip, mc), src=mine_ref) for j, chip in enumerate(chips)]
        passed = [copy(4 + j, (*chip, mc), sibling) for j, chip in enumerate(chips)]
        landed = [copy(1 + j, (*chip, mc), me) for j, chip in enumerate(chips)]
        from_sibling = [copy(0, sibling, me)] + [copy(4 + j, (*chip, 1 - mc), me) for j, chip in enumerate(chips)]
        return local, first, passed, landed, from_sibling

    def start(self, ins, outs, scr):
        local, first, _, _, _ = self._parts(ins, outs, scr)
        local.start()
        for cp in first:
            cp.start()

    def wait(self, ins, outs, scr):
        local, first, passed, landed, from_sibling = self._parts(ins, outs, scr)
        for arrived, forward in zip(landed, passed):
            arrived.wait_recv()
            forward.start()
        for cp in from_sibling:
            cp.wait_recv()
        for cp in first + passed:
            cp.wait_send()
        local.wait()


class _Both:
    def __init__(self, first, second):
        self.parts = (first, second)
        self.arrays = list(first.arrays) + list(second.arrays)
        self.out_shape = tuple(first.out_shape) + tuple(second.out_shape)
        self.scratch = list(first.scratch) + list(second.scratch)

    def _split(self, ins, outs, scr):
        a, b = self.parts
        na, oa, sa = len(a.arrays), len(a.out_shape), len(a.scratch)
        return (a, ins[:na], outs[:oa], scr[:sa]), (b, ins[na:], outs[oa:], scr[sa:])

    def start(self, ins, outs, scr):
        for part, i, o, s in self._split(ins, outs, scr):
            part.start(i, o, s)

    def wait(self, ins, outs, scr):
        for part, i, o, s in self._split(ins, outs, scr):
            part.wait(i, o, s)


def _hosted_call(body, comm, name, grid, in_specs, args, out_shape, out_specs, scratch_shapes):
    if comm is None:
        return pl.pallas_call(body, name=name, grid=grid, in_specs=in_specs, out_specs=out_specs, out_shape=out_shape,
                              scratch_shapes=scratch_shapes, compiler_params=_params(("arbitrary",)))(*args), None
    n_in, n_out, n_scr, n_c = len(in_specs), len(out_shape), len(scratch_shapes), len(comm.arrays)
    n_co = len(comm.out_shape)
    steps = grid[0]

    def hosted(*refs):
        ins, cins = refs[:n_in], refs[n_in:n_in + n_c]
        outs = refs[n_in + n_c:n_in + n_c + n_out]
        couts = refs[n_in + n_c + n_out:n_in + n_c + n_out + n_co]
        scr = refs[n_in + n_c + n_out + n_co:n_in + n_c + n_out + n_co + n_scr]
        cscr = refs[n_in + n_c + n_out + n_co + n_scr:]

        @pl.when(pl.program_id(0) == 0)
        def _():
            comm.start(cins, couts, cscr)

        body(*ins, *outs, *scr)

        @pl.when(pl.program_id(0) == steps - 1)
        def _():
            comm.wait(cins, couts, cscr)

    res = pl.pallas_call(
        hosted, name=name, grid=grid, in_specs=list(in_specs) + [HBM_SPEC] * n_c,
        out_specs=tuple(out_specs) + (HBM_SPEC,) * n_co, out_shape=tuple(out_shape) + tuple(comm.out_shape),
        scratch_shapes=list(scratch_shapes) + comm.scratch, compiler_params=_params(("arbitrary",)),
    )(*args, *comm.arrays)
    return res[:n_out], res[n_out:]


def _sum_parts(own, recv, name):
    n, w = own.shape
    tn = _tile(n, 256, 16)

    def body(own_ref, p_ref, o_ref):
        acc = own_ref[...]
        for d in range(N_DEV - 1):
            acc = acc + p_ref[d].astype(F32)
        o_ref[...] = acc

    return pl.pallas_call(
        body, name=name, out_shape=jax.ShapeDtypeStruct((n, w), F32), grid=(n // tn,),
        in_specs=[pl.BlockSpec((tn, w), lambda i: (i, 0)), pl.BlockSpec((N_DEV - 1, tn, w), lambda i: (0, i, 0))],
        out_specs=pl.BlockSpec((tn, w), lambda i: (i, 0)),
        compiler_params=_params(("parallel",)),
    )(own, recv)


def _sum8(parts, name):
    _, n, w = parts.shape
    tn = _tile(n, 256 if w > 128 else 1024, 8)

    def body(p_ref, o_ref):
        acc = p_ref[0]
        for d in range(1, N_DEV):
            acc = acc + p_ref[d]
        o_ref[...] = acc

    return pl.pallas_call(
        body, name=name, out_shape=jax.ShapeDtypeStruct((n, w), F32), grid=(n // tn,),
        in_specs=[pl.BlockSpec((N_DEV, tn, w), lambda i: (0, i, 0))],
        out_specs=pl.BlockSpec((tn, w), lambda i: (i, 0)),
        compiler_params=_params(("parallel",)),
    )(parts)


def _pack(arrays, dtype, row_mult):
    flat = jnp.concatenate([a.astype(dtype).reshape(-1) for a in arrays])
    pad = (-flat.shape[0]) % (128 * row_mult)
    if pad:
        flat = jnp.concatenate([flat, jnp.zeros((pad,), dtype)])
    return flat.reshape(-1, 128)


def _unpack(flat2d, shapes):
    lead = flat2d.shape[:-2]
    flat = flat2d.reshape(lead + (-1,))
    out, off = [], 0
    for s in shapes:
        size = 1
        for d in s:
            size *= d
        out.append(flat[..., off:off + size].reshape(lead + tuple(s)))
        off += size
    return out


def _blocks_to_full(blocks, axis):
    moved = jnp.moveaxis(blocks, 0, axis)
    shape = list(moved.shape)
    shape[axis:axis + 2] = [shape[axis] * shape[axis + 1]]
    return moved.reshape(shape)


def _weight_scratch(keys):
    return ([pltpu.VMEM((N_DEV * W_ROWS[k][1], D_MODEL), BF) for k in keys]
            + [pltpu.SemaphoreType.DMA((N_DEV * len(keys),))])


def _fetch_weights(slabs_ref, keys, bufs, sems, base=0):
    @pl.when(pl.program_id(0) == 0)
    def _():
        copies = []
        for ki, (key, buf) in enumerate(zip(keys, bufs)):
            row0, rows = W_ROWS[key][0] - base, W_ROWS[key][1]
            for j in range(N_DEV):
                copies.append(pltpu.make_async_copy(slabs_ref.at[j, pl.ds(row0, rows)],
                                                    buf.at[pl.ds(j * rows, rows)], sems.at[ki * N_DEV + j]))
        for cp in copies:
            cp.start()
        for cp in copies:
            cp.wait()


HBM_SPEC = pl.BlockSpec(memory_space=pl.ANY)


def _norm_matmul(h, gain, slabs, keys, out_dtype, name, comm=None):
    t_rows, d = h.shape
    widths = [N_DEV * W_ROWS[k][1] for k in keys]
    tm = _tile(t_rows, ROW_TILE)

    def body(h_ref, g_ref, slabs_ref, o_ref, *scratch):
        bufs, sems = scratch[:-1], scratch[-1]
        _fetch_weights(slabs_ref, keys, bufs, sems, slabs[1])
        u = _rms_fwd(h_ref[...], g_ref[...]).astype(BF)
        off = 0
        for buf, n in zip(bufs, widths):
            o_ref[:, off:off + n] = _dot_nt(u, buf[...]).astype(out_dtype)
            off += n

    outs, moved = _hosted_call(
        body, comm, name, (t_rows // tm,),
        [pl.BlockSpec((tm, d), lambda i: (i, 0)), _whole((1, d)), HBM_SPEC], (h, gain, slabs[0]),
        (jax.ShapeDtypeStruct((t_rows, sum(widths)), out_dtype),),
        (pl.BlockSpec((tm, sum(widths)), lambda i: (i, 0)),), _weight_scratch(keys))
    return outs[0], moved


def _chunk_masks():
    row = lax.broadcasted_iota(jnp.int32, (CHUNK, CHUNK), 0)
    col = lax.broadcasted_iota(jnp.int32, (CHUNK, CHUNK), 1)
    tri_lo = (col <= row).astype(BF)
    tri_up = (col >= row).astype(BF)
    rb = jnp.right_shift(row, 4)
    cb = jnp.right_shift(col, 4)
    diag = (rb == cb) & (col <= row)
    off = [(rb == i) & (col < SUB * i) for i in range(1, CHUNK // SUB)]
    return tri_lo, tri_up, diag, off


def _chunk_gates(fl, lb):
    sg = jax.nn.sigmoid(fl)
    s2 = jax.nn.sigmoid(-fl)
    f = lb + (1.0 - lb) * sg
    lf = jnp.log(jnp.maximum(f, LOG_FLOOR))
    k = (1.0 - lb) * s2
    return sg, s2, f, lf, k


def _chunk_factors(q, k, b):
    nb = CHUNK // SUB
    rq = jnp.concatenate([jnp.broadcast_to(b[SUB * i:SUB * i + 1], (SUB, HEAD_DIM)) for i in range(nb)], axis=0)
    mq = jnp.concatenate([jnp.broadcast_to(b[SUB * i + SUB // 2:SUB * i + SUB // 2 + 1], (SUB, HEAD_DIM))
                          for i in range(nb)], axis=0)
    e_qs = jnp.exp(b - rq)
    e_qm = jnp.exp(jnp.minimum(b - mq, EXP_CLAMP))
    e_km = jnp.exp(jnp.minimum(mq - b, EXP_CLAMP))
    e_ks = [jnp.exp(jnp.minimum(b[SUB * i:SUB * i + 1] - b, 0.0)) for i in range(1, nb)]
    qs = (q * e_qs).astype(BF)
    qm = (q * e_qm).astype(BF)
    km = (k * e_km).astype(BF)
    ks = [(k * e).astype(BF) for e in e_ks]
    return e_qs, e_qm, e_km, e_ks, qs, qm, km, ks


def _chunk_scores(qs, qm, km, ks, diag, off):
    a = jnp.where(diag, _dot_nt(qm, km), 0.0)
    for m, kk in zip(off, ks):
        a = jnp.where(m, _dot_nt(qs, kk), a)
    return a


def _hgrn_fwd(proj, lower_bounds, layer, name, comm=None):
    t_rows = proj.shape[0]
    tm = _tile(t_rows, ROW_TILE, CHUNK)
    nct = tm // CHUNK

    def body(q_ref, f_ref, i_ref, lb_ref, o_ref, sall_ref, st_ref):
        @pl.when(pl.program_id(0) == 0)
        def _():
            st_ref[...] = jnp.zeros_like(st_ref)

        lbs = _layer_lower_bound(lb_ref, layer)
        tri_lo, _, diag, off = _chunk_masks()

        group = HGRN_UNROLL if nct % HGRN_UNROLL == 0 else 1

        def chunk_group(gi, carry):
            pairs = [(u, hh) for u in range(group) for hh in range(HEADS)]
            rows = [pl.ds(pl.multiple_of((gi * group + u) * CHUNK, SUB), CHUNK) for u in range(group)]
            cols = [slice(hh * HEAD_DIM, (hh + 1) * HEAD_DIM) for hh in range(HEADS)]
            q = {p: q_ref[rows[p[0]], cols[p[1]]] for p in pairs}
            vb = {p: i_ref[rows[p[0]], cols[p[1]]].astype(BF) for p in pairs}
            gates = {p: _chunk_gates(f_ref[rows[p[0]], cols[p[1]]], lbs[:, cols[p[1]]]) for p in pairs}
            b = {p: _tri_mm(tri_lo, gates[p][3]) for p in pairs}
            fac = {p: _chunk_factors(q[p], gates[p][4], b[p]) for p in pairs}
            a = {p: _chunk_scores(*fac[p][4:], diag, off) for p in pairs}
            o_intra = {p: _dot(a[p].astype(BF), vb[p]) for p in pairs}
            upd = {p: _dot_tn(vb[p], (gates[p][4] * jnp.exp(b[p][CHUNK - 1:CHUNK] - b[p])).astype(BF)) for p in pairs}
            qc = {p: (q[p] * jnp.exp(b[p])).astype(BF) for p in pairs}
            for hh in range(HEADS):
                st = st_ref[hh]
                for u in range(group):
                    p = (u, hh)
                    sall_ref[gi * group + u, hh] = st
                    o_ref[rows[u], cols[hh]] = _dot_nt(qc[p], st.astype(BF)) + o_intra[p]
                    st = st * jnp.exp(b[p][CHUNK - 1:CHUNK]) + upd[p]
                st_ref[hh] = st
            return carry

        lax.fori_loop(0, nct // group, chunk_group, 0)

    outs, moved = _hosted_call(
        body, comm, name, (t_rows // tm,),
        [_cols(tm, 512, 0), _cols(tm, 512, 1), _cols(tm, 512, 2), _whole((DEPTH, HGRN_W))],
        (proj, proj, proj, lower_bounds),
        (jax.ShapeDtypeStruct((t_rows, HGRN_W), F32),
         jax.ShapeDtypeStruct((t_rows // CHUNK, HEADS, HEAD_DIM, HEAD_DIM), F32)),
        (pl.BlockSpec((tm, HGRN_W), lambda i: (i, 0)),
         pl.BlockSpec((nct, HEADS, HEAD_DIM, HEAD_DIM), lambda i: (i, 0, 0, 0))),
        [pltpu.VMEM((HEADS, HEAD_DIM, HEAD_DIM), F32)])
    return (*outs, moved)


def _head_norm(o):
    outs, rs = [], []
    for hh in range(HEADS):
        oh = o[:, hh * HEAD_DIM:(hh + 1) * HEAD_DIM]
        r = lax.rsqrt(jnp.mean(oh * oh, axis=-1, keepdims=True) + EPS)
        outs.append(oh * r)
        rs.append(r)
    return outs, rs


def _window_counts(row0, rows):
    t1 = (row0 + lax.broadcasted_iota(jnp.int32, (rows, 1), 0) + 1).astype(F32)
    return [1.0 / jnp.minimum(t1, float(w)) for w in POOL_WINDOWS]


def _pool_fwd(v, halo, row0):
    rows = v.shape[0]
    inv = _window_counts(row0, rows)
    outs = []
    for gi, w in enumerate(POOL_WINDOWS):
        cs = slice(gi * HEAD_DIM, (gi + 1) * HEAD_DIM)
        s = jnp.concatenate([halo[:, cs], v[:, cs]], axis=0)
        step = 1
        while step < w:
            s = s + pltpu.roll(s, step, 0)
            step *= 2
        outs.append(s[HALO:] * inv[gi] - v[:, cs])
    return outs


def _pool_bwd(dpooled, halo, row0):
    rows = dpooled[0].shape[0]
    inv = _window_counts(row0, rows)
    inv_h = _window_counts(row0 + rows, HALO)
    outs = []
    for gi, w in enumerate(POOL_WINDOWS):
        s = jnp.concatenate([dpooled[gi] * inv[gi], halo[gi] * inv_h[gi]], axis=0)
        n_ext = s.shape[0]
        step = 1
        while step < w:
            s = s + pltpu.roll(s, n_ext - step, 0)
            step *= 2
        outs.append(s[:rows] - dpooled[gi])
    return outs


def _pool_project(pooled, pp_ref, scale):
    y = jnp.concatenate([_dot(pooled[gi].astype(BF), pp_ref[gi].astype(BF)) for gi in range(4)], axis=1)
    return y, y * scale


def _mix_merge(proj, o, h, out_gain, pool_proj, pool_scale, slabs, gain_post, name, comm=None):
    t_rows = h.shape[0]
    tm = _tile(t_rows, ROW_TILE)
    hpt = tm // HALO
    keys = ['wb', 'w_out']

    def body(g_ref, v_ref, vh_ref, ga0_ref, ga1_ref, gb0_ref, gb1_ref, o_ref, h_ref, og_ref, pp_ref, ps_ref,
             slabs_ref, gp_ref, hn_ref, r_ref, a_ref, p_ref, wb_ref, wo_ref, sems):
        _fetch_weights(slabs_ref, keys, (wb_ref, wo_ref), sems, slabs[1])
        i = pl.program_id(0)
        on, _ = _head_norm(o_ref[...])
        a = jnp.concatenate(on, axis=1) * og_ref[...] * jax.nn.sigmoid(g_ref[...])
        halo = jnp.where(i == 0, 0.0, vh_ref[...])
        pooled = _pool_fwd(v_ref[...], halo, i * tm)
        _, p = _pool_project(pooled, pp_ref, ps_ref[...])
        ab, pb = a.astype(BF), p.astype(BF)
        sa = jax.nn.sigmoid(jnp.concatenate([ga0_ref[...], ga1_ref[...]], axis=1))
        sb = jax.nn.sigmoid(jnp.concatenate([gb0_ref[...], gb1_ref[...]], axis=1))
        z = sa * _dot_nt(ab, wb_ref[:, 0:HGRN_W]) + sb * _dot_nt(pb, wb_ref[:, HGRN_W:HGRN_W + POOL_W])
        r = _dot(z.astype(BF), wo_ref[...])
        hn_ref[...] = h_ref[...] + _rms_fwd(r, gp_ref[...])
        r_ref[...] = r
        a_ref[...] = ab
        p_ref[...] = pb

    rows = lambda width: pl.BlockSpec((tm, width), lambda i: (i, 0))
    outs, moved = _hosted_call(
        body, comm, name, (t_rows // tm,),
        in_specs=[_cols(tm, 512, 3), _cols(tm, 512, 4),
                  pl.BlockSpec((HALO, 512), lambda i: (jnp.maximum(i * hpt - 1, 0), 4)),
                  _cols(tm, 512, 5), _cols(tm, 512, 6), _cols(tm, 512, 7), _cols(tm, 512, 8),
                  rows(HGRN_W), rows(D_MODEL), _whole((1, HGRN_W)), _whole((4, HEAD_DIM, HEAD_DIM)),
                  _whole((1, POOL_W)), HBM_SPEC, _whole((1, D_MODEL))],
        args=(proj, proj, proj, proj, proj, proj, proj, o, h, out_gain, pool_proj, pool_scale, slabs[0], gain_post),
        out_shape=(jax.ShapeDtypeStruct((t_rows, D_MODEL), F32), jax.ShapeDtypeStruct((t_rows, D_MODEL), F32),
                   jax.ShapeDtypeStruct((t_rows, HGRN_W), BF), jax.ShapeDtypeStruct((t_rows, POOL_W), BF)),
        out_specs=(rows(D_MODEL), rows(D_MODEL), rows(HGRN_W), rows(POOL_W)),
        scratch_shapes=_weight_scratch(keys))
    return (*outs, moved)


def _conv_fwd(g, halo, cw):
    ext = jnp.concatenate([halo, g], axis=0)
    return (cw[0:1] * pltpu.roll(ext, 2, 0)[8:] + cw[1:2] * pltpu.roll(ext, 1, 0)[8:] + cw[2:3] * g)


def _ffn_down(gu, h, conv_w, conv_b, slabs, gain_post, name):
    t_rows = h.shape[0]
    tm = _tile(t_rows, FFN_TILE)
    hpt = tm // HALO

    def body(g_ref, gh_ref, up_ref, h_ref, cw_ref, cb_ref, slabs_ref, gp_ref, hn_ref, y_ref, gl_ref, gg_ref,
             wd_ref, sems):
        _fetch_weights(slabs_ref, ['wd'], (wd_ref,), sems, slabs[1])
        i = pl.program_id(0)
        halo = jnp.where(i == 0, 0.0, gh_ref[8:HALO, :].astype(F32))
        gt = _conv_fwd(g_ref[...].astype(F32), halo, cw_ref[...]) + cb_ref[...]
        gl, gg = _gelu_and_grad(gt)
        gl_ref[...] = gl.astype(BF)
        gg_ref[...] = gg.astype(BF)
        act = gl * up_ref[...].astype(F32)
        y = _dot(act.astype(BF), wd_ref[...])
        hn_ref[...] = h_ref[...] + _rms_fwd(y, gp_ref[...])
        y_ref[...] = y

    rows = lambda width: pl.BlockSpec((tm, width), lambda i: (i, 0))
    return pl.pallas_call(
        body, name=name,
        out_shape=(jax.ShapeDtypeStruct((t_rows, D_MODEL), F32), jax.ShapeDtypeStruct((t_rows, D_MODEL), F32),
                   jax.ShapeDtypeStruct((t_rows, D_FF), BF), jax.ShapeDtypeStruct((t_rows, D_FF), BF)),
        grid=(t_rows // tm,),
        in_specs=[_cols(tm, D_FF, 0), pl.BlockSpec((HALO, D_FF), lambda i: (jnp.maximum(i * hpt - 1, 0), 0)),
                  _cols(tm, D_FF, 1), rows(D_MODEL), _whole((3, D_FF)), _whole((1, D_FF)),
                  HBM_SPEC, _whole((1, D_MODEL))],
        out_specs=(rows(D_MODEL), rows(D_MODEL), rows(D_FF), rows(D_FF)),
        scratch_shapes=_weight_scratch(['wd']),
        compiler_params=_params(("arbitrary",)),
    )(gu, gu, gu, h, conv_w, conv_b, slabs[0], gain_post)


def _loss_grad(h, target, name):
    t_rows, d = h.shape
    tm = _tile(t_rows, ROW_TILE)

    def body(h_ref, t_ref, dh_ref, l_ref):
        i = pl.program_id(0)

        @pl.when(i == 0)
        def _():
            l_ref[...] = jnp.zeros_like(l_ref)

        row = i * tm + lax.broadcasted_iota(jnp.int32, (tm, 1), 0)
        err = jnp.where(row >= N_META, h_ref[...] - t_ref[...], 0.0)
        dh_ref[...] = err * (1.0 / d)
        l_ref[...] += jnp.sum(err * err, axis=0, keepdims=True) * (0.5 / d)

    rows = pl.BlockSpec((tm, d), lambda i: (i, 0))
    return pl.pallas_call(
        body, name=name,
        out_shape=(jax.ShapeDtypeStruct((t_rows, d), F32), jax.ShapeDtypeStruct((1, d), F32)),
        grid=(t_rows // tm,), in_specs=[rows, rows], out_specs=(rows, _whole((1, d))),
        compiler_params=_params(("arbitrary",)),
    )(h, target)


def _dw(lhs, rhs, name, gain=None):
    t_rows = lhs[0].shape[0]
    paired = len(rhs) > 1
    n_rows = lhs[0].shape[1] if paired else sum(x.shape[1] for x in lhs)
    n_cols = sum(x.shape[1] for x in rhs)
    row_bytes = 2 * sum(x.shape[1] * x.dtype.itemsize for x in list(lhs) + list(rhs))
    tm = _tile(t_rows, max(ROW_TILE, (DW_BUFFER_BYTES - 4 * n_rows * n_cols) // row_bytes))
    last = t_rows // tm - 1

    def body(*refs):
        lhs_refs = refs[:len(lhs)]
        rhs_refs = refs[len(lhs):len(lhs) + len(rhs)]
        rest = refs[len(lhs) + len(rhs):]
        if gain is not None:
            g_ref, o_ref, o16_ref, acc, stage = rest
            rv = [_rms_fwd(rhs_refs[0][...], g_ref[...]).astype(BF)]
        else:
            o_ref, o16_ref, acc, stage = rest
            rv = [r[...] for r in rhs_refs]
        i = pl.program_id(0)

        @pl.when(i == 0)
        def _():
            acc[...] = jnp.zeros_like(acc)

        r0, c0 = 0, 0
        for p, l_ref in enumerate(lhs_refs):
            r = rv[p] if paired else rv[0]
            n = l_ref.shape[1]
            step = 512 if n % 512 == 0 else 256
            for s in range(0, n, step):
                acc[r0 + s:r0 + s + step, c0:c0 + r.shape[1]] += _dot_tn(l_ref[:, s:s + step], r)
            if paired:
                c0 += r.shape[1]
            else:
                r0 += n

        @pl.when(i == last)
        def _():
            pltpu.sync_copy(acc, o_ref)
            for s in range(0, n_rows, 256):
                stage[...] = acc[s:s + 256, :].astype(BF)
                pltpu.sync_copy(stage, o16_ref.at[pl.ds(s, 256)])

    rows = lambda x: pl.BlockSpec((tm, x.shape[1]), lambda i: (i, 0))
    in_specs = [rows(x) for x in lhs] + [rows(x) for x in rhs]
    args = list(lhs) + list(rhs)
    if gain is not None:
        in_specs.append(_whole(gain.shape))
        args.append(gain)
    return pl.pallas_call(
        body, name=name,
        out_shape=(jax.ShapeDtypeStruct((n_rows, n_cols), F32), jax.ShapeDtypeStruct((n_rows, n_cols), BF)),
        grid=(t_rows // tm,), in_specs=in_specs, out_specs=(HBM_SPEC, HBM_SPEC),
        scratch_shapes=[pltpu.VMEM((n_rows, n_cols), F32), pltpu.VMEM((256, n_cols), BF)],
        compiler_params=_params(("arbitrary",)),
    )(*args)


def _ffn_bwd_down(dh, y, gain_post, gu, gl, gg, slabs, name, comm=None):
    t_rows = dh.shape[0]
    tm = _tile(t_rows, FFN_TILE)
    hpt = tm // HALO

    def body(dh_ref, y_ref, gp_ref, g_ref, gh_ref, up_ref, gl_ref, gg_ref, slabs_ref,
             dy_ref, act_ref, dgt_ref, dup_ref, dgp_ref, dcw_ref, dcb_ref, wd_ref, sems):
        _fetch_weights(slabs_ref, ['wd'], (wd_ref,), sems, slabs[1])
        i = pl.program_id(0)

        @pl.when(i == 0)
        def _():
            dgp_ref[...] = jnp.zeros_like(dgp_ref)
            dcw_ref[...] = jnp.zeros_like(dcw_ref)
            dcb_ref[...] = jnp.zeros_like(dcb_ref)

        dy, dgp = _rms_bwd(dh_ref[...], y_ref[...], gp_ref[...])
        dgp_ref[...] += dgp
        dyb = dy.astype(BF)
        dy_ref[...] = dyb
        g = g_ref[...].astype(F32)
        up = up_ref[...].astype(F32)
        ext = jnp.concatenate([jnp.where(i == 0, 0.0, gh_ref[8:HALO, :].astype(F32)), g], axis=0)
        g2 = pltpu.roll(ext, 2, 0)[8:]
        g1 = pltpu.roll(ext, 1, 0)[8:]
        gl = gl_ref[...].astype(F32)
        act_ref[...] = (gl * up).astype(BF)
        dact = _dot_nt(dyb, wd_ref[...])
        dup_ref[...] = (dact * gl).astype(BF)
        dgt = dact * up * gg_ref[...].astype(F32)
        dgt_ref[...] = dgt.astype(BF)
        dcb_ref[...] += jnp.sum(dgt, axis=0, keepdims=True)
        dcw_ref[...] += jnp.concatenate([jnp.sum(dgt * g2, axis=0, keepdims=True),
                                         jnp.sum(dgt * g1, axis=0, keepdims=True),
                                         jnp.sum(dgt * g, axis=0, keepdims=True)], axis=0)

    rows = lambda width: pl.BlockSpec((tm, width), lambda i: (i, 0))
    bf = lambda width: jax.ShapeDtypeStruct((t_rows, width), BF)
    outs, exchanged = _hosted_call(
        body, comm, name, (t_rows // tm,),
        [rows(D_MODEL), rows(D_MODEL), _whole((1, D_MODEL)), _cols(tm, D_FF, 0),
         pl.BlockSpec((HALO, D_FF), lambda i: (jnp.maximum(i * hpt - 1, 0), 0)), _cols(tm, D_FF, 1),
         rows(D_FF), rows(D_FF), HBM_SPEC],
        (dh, y, gain_post, gu, gu, gu, gl, gg, slabs[0]),
        (bf(D_MODEL), bf(D_FF), bf(D_FF), bf(D_FF), jax.ShapeDtypeStruct((1, D_MODEL), F32),
         jax.ShapeDtypeStruct((3, D_FF), F32), jax.ShapeDtypeStruct((1, D_FF), F32)),
        (rows(D_MODEL), rows(D_FF), rows(D_FF), rows(D_FF), _whole((1, D_MODEL)), _whole((3, D_FF)),
         _whole((1, D_FF))),
        _weight_scratch(['wd']))
    return (*outs, exchanged)


def _ffn_bwd_up(dgt, dup, conv_w, slabs, h, gain_pre, dh_out, name, comm=None):
    t_rows = h.shape[0]
    tm = _tile(t_rows, FFN_TILE)
    hpt = tm // HALO
    last = t_rows // tm - 1

    def body(dgt_ref, dgn_ref, dup_ref, cw_ref, slabs_ref, h_ref, gp_ref, dho_ref, dh_ref, dgu_ref, dg_ref,
             wg_ref, wu_ref, sems):
        _fetch_weights(slabs_ref, ['wg', 'wu'], (wg_ref, wu_ref), sems, slabs[1])
        i = pl.program_id(0)

        @pl.when(i == 0)
        def _():
            dg_ref[...] = jnp.zeros_like(dg_ref)

        dgt = dgt_ref[...].astype(F32)
        ext = jnp.concatenate([dgt, jnp.where(i == last, 0.0, dgn_ref[0:8, :].astype(F32))], axis=0)
        n_ext = tm + 8
        cw = cw_ref[...]
        dg = (cw[2:3] * dgt + cw[1:2] * pltpu.roll(ext, n_ext - 1, 0)[:tm]
              + cw[0:1] * pltpu.roll(ext, n_ext - 2, 0)[:tm])
        dgb = dg.astype(BF)
        dub = dup_ref[...]
        dgu_ref[:, 0:D_FF] = dgb
        dgu_ref[:, D_FF:2 * D_FF] = dub
        du = _dot(dgb, wg_ref[...]) + _dot(dub, wu_ref[...])
        dx, dgain = _rms_bwd(du, h_ref[...], gp_ref[...])
        dh_ref[...] = dho_ref[...] + dx
        dg_ref[...] += dgain

    rows = lambda width: pl.BlockSpec((tm, width), lambda i: (i, 0))
    outs, moved = _hosted_call(
        body, comm, name, (t_rows // tm,),
        [rows(D_FF), pl.BlockSpec((HALO, D_FF), lambda i: (jnp.minimum((i + 1) * hpt, (last + 1) * hpt - 1), 0)),
         rows(D_FF), _whole((3, D_FF)), HBM_SPEC, rows(D_MODEL), _whole((1, D_MODEL)), rows(D_MODEL)],
        (dgt, dgt, dup, conv_w, slabs[0], h, gain_pre, dh_out),
        (jax.ShapeDtypeStruct((t_rows, D_MODEL), F32), jax.ShapeDtypeStruct((t_rows, 2 * D_FF), BF),
         jax.ShapeDtypeStruct((1, D_MODEL), F32)),
        (rows(D_MODEL), rows(2 * D_FF), _whole((1, D_MODEL))),
        _weight_scratch(['wg', 'wu']))
    return (*outs, moved)


def _mix_bwd_a(dh, r, gain_post, proj, a, p, slabs, name, comm=None):
    t_rows = dh.shape[0]
    tm = _tile(t_rows, ROW_TILE)

    def body(dh_ref, r_ref, gp_ref, ga0_ref, ga1_ref, gb0_ref, gb1_ref, a_ref, p_ref, slabs_ref,
             dr_ref, z_ref, dya_ref, dyp_ref, da_ref, dp_ref, dgab_ref, dg_ref, wb_ref, wo_ref, sems):
        _fetch_weights(slabs_ref, ['wb', 'w_out'], (wb_ref, wo_ref), sems, slabs[1])

        @pl.when(pl.program_id(0) == 0)
        def _():
            dg_ref[...] = jnp.zeros_like(dg_ref)

        dr, dgain = _rms_bwd(dh_ref[...], r_ref[...], gp_ref[...])
        dg_ref[...] += dgain
        drb = dr.astype(BF)
        dr_ref[...] = drb
        dz = _dot_nt(drb, wo_ref[...])
        ya = _dot_nt(a_ref[...], wb_ref[:, 0:HGRN_W])
        yp = _dot_nt(p_ref[...], wb_ref[:, HGRN_W:HGRN_W + POOL_W])
        sa = jax.nn.sigmoid(jnp.concatenate([ga0_ref[...], ga1_ref[...]], axis=1))
        sb = jax.nn.sigmoid(jnp.concatenate([gb0_ref[...], gb1_ref[...]], axis=1))
        z_ref[...] = (sa * ya + sb * yp).astype(BF)
        dgab_ref[:, 0:D_MODEL] = (dz * ya * sa * (1.0 - sa)).astype(BF)
        dgab_ref[:, D_MODEL:2 * D_MODEL] = (dz * yp * sb * (1.0 - sb)).astype(BF)
        dya = (dz * sa).astype(BF)
        dyp = (dz * sb).astype(BF)
        dya_ref[...] = dya
        dyp_ref[...] = dyp
        da_ref[...] = _dot(dya, wb_ref[:, 0:HGRN_W])
        dp_ref[...] = _dot(dyp, wb_ref[:, HGRN_W:HGRN_W + POOL_W])

    rows = lambda width: pl.BlockSpec((tm, width), lambda i: (i, 0))
    bf = lambda width: jax.ShapeDtypeStruct((t_rows, width), BF)
    f32 = lambda width: jax.ShapeDtypeStruct((t_rows, width), F32)
    outs, moved = _hosted_call(
        body, comm, name, (t_rows // tm,),
        [rows(D_MODEL), rows(D_MODEL), _whole((1, D_MODEL)),
         _cols(tm, 512, 5), _cols(tm, 512, 6), _cols(tm, 512, 7), _cols(tm, 512, 8),
         rows(HGRN_W), rows(POOL_W), HBM_SPEC],
        (dh, r, gain_post, proj, proj, proj, proj, a, p, slabs[0]),
        (bf(D_MODEL), bf(D_MODEL), bf(D_MODEL), bf(D_MODEL), f32(HGRN_W), f32(POOL_W), bf(2 * D_MODEL),
         jax.ShapeDtypeStruct((1, D_MODEL), F32)),
        (rows(D_MODEL), rows(D_MODEL), rows(D_MODEL), rows(D_MODEL), rows(HGRN_W), rows(POOL_W),
         rows(2 * D_MODEL), _whole((1, D_MODEL))),
        _weight_scratch(['wb', 'w_out']))
    return (*outs, moved)


def _mix_bwd_b(da, dp, proj, o, out_gain, pool_proj, pool_scale, name):
    t_rows = da.shape[0]
    tm = _tile(t_rows, ROW_TILE)
    hpt = tm // HALO
    last = t_rows // tm - 1

    def body(da_ref, dp_ref, dpn_ref, g_ref, v_ref, vh_ref, o_ref, og_ref, pp_ref, ps_ref,
             do_ref, dgv_ref, dog_ref, dpp_ref, dps_ref):
        i = pl.program_id(0)

        @pl.when(i == 0)
        def _():
            dog_ref[...] = jnp.zeros_like(dog_ref)
            dpp_ref[...] = jnp.zeros_like(dpp_ref)
            dps_ref[...] = jnp.zeros_like(dps_ref)

        da = da_ref[...]
        og = og_ref[...]
        o = o_ref[...]
        on, rs = _head_norm(o)
        onc = jnp.concatenate(on, axis=1)
        sg = jax.nn.sigmoid(g_ref[...])
        dog_ref[...] += jnp.sum(da * onc * sg, axis=0, keepdims=True)
        dgv_ref[:, 0:HGRN_W] = (da * onc * og * sg * (1.0 - sg)).astype(BF)
        don = da * og * sg
        for hh in range(HEADS):
            cs = slice(hh * HEAD_DIM, (hh + 1) * HEAD_DIM)
            d = don[:, cs]
            do_ref[:, cs] = rs[hh] * (d - on[hh] * jnp.mean(d * on[hh], axis=-1, keepdims=True))

        scale = ps_ref[...]
        halo = jnp.where(i == 0, 0.0, vh_ref[...])
        pooled = _pool_fwd(v_ref[...], halo, i * tm)
        y, _ = _pool_project(pooled, pp_ref, scale)
        dp = dp_ref[...]
        dps_ref[...] += jnp.sum(dp * y, axis=0, keepdims=True)
        dy = dp * scale
        dyn = jnp.where(i == last, 0.0, dpn_ref[...]) * scale
        dpooled, dhalo = [], []
        for gi in range(4):
            cs = slice(gi * HEAD_DIM, (gi + 1) * HEAD_DIM)
            ppb = pp_ref[gi].astype(BF)
            dyb = dy[:, cs].astype(BF)
            dpooled.append(_dot_nt(dyb, ppb))
            dhalo.append(_dot_nt(dyn[:, cs].astype(BF), ppb))
            dpp_ref[gi] += _dot_tn(pooled[gi].astype(BF), dyb)
        dv = _pool_bwd(dpooled, dhalo, i * tm)
        dgv_ref[:, HGRN_W:HGRN_W + POOL_W] = jnp.concatenate(dv, axis=1).astype(BF)

    rows = lambda width: pl.BlockSpec((tm, width), lambda i: (i, 0))
    return pl.pallas_call(
        body, name=name,
        out_shape=(jax.ShapeDtypeStruct((t_rows, HGRN_W), F32), jax.ShapeDtypeStruct((t_rows, HGRN_W + POOL_W), BF),
                   jax.ShapeDtypeStruct((1, HGRN_W), F32), jax.ShapeDtypeStruct((4, HEAD_DIM, HEAD_DIM), F32),
                   jax.ShapeDtypeStruct((1, POOL_W), F32)),
        grid=(t_rows // tm,),
        in_specs=[rows(HGRN_W), rows(POOL_W),
                  pl.BlockSpec((HALO, POOL_W), lambda i: (jnp.minimum((i + 1) * hpt, (last + 1) * hpt - 1), 0)),
                  _cols(tm, 512, 3), _cols(tm, 512, 4),
                  pl.BlockSpec((HALO, 512), lambda i: (jnp.maximum(i * hpt - 1, 0), 4)),
                  rows(HGRN_W), _whole((1, HGRN_W)), _whole((4, HEAD_DIM, HEAD_DIM)), _whole((1, POOL_W))],
        out_specs=(rows(HGRN_W), rows(HGRN_W + POOL_W), _whole((1, HGRN_W)), _whole((4, HEAD_DIM, HEAD_DIM)),
                   _whole((1, POOL_W))),
        compiler_params=_params(("arbitrary",)),
    )(da, dp, dp, proj, proj, proj, o, out_gain, pool_proj, pool_scale)


def _hgrn_bwd(proj, lower_bounds, layer, states, do, name, comm=None):
    t_rows = proj.shape[0]
    tm = _tile(t_rows, ROW_TILE, CHUNK)
    nct = tm // CHUNK
    n_tiles = t_rows // tm

    def body(q_ref, f_ref, i_ref, lb_ref, sall_ref, do_ref, dqfi_ref, dlb_ref, dst_ref):
        @pl.when(pl.program_id(0) == 0)
        def _():
            dst_ref[...] = jnp.zeros_like(dst_ref)
            dlb_ref[...] = jnp.zeros_like(dlb_ref)

        lbs = _layer_lower_bound(lb_ref, layer)
        tri_lo, tri_up, diag, off = _chunk_masks()
        is_last_row = lax.broadcasted_iota(jnp.int32, (CHUNK, 1), 0) == CHUNK - 1

        group = HGRN_UNROLL if nct % HGRN_UNROLL == 0 else 1

        def chunk_group(gi, carry):
            pairs = [(u, hh) for u in range(group) for hh in range(HEADS)]
            cidx = [nct - 1 - (gi * group + u) for u in range(group)]
            rows = [pl.ds(pl.multiple_of(c * CHUNK, SUB), CHUNK) for c in cidx]
            cols = [slice(hh * HEAD_DIM, (hh + 1) * HEAD_DIM) for hh in range(HEADS)]
            lb = {p: lbs[:, cols[p[1]]] for p in pairs}
            q = {p: q_ref[rows[p[0]], cols[p[1]]] for p in pairs}
            vb = {p: i_ref[rows[p[0]], cols[p[1]]].astype(BF) for p in pairs}
            dob = {p: do_ref[rows[p[0]], cols[p[1]]].astype(BF) for p in pairs}
            gates = {p: _chunk_gates(f_ref[rows[p[0]], cols[p[1]]], lb[p]) for p in pairs}
            k = {p: gates[p][4] for p in pairs}
            b = {p: _tri_mm(tri_lo, gates[p][3]) for p in pairs}
            fac = {p: _chunk_factors(q[p], k[p], b[p]) for p in pairs}
            e_b = {p: jnp.exp(b[p]) for p in pairs}
            e_last = {p: jnp.exp(b[p][CHUNK - 1:CHUNK]) for p in pairs}
            e_kl = {p: jnp.exp(b[p][CHUNK - 1:CHUNK] - b[p]) for p in pairs}
            qc = {p: (q[p] * e_b[p]).astype(BF) for p in pairs}
            kdec = {p: (k[p] * e_kl[p]).astype(BF) for p in pairs}
            a = {p: _chunk_scores(*fac[p][4:], diag, off) for p in pairs}
            da_full = {p: _dot_nt(dob[p], vb[p]) for p in pairs}
            dv, dq, dk, upd, dupd = {}, {}, {}, {}, {}
            for p in pairs:
                e_qs, e_qm, e_km, e_ks, qs, qm, km, ks = fac[p]
                da_d = jnp.where(diag, da_full[p], 0.0).astype(BF)
                dv[p] = _dot_tn(a[p].astype(BF), dob[p])
                dq[p] = e_qm * _dot(da_d, km)
                dk[p] = e_km * _dot_tn(da_d, qm)
                for m, kk, ek in zip(off, ks, e_ks):
                    da_i = jnp.where(m, da_full[p], 0.0).astype(BF)
                    dq[p] = dq[p] + e_qs * _dot(da_i, kk)
                    dk[p] = dk[p] + ek * _dot_tn(da_i, qs)
                upd[p] = _dot_tn(vb[p], kdec[p])
                dupd[p] = _dot_tn(dob[p], qc[p])
            db = {}
            for hh in range(HEADS):
                dst = dst_ref[hh]
                for u in range(group):
                    p = (u, hh)
                    st = sall_ref[cidx[u], hh]
                    dstb = dst.astype(BF)
                    dv[p] = dv[p] + _dot_nt(kdec[p], dstb)
                    dq[p] = dq[p] + e_b[p] * _dot(dob[p], st.astype(BF))
                    dk[p] = dk[p] + e_kl[p] * _dot(vb[p], dstb)
                    st_new = st * e_last[p] + upd[p]
                    db[p] = (q[p] * dq[p] - k[p] * dk[p]
                             + jnp.where(is_last_row, jnp.sum(st_new * dst, axis=0, keepdims=True), 0.0))
                    dst = dst * e_last[p] + dupd[p]
                dst_ref[hh] = dst
            for p in pairs:
                u, hh = p
                sg, s2, f = gates[p][:3]
                dlf = _tri_mm(tri_up, db[p])
                df = jnp.where(f > LOG_FLOOR, dlf / f, 0.0)
                dfl = df * (1.0 - lb[p]) * sg * (1.0 - sg) - dk[p] * (1.0 - lb[p]) * s2 * (1.0 - s2)
                dlb_ref[:, cols[hh]] += jnp.sum(df * (1.0 - sg) - dk[p] * s2, axis=0, keepdims=True)
                dqfi_ref[rows[u], cols[hh]] = dq[p].astype(BF)
                dqfi_ref[rows[u], pl.ds(HGRN_W + hh * HEAD_DIM, HEAD_DIM)] = dfl.astype(BF)
                dqfi_ref[rows[u], pl.ds(2 * HGRN_W + hh * HEAD_DIM, HEAD_DIM)] = dv[p].astype(BF)
            return carry

        lax.fori_loop(0, nct // group, chunk_group, 0)

    rev = lambda width, j: pl.BlockSpec((tm, width), lambda i, j=j: (n_tiles - 1 - i, j))
    outs, exchanged = _hosted_call(
        body, comm, name, (n_tiles,),
        [rev(512, 0), rev(512, 1), rev(512, 2), _whole((DEPTH, HGRN_W)),
         pl.BlockSpec((nct, HEADS, HEAD_DIM, HEAD_DIM), lambda i: (n_tiles - 1 - i, 0, 0, 0)), rev(HGRN_W, 0)],
        (proj, proj, proj, lower_bounds, states, do),
        (jax.ShapeDtypeStruct((t_rows, 3 * HGRN_W), BF), jax.ShapeDtypeStruct((1, HGRN_W), F32)),
        (rev(3 * HGRN_W, 0), _whole((1, HGRN_W))),
        [pltpu.VMEM((HEADS, HEAD_DIM, HEAD_DIM), F32)])
    return (*outs, exchanged)


def _in_bwd(dqfi, dgv, dgab, slabs, h, gain_pre, dh_out, name, comm=None):
    t_rows = h.shape[0]
    tm = _tile(t_rows, ROW_TILE)
    c1 = 3 * HGRN_W
    c2 = c1 + HGRN_W + POOL_W

    def body(d1_ref, d2_ref, d3_ref, slabs_ref, h_ref, gp_ref, dho_ref, dh_ref, dg_ref, w_ref, sems):
        _fetch_weights(slabs_ref, ['w_in'], (w_ref,), sems, slabs[1])

        @pl.when(pl.program_id(0) == 0)
        def _():
            dg_ref[...] = jnp.zeros_like(dg_ref)

        du = (_dot(d1_ref[...], w_ref[0:c1, :]) + _dot(d2_ref[...], w_ref[c1:c2, :])
              + _dot(d3_ref[...], w_ref[c2:IN_COLS, :]))
        dx, dgain = _rms_bwd(du, h_ref[...], gp_ref[...])
        dh_ref[...] = dho_ref[...] + dx
        dg_ref[...] += dgain

    rows = lambda width: pl.BlockSpec((tm, width), lambda i: (i, 0))
    outs, exchanged = _hosted_call(
        body, comm, name, (t_rows // tm,),
        [rows(c1), rows(c2 - c1), rows(IN_COLS - c2), HBM_SPEC, rows(D_MODEL), _whole((1, D_MODEL)), rows(D_MODEL)],
        (dqfi, dgv, dgab, slabs[0], h, gain_pre, dh_out),
        (jax.ShapeDtypeStruct((t_rows, D_MODEL), F32), jax.ShapeDtypeStruct((1, D_MODEL), F32)),
        (rows(D_MODEL), _whole((1, D_MODEL))),
        _weight_scratch(['w_in']))
    return (*outs, exchanged)


def _lower_bound_grad(lower_bounds, dlbs, name):
    def body(lb_ref, d_ref, o_ref):
        g0, g1 = _softmax2(lb_ref)
        bound = (g0 + g1) - g0
        inside = (bound > 0.0) & (bound < 1.0)
        dg1 = jnp.where(inside, d_ref[1:2, :], 0.0)
        inner = g1 * dg1
        o_ref[0:1, :] = g0 * (0.0 - inner)
        o_ref[1:2, :] = g1 * (dg1 - inner)

    return pl.pallas_call(body, name=name, out_shape=jax.ShapeDtypeStruct(lower_bounds.shape, F32))(lower_bounds, dlbs)


def _adamw(w, g, m, v, name):
    r, c = w.shape
    tr = r if (r % 8 or r <= 512) else _tile(r, 512, 8)
    c1 = 1.0 - ADAM_B1 ** ADAM_STEP
    c2 = 1.0 - ADAM_B2 ** ADAM_STEP

    def body(w_ref, g_ref, m_ref, v_ref, d_ref, nm_ref, nv_ref):
        gg = g_ref[...]
        nm = ADAM_B1 * m_ref[...] + (1.0 - ADAM_B1) * gg
        nv = ADAM_B2 * v_ref[...] + (1.0 - ADAM_B2) * (gg * gg)
        d_ref[...] = -ADAM_LR * ((nm / c1) / (jnp.sqrt(nv / c2) + ADAM_EPS) + ADAM_WD * w_ref[...])
        nm_ref[...] = nm
        nv_ref[...] = nv

    blk = pl.BlockSpec((tr, c), lambda i: (i, 0))
    shp = jax.ShapeDtypeStruct((r, c), F32)
    return pl.pallas_call(
        body, name=name, out_shape=(shp, shp, shp), grid=(r // tr,),
        in_specs=[blk, blk, blk, blk], out_specs=(blk, blk, blk),
        compiler_params=_params(("parallel",)),
    )(w, g, m, v)


def _as2d(a):
    return a.reshape(-1, a.shape[-1])


def _layer_slab(w, l):
    t = lambda a: jnp.swapaxes(a, 0, 1)
    parts = [t(w['w_in'][l]), jnp.concatenate([t(w['w_branch_hgrn'][l]), t(w['w_branch_pool'][l])], axis=1),
             w['w_out'][l], t(w['ffn_w_gate'][l]), t(w['ffn_w_up'][l]), w['ffn_w_down'][l]]
    return jnp.concatenate(parts, axis=0).astype(BF)


def _slab_grads(sums):
    t = lambda a: jnp.swapaxes(a, 0, 1)
    wb = sums['wb']
    return {'w_in': t(sums['w_in']), 'w_branch_hgrn': t(wb[:, :HGRN_W]), 'w_branch_pool': t(wb[:, HGRN_W:]),
            'w_out': sums['w_out'], 'ffn_w_gate': t(sums['wg']), 'ffn_w_up': t(sums['wu']), 'ffn_w_down': sums['wd']}


def _train_step(x, target, w, m, v):
    my_slabs = [_layer_slab(w, l) for l in range(DEPTH)]
    n_in = W_ROWS['w_in'][1]
    slabs = [{} for _ in range(DEPTH)]
    slabs[0]['w_in'] = (_all_gather(my_slabs[0][:n_in], "gather_w_in_0"), 0)
    f32_shapes = [w[n].shape for n in F32_GATHERED]
    gathered32 = _all_gather(_pack([w[n] for n in F32_GATHERED], F32, 8), "gather_meta_conv")
    full = {n: _blocks_to_full(blk, SHARD_AXIS[n]) for n, blk in zip(F32_GATHERED, _unpack(gathered32, f32_shapes))}
    row = lambda name, l: w[name][l][None]

    h = jnp.concatenate([full['meta_tokens'], x], axis=0)
    saved = []
    for l in range(DEPTH):
        s = {'h_in': h}
        nxt = l + 1 if l + 1 < DEPTH else None
        plan = {'in_proj': [(0, ['wb', 'w_out', 'wg', 'wu'])] if l == 0 else [],
                'hgrn_fwd': ([(0, ['wd'])] if l == 0 else []) + ([(nxt, ['w_in'])] if nxt else []),
                'mix_merge': [(nxt, ['wb', 'w_out']), (nxt, ['wd'])] if nxt else [],
                'ffn_proj': [(nxt, ['wg', 'wu'])] if nxt else []}

        def hosted(kernel):
            comms = []
            for layer, keys in plan[kernel]:
                first, last = W_ROWS[keys[0]], W_ROWS[keys[-1]]
                comms.append(_SlabGather(my_slabs[layer], first[0], last[0] + last[1] - first[0]))
            return None if not comms else comms[0] if len(comms) == 1 else _Both(*comms)

        def landed(kernel, moved):
            for (layer, keys), arr in zip(plan[kernel], moved or ()):
                slabs[layer].update({k: (arr, W_ROWS[keys[0]][0]) for k in keys})

        s['proj'], moved = _norm_matmul(h, row('mix_norm_pre', l), slabs[l]['w_in'], ['w_in'], F32, f"in_proj_{l}",
                                        comm=hosted('in_proj'))
        landed('in_proj', moved)
        s['o'], s['states'], moved = _hgrn_fwd(s['proj'], w['hgrn_lower_bounds'], l, f"hgrn_fwd_{l}",
                                               comm=hosted('hgrn_fwd'))
        landed('hgrn_fwd', moved)
        h, s['r'], s['a'], s['p'], moved = _mix_merge(
            s['proj'], s['o'], h, row('hgrn_out_norm', l), w['pool_proj'][l], row('pool_scale', l), slabs[l]['wb'],
            row('mix_norm_post', l), f"mix_merge_{l}", comm=hosted('mix_merge'))
        landed('mix_merge', moved)
        s['h_mid'] = h
        s['gu'], moved = _norm_matmul(h, row('ffn_norm_pre', l), slabs[l]['wg'], ['wg', 'wu'], BF, f"ffn_proj_{l}",
                                      comm=hosted('ffn_proj'))
        landed('ffn_proj', moved)
        h, s['y'], s['gl'], s['gg'] = _ffn_down(s['gu'], h, full['ffn_conv_w'][l], row('ffn_conv_b', l),
                                                slabs[l]['wd'], row('ffn_norm_post', l), f"ffn_down_{l}")
        saved.append(s)

    dh, loss_cols = _loss_grad(h, jnp.pad(target, ((N_META, 0), (0, 0))), "loss_grad")
    loss = jnp.sum(loss_cols)

    small = {n: [None] * DEPTH for n in REPLICATED + ['ffn_conv_w']}
    exchanged = [{} for _ in range(DEPTH)]
    pending = None
    for l in reversed(range(DEPTH)):
        s = saved[l]
        g = {}
        (dy, act, dgt, dup, small['ffn_norm_post'][l], small['ffn_conv_w'][l], small['ffn_conv_b'][l],
         done) = _ffn_bwd_down(dh, s['y'], row('ffn_norm_post', l), s['gu'], s['gl'], s['gg'], slabs[l]['wd'],
                               f"ffn_bwd_down_{l}", comm=pending)
        if pending is not None:
            exchanged[l + 1]['in'] = done
        g['d_down'] = _dw([act], [dy], f"dw_down_{l}")
        dh, dgu, small['ffn_norm_pre'][l], exchanged[l]['wd'] = _ffn_bwd_up(
            dgt, dup, full['ffn_conv_w'][l], slabs[l]['wg'], s['h_mid'], row('ffn_norm_pre', l), dh,
            f"ffn_bwd_up_{l}", comm=_GradExchange(EXCHANGE_GROUPS['wd'], g))
        g['d_gu'] = _dw([dgu], [s['h_mid']], f"dw_gate_up_{l}", gain=row('ffn_norm_pre', l))

        dr, z, dya, dyp, da, dp, dgab, small['mix_norm_post'][l], exchanged[l]['wg'] = _mix_bwd_a(
            dh, s['r'], row('mix_norm_post', l), s['proj'], s['a'], s['p'], slabs[l]['wb'], f"mix_bwd_a_{l}",
            comm=_GradExchange(EXCHANGE_GROUPS['wg'], g))
        g['d_out'] = _dw([z], [dr], f"dw_out_{l}")
        g['d_b'] = _dw([dya, dyp], [s['a'], s['p']], f"dw_branch_{l}")
        do, dgv, small['hgrn_out_norm'][l], small['pool_proj'][l], small['pool_scale'][l] = _mix_bwd_b(
            da, dp, s['proj'], s['o'], row('hgrn_out_norm', l), w['pool_proj'][l], row('pool_scale', l),
            f"mix_bwd_b_{l}")
        dqfi, small['hgrn_lower_bounds'][l], exchanged[l]['rest'] = _hgrn_bwd(
            s['proj'], w['hgrn_lower_bounds'], l, s['states'], do, f"hgrn_bwd_{l}",
            comm=_GradExchange(EXCHANGE_GROUPS['rest'], g))
        g['d_in'] = _dw([dqfi, dgv, dgab], [s['h_in']], f"dw_in_{l}", gain=row('mix_norm_pre', l))
        pending = _GradExchange(EXCHANGE_GROUPS['in'], g)
        if l == 0:
            dh, small['mix_norm_pre'][l], exchanged[l]['in'] = _in_bwd(
                dqfi, dgv, dgab, slabs[l]['w_in'], s['h_in'], row('mix_norm_pre', l), dh, f"in_bwd_{l}", comm=pending)
        else:
            dh, small['mix_norm_pre'][l], _ = _in_bwd(
                dqfi, dgv, dgab, slabs[l]['w_in'], s['h_in'], row('mix_norm_pre', l), dh, f"in_bwd_{l}")

    grad_x = dh[N_META:]

    per_layer = []
    for l in range(DEPTH):
        sums = {}
        for part, keys in EXCHANGE_GROUPS.items():
            recv, own = exchanged[l][part]
            total = _sum_parts(own, recv, f"sum_grads_{part}_{l}")
            off = 0
            for k in keys:
                sums[k] = total[off:off + W_ROWS[k][1]]
                off += W_ROWS[k][1]
        per_layer.append(_slab_grads(sums))
    grads = {n: jnp.stack([per_layer[l][n] for l in range(DEPTH)]) for n in per_layer[0]}

    stack = lambda n: jnp.stack(small[n]) if small[n][0].shape[0] != 1 else jnp.concatenate(small[n], axis=0)
    names = list(small) + ['meta_tokens']
    partial = {n: stack(n) for n in small}
    partial['meta_tokens'] = dh[:N_META]
    gathered = _all_gather(_pack([partial[n] for n in names], F32, 8), "gather_small_grads")
    rep = dict(zip(names, _unpack(_sum8(gathered, "sum_small_grads"), [partial[n].shape for n in names])))
    rep['hgrn_lower_bounds'] = _lower_bound_grad(w['hgrn_lower_bounds'], rep['hgrn_lower_bounds'], "lower_bound_grad")
    me = 4 * lax.axis_index("x") + 2 * lax.axis_index("y") + lax.axis_index("c")
    for n in F32_GATHERED:
        width = w[n].shape[SHARD_AXIS[n]]
        rep[n] = lax.dynamic_slice_in_dim(rep[n], me * width, width, axis=SHARD_AXIS[n])
    grads.update(rep)

    delta, new_m, new_v = {}, {}, {}
    for n in WEIGHT_NAMES:
        shape = w[n].shape
        d2, m2, v2 = _adamw(_as2d(w[n]), _as2d(grads[n]), _as2d(m[n]), _as2d(v[n]), f"adamw_{n}")
        delta[n], new_m[n], new_v[n] = d2.reshape(shape), m2.reshape(shape), v2.reshape(shape)
    return loss, grad_x, grads, delta, new_m, new_v


def kernel(x, meta_tokens, mix_norm_pre, mix_norm_post, w_in, hgrn_lower_bounds, hgrn_out_norm, w_branch_hgrn, pool_proj, pool_scale, w_branch_pool, w_out, ffn_norm_pre, ffn_norm_post, ffn_w_gate, ffn_w_up, ffn_conv_w, ffn_conv_b, ffn_w_down, loss_target, m_meta_tokens, m_mix_norm_pre, m_mix_norm_post, m_w_in, m_hgrn_lower_bounds, m_hgrn_out_norm, m_w_branch_hgrn, m_pool_proj, m_pool_scale, m_w_branch_pool, m_w_out, m_ffn_norm_pre, m_ffn_norm_post, m_ffn_w_gate, m_ffn_w_up, m_ffn_conv_w, m_ffn_conv_b, m_ffn_w_down, v_meta_tokens, v_mix_norm_pre, v_mix_norm_post, v_w_in, v_hgrn_lower_bounds, v_hgrn_out_norm, v_w_branch_hgrn, v_pool_proj, v_pool_scale, v_w_branch_pool, v_w_out, v_ffn_norm_pre, v_ffn_norm_post, v_ffn_w_gate, v_ffn_w_up, v_ffn_conv_w, v_ffn_conv_b, v_ffn_w_down):
    w = dict(zip(WEIGHT_NAMES, (meta_tokens, mix_norm_pre, mix_norm_post, w_in, hgrn_lower_bounds, hgrn_out_norm,
                                w_branch_hgrn, pool_proj, pool_scale, w_branch_pool, w_out, ffn_norm_pre,
                                ffn_norm_post, ffn_w_gate, ffn_w_up, ffn_conv_w, ffn_conv_b, ffn_w_down)))
    m = dict(zip(WEIGHT_NAMES, (m_meta_tokens, m_mix_norm_pre, m_mix_norm_post, m_w_in, m_hgrn_lower_bounds,
                                m_hgrn_out_norm, m_w_branch_hgrn, m_pool_proj, m_pool_scale, m_w_branch_pool, m_w_out,
                                m_ffn_norm_pre, m_ffn_norm_post, m_ffn_w_gate, m_ffn_w_up, m_ffn_conv_w,
                                m_ffn_conv_b, m_ffn_w_down)))
    v = dict(zip(WEIGHT_NAMES, (v_meta_tokens, v_mix_norm_pre, v_mix_norm_post, v_w_in, v_hgrn_lower_bounds,
                                v_hgrn_out_norm, v_w_branch_hgrn, v_pool_proj, v_pool_scale, v_w_branch_pool, v_w_out,
                                v_ffn_norm_pre, v_ffn_norm_post, v_ffn_w_gate, v_ffn_w_up, v_ffn_conv_w,
                                v_ffn_conv_b, v_ffn_w_down)))
    loss_local, grad_x, grads, delta, new_m, new_v = _train_step(x[0], loss_target[0], w, m, v)
    loss = lax.psum(loss_local, ("x", "y", "c"))
    return (loss, grad_x[None], *[grads[n] for n in WEIGHT_NAMES], *[delta[n] for n in WEIGHT_NAMES],
            *[new_m[n] for n in WEIGHT_NAMES], *[new_v[n] for n in WEIGHT_NAMES])
```

```python
import jax
import jax.numpy as jnp
from jax import lax
from jax.experimental import pallas as pl
from jax.experimental.pallas import tpu as pltpu

F32 = jnp.float32
BF = jnp.bfloat16

D_MODEL = 1024
N_META = 16
DEPTH = 2
HEADS = 4
HEAD_DIM = 128
HGRN_W = 512
POOL_W = 512
POOL_WINDOWS = (2, 4, 8, 16)
D_FF = 2816
IN_COLS = 4608
EPS = 1e-6
LOG_FLOOR = 1e-30
N_DEV = 8

ADAM_LR = 0.001
ADAM_B1 = 0.9
ADAM_B2 = 0.999
ADAM_EPS = 1e-08
ADAM_WD = 0.01
ADAM_STEP = 10

SUB = 16
CHUNK = 48
HGRN_UNROLL = 9
EXP_CLAMP = 80.0
ROW_TILE = 432
FFN_TILE = 144
HALO = 16
VMEM_LIMIT = 56 * 1024 * 1024
DW_BUFFER_BYTES = 44 * 1024 * 1024
MESH = pl.DeviceIdType.MESH

WEIGHT_NAMES = ['meta_tokens', 'mix_norm_pre', 'mix_norm_post', 'w_in', 'hgrn_lower_bounds', 'hgrn_out_norm',
                'w_branch_hgrn', 'pool_proj', 'pool_scale', 'w_branch_pool', 'w_out', 'ffn_norm_pre', 'ffn_norm_post',
                'ffn_w_gate', 'ffn_w_up', 'ffn_conv_w', 'ffn_conv_b', 'ffn_w_down']
SHARD_AXIS = {'meta_tokens': 1, 'w_in': 2, 'w_branch_hgrn': 2, 'w_branch_pool': 2, 'w_out': 1,
              'ffn_w_gate': 2, 'ffn_w_up': 2, 'ffn_conv_w': 2, 'ffn_w_down': 1}
F32_GATHERED = ['meta_tokens', 'ffn_conv_w']
REPLICATED = [n for n in WEIGHT_NAMES if n not in SHARD_AXIS]

W_ROWS = {'w_in': (0, 576),
          'wb': (576, 128),
          'w_out': (704, 128),
          'wg': (832, 352),
          'wu': (1184, 352),
          'wd': (1536, 352)}
SLAB_ROWS = 1888
EXCHANGE_GROUPS = {'ffn': ['wg', 'wd'], 'rest': ['wu', 'wb', 'w_out'], 'in': ['w_in']}


def _params(sem=None):
    return pltpu.CompilerParams(dimension_semantics=sem, vmem_limit_bytes=VMEM_LIMIT)


def _tile(total, pref, mult=16):
    best = None
    for t in range(mult, min(total, pref) + 1, mult):
        if total % t == 0:
            best = t
    assert best is not None, (total, pref, mult)
    return best


def _whole(shape):
    return pl.BlockSpec(shape, lambda *_: (0,) * len(shape))


def _cols(tm, width, j):
    return pl.BlockSpec((tm, width), lambda i, j=j: (i, j))


def _dot(a, b):
    return jnp.dot(a, b, preferred_element_type=F32)


def _dot_nt(a, b):
    return lax.dot_general(a, b, (((1,), (1,)), ((), ())), preferred_element_type=F32)


def _dot_tn(a, b):
    return lax.dot_general(a, b, (((0,), (0,)), ((), ())), preferred_element_type=F32)


def _rms_fwd(x, g):
    r = lax.rsqrt(jnp.mean(x * x, axis=-1, keepdims=True) + EPS)
    return x * r * g


def _rms_bwd(dy, x, g):
    r = lax.rsqrt(jnp.mean(x * x, axis=-1, keepdims=True) + EPS)
    xh = x * r
    dyg = dy * g
    dx = r * (dyg - xh * jnp.mean(dyg * xh, axis=-1, keepdims=True))
    return dx, jnp.sum(dy * xh, axis=0, keepdims=True)


_GELU_C = 0.7978845608028654
_GELU_A = 0.044715


def _gelu_and_grad(x):
    x2 = x * x
    t = jnp.tanh(x * (_GELU_C + (_GELU_C * _GELU_A) * x2))
    u = 1.0 + t
    hx = 0.5 * x
    return hx * u, 0.5 * u + (hx * (1.0 - t * t)) * (_GELU_C + (3.0 * _GELU_C * _GELU_A) * x2)


def _split3(x):
    x1 = x.astype(BF)
    r1 = x - x1.astype(F32)
    x2 = r1.astype(BF)
    x3 = (r1 - x2.astype(F32)).astype(BF)
    return x1, x2, x3


def _tri_mm(tri, x):
    x1, x2, x3 = _split3(x)
    return _dot(tri, x1) + _dot(tri, x2) + _dot(tri, x3)


def _softmax2(lb_ref):
    l0 = lb_ref[0:1, :]
    l1 = lb_ref[1:2, :]
    m = jnp.maximum(l0, l1)
    e0 = jnp.exp(l0 - m)
    e1 = jnp.exp(l1 - m)
    return e0 / (e0 + e1), e1 / (e0 + e1)


def _layer_lower_bound(lb_ref, layer):
    g0, g1 = _softmax2(lb_ref)
    if layer == 0:
        return jnp.clip(g0 - g0, 0.0, 1.0)
    return jnp.clip((g0 + g1) - g0, 0.0, 1.0)


def _all_gather(x, name):
    n, w = x.shape

    def body(x_ref, out_ref, send_sems, recv_sems, local_sem):
        mx, my, mc = lax.axis_index("x"), lax.axis_index("y"), lax.axis_index("c")
        me, sibling = (mx, my, mc), (mx, my, 1 - mc)
        chips = [(1 - mx, my), (mx, 1 - my), (1 - mx, 1 - my)]

        def rows(px, py, pc):
            return out_ref.at[4 * px + 2 * py + pc]

        def copy(k, block, to, src=None):
            return pltpu.make_async_remote_copy(
                src_ref=rows(*block) if src is None else src, dst_ref=rows(*block),
                send_sem=send_sems.at[k], recv_sem=recv_sems.at[k], device_id=to, device_id_type=MESH)

        mine = pltpu.make_async_copy(x_ref, rows(*me), local_sem)
        mine.start()
        first = [copy(0, me, sibling, src=x_ref)]
        first += [copy(1 + j, me, (*chip, mc), src=x_ref) for j, chip in enumerate(chips)]
        for cp in first:
            cp.start()
        passed = [copy(4 + j, (*chip, mc), sibling) for j, chip in enumerate(chips)]
        for j, chip in enumerate(chips):
            copy(1 + j, (*chip, mc), me).wait_recv()
            passed[j].start()
        copy(0, sibling, me).wait_recv()
        for j, chip in enumerate(chips):
            copy(4 + j, (*chip, 1 - mc), me).wait_recv()
        for cp in first + passed:
            cp.wait_send()
        mine.wait()

    return pl.pallas_call(
        body, name=name,
        out_shape=jax.ShapeDtypeStruct((N_DEV, n, w), x.dtype),
        in_specs=[pl.BlockSpec(memory_space=pl.ANY)],
        out_specs=pl.BlockSpec(memory_space=pl.ANY),
        scratch_shapes=[pltpu.SemaphoreType.DMA((7,)), pltpu.SemaphoreType.DMA((7,)), pltpu.SemaphoreType.DMA],
    )(x)


GRAD_SRC = {'w_in': ('d_in', 0), 'wb': ('d_b', 0), 'w_out': ('d_out', 0), 'wg': ('d_gu', 0), 'wu': ('d_gu', D_FF),
            'wd': ('d_down', 0)}


class _GradExchange:
    def __init__(self, keys, grads):
        self.names = sorted({GRAD_SRC[k][0] for k in keys})
        self.arrays = [grads[n][0] for n in self.names] + [grads[n][1] for n in self.names]
        self.pieces, off = [], 0
        for k in keys:
            name, base = GRAD_SRC[k]
            rows = W_ROWS[k][1]
            self.pieces.append((self.names.index(name), base, rows, off))
            off += rows
        self.keys, self.rows = keys, off
        self.out_shape = (jax.ShapeDtypeStruct((N_DEV - 1, off, D_MODEL), BF), jax.ShapeDtypeStruct((off, D_MODEL), F32))
        self.scratch = [pltpu.SemaphoreType.DMA((N_DEV - 1,)), pltpu.SemaphoreType.DMA((N_DEV - 1,)),
                        pltpu.SemaphoreType.DMA((len(self.pieces),))]

    def _local(self, ins, outs, scr):
        mx, my, mc = lax.axis_index("x"), lax.axis_index("y"), lax.axis_index("c")
        me = 4 * mx + 2 * my + mc
        return [pltpu.make_async_copy(ins[ai].at[pl.ds(pl.multiple_of(base + rows * me, 8), rows)],
                                      outs[1].at[pl.ds(off, rows)], scr[2].at[k])
                for k, (ai, base, rows, off) in enumerate(self.pieces)]

    def start(self, ins, outs, scr):
        mx, my, mc = lax.axis_index("x"), lax.axis_index("y"), lax.axis_index("c")
        bf16 = ins[len(self.names):]
        for cp in self._local(ins, outs, scr):
            cp.start()
        for d in range(1, N_DEV):
            px = (1 - mx) if (d >> 2) & 1 else mx
            py = (1 - my) if (d >> 1) & 1 else my
            pc = (1 - mc) if d & 1 else mc
            peer = 4 * px + 2 * py + pc
            for ai, base, rows, off in self.pieces:
                pltpu.make_async_remote_copy(
                    src_ref=bf16[ai].at[pl.ds(pl.multiple_of(base + rows * peer, 16), rows)],
                    dst_ref=outs[0].at[d - 1, pl.ds(off, rows)], send_sem=scr[0].at[d - 1], recv_sem=scr[1].at[d - 1],
                    device_id=(px, py, pc), device_id_type=MESH).start()

    def wait(self, ins, outs, scr):
        me = (lax.axis_index("x"), lax.axis_index("y"), lax.axis_index("c"))
        for d in range(1, N_DEV):
            slot = pltpu.make_async_remote_copy(
                src_ref=outs[0].at[d - 1], dst_ref=outs[0].at[d - 1], send_sem=scr[0].at[d - 1],
                recv_sem=scr[1].at[d - 1], device_id=me, device_id_type=MESH)
            slot.wait_recv()
            slot.wait_send()
        for cp in self._local(ins, outs, scr):
            cp.wait()


class _SlabGather:
    def __init__(self, slab, row0, rows):
        self.arrays = [slab]
        self.row0, self.rows = row0, rows
        self.out_shape = (jax.ShapeDtypeStruct((N_DEV, rows, D_MODEL), slab.dtype),)
        self.scratch = [pltpu.SemaphoreType.DMA((7,)), pltpu.SemaphoreType.DMA((7,)), pltpu.SemaphoreType.DMA]

    def _parts(self, ins, outs, scr):
        mx, my, mc = lax.axis_index("x"), lax.axis_index("y"), lax.axis_index("c")
        me, sibling = (mx, my, mc), (mx, my, 1 - mc)
        chips = [(1 - mx, my), (mx, 1 - my), (1 - mx, 1 - my)]
        mine_ref = ins[0].at[pl.ds(self.row0, self.rows)]
        block = lambda px, py, pc: outs[0].at[4 * px + 2 * py + pc]

        def copy(k, blk, to, src=None):
            return pltpu.make_async_remote_copy(
                src_ref=block(*blk) if src is None else src, dst_ref=block(*blk),
                send_sem=scr[0].at[k], recv_sem=scr[1].at[k], device_id=to, device_id_type=MESH)

        local = pltpu.make_async_copy(mine_ref, block(*me), scr[2])
        first = [copy(0, me, sibling, src=mine_ref)]
        first += [copy(1 + j, me, (*chip, mc), src=mine_ref) for j, chip in enumerate(chips)]
        passed = [copy(4 + j, (*chip, mc), sibling) for j, chip in enumerate(chips)]
        landed = [copy(1 + j, (*chip, mc), me) for j, chip in enumerate(chips)]
        from_sibling = [copy(0, sibling, me)] + [copy(4 + j, (*chip, 1 - mc), me) for j, chip in enumerate(chips)]
        return local, first, passed, landed, from_sibling

    def start(self, ins, outs, scr):
        local, first, _, _, _ = self._parts(ins, outs, scr)
        local.start()
        for cp in first:
            cp.start()

    def wait(self, ins, outs, scr):
        local, first, passed, landed, from_sibling = self._parts(ins, outs, scr)
        for arrived, forward in zip(landed, passed):
            arrived.wait_recv()
            forward.start()
        for cp in from_sibling:
            cp.wait_recv()
        for cp in first + passed:
            cp.wait_send()
        local.wait()


class _Both:
    def __init__(self, first, second):
        self.parts = (first, second)
        self.arrays = list(first.arrays) + list(second.arrays)
        self.out_shape = tuple(first.out_shape) + tuple(second.out_shape)
        self.scratch = list(first.scratch) + list(second.scratch)

    def _split(self, ins, outs, scr):
        a, b = self.parts
        na, oa, sa = len(a.arrays), len(a.out_shape), len(a.scratch)
        return (a, ins[:na], outs[:oa], scr[:sa]), (b, ins[na:], outs[oa:], scr[sa:])

    def start(self, ins, outs, scr):
        for part, i, o, s in self._split(ins, outs, scr):
            part.start(i, o, s)

    def wait(self, ins, outs, scr):
        for part, i, o, s in self._split(ins, outs, scr):
            part.wait(i, o, s)


def _hosted_call(body, comm, name, grid, in_specs, args, out_shape, out_specs, scratch_shapes):
    if comm is None:
        return pl.pallas_call(body, name=name, grid=grid, in_specs=in_specs, out_specs=out_specs, out_shape=out_shape,
                              scratch_shapes=scratch_shapes, compiler_params=_params(("arbitrary",)))(*args), None
    n_in, n_out, n_scr, n_c = len(in_specs), len(out_shape), len(scratch_shapes), len(comm.arrays)
    n_co = len(comm.out_shape)
    steps = grid[0]

    def hosted(*refs):
        ins, cins = refs[:n_in], refs[n_in:n_in + n_c]
        outs = refs[n_in + n_c:n_in + n_c + n_out]
        couts = refs[n_in + n_c + n_out:n_in + n_c + n_out + n_co]
        scr = refs[n_in + n_c + n_out + n_co:n_in + n_c + n_out + n_co + n_scr]
        cscr = refs[n_in + n_c + n_out + n_co + n_scr:]

        @pl.when(pl.program_id(0) == 0)
        def _():
            comm.start(cins, couts, cscr)

        body(*ins, *outs, *scr)

        @pl.when(pl.program_id(0) == steps - 1)
        def _():
            comm.wait(cins, couts, cscr)

    res = pl.pallas_call(
        hosted, name=name, grid=grid, in_specs=list(in_specs) + [HBM_SPEC] * n_c,
        out_specs=tuple(out_specs) + (HBM_SPEC,) * n_co, out_shape=tuple(out_shape) + tuple(comm.out_shape),
        scratch_shapes=list(scratch_shapes) + comm.scratch, compiler_params=_params(("arbitrary",)),
    )(*args, *comm.arrays)
    return res[:n_out], res[n_out:]


def _sum_parts(own, recv, name):
    n, w = own.shape
    tn = _tile(n, 256, 16)

    def body(own_ref, p_ref, o_ref):
        acc = own_ref[...]
        for d in range(N_DEV - 1):
            acc = acc + p_ref[d].astype(F32)
        o_ref[...] = acc

    return pl.pallas_call(
        body, name=name, out_shape=jax.ShapeDtypeStruct((n, w), F32), grid=(n // tn,),
        in_specs=[pl.BlockSpec((tn, w), lambda i: (i, 0)), pl.BlockSpec((N_DEV - 1, tn, w), lambda i: (0, i, 0))],
        out_specs=pl.BlockSpec((tn, w), lambda i: (i, 0)),
        compiler_params=_params(("parallel",)),
    )(own, recv)


def _sum8(parts, name):
    _, n, w = parts.shape
    tn = _tile(n, 256 if w > 128 else 1024, 8)

    def body(p_ref, o_ref):
        acc = p_ref[0]
        for d in range(1, N_DEV):
            acc = acc + p_ref[d]
        o_ref[...] = acc

    return pl.pallas_call(
        body, name=name, out_shape=jax.ShapeDtypeStruct((n, w), F32), grid=(n // tn,),
        in_specs=[pl.BlockSpec((N_DEV, tn, w), lambda i: (0, i, 0))],
        out_specs=pl.BlockSpec((tn, w), lambda i: (i, 0)),
        compiler_params=_params(("parallel",)),
    )(parts)


def _pack(arrays, dtype, row_mult):
    flat = jnp.concatenate([a.astype(dtype).reshape(-1) for a in arrays])
    pad = (-flat.shape[0]) % (128 * row_mult)
    if pad:
        flat = jnp.concatenate([flat, jnp.zeros((pad,), dtype)])
    return flat.reshape(-1, 128)


def _unpack(flat2d, shapes):
    lead = flat2d.shape[:-2]
    flat = flat2d.reshape(lead + (-1,))
    out, off = [], 0
    for s in shapes:
        size = 1
        for d in s:
            size *= d
        out.append(flat[..., off:off + size].reshape(lead + tuple(s)))
        off += size
    return out


def _blocks_to_full(blocks, axis):
    moved = jnp.moveaxis(blocks, 0, axis)
    shape = list(moved.shape)
    shape[axis:axis + 2] = [shape[axis] * shape[axis + 1]]
    return moved.reshape(shape)


def _weight_scratch(keys):
    return ([pltpu.VMEM((N_DEV * W_ROWS[k][1], D_MODEL), BF) for k in keys]
            + [pltpu.SemaphoreType.DMA((N_DEV * len(keys),))])


def _fetch_weights(slabs_ref, keys, bufs, sems, base=0):
    @pl.when(pl.program_id(0) == 0)
    def _():
        copies = []
        for ki, (key, buf) in enumerate(zip(keys, bufs)):
            row0, rows = W_ROWS[key][0] - base, W_ROWS[key][1]
            for j in range(N_DEV):
                copies.append(pltpu.make_async_copy(slabs_ref.at[j, pl.ds(row0, rows)],
                                                    buf.at[pl.ds(j * rows, rows)], sems.at[ki * N_DEV + j]))
        for cp in copies:
            cp.start()
        for cp in copies:
            cp.wait()


HBM_SPEC = pl.BlockSpec(memory_space=pl.ANY)


def _norm_matmul(h, gain, slabs, keys, out_dtype, name, comm=None):
    t_rows, d = h.shape
    widths = [N_DEV * W_ROWS[k][1] for k in keys]
    tm = _tile(t_rows, ROW_TILE)

    def body(h_ref, g_ref, slabs_ref, o_ref, *scratch):
        bufs, sems = scratch[:-1], scratch[-1]
        _fetch_weights(slabs_ref, keys, bufs, sems, slabs[1])
        u = _rms_fwd(h_ref[...], g_ref[...]).astype(BF)
        off = 0
        for buf, n in zip(bufs, widths):
            o_ref[:, off:off + n] = _dot_nt(u, buf[...]).astype(out_dtype)
            off += n

    outs, moved = _hosted_call(
        body, comm, name, (t_rows // tm,),
        [pl.BlockSpec((tm, d), lambda i: (i, 0)), _whole((1, d)), HBM_SPEC], (h, gain, slabs[0]),
        (jax.ShapeDtypeStruct((t_rows, sum(widths)), out_dtype),),
        (pl.BlockSpec((tm, sum(widths)), lambda i: (i, 0)),), _weight_scratch(keys))
    return outs[0], moved


def _chunk_masks():
    row = lax.broadcasted_iota(jnp.int32, (CHUNK, CHUNK), 0)
    col = lax.broadcasted_iota(jnp.int32, (CHUNK, CHUNK), 1)
    tri_lo = (col <= row).astype(BF)
    tri_up = (col >= row).astype(BF)
    rb = jnp.right_shift(row, 4)
    cb = jnp.right_shift(col, 4)
    diag = (rb == cb) & (col <= row)
    off = [(rb == i) & (col < SUB * i) for i in range(1, CHUNK // SUB)]
    return tri_lo, tri_up, diag, off


def _chunk_gates(fl, lb):
    sg = jax.nn.sigmoid(fl)
    s2 = jax.nn.sigmoid(-fl)
    f = lb + (1.0 - lb) * sg
    lf = jnp.log(jnp.maximum(f, LOG_FLOOR))
    k = (1.0 - lb) * s2
    return sg, s2, f, lf, k


def _chunk_factors(q, k, b):
    nb = CHUNK // SUB
    rq = jnp.concatenate([jnp.broadcast_to(b[SUB * i:SUB * i + 1], (SUB, HEAD_DIM)) for i in range(nb)], axis=0)
    mq = jnp.concatenate([jnp.broadcast_to(b[SUB * i + SUB // 2:SUB * i + SUB // 2 + 1], (SUB, HEAD_DIM))
                          for i in range(nb)], axis=0)
    e_qs = jnp.exp(b - rq)
    e_qm = jnp.exp(jnp.minimum(b - mq, EXP_CLAMP))
    e_km = jnp.exp(jnp.minimum(mq - b, EXP_CLAMP))
    e_ks = [jnp.exp(jnp.minimum(b[SUB * i:SUB * i + 1] - b, 0.0)) for i in range(1, nb)]
    qs = (q * e_qs).astype(BF)
    qm = (q * e_qm).astype(BF)
    km = (k * e_km).astype(BF)
    ks = [(k * e).astype(BF) for e in e_ks]
    return e_qs, e_qm, e_km, e_ks, qs, qm, km, ks


def _chunk_scores(qs, qm, km, ks, diag, off):
    a = jnp.where(diag, _dot_nt(qm, km), 0.0)
    for m, kk in zip(off, ks):
        a = jnp.where(m, _dot_nt(qs, kk), a)
    return a


def _hgrn_fwd(proj, lower_bounds, layer, name, comm=None):
    t_rows = proj.shape[0]
    tm = _tile(t_rows, ROW_TILE, CHUNK)
    nct = tm // CHUNK

    def body(q_ref, f_ref, i_ref, lb_ref, o_ref, sall_ref, st_ref):
        @pl.when(pl.program_id(0) == 0)
        def _():
            st_ref[...] = jnp.zeros_like(st_ref)

        lbs = _layer_lower_bound(lb_ref, layer)
        tri_lo, _, diag, off = _chunk_masks()

        group = HGRN_UNROLL if nct % HGRN_UNROLL == 0 else 1

        def chunk_group(gi, carry):
            pairs = [(u, hh) for u in range(group) for hh in range(HEADS)]
            rows = [pl.ds(pl.multiple_of((gi * group + u) * CHUNK, SUB), CHUNK) for u in range(group)]
            cols = [slice(hh * HEAD_DIM, (hh + 1) * HEAD_DIM) for hh in range(HEADS)]
            q = {p: q_ref[rows[p[0]], cols[p[1]]] for p in pairs}
            vb = {p: i_ref[rows[p[0]], cols[p[1]]].astype(BF) for p in pairs}
            gates = {p: _chunk_gates(f_ref[rows[p[0]], cols[p[1]]], lbs[:, cols[p[1]]]) for p in pairs}
            b = {p: _tri_mm(tri_lo, gates[p][3]) for p in pairs}
            fac = {p: _chunk_factors(q[p], gates[p][4], b[p]) for p in pairs}
            a = {p: _chunk_scores(*fac[p][4:], diag, off) for p in pairs}
            o_intra = {p: _dot(a[p].astype(BF), vb[p]) for p in pairs}
            upd = {p: _dot_tn(vb[p], (gates[p][4] * jnp.exp(b[p][CHUNK - 1:CHUNK] - b[p])).astype(BF)) for p in pairs}
            qc = {p: (q[p] * jnp.exp(b[p])).astype(BF) for p in pairs}
            for hh in range(HEADS):
                st = st_ref[hh]
                for u in range(group):
                    p = (u, hh)
                    sall_ref[gi * group + u, hh] = st
                    o_ref[rows[u], cols[hh]] = _dot_nt(qc[p], st.astype(BF)) + o_intra[p]
                    st = st * jnp.exp(b[p][CHUNK - 1:CHUNK]) + upd[p]
                st_ref[hh] = st
            return carry

        lax.fori_loop(0, nct // group, chunk_group, 0)

    outs, moved = _hosted_call(
        body, comm, name, (t_rows // tm,),
        [_cols(tm, 512, 0), _cols(tm, 512, 1), _cols(tm, 512, 2), _whole((DEPTH, HGRN_W))],
        (proj, proj, proj, lower_bounds),
        (jax.ShapeDtypeStruct((t_rows, HGRN_W), F32),
         jax.ShapeDtypeStruct((t_rows // CHUNK, HEADS, HEAD_DIM, HEAD_DIM), F32)),
        (pl.BlockSpec((tm, HGRN_W), lambda i: (i, 0)),
         pl.BlockSpec((nct, HEADS, HEAD_DIM, HEAD_DIM), lambda i: (i, 0, 0, 0))),
        [pltpu.VMEM((HEADS, HEAD_DIM, HEAD_DIM), F32)])
    return (*outs, moved)


def _head_norm(o):
    outs, rs = [], []
    for hh in range(HEADS):
        oh = o[:, hh * HEAD_DIM:(hh + 1) * HEAD_DIM]
        r = lax.rsqrt(jnp.mean(oh * oh, axis=-1, keepdims=True) + EPS)
        outs.append(oh * r)
        rs.append(r)
    return outs, rs


def _window_counts(row0, rows):
    t1 = (row0 + lax.broadcasted_iota(jnp.int32, (rows, 1), 0) + 1).astype(F32)
    return [1.0 / jnp.minimum(t1, float(w)) for w in POOL_WINDOWS]


def _pool_fwd(v, halo, row0):
    rows = v.shape[0]
    inv = _window_counts(row0, rows)
    outs = []
    for gi, w in enumerate(POOL_WINDOWS):
        cs = slice(gi * HEAD_DIM, (gi + 1) * HEAD_DIM)
        s = jnp.concatenate([halo[:, cs], v[:, cs]], axis=0)
        step = 1
        while step < w:
            s = s + pltpu.roll(s, step, 0)
            step *= 2
        outs.append(s[HALO:] * inv[gi] - v[:, cs])
    return outs


def _pool_bwd(dpooled, halo, row0):
    rows = dpooled[0].shape[0]
    inv = _window_counts(row0, rows)
    inv_h = _window_counts(row0 + rows, HALO)
    outs = []
    for gi, w in enumerate(POOL_WINDOWS):
        s = jnp.concatenate([dpooled[gi] * inv[gi], halo[gi] * inv_h[gi]], axis=0)
        n_ext = s.shape[0]
        step = 1
        while step < w:
            s = s + pltpu.roll(s, n_ext - step, 0)
            step *= 2
        outs.append(s[:rows] - dpooled[gi])
    return outs


def _pool_project(pooled, pp_ref, scale):
    y = jnp.concatenate([_dot(pooled[gi].astype(BF), pp_ref[gi].astype(BF)) for gi in range(4)], axis=1)
    return y, y * scale


def _mix_merge(proj, o, h, out_gain, pool_proj, pool_scale, slabs, gain_post, name, comm=None):
    t_rows = h.shape[0]
    tm = _tile(t_rows, ROW_TILE)
    hpt = tm // HALO
    keys = ['wb', 'w_out']

    def body(g_ref, v_ref, vh_ref, ga0_ref, ga1_ref, gb0_ref, gb1_ref, o_ref, h_ref, og_ref, pp_ref, ps_ref,
             slabs_ref, gp_ref, hn_ref, r_ref, a_ref, p_ref, wb_ref, wo_ref, sems):
        _fetch_weights(slabs_ref, keys, (wb_ref, wo_ref), sems, slabs[1])
        i = pl.program_id(0)
        on, _ = _head_norm(o_ref[...])
        a = jnp.concatenate(on, axis=1) * og_ref[...] * jax.nn.sigmoid(g_ref[...])
        halo = jnp.where(i == 0, 0.0, vh_ref[...])
        pooled = _pool_fwd(v_ref[...], halo, i * tm)
        _, p = _pool_project(pooled, pp_ref, ps_ref[...])
        ab, pb = a.astype(BF), p.astype(BF)
        sa = jax.nn.sigmoid(jnp.concatenate([ga0_ref[...], ga1_ref[...]], axis=1))
        sb = jax.nn.sigmoid(jnp.concatenate([gb0_ref[...], gb1_ref[...]], axis=1))
        z = sa * _dot_nt(ab, wb_ref[:, 0:HGRN_W]) + sb * _dot_nt(pb, wb_ref[:, HGRN_W:HGRN_W + POOL_W])
        r = _dot(z.astype(BF), wo_ref[...])
        hn_ref[...] = h_ref[...] + _rms_fwd(r, gp_ref[...])
        r_ref[...] = r
        a_ref[...] = ab
        p_ref[...] = pb

    rows = lambda width: pl.BlockSpec((tm, width), lambda i: (i, 0))
    outs, moved = _hosted_call(
        body, comm, name, (t_rows // tm,),
        in_specs=[_cols(tm, 512, 3), _cols(tm, 512, 4),
                  pl.BlockSpec((HALO, 512), lambda i: (jnp.maximum(i * hpt - 1, 0), 4)),
                  _cols(tm, 512, 5), _cols(tm, 512, 6), _cols(tm, 512, 7), _cols(tm, 512, 8),
                  rows(HGRN_W), rows(D_MODEL), _whole((1, HGRN_W)), _whole((4, HEAD_DIM, HEAD_DIM)),
                  _whole((1, POOL_W)), HBM_SPEC, _whole((1, D_MODEL))],
        args=(proj, proj, proj, proj, proj, proj, proj, o, h, out_gain, pool_proj, pool_scale, slabs[0], gain_post),
        out_shape=(jax.ShapeDtypeStruct((t_rows, D_MODEL), F32), jax.ShapeDtypeStruct((t_rows, D_MODEL), F32),
                   jax.ShapeDtypeStruct((t_rows, HGRN_W), BF), jax.ShapeDtypeStruct((t_rows, POOL_W), BF)),
        out_specs=(rows(D_MODEL), rows(D_MODEL), rows(HGRN_W), rows(POOL_W)),
        scratch_shapes=_weight_scratch(keys))
    return (*outs, moved)


def _conv_fwd(g, halo, cw):
    ext = jnp.concatenate([halo, g], axis=0)
    return (cw[0:1] * pltpu.roll(ext, 2, 0)[8:] + cw[1:2] * pltpu.roll(ext, 1, 0)[8:] + cw[2:3] * g)


def _ffn_down(gu, h, conv_w, conv_b, slabs, gain_post, name):
    t_rows = h.shape[0]
    tm = _tile(t_rows, FFN_TILE)
    hpt = tm // HALO

    def body(g_ref, gh_ref, up_ref, h_ref, cw_ref, cb_ref, slabs_ref, gp_ref, hn_ref, y_ref, gl_ref, gg_ref,
             wd_ref, sems):
        _fetch_weights(slabs_ref, ['wd'], (wd_ref,), sems, slabs[1])
        i = pl.program_id(0)
        halo = jnp.where(i == 0, 0.0, gh_ref[8:HALO, :].astype(F32))
        gt = _conv_fwd(g_ref[...].astype(F32), halo, cw_ref[...]) + cb_ref[...]
        gl, gg = _gelu_and_grad(gt)
        gl_ref[...] = gl.astype(BF)
        gg_ref[...] = gg.astype(BF)
        act = gl * up_ref[...].astype(F32)
        y = _dot(act.astype(BF), wd_ref[...])
        hn_ref[...] = h_ref[...] + _rms_fwd(y, gp_ref[...])
        y_ref[...] = y

    rows = lambda width: pl.BlockSpec((tm, width), lambda i: (i, 0))
    return pl.pallas_call(
        body, name=name,
        out_shape=(jax.ShapeDtypeStruct((t_rows, D_MODEL), F32), jax.ShapeDtypeStruct((t_rows, D_MODEL), F32),
                   jax.ShapeDtypeStruct((t_rows, D_FF), BF), jax.ShapeDtypeStruct((t_rows, D_FF), BF)),
        grid=(t_rows // tm,),
        in_specs=[_cols(tm, D_FF, 0), pl.BlockSpec((HALO, D_FF), lambda i: (jnp.maximum(i * hpt - 1, 0), 0)),
                  _cols(tm, D_FF, 1), rows(D_MODEL), _whole((3, D_FF)), _whole((1, D_FF)),
                  HBM_SPEC, _whole((1, D_MODEL))],
        out_specs=(rows(D_MODEL), rows(D_MODEL), rows(D_FF), rows(D_FF)),
        scratch_shapes=_weight_scratch(['wd']),
        compiler_params=_params(("arbitrary",)),
    )(gu, gu, gu, h, conv_w, conv_b, slabs[0], gain_post)


def _loss_grad(h, target, name):
    t_rows, d = h.shape
    tm = _tile(t_rows, ROW_TILE)

    def body(h_ref, t_ref, dh_ref, l_ref):
        i = pl.program_id(0)

        @pl.when(i == 0)
        def _():
            l_ref[...] = jnp.zeros_like(l_ref)

        row = i * tm + lax.broadcasted_iota(jnp.int32, (tm, 1), 0)
        err = jnp.where(row >= N_META, h_ref[...] - t_ref[...], 0.0)
        dh_ref[...] = err * (1.0 / d)
        l_ref[...] += jnp.sum(err * err, axis=0, keepdims=True) * (0.5 / d)

    rows = pl.BlockSpec((tm, d), lambda i: (i, 0))
    return pl.pallas_call(
        body, name=name,
        out_shape=(jax.ShapeDtypeStruct((t_rows, d), F32), jax.ShapeDtypeStruct((1, d), F32)),
        grid=(t_rows // tm,), in_specs=[rows, rows], out_specs=(rows, _whole((1, d))),
        compiler_params=_params(("arbitrary",)),
    )(h, target)


def _dw(lhs, rhs, name, gain=None):
    t_rows = lhs[0].shape[0]
    paired = len(rhs) > 1
    n_rows = lhs[0].shape[1] if paired else sum(x.shape[1] for x in lhs)
    n_cols = sum(x.shape[1] for x in rhs)
    row_bytes = 2 * sum(x.shape[1] * x.dtype.itemsize for x in list(lhs) + list(rhs))
    tm = _tile(t_rows, max(ROW_TILE, (DW_BUFFER_BYTES - 4 * n_rows * n_cols) // row_bytes))
    last = t_rows // tm - 1

    def body(*refs):
        lhs_refs = refs[:len(lhs)]
        rhs_refs = refs[len(lhs):len(lhs) + len(rhs)]
        rest = refs[len(lhs) + len(rhs):]
        if gain is not None:
            g_ref, o_ref, o16_ref, acc, stage = rest
            rv = [_rms_fwd(rhs_refs[0][...], g_ref[...]).astype(BF)]
        else:
            o_ref, o16_ref, acc, stage = rest
            rv = [r[...] for r in rhs_refs]
        i = pl.program_id(0)

        @pl.when(i == 0)
        def _():
            acc[...] = jnp.zeros_like(acc)

        r0, c0 = 0, 0
        for p, l_ref in enumerate(lhs_refs):
            r = rv[p] if paired else rv[0]
            n = l_ref.shape[1]
            step = 512 if n % 512 == 0 else 256
            for s in range(0, n, step):
                acc[r0 + s:r0 + s + step, c0:c0 + r.shape[1]] += _dot_tn(l_ref[:, s:s + step], r)
            if paired:
                c0 += r.shape[1]
            else:
                r0 += n

        @pl.when(i == last)
        def _():
            pltpu.sync_copy(acc, o_ref)
            for s in range(0, n_rows, 256):
                stage[...] = acc[s:s + 256, :].astype(BF)
                pltpu.sync_copy(stage, o16_ref.at[pl.ds(s, 256)])

    rows = lambda x: pl.BlockSpec((tm, x.shape[1]), lambda i: (i, 0))
    in_specs = [rows(x) for x in lhs] + [rows(x) for x in rhs]
    args = list(lhs) + list(rhs)
    if gain is not None:
        in_specs.append(_whole(gain.shape))
        args.append(gain)
    return pl.pallas_call(
        body, name=name,
        out_shape=(jax.ShapeDtypeStruct((n_rows, n_cols), F32), jax.ShapeDtypeStruct((n_rows, n_cols), BF)),
        grid=(t_rows // tm,), in_specs=in_specs, out_specs=(HBM_SPEC, HBM_SPEC),
        scratch_shapes=[pltpu.VMEM((n_rows, n_cols), F32), pltpu.VMEM((256, n_cols), BF)],
        compiler_params=_params(("arbitrary",)),
    )(*args)


def _ffn_bwd(dh, y, gain_post, gu, gl, gg, conv_w, slabs_down, slabs_up, h, gain_pre, name, comm=None):
    t_rows = dh.shape[0]
    tm = _tile(t_rows, FFN_TILE)
    hpt = tm // HALO
    n_tiles = t_rows // tm

    def body(dh_ref, y_ref, gp_ref, g_ref, gh_ref, up_ref, gl_ref, gg_ref, cw_ref, h_ref, gpre_ref, sd_ref, su_ref,
             dhm_ref, dy_ref, act_ref, dgu_ref, dgp_ref, dcw_ref, dcb_ref, dgpre_ref,
             wd_ref, sems_d, wg_ref, wu_ref, sems_u, carry_ref):
        _fetch_weights(sd_ref, ['wd'], (wd_ref,), sems_d, slabs_down[1])
        _fetch_weights(su_ref, ['wg', 'wu'], (wg_ref, wu_ref), sems_u, slabs_up[1])
        i = pl.program_id(0)

        @pl.when(i == 0)
        def _():
            dgp_ref[...] = jnp.zeros_like(dgp_ref)
            dcw_ref[...] = jnp.zeros_like(dcw_ref)
            dcb_ref[...] = jnp.zeros_like(dcb_ref)
            dgpre_ref[...] = jnp.zeros_like(dgpre_ref)
            carry_ref[...] = jnp.zeros_like(carry_ref)

        dh = dh_ref[...]
        dy, dgp = _rms_bwd(dh, y_ref[...], gp_ref[...])
        dgp_ref[...] += dgp
        dyb = dy.astype(BF)
        dy_ref[...] = dyb
        g = g_ref[...].astype(F32)
        up = up_ref[...].astype(F32)
        first_tile = i == n_tiles - 1
        ext = jnp.concatenate([jnp.where(first_tile, 0.0, gh_ref[8:HALO, :].astype(F32)), g], axis=0)
        g2 = pltpu.roll(ext, 2, 0)[8:]
        g1 = pltpu.roll(ext, 1, 0)[8:]
        gl = gl_ref[...].astype(F32)
        act_ref[...] = (gl * up).astype(BF)
        dact = _dot_nt(dyb, wd_ref[...])
        dub = (dact * gl).astype(BF)
        dgt = dact * up * gg_ref[...].astype(F32)
        dcb_ref[...] += jnp.sum(dgt, axis=0, keepdims=True)
        dcw_ref[...] += jnp.concatenate([jnp.sum(dgt * g2, axis=0, keepdims=True),
                                         jnp.sum(dgt * g1, axis=0, keepdims=True),
                                         jnp.sum(dgt * g, axis=0, keepdims=True)], axis=0)
        after = jnp.concatenate([dgt, carry_ref[...]], axis=0)
        carry_ref[...] = dgt[0:8, :]
        cw = cw_ref[...]
        dg = (cw[2:3] * dgt + cw[1:2] * pltpu.roll(after, tm + 8 - 1, 0)[:tm]
              + cw[0:1] * pltpu.roll(after, tm + 8 - 2, 0)[:tm])
        dgb = dg.astype(BF)
        dgu_ref[:, 0:D_FF] = dgb
        dgu_ref[:, D_FF:2 * D_FF] = dub
        du = _dot(dgb, wg_ref[...]) + _dot(dub, wu_ref[...])
        dx, dgain = _rms_bwd(du, h_ref[...], gpre_ref[...])
        dhm_ref[...] = dh + dx
        dgpre_ref[...] += dgain

    rev = lambda width, j=0: pl.BlockSpec((tm, width), lambda i, j=j: (n_tiles - 1 - i, j))
    bf = lambda width: jax.ShapeDtypeStruct((t_rows, width), BF)
    outs, moved = _hosted_call(
        body, comm, name, (n_tiles,),
        [rev(D_MODEL), rev(D_MODEL), _whole((1, D_MODEL)), rev(D_FF, 0),
         pl.BlockSpec((HALO, D_FF), lambda i: (jnp.maximum((n_tiles - 1 - i) * hpt - 1, 0), 0)), rev(D_FF, 1),
         rev(D_FF), rev(D_FF), _whole((3, D_FF)), rev(D_MODEL), _whole((1, D_MODEL)), HBM_SPEC, HBM_SPEC],
        (dh, y, gain_post, gu, gu, gu, gl, gg, conv_w, h, gain_pre, slabs_down[0], slabs_up[0]),
        (jax.ShapeDtypeStruct((t_rows, D_MODEL), F32), bf(D_MODEL), bf(D_FF), bf(2 * D_FF),
         jax.ShapeDtypeStruct((1, D_MODEL), F32), jax.ShapeDtypeStruct((3, D_FF), F32),
         jax.ShapeDtypeStruct((1, D_FF), F32), jax.ShapeDtypeStruct((1, D_MODEL), F32)),
        (rev(D_MODEL), rev(D_MODEL), rev(D_FF), rev(2 * D_FF), _whole((1, D_MODEL)), _whole((3, D_FF)),
         _whole((1, D_FF)), _whole((1, D_MODEL))),
        _weight_scratch(['wd']) + _weight_scratch(['wg', 'wu']) + [pltpu.VMEM((8, D_FF), F32)])
    return (*outs, moved)


def _mix_bwd_a(dh, r, gain_post, proj, a, p, slabs, name, comm=None):
    t_rows = dh.shape[0]
    tm = _tile(t_rows, ROW_TILE)

    def body(dh_ref, r_ref, gp_ref, ga0_ref, ga1_ref, gb0_ref, gb1_ref, a_ref, p_ref, slabs_ref,
             dr_ref, z_ref, dya_ref, dyp_ref, da_ref, dp_ref, dgab_ref, dg_ref, wb_ref, wo_ref, sems):
        _fetch_weights(slabs_ref, ['wb', 'w_out'], (wb_ref, wo_ref), sems, slabs[1])

        @pl.when(pl.program_id(0) == 0)
        def _():
            dg_ref[...] = jnp.zeros_like(dg_ref)

        dr, dgain = _rms_bwd(dh_ref[...], r_ref[...], gp_ref[...])
        dg_ref[...] += dgain
        drb = dr.astype(BF)
        dr_ref[...] = drb
        dz = _dot_nt(drb, wo_ref[...])
        ya = _dot_nt(a_ref[...], wb_ref[:, 0:HGRN_W])
        yp = _dot_nt(p_ref[...], wb_ref[:, HGRN_W:HGRN_W + POOL_W])
        sa = jax.nn.sigmoid(jnp.concatenate([ga0_ref[...], ga1_ref[...]], axis=1))
        sb = jax.nn.sigmoid(jnp.concatenate([gb0_ref[...], gb1_ref[...]], axis=1))
        z_ref[...] = (sa * ya + sb * yp).astype(BF)
        dgab_ref[:, 0:D_MODEL] = (dz * ya * sa * (1.0 - sa)).astype(BF)
        dgab_ref[:, D_MODEL:2 * D_MODEL] = (dz * yp * sb * (1.0 - sb)).astype(BF)
        dya = (dz * sa).astype(BF)
        dyp = (dz * sb).astype(BF)
        dya_ref[...] = dya
        dyp_ref[...] = dyp
        da_ref[...] = _dot(dya, wb_ref[:, 0:HGRN_W])
        dp_ref[...] = _dot(dyp, wb_ref[:, HGRN_W:HGRN_W + POOL_W])

    rows = lambda width: pl.BlockSpec((tm, width), lambda i: (i, 0))
    bf = lambda width: jax.ShapeDtypeStruct((t_rows, width), BF)
    f32 = lambda width: jax.ShapeDtypeStruct((t_rows, width), F32)
    outs, moved = _hosted_call(
        body, comm, name, (t_rows // tm,),
        [rows(D_MODEL), rows(D_MODEL), _whole((1, D_MODEL)),
         _cols(tm, 512, 5), _cols(tm, 512, 6), _cols(tm, 512, 7), _cols(tm, 512, 8),
         rows(HGRN_W), rows(POOL_W), HBM_SPEC],
        (dh, r, gain_post, proj, proj, proj, proj, a, p, slabs[0]),
        (bf(D_MODEL), bf(D_MODEL), bf(D_MODEL), bf(D_MODEL), f32(HGRN_W), f32(POOL_W), bf(2 * D_MODEL),
         jax.ShapeDtypeStruct((1, D_MODEL), F32)),
        (rows(D_MODEL), rows(D_MODEL), rows(D_MODEL), rows(D_MODEL), rows(HGRN_W), rows(POOL_W),
         rows(2 * D_MODEL), _whole((1, D_MODEL))),
        _weight_scratch(['wb', 'w_out']))
    return (*outs, moved)


def _mix_bwd_b(da, dp, proj, o, out_gain, pool_proj, pool_scale, name):
    t_rows = da.shape[0]
    tm = _tile(t_rows, ROW_TILE)
    hpt = tm // HALO
    last = t_rows // tm - 1

    def body(da_ref, dp_ref, dpn_ref, g_ref, v_ref, vh_ref, o_ref, og_ref, pp_ref, ps_ref,
             do_ref, dgv_ref, dog_ref, dpp_ref, dps_ref):
        i = pl.program_id(0)

        @pl.when(i == 0)
        def _():
            dog_ref[...] = jnp.zeros_like(dog_ref)
            dpp_ref[...] = jnp.zeros_like(dpp_ref)
            dps_ref[...] = jnp.zeros_like(dps_ref)

        da = da_ref[...]
        og = og_ref[...]
        o = o_ref[...]
        on, rs = _head_norm(o)
        onc = jnp.concatenate(on, axis=1)
        sg = jax.nn.sigmoid(g_ref[...])
        dog_ref[...] += jnp.sum(da * onc * sg, axis=0, keepdims=True)
        dgv_ref[:, 0:HGRN_W] = (da * onc * og * sg * (1.0 - sg)).astype(BF)
        don = da * og * sg
        for hh in range(HEADS):
            cs = slice(hh * HEAD_DIM, (hh + 1) * HEAD_DIM)
            d = don[:, cs]
            do_ref[:, cs] = rs[hh] * (d - on[hh] * jnp.mean(d * on[hh], axis=-1, keepdims=True))

        scale = ps_ref[...]
        halo = jnp.where(i == 0, 0.0, vh_ref[...])
        pooled = _pool_fwd(v_ref[...], halo, i * tm)
        y, _ = _pool_project(pooled, pp_ref, scale)
        dp = dp_ref[...]
        dps_ref[...] += jnp.sum(dp * y, axis=0, keepdims=True)
        dy = dp * scale
        dyn = jnp.where(i == last, 0.0, dpn_ref[...]) * scale
        dpooled, dhalo = [], []
        for gi in range(4):
            cs = slice(gi * HEAD_DIM, (gi + 1) * HEAD_DIM)
            ppb = pp_ref[gi].astype(BF)
            dyb = dy[:, cs].astype(BF)
            dpooled.append(_dot_nt(dyb, ppb))
            dhalo.append(_dot_nt(dyn[:, cs].astype(BF), ppb))
            dpp_ref[gi] += _dot_tn(pooled[gi].astype(BF), dyb)
        dv = _pool_bwd(dpooled, dhalo, i * tm)
        dgv_ref[:, HGRN_W:HGRN_W + POOL_W] = jnp.concatenate(dv, axis=1).astype(BF)

    rows = lambda width: pl.BlockSpec((tm, width), lambda i: (i, 0))
    return pl.pallas_call(
        body, name=name,
        out_shape=(jax.ShapeDtypeStruct((t_rows, HGRN_W), F32), jax.ShapeDtypeStruct((t_rows, HGRN_W + POOL_W), BF),
                   jax.ShapeDtypeStruct((1, HGRN_W), F32), jax.ShapeDtypeStruct((4, HEAD_DIM, HEAD_DIM), F32),
                   jax.ShapeDtypeStruct((1, POOL_W), F32)),
        grid=(t_rows // tm,),
        in_specs=[rows(HGRN_W), rows(POOL_W),
                  pl.BlockSpec((HALO, POOL_W), lambda i: (jnp.minimum((i + 1) * hpt, (last + 1) * hpt - 1), 0)),
                  _cols(tm, 512, 3), _cols(tm, 512, 4),
                  pl.BlockSpec((HALO, 512), lambda i: (jnp.maximum(i * hpt - 1, 0), 4)),
                  rows(HGRN_W), _whole((1, HGRN_W)), _whole((4, HEAD_DIM, HEAD_DIM)), _whole((1, POOL_W))],
        out_specs=(rows(HGRN_W), rows(HGRN_W + POOL_W), _whole((1, HGRN_W)), _whole((4, HEAD_DIM, HEAD_DIM)),
                   _whole((1, POOL_W))),
        compiler_params=_params(("arbitrary",)),
    )(da, dp, dp, proj, proj, proj, o, out_gain, pool_proj, pool_scale)


def _hgrn_bwd(proj, lower_bounds, layer, states, do, name, comm=None):
    t_rows = proj.shape[0]
    tm = _tile(t_rows, ROW_TILE, CHUNK)
    nct = tm // CHUNK
    n_tiles = t_rows // tm

    def body(q_ref, f_ref, i_ref, lb_ref, sall_ref, do_ref, dqfi_ref, dlb_ref, dst_ref):
        @pl.when(pl.program_id(0) == 0)
        def _():
            dst_ref[...] = jnp.zeros_like(dst_ref)
            dlb_ref[...] = jnp.zeros_like(dlb_ref)

        lbs = _layer_lower_bound(lb_ref, layer)
        tri_lo, tri_up, diag, off = _chunk_masks()
        is_last_row = lax.broadcasted_iota(jnp.int32, (CHUNK, 1), 0) == CHUNK - 1

        group = HGRN_UNROLL if nct % HGRN_UNROLL == 0 else 1

        def chunk_group(gi, carry):
            pairs = [(u, hh) for u in range(group) for hh in range(HEADS)]
            cidx = [nct - 1 - (gi * group + u) for u in range(group)]
            rows = [pl.ds(pl.multiple_of(c * CHUNK, SUB), CHUNK) for c in cidx]
            cols = [slice(hh * HEAD_DIM, (hh + 1) * HEAD_DIM) for hh in range(HEADS)]
            lb = {p: lbs[:, cols[p[1]]] for p in pairs}
            q = {p: q_ref[rows[p[0]], cols[p[1]]] for p in pairs}
            vb = {p: i_ref[rows[p[0]], cols[p[1]]].astype(BF) for p in pairs}
            dob = {p: do_ref[rows[p[0]], cols[p[1]]].astype(BF) for p in pairs}
            gates = {p: _chunk_gates(f_ref[rows[p[0]], cols[p[1]]], lb[p]) for p in pairs}
            k = {p: gates[p][4] for p in pairs}
            b = {p: _tri_mm(tri_lo, gates[p][3]) for p in pairs}
            fac = {p: _chunk_factors(q[p], k[p], b[p]) for p in pairs}
            e_b = {p: jnp.exp(b[p]) for p in pairs}
            e_last = {p: jnp.exp(b[p][CHUNK - 1:CHUNK]) for p in pairs}
            e_kl = {p: jnp.exp(b[p][CHUNK - 1:CHUNK] - b[p]) for p in pairs}
            qc = {p: (q[p] * e_b[p]).astype(BF) for p in pairs}
            kdec = {p: (k[p] * e_kl[p]).astype(BF) for p in pairs}
            a = {p: _chunk_scores(*fac[p][4:], diag, off) for p in pairs}
            da_full = {p: _dot_nt(dob[p], vb[p]) for p in pairs}
            dv, dq, dk, upd, dupd = {}, {}, {}, {}, {}
            for p in pairs:
                e_qs, e_qm, e_km, e_ks, qs, qm, km, ks = fac[p]
                da_d = jnp.where(diag, da_full[p], 0.0).astype(BF)
                dv[p] = _dot_tn(a[p].astype(BF), dob[p])
                dq[p] = e_qm * _dot(da_d, km)
                dk[p] = e_km * _dot_tn(da_d, qm)
                for m, kk, ek in zip(off, ks, e_ks):
                    da_i = jnp.where(m, da_full[p], 0.0).astype(BF)
                    dq[p] = dq[p] + e_qs * _dot(da_i, kk)
                    dk[p] = dk[p] + ek * _dot_tn(da_i, qs)
                upd[p] = _dot_tn(vb[p], kdec[p])
                dupd[p] = _dot_tn(dob[p], qc[p])
            db = {}
            for hh in range(HEADS):
                dst = dst_ref[hh]
                for u in range(group):
                    p = (u, hh)
                    st = sall_ref[cidx[u], hh]
                    dstb = dst.astype(BF)
                    dv[p] = dv[p] + _dot_nt(kdec[p], dstb)
                    dq[p] = dq[p] + e_b[p] * _dot(dob[p], st.astype(BF))
                    dk[p] = dk[p] + e_kl[p] * _dot(vb[p], dstb)
                    st_new = st * e_last[p] + upd[p]
                    db[p] = (q[p] * dq[p] - k[p] * dk[p]
                             + jnp.where(is_last_row, jnp.sum(st_new * dst, axis=0, keepdims=True), 0.0))
                    dst = dst * e_last[p] + dupd[p]
                dst_ref[hh] = dst
            for p in pairs:
                u, hh = p
                sg, s2, f = gates[p][:3]
                dlf = _tri_mm(tri_up, db[p])
                df = jnp.where(f > LOG_FLOOR, dlf / f, 0.0)
                dfl = df * (1.0 - lb[p]) * sg * (1.0 - sg) - dk[p] * (1.0 - lb[p]) * s2 * (1.0 - s2)
                dlb_ref[:, cols[hh]] += jnp.sum(df * (1.0 - sg) - dk[p] * s2, axis=0, keepdims=True)
                dqfi_ref[rows[u], cols[hh]] = dq[p].astype(BF)
                dqfi_ref[rows[u], pl.ds(HGRN_W + hh * HEAD_DIM, HEAD_DIM)] = dfl.astype(BF)
                dqfi_ref[rows[u], pl.ds(2 * HGRN_W + hh * HEAD_DIM, HEAD_DIM)] = dv[p].astype(BF)
            return carry

        lax.fori_loop(0, nct // group, chunk_group, 0)

    rev = lambda width, j: pl.BlockSpec((tm, width), lambda i, j=j: (n_tiles - 1 - i, j))
    outs, exchanged = _hosted_call(
        body, comm, name, (n_tiles,),
        [rev(512, 0), rev(512, 1), rev(512, 2), _whole((DEPTH, HGRN_W)),
         pl.BlockSpec((nct, HEADS, HEAD_DIM, HEAD_DIM), lambda i: (n_tiles - 1 - i, 0, 0, 0)), rev(HGRN_W, 0)],
        (proj, proj, proj, lower_bounds, states, do),
        (jax.ShapeDtypeStruct((t_rows, 3 * HGRN_W), BF), jax.ShapeDtypeStruct((1, HGRN_W), F32)),
        (rev(3 * HGRN_W, 0), _whole((1, HGRN_W))),
        [pltpu.VMEM((HEADS, HEAD_DIM, HEAD_DIM), F32)])
    return (*outs, exchanged)


def _in_bwd(dqfi, dgv, dgab, slabs, h, gain_pre, dh_out, name, comm=None):
    t_rows = h.shape[0]
    tm = _tile(t_rows, ROW_TILE)
    c1 = 3 * HGRN_W
    c2 = c1 + HGRN_W + POOL_W

    def body(d1_ref, d2_ref, d3_ref, slabs_ref, h_ref, gp_ref, dho_ref, dh_ref, dg_ref, w_ref, sems):
        _fetch_weights(slabs_ref, ['w_in'], (w_ref,), sems, slabs[1])

        @pl.when(pl.program_id(0) == 0)
        def _():
            dg_ref[...] = jnp.zeros_like(dg_ref)

        du = (_dot(d1_ref[...], w_ref[0:c1, :]) + _dot(d2_ref[...], w_ref[c1:c2, :])
              + _dot(d3_ref[...], w_ref[c2:IN_COLS, :]))
        dx, dgain = _rms_bwd(du, h_ref[...], gp_ref[...])
        dh_ref[...] = dho_ref[...] + dx
        dg_ref[...] += dgain

    rows = lambda width: pl.BlockSpec((tm, width), lambda i: (i, 0))
    outs, exchanged = _hosted_call(
        body, comm, name, (t_rows // tm,),
        [rows(c1), rows(c2 - c1), rows(IN_COLS - c2), HBM_SPEC, rows(D_MODEL), _whole((1, D_MODEL)), rows(D_MODEL)],
        (dqfi, dgv, dgab, slabs[0], h, gain_pre, dh_out),
        (jax.ShapeDtypeStruct((t_rows, D_MODEL), F32), jax.ShapeDtypeStruct((1, D_MODEL), F32)),
        (rows(D_MODEL), _whole((1, D_MODEL))),
        _weight_scratch(['w_in']))
    return (*outs, exchanged)


def _lower_bound_grad(lower_bounds, dlbs, name):
    assert DEPTH == 2 and lower_bounds.shape[0] == DEPTH, "the softmax over layers is written out for two layers"
    def body(lb_ref, d_ref, o_ref):
        g0, g1 = _softmax2(lb_ref)
        bound = (g0 + g1) - g0
        inside = (bound > 0.0) & (bound < 1.0)
        dg1 = jnp.where(inside, d_ref[1:2, :], 0.0)
        inner = g1 * dg1
        o_ref[0:1, :] = g0 * (0.0 - inner)
        o_ref[1:2, :] = g1 * (dg1 - inner)

    return pl.pallas_call(body, name=name, out_shape=jax.ShapeDtypeStruct(lower_bounds.shape, F32))(lower_bounds, dlbs)


def _adamw(w, g, m, v, name):
    r, c = w.shape
    tr = r if (r % 8 or r <= 512) else _tile(r, 512, 8)
    c1 = 1.0 - ADAM_B1 ** ADAM_STEP
    c2 = 1.0 - ADAM_B2 ** ADAM_STEP

    def body(w_ref, g_ref, m_ref, v_ref, d_ref, nm_ref, nv_ref):
        gg = g_ref[...]
        nm = ADAM_B1 * m_ref[...] + (1.0 - ADAM_B1) * gg
        nv = ADAM_B2 * v_ref[...] + (1.0 - ADAM_B2) * (gg * gg)
        d_ref[...] = -ADAM_LR * ((nm / c1) / (jnp.sqrt(nv / c2) + ADAM_EPS) + ADAM_WD * w_ref[...])
        nm_ref[...] = nm
        nv_ref[...] = nv

    blk = pl.BlockSpec((tr, c), lambda i: (i, 0))
    shp = jax.ShapeDtypeStruct((r, c), F32)
    return pl.pallas_call(
        body, name=name, out_shape=(shp, shp, shp), grid=(r // tr,),
        in_specs=[blk, blk, blk, blk], out_specs=(blk, blk, blk),
        compiler_params=_params(("parallel",)),
    )(w, g, m, v)


def _as2d(a):
    return a.reshape(-1, a.shape[-1])


def _layer_slab(w, l):
    t = lambda a: jnp.swapaxes(a, 0, 1)
    parts = [t(w['w_in'][l]), jnp.concatenate([t(w['w_branch_hgrn'][l]), t(w['w_branch_pool'][l])], axis=1),
             w['w_out'][l], t(w['ffn_w_gate'][l]), t(w['ffn_w_up'][l]), w['ffn_w_down'][l]]
    return jnp.concatenate(parts, axis=0).astype(BF)


def _slab_grads(sums):
    t = lambda a: jnp.swapaxes(a, 0, 1)
    wb = sums['wb']
    return {'w_in': t(sums['w_in']), 'w_branch_hgrn': t(wb[:, :HGRN_W]), 'w_branch_pool': t(wb[:, HGRN_W:]),
            'w_out': sums['w_out'], 'ffn_w_gate': t(sums['wg']), 'ffn_w_up': t(sums['wu']), 'ffn_w_down': sums['wd']}


def _train_step(x, target, w, m, v):
    my_slabs = [_layer_slab(w, l) for l in range(DEPTH)]
    n_in = W_ROWS['w_in'][1]
    slabs = [{} for _ in range(DEPTH)]
    slabs[0]['w_in'] = (_all_gather(my_slabs[0][:n_in], "gather_w_in_0"), 0)
    f32_shapes = [w[n].shape for n in F32_GATHERED]
    gathered32 = _all_gather(_pack([w[n] for n in F32_GATHERED], F32, 8), "gather_meta_conv")
    full = {n: _blocks_to_full(blk, SHARD_AXIS[n]) for n, blk in zip(F32_GATHERED, _unpack(gathered32, f32_shapes))}
    row = lambda name, l: w[name][l][None]

    h = jnp.concatenate([full['meta_tokens'], x], axis=0)
    saved = []
    for l in range(DEPTH):
        s = {'h_in': h}
        nxt = l + 1 if l + 1 < DEPTH else None
        plan = {'in_proj': [(0, ['wb', 'w_out', 'wg', 'wu'])] if l == 0 else [],
                'hgrn_fwd': ([(0, ['wd'])] if l == 0 else []) + ([(nxt, ['w_in'])] if nxt else []),
                'mix_merge': [(nxt, ['wb', 'w_out']), (nxt, ['wd'])] if nxt else [],
                'ffn_proj': [(nxt, ['wg', 'wu'])] if nxt else []}

        def hosted(kernel):
            comms = []
            for layer, keys in plan[kernel]:
                first, last = W_ROWS[keys[0]], W_ROWS[keys[-1]]
                comms.append(_SlabGather(my_slabs[layer], first[0], last[0] + last[1] - first[0]))
            return None if not comms else comms[0] if len(comms) == 1 else _Both(*comms)

        def landed(kernel, moved):
            for (layer, keys), arr in zip(plan[kernel], moved or ()):
                slabs[layer].update({k: (arr, W_ROWS[keys[0]][0]) for k in keys})

        s['proj'], moved = _norm_matmul(h, row('mix_norm_pre', l), slabs[l]['w_in'], ['w_in'], F32, f"in_proj_{l}",
                                        comm=hosted('in_proj'))
        landed('in_proj', moved)
        s['o'], s['states'], moved = _hgrn_fwd(s['proj'], w['hgrn_lower_bounds'], l, f"hgrn_fwd_{l}",
                                               comm=hosted('hgrn_fwd'))
        landed('hgrn_fwd', moved)
        h, s['r'], s['a'], s['p'], moved = _mix_merge(
            s['proj'], s['o'], h, row('hgrn_out_norm', l), w['pool_proj'][l], row('pool_scale', l), slabs[l]['wb'],
            row('mix_norm_post', l), f"mix_merge_{l}", comm=hosted('mix_merge'))
        landed('mix_merge', moved)
        s['h_mid'] = h
        s['gu'], moved = _norm_matmul(h, row('ffn_norm_pre', l), slabs[l]['wg'], ['wg', 'wu'], BF, f"ffn_proj_{l}",
                                      comm=hosted('ffn_proj'))
        landed('ffn_proj', moved)
        h, s['y'], s['gl'], s['gg'] = _ffn_down(s['gu'], h, full['ffn_conv_w'][l], row('ffn_conv_b', l),
                                                slabs[l]['wd'], row('ffn_norm_post', l), f"ffn_down_{l}")
        saved.append(s)

    dh, loss_cols = _loss_grad(h, jnp.pad(target, ((N_META, 0), (0, 0))), "loss_grad")
    loss = jnp.sum(loss_cols)

    small = {n: [None] * DEPTH for n in REPLICATED + ['ffn_conv_w']}
    exchanged = [{} for _ in range(DEPTH)]
    pending = None
    for l in reversed(range(DEPTH)):
        s = saved[l]
        g = {}
        (dh, dy, act, dgu, small['ffn_norm_post'][l], small['ffn_conv_w'][l], small['ffn_conv_b'][l],
         small['ffn_norm_pre'][l], done) = _ffn_bwd(
            dh, s['y'], row('ffn_norm_post', l), s['gu'], s['gl'], s['gg'], full['ffn_conv_w'][l], slabs[l]['wd'],
            slabs[l]['wg'], s['h_mid'], row('ffn_norm_pre', l), f"ffn_bwd_{l}", comm=pending)
        if pending is not None:
            exchanged[l + 1]['in'] = done
        g['d_down'] = _dw([act], [dy], f"dw_down_{l}")
        g['d_gu'] = _dw([dgu], [s['h_mid']], f"dw_gate_up_{l}", gain=row('ffn_norm_pre', l))

        dr, z, dya, dyp, da, dp, dgab, small['mix_norm_post'][l], exchanged[l]['ffn'] = _mix_bwd_a(
            dh, s['r'], row('mix_norm_post', l), s['proj'], s['a'], s['p'], slabs[l]['wb'], f"mix_bwd_a_{l}",
            comm=_GradExchange(EXCHANGE_GROUPS['ffn'], g))
        g['d_out'] = _dw([z], [dr], f"dw_out_{l}")
        g['d_b'] = _dw([dya, dyp], [s['a'], s['p']], f"dw_branch_{l}")
        do, dgv, small['hgrn_out_norm'][l], small['pool_proj'][l], small['pool_scale'][l] = _mix_bwd_b(
            da, dp, s['proj'], s['o'], row('hgrn_out_norm', l), w['pool_proj'][l], row('pool_scale', l),
            f"mix_bwd_b_{l}")
        dqfi, small['hgrn_lower_bounds'][l], exchanged[l]['rest'] = _hgrn_bwd(
            s['proj'], w['hgrn_lower_bounds'], l, s['states'], do, f"hgrn_bwd_{l}",
            comm=_GradExchange(EXCHANGE_GROUPS['rest'], g))
        g['d_in'] = _dw([dqfi, dgv, dgab], [s['h_in']], f"dw_in_{l}", gain=row('mix_norm_pre', l))
        pending = _GradExchange(EXCHANGE_GROUPS['in'], g)
        if l == 0:
            dh, small['mix_norm_pre'][l], exchanged[l]['in'] = _in_bwd(
                dqfi, dgv, dgab, slabs[l]['w_in'], s['h_in'], row('mix_norm_pre', l), dh, f"in_bwd_{l}", comm=pending)
        else:
            dh, small['mix_norm_pre'][l], _ = _in_bwd(
                dqfi, dgv, dgab, slabs[l]['w_in'], s['h_in'], row('mix_norm_pre', l), dh, f"in_bwd_{l}")

    grad_x = dh[N_META:]

    per_layer = []
    for l in range(DEPTH):
        sums = {}
        for part, keys in EXCHANGE_GROUPS.items():
            recv, own = exchanged[l][part]
            total = _sum_parts(own, recv, f"sum_grads_{part}_{l}")
            off = 0
            for k in keys:
                sums[k] = total[off:off + W_ROWS[k][1]]
                off += W_ROWS[k][1]
        per_layer.append(_slab_grads(sums))
    grads = {n: jnp.stack([per_layer[l][n] for l in range(DEPTH)]) for n in per_layer[0]}

    stack = lambda n: jnp.stack(small[n]) if small[n][0].shape[0] != 1 else jnp.concatenate(small[n], axis=0)
    names = list(small) + ['meta_tokens']
    partial = {n: stack(n) for n in small}
    partial['meta_tokens'] = dh[:N_META]
    gathered = _all_gather(_pack([partial[n] for n in names], F32, 8), "gather_small_grads")
    rep = dict(zip(names, _unpack(_sum8(gathered, "sum_small_grads"), [partial[n].shape for n in names])))
    rep['hgrn_lower_bounds'] = _lower_bound_grad(w['hgrn_lower_bounds'], rep['hgrn_lower_bounds'], "lower_bound_grad")
    me = 4 * lax.axis_index("x") + 2 * lax.axis_index("y") + lax.axis_index("c")
    for n in F32_GATHERED:
        width = w[n].shape[SHARD_AXIS[n]]
        rep[n] = lax.dynamic_slice_in_dim(rep[n], me * width, width, axis=SHARD_AXIS[n])
    grads.update(rep)

    delta, new_m, new_v = {}, {}, {}
    for n in WEIGHT_NAMES:
        shape = w[n].shape
        d2, m2, v2 = _adamw(_as2d(w[n]), _as2d(grads[n]), _as2d(m[n]), _as2d(v[n]), f"adamw_{n}")
        delta[n], new_m[n], new_v[n] = d2.reshape(shape), m2.reshape(shape), v2.reshape(shape)
    return loss, grad_x, grads, delta, new_m, new_v


def kernel(x, meta_tokens, mix_norm_pre, mix_norm_post, w_in, hgrn_lower_bounds, hgrn_out_norm, w_branch_hgrn, pool_proj, pool_scale, w_branch_pool, w_out, ffn_norm_pre, ffn_norm_post, ffn_w_gate, ffn_w_up, ffn_conv_w, ffn_conv_b, ffn_w_down, loss_target, m_meta_tokens, m_mix_norm_pre, m_mix_norm_post, m_w_in, m_hgrn_lower_bounds, m_hgrn_out_norm, m_w_branch_hgrn, m_pool_proj, m_pool_scale, m_w_branch_pool, m_w_out, m_ffn_norm_pre, m_ffn_norm_post, m_ffn_w_gate, m_ffn_w_up, m_ffn_conv_w, m_ffn_conv_b, m_ffn_w_down, v_meta_tokens, v_mix_norm_pre, v_mix_norm_post, v_w_in, v_hgrn_lower_bounds, v_hgrn_out_norm, v_w_branch_hgrn, v_pool_proj, v_pool_scale, v_w_branch_pool, v_w_out, v_ffn_norm_pre, v_ffn_norm_post, v_ffn_w_gate, v_ffn_w_up, v_ffn_conv_w, v_ffn_conv_b, v_ffn_w_down):
    w = dict(zip(WEIGHT_NAMES, (meta_tokens, mix_norm_pre, mix_norm_post, w_in, hgrn_lower_bounds, hgrn_out_norm,
                                w_branch_hgrn, pool_proj, pool_scale, w_branch_pool, w_out, ffn_norm_pre,
                                ffn_norm_post, ffn_w_gate, ffn_w_up, ffn_conv_w, ffn_conv_b, ffn_w_down)))
    m = dict(zip(WEIGHT_NAMES, (m_meta_tokens, m_mix_norm_pre, m_mix_norm_post, m_w_in, m_hgrn_lower_bounds,
                                m_hgrn_out_norm, m_w_branch_hgrn, m_pool_proj, m_pool_scale, m_w_branch_pool, m_w_out,
                                m_ffn_norm_pre, m_ffn_norm_post, m_ffn_w_gate, m_ffn_w_up, m_ffn_conv_w,
                                m_ffn_conv_b, m_ffn_w_down)))
    v = dict(zip(WEIGHT_NAMES, (v_meta_tokens, v_mix_norm_pre, v_mix_norm_post, v_w_in, v_hgrn_lower_bounds,
                                v_hgrn_out_norm, v_w_branch_hgrn, v_pool_proj, v_pool_scale, v_w_branch_pool, v_w_out,
                                v_ffn_norm_pre, v_ffn_norm_post, v_ffn_w_gate, v_ffn_w_up, v_ffn_conv_w,
                                v_ffn_conv_b, v_ffn_w_down)))
    loss_local, grad_x, grads, delta, new_m, new_v = _train_step(x[0], loss_target[0], w, m, v)
    loss = lax.psum(loss_local, ("x", "y", "c"))
    return (loss, grad_x[None], *[grads[n] for n in WEIGHT_NAMES], *[delta[n] for n in WEIGHT_NAMES],
            *[new_m[n] for n in WEIGHT_NAMES], *[new_v[n] for n in WEIGHT_NAMES])
```

```python
import jax
import jax.numpy as jnp
from jax import lax
from jax.experimental import pallas as pl
from jax.experimental.pallas import tpu as pltpu

F32 = jnp.float32
BF = jnp.bfloat16

D_MODEL = 1024
N_META = 16
DEPTH = 2
HEADS = 4
HEAD_DIM = 128
HGRN_W = 512
POOL_W = 512
POOL_WINDOWS = (2, 4, 8, 16)
D_FF = 2816
IN_COLS = 4608
EPS = 1e-6
LOG_FLOOR = 1e-30
N_DEV = 8

ADAM_LR = 0.001
ADAM_B1 = 0.9
ADAM_B2 = 0.999
ADAM_EPS = 1e-08
ADAM_WD = 0.01
ADAM_STEP = 10

SUB = 16
CHUNK = 48
HGRN_UNROLL = 9
EXP_CLAMP = 80.0
ROW_TILE = 432
FFN_TILE = 144
HALO = 16
VMEM_LIMIT = 56 * 1024 * 1024
DW_BUFFER_BYTES = 44 * 1024 * 1024
MESH = pl.DeviceIdType.MESH

WEIGHT_NAMES = ['meta_tokens', 'mix_norm_pre', 'mix_norm_post', 'w_in', 'hgrn_lower_bounds', 'hgrn_out_norm',
                'w_branch_hgrn', 'pool_proj', 'pool_scale', 'w_branch_pool', 'w_out', 'ffn_norm_pre', 'ffn_norm_post',
                'ffn_w_gate', 'ffn_w_up', 'ffn_conv_w', 'ffn_conv_b', 'ffn_w_down']
SHARD_AXIS = {'meta_tokens': 1, 'w_in': 2, 'w_branch_hgrn': 2, 'w_branch_pool': 2, 'w_out': 1,
              'ffn_w_gate': 2, 'ffn_w_up': 2, 'ffn_conv_w': 2, 'ffn_w_down': 1}
F32_GATHERED = ['meta_tokens', 'ffn_conv_w']
REPLICATED = [n for n in WEIGHT_NAMES if n not in SHARD_AXIS]

W_ROWS = {'w_in': (0, 576),
          'wb': (576, 128),
          'w_out': (704, 128),
          'wg': (832, 352),
          'wu': (1184, 352),
          'wd': (1536, 352)}
SLAB_ROWS = 1888
EXCHANGE_GROUPS = {'wd': ['wd'], 'wg': ['wg'], 'rest': ['wu', 'wb', 'w_out'], 'in': ['w_in']}


def _params(sem=None):
    return pltpu.CompilerParams(dimension_semantics=sem, vmem_limit_bytes=VMEM_LIMIT)


def _tile(total, pref, mult=16):
    best = None
    for t in range(mult, min(total, pref) + 1, mult):
        if total % t == 0:
            best = t
    assert best is not None, (total, pref, mult)
    return best


def _whole(shape):
    return pl.BlockSpec(shape, lambda *_: (0,) * len(shape))


def _cols(tm, width, j):
    return pl.BlockSpec((tm, width), lambda i, j=j: (i, j))


def _dot(a, b):
    return jnp.dot(a, b, preferred_element_type=F32)


def _dot_nt(a, b):
    return lax.dot_general(a, b, (((1,), (1,)), ((), ())), preferred_element_type=F32)


def _dot_tn(a, b):
    return lax.dot_general(a, b, (((0,), (0,)), ((), ())), preferred_element_type=F32)


def _rms_fwd(x, g):
    r = lax.rsqrt(jnp.mean(x * x, axis=-1, keepdims=True) + EPS)
    return x * r * g


def _rms_bwd(dy, x, g):
    r = lax.rsqrt(jnp.mean(x * x, axis=-1, keepdims=True) + EPS)
    xh = x * r
    dyg = dy * g
    dx = r * (dyg - xh * jnp.mean(dyg * xh, axis=-1, keepdims=True))
    return dx, jnp.sum(dy * xh, axis=0, keepdims=True)


_GELU_C = 0.7978845608028654
_GELU_A = 0.044715


def _gelu_and_grad(x):
    x2 = x * x
    t = jnp.tanh(x * (_GELU_C + (_GELU_C * _GELU_A) * x2))
    u = 1.0 + t
    hx = 0.5 * x
    return hx * u, 0.5 * u + (hx * (1.0 - t * t)) * (_GELU_C + (3.0 * _GELU_C * _GELU_A) * x2)


def _split3(x):
    x1 = x.astype(BF)
    r1 = x - x1.astype(F32)
    x2 = r1.astype(BF)
    x3 = (r1 - x2.astype(F32)).astype(BF)
    return x1, x2, x3


def _tri_mm(tri, x):
    x1, x2, x3 = _split3(x)
    return _dot(tri, x1) + _dot(tri, x2) + _dot(tri, x3)


def _softmax2(lb_ref):
    l0 = lb_ref[0:1, :]
    l1 = lb_ref[1:2, :]
    m = jnp.maximum(l0, l1)
    e0 = jnp.exp(l0 - m)
    e1 = jnp.exp(l1 - m)
    return e0 / (e0 + e1), e1 / (e0 + e1)


def _layer_lower_bound(lb_ref, layer):
    g0, g1 = _softmax2(lb_ref)
    if layer == 0:
        return jnp.clip(g0 - g0, 0.0, 1.0)
    return jnp.clip((g0 + g1) - g0, 0.0, 1.0)


def _all_gather(x, name):
    n, w = x.shape

    def body(x_ref, out_ref, send_sems, recv_sems, local_sem):
        mx, my, mc = lax.axis_index("x"), lax.axis_index("y"), lax.axis_index("c")
        me, sibling = (mx, my, mc), (mx, my, 1 - mc)
        chips = [(1 - mx, my), (mx, 1 - my), (1 - mx, 1 - my)]

        def rows(px, py, pc):
            return out_ref.at[4 * px + 2 * py + pc]

        def copy(k, block, to, src=None):
            return pltpu.make_async_remote_copy(
                src_ref=rows(*block) if src is None else src, dst_ref=rows(*block),
                send_sem=send_sems.at[k], recv_sem=recv_sems.at[k], device_id=to, device_id_type=MESH)

        mine = pltpu.make_async_copy(x_ref, rows(*me), local_sem)
        mine.start()
        first = [copy(0, me, sibling, src=x_ref)]
        first += [copy(1 + j, me, (*chip, mc), src=x_ref) for j, chip in enumerate(chips)]
        for cp in first:
            cp.start()
        passed = [copy(4 + j, (*chip, mc), sibling) for j, chip in enumerate(chips)]
        for j, chip in enumerate(chips):
            copy(1 + j, (*chip, mc), me).wait_recv()
            passed[j].start()
        copy(0, sibling, me).wait_recv()
        for j, chip in enumerate(chips):
            copy(4 + j, (*chip, 1 - mc), me).wait_recv()
        for cp in first + passed:
            cp.wait_send()
        mine.wait()

    return pl.pallas_call(
        body, name=name,
        out_shape=jax.ShapeDtypeStruct((N_DEV, n, w), x.dtype),
        in_specs=[pl.BlockSpec(memory_space=pl.ANY)],
        out_specs=pl.BlockSpec(memory_space=pl.ANY),
        scratch_shapes=[pltpu.SemaphoreType.DMA((7,)), pltpu.SemaphoreType.DMA((7,)), pltpu.SemaphoreType.DMA],
    )(x)


GRAD_SRC = {'w_in': ('d_in', 0), 'wb': ('d_b', 0), 'w_out': ('d_out', 0), 'wg': ('d_gu', 0), 'wu': ('d_gu', D_FF),
            'wd': ('d_down', 0)}


class _GradExchange:
    def __init__(self, keys, grads):
        self.names = sorted({GRAD_SRC[k][0] for k in keys})
        self.arrays = [grads[n][0] for n in self.names] + [grads[n][1] for n in self.names]
        self.pieces, off = [], 0
        for k in keys:
            name, base = GRAD_SRC[k]
            rows = W_ROWS[k][1]
            self.pieces.append((self.names.index(name), base, rows, off))
            off += rows
        self.keys, self.rows = keys, off
        self.out_shape = (jax.ShapeDtypeStruct((N_DEV - 1, off, D_MODEL), BF), jax.ShapeDtypeStruct((off, D_MODEL), F32))
        self.scratch = [pltpu.SemaphoreType.DMA((N_DEV - 1,)), pltpu.SemaphoreType.DMA((N_DEV - 1,)),
                        pltpu.SemaphoreType.DMA((len(self.pieces),))]

    def _local(self, ins, outs, scr):
        mx, my, mc = lax.axis_index("x"), lax.axis_index("y"), lax.axis_index("c")
        me = 4 * mx + 2 * my + mc
        return [pltpu.make_async_copy(ins[ai].at[pl.ds(pl.multiple_of(base + rows * me, 8), rows)],
                                      outs[1].at[pl.ds(off, rows)], scr[2].at[k])
                for k, (ai, base, rows, off) in enumerate(self.pieces)]

    def start(self, ins, outs, scr):
        mx, my, mc = lax.axis_index("x"), lax.axis_index("y"), lax.axis_index("c")
        bf16 = ins[len(self.names):]
        for cp in self._local(ins, outs, scr):
            cp.start()
        for d in range(1, N_DEV):
            px = (1 - mx) if (d >> 2) & 1 else mx
            py = (1 - my) if (d >> 1) & 1 else my
            pc = (1 - mc) if d & 1 else mc
            peer = 4 * px + 2 * py + pc
            for ai, base, rows, off in self.pieces:
                pltpu.make_async_remote_copy(
                    src_ref=bf16[ai].at[pl.ds(pl.multiple_of(base + rows * peer, 16), rows)],
                    dst_ref=outs[0].at[d - 1, pl.ds(off, rows)], send_sem=scr[0].at[d - 1], recv_sem=scr[1].at[d - 1],
                    device_id=(px, py, pc), device_id_type=MESH).start()

    def wait(self, ins, outs, scr):
        me = (lax.axis_index("x"), lax.axis_index("y"), lax.axis_index("c"))
        for d in range(1, N_DEV):
            slot = pltpu.make_async_remote_copy(
                src_ref=outs[0].at[d - 1], dst_ref=outs[0].at[d - 1], send_sem=scr[0].at[d - 1],
                recv_sem=scr[1].at[d - 1], device_id=me, device_id_type=MESH)
            slot.wait_recv()
            slot.wait_send()
        for cp in self._local(ins, outs, scr):
            cp.wait()


class _SlabGather:
    def __init__(self, slab, row0, rows):
        self.arrays = [slab]
        self.row0, self.rows = row0, rows
        self.out_shape = (jax.ShapeDtypeStruct((N_DEV, rows, D_MODEL), slab.dtype),)
        self.scratch = [pltpu.SemaphoreType.DMA((7,)), pltpu.SemaphoreType.DMA((7,)), pltpu.SemaphoreType.DMA]

    def _parts(self, ins, outs, scr):
        mx, my, mc = lax.axis_index("x"), lax.axis_index("y"), lax.axis_index("c")
        me, sibling = (mx, my, mc), (mx, my, 1 - mc)
        chips = [(1 - mx, my), (mx, 1 - my), (1 - mx, 1 - my)]
        mine_ref = ins[0].at[pl.ds(self.row0, self.rows)]
        block = lambda px, py, pc: outs[0].at[4 * px + 2 * py + pc]

        def copy(k, blk, to, src=None):
            return pltpu.make_async_remote_copy(
                src_ref=block(*blk) if src is None else src, dst_ref=block(*blk),
                send_sem=scr[0].at[k], recv_sem=scr[1].at[k], device_id=to, device_id_type=MESH)

        local = pltpu.make_async_copy(mine_ref, block(*me), scr[2])
        first = [copy(0, me, sibling, src=mine_ref)]
        first += [copy(1 + j, me, (*chip, mc), src=mine_ref) for j, chip in enumerate(chips)]
        passed = [copy(4 + j, (*chip, mc), sibling) for j, chip in enumerate(chips)]
        landed = [copy(1 + j, (*chip, mc), me) for j, chip in enumerate(chips)]
        from_sibling = [copy(0, sibling, me)] + [copy(4 + j, (*chip, 1 - mc), me) for j, chip in enumerate(chips)]
        return local, first, passed, landed, from_sibling

    def start(self, ins, outs, scr):
        local, first, _, _, _ = self._parts(ins, outs, scr)
        local.start()
        for cp in first:
            cp.start()

    def wait(self, ins, outs, scr):
        local, first, passed, landed, from_sibling = self._parts(ins, outs, scr)
        for arrived, forward in zip(landed, passed):
            arrived.wait_recv()
            forward.start()
        for cp in from_sibling:
            cp.wait_recv()
        for cp in first + passed:
            cp.wait_send()
        local.wait()


class _Both:
    def __init__(self, first, second):
        self.parts = (first, second)
        self.arrays = list(first.arrays) + list(second.arrays)
        self.out_shape = tuple(first.out_shape) + tuple(second.out_shape)
        self.scratch = list(first.scratch) + list(second.scratch)

    def _split(self, ins, outs, scr):
        a, b = self.parts
        na, oa, sa = len(a.arrays), len(a.out_shape), len(a.scratch)
        return (a, ins[:na], outs[:oa], scr[:sa]), (b, ins[na:], outs[oa:], scr[sa:])

    def start(self, ins, outs, scr):
        for part, i, o, s in self._split(ins, outs, scr):
            part.start(i, o, s)

    def wait(self, ins, outs, scr):
        for part, i, o, s in self._split(ins, outs, scr):
            part.wait(i, o, s)


def _hosted_call(body, comm, name, grid, in_specs, args, out_shape, out_specs, scratch_shapes):
    if comm is None:
        return pl.pallas_call(body, name=name, grid=grid, in_specs=in_specs, out_specs=out_specs, out_shape=out_shape,
                              scratch_shapes=scratch_shapes, compiler_params=_params(("arbitrary",)))(*args), None
    n_in, n_out, n_scr, n_c = len(in_specs), len(out_shape), len(scratch_shapes), len(comm.arrays)
    n_co = len(comm.out_shape)
    steps = grid[0]

    def hosted(*refs):
        ins, cins = refs[:n_in], refs[n_in:n_in + n_c]
        outs = refs[n_in + n_c:n_in + n_c + n_out]
        couts = refs[n_in + n_c + n_out:n_in + n_c + n_out + n_co]
        scr = refs[n_in + n_c + n_out + n_co:n_in + n_c + n_out + n_co + n_scr]
        cscr = refs[n_in + n_c + n_out + n_co + n_scr:]

        @pl.when(pl.program_id(0) == 0)
        def _():
            comm.start(cins, couts, cscr)

        body(*ins, *outs, *scr)

        @pl.when(pl.program_id(0) == steps - 1)
        def _():
            comm.wait(cins, couts, cscr)

    res = pl.pallas_call(
        hosted, name=name, grid=grid, in_specs=list(in_specs) + [HBM_SPEC] * n_c,
        out_specs=tuple(out_specs) + (HBM_SPEC,) * n_co, out_shape=tuple(out_shape) + tuple(comm.out_shape),
        scratch_shapes=list(scratch_shapes) + comm.scratch, compiler_params=_params(("arbitrary",)),
    )(*args, *comm.arrays)
    return res[:n_out], res[n_out:]


def _sum_parts(own, recv, name):
    n, w = own.shape
    tn = _tile(n, 256, 16)

    def body(own_ref, p_ref, o_ref):
        acc = own_ref[...]
        for d in range(N_DEV - 1):
            acc = acc + p_ref[d].astype(F32)
        o_ref[...] = acc

    return pl.pallas_call(
        body, name=name, out_shape=jax.ShapeDtypeStruct((n, w), F32), grid=(n // tn,),
        in_specs=[pl.BlockSpec((tn, w), lambda i: (i, 0)), pl.BlockSpec((N_DEV - 1, tn, w), lambda i: (0, i, 0))],
        out_specs=pl.BlockSpec((tn, w), lambda i: (i, 0)),
        compiler_params=_params(("parallel",)),
    )(own, recv)


def _sum8(parts, name):
    _, n, w = parts.shape
    tn = _tile(n, 256 if w > 128 else 1024, 8)

    def body(p_ref, o_ref):
        acc = p_ref[0]
        for d in range(1, N_DEV):
            acc = acc + p_ref[d]
        o_ref[...] = acc

    return pl.pallas_call(
        body, name=name, out_shape=jax.ShapeDtypeStruct((n, w), F32), grid=(n // tn,),
        in_specs=[pl.BlockSpec((N_DEV, tn, w), lambda i: (0, i, 0))],
        out_specs=pl.BlockSpec((tn, w), lambda i: (i, 0)),
        compiler_params=_params(("parallel",)),
    )(parts)


def _pack(arrays, dtype, row_mult):
    flat = jnp.concatenate([a.astype(dtype).reshape(-1) for a in arrays])
    pad = (-flat.shape[0]) % (128 * row_mult)
    if pad:
        flat = jnp.concatenate([flat, jnp.zeros((pad,), dtype)])
    return flat.reshape(-1, 128)


def _unpack(flat2d, shapes):
    lead = flat2d.shape[:-2]
    flat = flat2d.reshape(lead + (-1,))
    out, off = [], 0
    for s in shapes:
        size = 1
        for d in s:
            size *= d
        out.append(flat[..., off:off + size].reshape(lead + tuple(s)))
        off += size
    return out


def _blocks_to_full(blocks, axis):
    moved = jnp.moveaxis(blocks, 0, axis)
    shape = list(moved.shape)
    shape[axis:axis + 2] = [shape[axis] * shape[axis + 1]]
    return moved.reshape(shape)


def _weight_scratch(keys):
    return ([pltpu.VMEM((N_DEV * W_ROWS[k][1], D_MODEL), BF) for k in keys]
            + [pltpu.SemaphoreType.DMA((N_DEV * len(keys),))])


def _fetch_weights(slabs_ref, keys, bufs, sems, base=0):
    @pl.when(pl.program_id(0) == 0)
    def _():
        copies = []
        for ki, (key, buf) in enumerate(zip(keys, bufs)):
            row0, rows = W_ROWS[key][0] - base, W_ROWS[key][1]
            for j in range(N_DEV):
                copies.append(pltpu.make_async_copy(slabs_ref.at[j, pl.ds(row0, rows)],
                                                    buf.at[pl.ds(j * rows, rows)], sems.at[ki * N_DEV + j]))
        for cp in copies:
            cp.start()
        for cp in copies:
            cp.wait()


HBM_SPEC = pl.BlockSpec(memory_space=pl.ANY)


def _norm_matmul(h, gain, slabs, keys, out_dtype, name, comm=None):
    t_rows, d = h.shape
    widths = [N_DEV * W_ROWS[k][1] for k in keys]
    tm = _tile(t_rows, ROW_TILE)

    def body(h_ref, g_ref, slabs_ref, o_ref, *scratch):
        bufs, sems = scratch[:-1], scratch[-1]
        _fetch_weights(slabs_ref, keys, bufs, sems, slabs[1])
        u = _rms_fwd(h_ref[...], g_ref[...]).astype(BF)
        off = 0
        for buf, n in zip(bufs, widths):
            o_ref[:, off:off + n] = _dot_nt(u, buf[...]).astype(out_dtype)
            off += n

    outs, moved = _hosted_call(
        body, comm, name, (t_rows // tm,),
        [pl.BlockSpec((tm, d), lambda i: (i, 0)), _whole((1, d)), HBM_SPEC], (h, gain, slabs[0]),
        (jax.ShapeDtypeStruct((t_rows, sum(widths)), out_dtype),),
        (pl.BlockSpec((tm, sum(widths)), lambda i: (i, 0)),), _weight_scratch(keys))
    return outs[0], moved


def _chunk_masks():
    row = lax.broadcasted_iota(jnp.int32, (CHUNK, CHUNK), 0)
    col = lax.broadcasted_iota(jnp.int32, (CHUNK, CHUNK), 1)
    tri_lo = (col <= row).astype(BF)
    tri_up = (col >= row).astype(BF)
    rb = jnp.right_shift(row, 4)
    cb = jnp.right_shift(col, 4)
    diag = (rb == cb) & (col <= row)
    off = [(rb == i) & (col < SUB * i) for i in range(1, CHUNK // SUB)]
    return tri_lo, tri_up, diag, off


def _chunk_gates(fl, lb):
    sg = jax.nn.sigmoid(fl)
    s2 = jax.nn.sigmoid(-fl)
    f = lb + (1.0 - lb) * sg
    lf = jnp.log(jnp.maximum(f, LOG_FLOOR))
    k = (1.0 - lb) * s2
    return sg, s2, f, lf, k


def _chunk_factors(q, k, b):
    nb = CHUNK // SUB
    rq = jnp.concatenate([jnp.broadcast_to(b[SUB * i:SUB * i + 1], (SUB, HEAD_DIM)) for i in range(nb)], axis=0)
    mq = jnp.concatenate([jnp.broadcast_to(b[SUB * i + SUB // 2:SUB * i + SUB // 2 + 1], (SUB, HEAD_DIM))
                          for i in range(nb)], axis=0)
    e_qs = jnp.exp(b - rq)
    e_qm = jnp.exp(jnp.minimum(b - mq, EXP_CLAMP))
    e_km = jnp.exp(jnp.minimum(mq - b, EXP_CLAMP))
    e_ks = [jnp.exp(jnp.minimum(b[SUB * i:SUB * i + 1] - b, 0.0)) for i in range(1, nb)]
    qs = (q * e_qs).astype(BF)
    qm = (q * e_qm).astype(BF)
    km = (k * e_km).astype(BF)
    ks = [(k * e).astype(BF) for e in e_ks]
    return e_qs, e_qm, e_km, e_ks, qs, qm, km, ks


def _chunk_scores(qs, qm, km, ks, diag, off):
    a = jnp.where(diag, _dot_nt(qm, km), 0.0)
    for m, kk in zip(off, ks):
        a = jnp.where(m, _dot_nt(qs, kk), a)
    return a


def _hgrn_fwd(proj, lower_bounds, layer, name, comm=None):
    t_rows = proj.shape[0]
    tm = _tile(t_rows, ROW_TILE, CHUNK)
    nct = tm // CHUNK

    def body(q_ref, f_ref, i_ref, lb_ref, o_ref, sall_ref, st_ref):
        @pl.when(pl.program_id(0) == 0)
        def _():
            st_ref[...] = jnp.zeros_like(st_ref)

        lbs = _layer_lower_bound(lb_ref, layer)
        tri_lo, _, diag, off = _chunk_masks()

        group = HGRN_UNROLL if nct % HGRN_UNROLL == 0 else 1

        def chunk_group(gi, carry):
            pairs = [(u, hh) for u in range(group) for hh in range(HEADS)]
            rows = [pl.ds(pl.multiple_of((gi * group + u) * CHUNK, SUB), CHUNK) for u in range(group)]
            cols = [slice(hh * HEAD_DIM, (hh + 1) * HEAD_DIM) for hh in range(HEADS)]
            q = {p: q_ref[rows[p[0]], cols[p[1]]] for p in pairs}
            vb = {p: i_ref[rows[p[0]], cols[p[1]]].astype(BF) for p in pairs}
            gates = {p: _chunk_gates(f_ref[rows[p[0]], cols[p[1]]], lbs[:, cols[p[1]]]) for p in pairs}
            b = {p: _tri_mm(tri_lo, gates[p][3]) for p in pairs}
            fac = {p: _chunk_factors(q[p], gates[p][4], b[p]) for p in pairs}
            a = {p: _chunk_scores(*fac[p][4:], diag, off) for p in pairs}
            o_intra = {p: _dot(a[p].astype(BF), vb[p]) for p in pairs}
            upd = {p: _dot_tn(vb[p], (gates[p][4] * jnp.exp(b[p][CHUNK - 1:CHUNK] - b[p])).astype(BF)) for p in pairs}
            qc = {p: (q[p] * jnp.exp(b[p])).astype(BF) for p in pairs}
            for hh in range(HEADS):
                st = st_ref[hh]
                for u in range(group):
                    p = (u, hh)
                    sall_ref[gi * group + u, hh] = st
                    o_ref[rows[u], cols[hh]] = _dot_nt(qc[p], st.astype(BF)) + o_intra[p]
                    st = st * jnp.exp(b[p][CHUNK - 1:CHUNK]) + upd[p]
                st_ref[hh] = st
            return carry

        lax.fori_loop(0, nct // group, chunk_group, 0)

    outs, moved = _hosted_call(
        body, comm, name, (t_rows // tm,),
        [_cols(tm, 512, 0), _cols(tm, 512, 1), _cols(tm, 512, 2), _whole((DEPTH, HGRN_W))],
        (proj, proj, proj, lower_bounds),
        (jax.ShapeDtypeStruct((t_rows, HGRN_W), F32),
         jax.ShapeDtypeStruct((t_rows // CHUNK, HEADS, HEAD_DIM, HEAD_DIM), F32)),
        (pl.BlockSpec((tm, HGRN_W), lambda i: (i, 0)),
         pl.BlockSpec((nct, HEADS, HEAD_DIM, HEAD_DIM), lambda i: (i, 0, 0, 0))),
        [pltpu.VMEM((HEADS, HEAD_DIM, HEAD_DIM), F32)])
    return (*outs, moved)


def _head_norm(o):
    outs, rs = [], []
    for hh in range(HEADS):
        oh = o[:, hh * HEAD_DIM:(hh + 1) * HEAD_DIM]
        r = lax.rsqrt(jnp.mean(oh * oh, axis=-1, keepdims=True) + EPS)
        outs.append(oh * r)
        rs.append(r)
    return outs, rs


def _window_counts(row0, rows):
    t1 = (row0 + lax.broadcasted_iota(jnp.int32, (rows, 1), 0) + 1).astype(F32)
    return [1.0 / jnp.minimum(t1, float(w)) for w in POOL_WINDOWS]


def _pool_fwd(v, halo, row0):
    rows = v.shape[0]
    inv = _window_counts(row0, rows)
    outs = []
    for gi, w in enumerate(POOL_WINDOWS):
        cs = slice(gi * HEAD_DIM, (gi + 1) * HEAD_DIM)
        s = jnp.concatenate([halo[:, cs], v[:, cs]], axis=0)
        step = 1
        while step < w:
            s = s + pltpu.roll(s, step, 0)
            step *= 2
        outs.append(s[HALO:] * inv[gi] - v[:, cs])
    return outs


def _pool_bwd(dpooled, halo, row0):
    rows = dpooled[0].shape[0]
    inv = _window_counts(row0, rows)
    inv_h = _window_counts(row0 + rows, HALO)
    outs = []
    for gi, w in enumerate(POOL_WINDOWS):
        s = jnp.concatenate([dpooled[gi] * inv[gi], halo[gi] * inv_h[gi]], axis=0)
        n_ext = s.shape[0]
        step = 1
        while step < w:
            s = s + pltpu.roll(s, n_ext - step, 0)
            step *= 2
        outs.append(s[:rows] - dpooled[gi])
    return outs


def _pool_project(pooled, pp_ref, scale):
    y = jnp.concatenate([_dot(pooled[gi].astype(BF), pp_ref[gi].astype(BF)) for gi in range(4)], axis=1)
    return y, y * scale


def _mix_merge(proj, o, h, out_gain, pool_proj, pool_scale, slabs, gain_post, name, comm=None):
    t_rows = h.shape[0]
    tm = _tile(t_rows, ROW_TILE)
    hpt = tm // HALO
    keys = ['wb', 'w_out']

    def body(g_ref, v_ref, vh_ref, ga0_ref, ga1_ref, gb0_ref, gb1_ref, o_ref, h_ref, og_ref, pp_ref, ps_ref,
             slabs_ref, gp_ref, hn_ref, r_ref, a_ref, p_ref, wb_ref, wo_ref, sems):
        _fetch_weights(slabs_ref, keys, (wb_ref, wo_ref), sems, slabs[1])
        i = pl.program_id(0)
        on, _ = _head_norm(o_ref[...])
        a = jnp.concatenate(on, axis=1) * og_ref[...] * jax.nn.sigmoid(g_ref[...])
        halo = jnp.where(i == 0, 0.0, vh_ref[...])
        pooled = _pool_fwd(v_ref[...], halo, i * tm)
        _, p = _pool_project(pooled, pp_ref, ps_ref[...])
        ab, pb = a.astype(BF), p.astype(BF)
        sa = jax.nn.sigmoid(jnp.concatenate([ga0_ref[...], ga1_ref[...]], axis=1))
        sb = jax.nn.sigmoid(jnp.concatenate([gb0_ref[...], gb1_ref[...]], axis=1))
        z = sa * _dot_nt(ab, wb_ref[:, 0:HGRN_W]) + sb * _dot_nt(pb, wb_ref[:, HGRN_W:HGRN_W + POOL_W])
        r = _dot(z.astype(BF), wo_ref[...])
        hn_ref[...] = h_ref[...] + _rms_fwd(r, gp_ref[...])
        r_ref[...] = r
        a_ref[...] = ab
        p_ref[...] = pb

    rows = lambda width: pl.BlockSpec((tm, width), lambda i: (i, 0))
    outs, moved = _hosted_call(
        body, comm, name, (t_rows // tm,),
        in_specs=[_cols(tm, 512, 3), _cols(tm, 512, 4),
                  pl.BlockSpec((HALO, 512), lambda i: (jnp.maximum(i * hpt - 1, 0), 4)),
                  _cols(tm, 512, 5), _cols(tm, 512, 6), _cols(tm, 512, 7), _cols(tm, 512, 8),
                  rows(HGRN_W), rows(D_MODEL), _whole((1, HGRN_W)), _whole((4, HEAD_DIM, HEAD_DIM)),
                  _whole((1, POOL_W)), HBM_SPEC, _whole((1, D_MODEL))],
        args=(proj, proj, proj, proj, proj, proj, proj, o, h, out_gain, pool_proj, pool_scale, slabs[0], gain_post),
        out_shape=(jax.ShapeDtypeStruct((t_rows, D_MODEL), F32), jax.ShapeDtypeStruct((t_rows, D_MODEL), F32),
                   jax.ShapeDtypeStruct((t_rows, HGRN_W), BF), jax.ShapeDtypeStruct((t_rows, POOL_W), BF)),
        out_specs=(rows(D_MODEL), rows(D_MODEL), rows(HGRN_W), rows(POOL_W)),
        scratch_shapes=_weight_scratch(keys))
    return (*outs, moved)


def _conv_fwd(g, halo, cw):
    ext = jnp.concatenate([halo, g], axis=0)
    return (cw[0:1] * pltpu.roll(ext, 2, 0)[8:] + cw[1:2] * pltpu.roll(ext, 1, 0)[8:] + cw[2:3] * g)


def _ffn_down(gu, h, conv_w, conv_b, slabs, gain_post, name):
    t_rows = h.shape[0]
    tm = _tile(t_rows, FFN_TILE)
    hpt = tm // HALO

    def body(g_ref, gh_ref, up_ref, h_ref, cw_ref, cb_ref, slabs_ref, gp_ref, hn_ref, y_ref, gl_ref, gg_ref,
             wd_ref, sems):
        _fetch_weights(slabs_ref, ['wd'], (wd_ref,), sems, slabs[1])
        i = pl.program_id(0)
        halo = jnp.where(i == 0, 0.0, gh_ref[8:HALO, :].astype(F32))
        gt = _conv_fwd(g_ref[...].astype(F32), halo, cw_ref[...]) + cb_ref[...]
        gl, gg = _gelu_and_grad(gt)
        gl_ref[...] = gl.astype(BF)
        gg_ref[...] = gg.astype(BF)
        act = gl * up_ref[...].astype(F32)
        y = _dot(act.astype(BF), wd_ref[...])
        hn_ref[...] = h_ref[...] + _rms_fwd(y, gp_ref[...])
        y_ref[...] = y

    rows = lambda width: pl.BlockSpec((tm, width), lambda i: (i, 0))
    return pl.pallas_call(
        body, name=name,
        out_shape=(jax.ShapeDtypeStruct((t_rows, D_MODEL), F32), jax.ShapeDtypeStruct((t_rows, D_MODEL), F32),
                   jax.ShapeDtypeStruct((t_rows, D_FF), BF), jax.ShapeDtypeStruct((t_rows, D_FF), BF)),
        grid=(t_rows // tm,),
        in_specs=[_cols(tm, D_FF, 0), pl.BlockSpec((HALO, D_FF), lambda i: (jnp.maximum(i * hpt - 1, 0), 0)),
                  _cols(tm, D_FF, 1), rows(D_MODEL), _whole((3, D_FF)), _whole((1, D_FF)),
                  HBM_SPEC, _whole((1, D_MODEL))],
        out_specs=(rows(D_MODEL), rows(D_MODEL), rows(D_FF), rows(D_FF)),
        scratch_shapes=_weight_scratch(['wd']),
        compiler_params=_params(("arbitrary",)),
    )(gu, gu, gu, h, conv_w, conv_b, slabs[0], gain_post)


def _loss_grad(h, target, name):
    t_rows, d = h.shape
    tm = _tile(t_rows, ROW_TILE)

    def body(h_ref, t_ref, dh_ref, l_ref):
        i = pl.program_id(0)

        @pl.when(i == 0)
        def _():
            l_ref[...] = jnp.zeros_like(l_ref)

        row = i * tm + lax.broadcasted_iota(jnp.int32, (tm, 1), 0)
        err = jnp.where(row >= N_META, h_ref[...] - t_ref[...], 0.0)
        dh_ref[...] = err * (1.0 / d)
        l_ref[...] += jnp.sum(err * err, axis=0, keepdims=True) * (0.5 / d)

    rows = pl.BlockSpec((tm, d), lambda i: (i, 0))
    return pl.pallas_call(
        body, name=name,
        out_shape=(jax.ShapeDtypeStruct((t_rows, d), F32), jax.ShapeDtypeStruct((1, d), F32)),
        grid=(t_rows // tm,), in_specs=[rows, rows], out_specs=(rows, _whole((1, d))),
        compiler_params=_params(("arbitrary",)),
    )(h, target)


def _dw(lhs, rhs, name, gain=None, comm=None):
    t_rows = lhs[0].shape[0]
    paired = len(rhs) > 1
    n_rows = lhs[0].shape[1] if paired else sum(x.shape[1] for x in lhs)
    n_cols = sum(x.shape[1] for x in rhs)
    row_bytes = 2 * sum(x.shape[1] * x.dtype.itemsize for x in list(lhs) + list(rhs))
    tm = _tile(t_rows, max(ROW_TILE, (DW_BUFFER_BYTES - 4 * n_rows * n_cols) // row_bytes))
    last = t_rows // tm - 1

    def body(*refs):
        lhs_refs = refs[:len(lhs)]
        rhs_refs = refs[len(lhs):len(lhs) + len(rhs)]
        rest = refs[len(lhs) + len(rhs):]
        if gain is not None:
            g_ref, o_ref, o16_ref, acc, stage = rest
            rv = [_rms_fwd(rhs_refs[0][...], g_ref[...]).astype(BF)]
        else:
            o_ref, o16_ref, acc, stage = rest
            rv = [r[...] for r in rhs_refs]
        i = pl.program_id(0)

        @pl.when(i == 0)
        def _():
            acc[...] = jnp.zeros_like(acc)

        r0, c0 = 0, 0
        for p, l_ref in enumerate(lhs_refs):
            r = rv[p] if paired else rv[0]
            n = l_ref.shape[1]
            step = 512 if n % 512 == 0 else 256
            for s in range(0, n, step):
                acc[r0 + s:r0 + s + step, c0:c0 + r.shape[1]] += _dot_tn(l_ref[:, s:s + step], r)
            if paired:
                c0 += r.shape[1]
            else:
                r0 += n

        @pl.when(i == last)
        def _():
            pltpu.sync_copy(acc, o_ref)
            for s in range(0, n_rows, 256):
                stage[...] = acc[s:s + 256, :].astype(BF)
                pltpu.sync_copy(stage, o16_ref.at[pl.ds(s, 256)])

    rows = lambda x: pl.BlockSpec((tm, x.shape[1]), lambda i: (i, 0))
    in_specs = [rows(x) for x in lhs] + [rows(x) for x in rhs]
    args = list(lhs) + list(rhs)
    if gain is not None:
        in_specs.append(_whole(gain.shape))
        args.append(gain)
    outs, moved = _hosted_call(
        body, comm, name, (t_rows // tm,), in_specs, args,
        (jax.ShapeDtypeStruct((n_rows, n_cols), F32), jax.ShapeDtypeStruct((n_rows, n_cols), BF)),
        (HBM_SPEC, HBM_SPEC), [pltpu.VMEM((n_rows, n_cols), F32), pltpu.VMEM((256, n_cols), BF)])
    return (outs[0], outs[1]), moved


def _ffn_bwd(dh, y, gain_post, gu, gl, gg, conv_w, slabs_down, slabs_up, h, gain_pre, name, comm=None):
    t_rows = dh.shape[0]
    tm = _tile(t_rows, FFN_TILE)
    hpt = tm // HALO
    n_tiles = t_rows // tm

    def body(dh_ref, y_ref, gp_ref, g_ref, gh_ref, up_ref, gl_ref, gg_ref, cw_ref, h_ref, gpre_ref, sd_ref, su_ref,
             dhm_ref, dy_ref, act_ref, dgu_ref, dgp_ref, dcw_ref, dcb_ref, dgpre_ref,
             wd_ref, sems_d, wg_ref, wu_ref, sems_u, carry_ref):
        _fetch_weights(sd_ref, ['wd'], (wd_ref,), sems_d, slabs_down[1])
        _fetch_weights(su_ref, ['wg', 'wu'], (wg_ref, wu_ref), sems_u, slabs_up[1])
        i = pl.program_id(0)

        @pl.when(i == 0)
        def _():
            dgp_ref[...] = jnp.zeros_like(dgp_ref)
            dcw_ref[...] = jnp.zeros_like(dcw_ref)
            dcb_ref[...] = jnp.zeros_like(dcb_ref)
            dgpre_ref[...] = jnp.zeros_like(dgpre_ref)
            carry_ref[...] = jnp.zeros_like(carry_ref)

        dh = dh_ref[...]
        dy, dgp = _rms_bwd(dh, y_ref[...], gp_ref[...])
        dgp_ref[...] += dgp
        dyb = dy.astype(BF)
        dy_ref[...] = dyb
        g = g_ref[...].astype(F32)
        up = up_ref[...].astype(F32)
        first_tile = i == n_tiles - 1
        ext = jnp.concatenate([jnp.where(first_tile, 0.0, gh_ref[8:HALO, :].astype(F32)), g], axis=0)
        g2 = pltpu.roll(ext, 2, 0)[8:]
        g1 = pltpu.roll(ext, 1, 0)[8:]
        gl = gl_ref[...].astype(F32)
        act_ref[...] = (gl * up).astype(BF)
        dact = _dot_nt(dyb, wd_ref[...])
        dub = (dact * gl).astype(BF)
        dgt = dact * up * gg_ref[...].astype(F32)
        dcb_ref[...] += jnp.sum(dgt, axis=0, keepdims=True)
        dcw_ref[...] += jnp.concatenate([jnp.sum(dgt * g2, axis=0, keepdims=True),
                                         jnp.sum(dgt * g1, axis=0, keepdims=True),
                                         jnp.sum(dgt * g, axis=0, keepdims=True)], axis=0)
        after = jnp.concatenate([dgt, carry_ref[...]], axis=0)
        carry_ref[...] = dgt[0:8, :]
        cw = cw_ref[...]
        dg = (cw[2:3] * dgt + cw[1:2] * pltpu.roll(after, tm + 8 - 1, 0)[:tm]
              + cw[0:1] * pltpu.roll(after, tm + 8 - 2, 0)[:tm])
        dgb = dg.astype(BF)
        dgu_ref[:, 0:D_FF] = dgb
        dgu_ref[:, D_FF:2 * D_FF] = dub
        du = _dot(dgb, wg_ref[...]) + _dot(dub, wu_ref[...])
        dx, dgain = _rms_bwd(du, h_ref[...], gpre_ref[...])
        dhm_ref[...] = dh + dx
        dgpre_ref[...] += dgain

    rev = lambda width, j=0: pl.BlockSpec((tm, width), lambda i, j=j: (n_tiles - 1 - i, j))
    bf = lambda width: jax.ShapeDtypeStruct((t_rows, width), BF)
    outs, moved = _hosted_call(
        body, comm, name, (n_tiles,),
        [rev(D_MODEL), rev(D_MODEL), _whole((1, D_MODEL)), rev(D_FF, 0),
         pl.BlockSpec((HALO, D_FF), lambda i: (jnp.maximum((n_tiles - 1 - i) * hpt - 1, 0), 0)), rev(D_FF, 1),
         rev(D_FF), rev(D_FF), _whole((3, D_FF)), rev(D_MODEL), _whole((1, D_MODEL)), HBM_SPEC, HBM_SPEC],
        (dh, y, gain_post, gu, gu, gu, gl, gg, conv_w, h, gain_pre, slabs_down[0], slabs_up[0]),
        (jax.ShapeDtypeStruct((t_rows, D_MODEL), F32), bf(D_MODEL), bf(D_FF), bf(2 * D_FF),
         jax.ShapeDtypeStruct((1, D_MODEL), F32), jax.ShapeDtypeStruct((3, D_FF), F32),
         jax.ShapeDtypeStruct((1, D_FF), F32), jax.ShapeDtypeStruct((1, D_MODEL), F32)),
        (rev(D_MODEL), rev(D_MODEL), rev(D_FF), rev(2 * D_FF), _whole((1, D_MODEL)), _whole((3, D_FF)),
         _whole((1, D_FF)), _whole((1, D_MODEL))),
        _weight_scratch(['wd']) + _weight_scratch(['wg', 'wu']) + [pltpu.VMEM((8, D_FF), F32)])
    return (*outs, moved)


def _mix_bwd_a(dh, r, gain_post, proj, a, p, slabs, name, comm=None):
    t_rows = dh.shape[0]
    tm = _tile(t_rows, ROW_TILE)

    def body(dh_ref, r_ref, gp_ref, ga0_ref, ga1_ref, gb0_ref, gb1_ref, a_ref, p_ref, slabs_ref,
             dr_ref, z_ref, dya_ref, dyp_ref, da_ref, dp_ref, dgab_ref, dg_ref, wb_ref, wo_ref, sems):
        _fetch_weights(slabs_ref, ['wb', 'w_out'], (wb_ref, wo_ref), sems, slabs[1])

        @pl.when(pl.program_id(0) == 0)
        def _():
            dg_ref[...] = jnp.zeros_like(dg_ref)

        dr, dgain = _rms_bwd(dh_ref[...], r_ref[...], gp_ref[...])
        dg_ref[...] += dgain
        drb = dr.astype(BF)
        dr_ref[...] = drb
        dz = _dot_nt(drb, wo_ref[...])
        ya = _dot_nt(a_ref[...], wb_ref[:, 0:HGRN_W])
        yp = _dot_nt(p_ref[...], wb_ref[:, HGRN_W:HGRN_W + POOL_W])
        sa = jax.nn.sigmoid(jnp.concatenate([ga0_ref[...], ga1_ref[...]], axis=1))
        sb = jax.nn.sigmoid(jnp.concatenate([gb0_ref[...], gb1_ref[...]], axis=1))
        z_ref[...] = (sa * ya + sb * yp).astype(BF)
        dgab_ref[:, 0:D_MODEL] = (dz * ya * sa * (1.0 - sa)).astype(BF)
        dgab_ref[:, D_MODEL:2 * D_MODEL] = (dz * yp * sb * (1.0 - sb)).astype(BF)
        dya = (dz * sa).astype(BF)
        dyp = (dz * sb).astype(BF)
        dya_ref[...] = dya
        dyp_ref[...] = dyp
        da_ref[...] = _dot(dya, wb_ref[:, 0:HGRN_W])
        dp_ref[...] = _dot(dyp, wb_ref[:, HGRN_W:HGRN_W + POOL_W])

    rows = lambda width: pl.BlockSpec((tm, width), lambda i: (i, 0))
    bf = lambda width: jax.ShapeDtypeStruct((t_rows, width), BF)
    f32 = lambda width: jax.ShapeDtypeStruct((t_rows, width), F32)
    outs, moved = _hosted_call(
        body, comm, name, (t_rows // tm,),
        [rows(D_MODEL), rows(D_MODEL), _whole((1, D_MODEL)),
         _cols(tm, 512, 5), _cols(tm, 512, 6), _cols(tm, 512, 7), _cols(tm, 512, 8),
         rows(HGRN_W), rows(POOL_W), HBM_SPEC],
        (dh, r, gain_post, proj, proj, proj, proj, a, p, slabs[0]),
        (bf(D_MODEL), bf(D_MODEL), bf(D_MODEL), bf(D_MODEL), f32(HGRN_W), f32(POOL_W), bf(2 * D_MODEL),
         jax.ShapeDtypeStruct((1, D_MODEL), F32)),
        (rows(D_MODEL), rows(D_MODEL), rows(D_MODEL), rows(D_MODEL), rows(HGRN_W), rows(POOL_W),
         rows(2 * D_MODEL), _whole((1, D_MODEL))),
        _weight_scratch(['wb', 'w_out']))
    return (*outs, moved)


def _mix_bwd_b(da, dp, proj, o, out_gain, pool_proj, pool_scale, name):
    t_rows = da.shape[0]
    tm = _tile(t_rows, ROW_TILE)
    hpt = tm // HALO
    last = t_rows // tm - 1

    def body(da_ref, dp_ref, dpn_ref, g_ref, v_ref, vh_ref, o_ref, og_ref, pp_ref, ps_ref,
             do_ref, dgv_ref, dog_ref, dpp_ref, dps_ref):
        i = pl.program_id(0)

        @pl.when(i == 0)
        def _():
            dog_ref[...] = jnp.zeros_like(dog_ref)
            dpp_ref[...] = jnp.zeros_like(dpp_ref)
            dps_ref[...] = jnp.zeros_like(dps_ref)

        da = da_ref[...]
        og = og_ref[...]
        o = o_ref[...]
        on, rs = _head_norm(o)
        onc = jnp.concatenate(on, axis=1)
        sg = jax.nn.sigmoid(g_ref[...])
        dog_ref[...] += jnp.sum(da * onc * sg, axis=0, keepdims=True)
        dgv_ref[:, 0:HGRN_W] = (da * onc * og * sg * (1.0 - sg)).astype(BF)
        don = da * og * sg
        for hh in range(HEADS):
            cs = slice(hh * HEAD_DIM, (hh + 1) * HEAD_DIM)
            d = don[:, cs]
            do_ref[:, cs] = rs[hh] * (d - on[hh] * jnp.mean(d * on[hh], axis=-1, keepdims=True))

        scale = ps_ref[...]
        halo = jnp.where(i == 0, 0.0, vh_ref[...])
        pooled = _pool_fwd(v_ref[...], halo, i * tm)
        y, _ = _pool_project(pooled, pp_ref, scale)
        dp = dp_ref[...]
        dps_ref[...] += jnp.sum(dp * y, axis=0, keepdims=True)
        dy = dp * scale
        dyn = jnp.where(i == last, 0.0, dpn_ref[...]) * scale
        dpooled, dhalo = [], []
        for gi in range(4):
            cs = slice(gi * HEAD_DIM, (gi + 1) * HEAD_DIM)
            ppb = pp_ref[gi].astype(BF)
            dyb = dy[:, cs].astype(BF)
            dpooled.append(_dot_nt(dyb, ppb))
            dhalo.append(_dot_nt(dyn[:, cs].astype(BF), ppb))
            dpp_ref[gi] += _dot_tn(pooled[gi].astype(BF), dyb)
        dv = _pool_bwd(dpooled, dhalo, i * tm)
        dgv_ref[:, HGRN_W:HGRN_W + POOL_W] = jnp.concatenate(dv, axis=1).astype(BF)

    rows = lambda width: pl.BlockSpec((tm, width), lambda i: (i, 0))
    return pl.pallas_call(
        body, name=name,
        out_shape=(jax.ShapeDtypeStruct((t_rows, HGRN_W), F32), jax.ShapeDtypeStruct((t_rows, HGRN_W + POOL_W), BF),
                   jax.ShapeDtypeStruct((1, HGRN_W), F32), jax.ShapeDtypeStruct((4, HEAD_DIM, HEAD_DIM), F32),
                   jax.ShapeDtypeStruct((1, POOL_W), F32)),
        grid=(t_rows // tm,),
        in_specs=[rows(HGRN_W), rows(POOL_W),
                  pl.BlockSpec((HALO, POOL_W), lambda i: (jnp.minimum((i + 1) * hpt, (last + 1) * hpt - 1), 0)),
                  _cols(tm, 512, 3), _cols(tm, 512, 4),
                  pl.BlockSpec((HALO, 512), lambda i: (jnp.maximum(i * hpt - 1, 0), 4)),
                  rows(HGRN_W), _whole((1, HGRN_W)), _whole((4, HEAD_DIM, HEAD_DIM)), _whole((1, POOL_W))],
        out_specs=(rows(HGRN_W), rows(HGRN_W + POOL_W), _whole((1, HGRN_W)), _whole((4, HEAD_DIM, HEAD_DIM)),
                   _whole((1, POOL_W))),
        compiler_params=_params(("arbitrary",)),
    )(da, dp, dp, proj, proj, proj, o, out_gain, pool_proj, pool_scale)


def _hgrn_bwd(proj, lower_bounds, layer, states, do, name, comm=None):
    t_rows = proj.shape[0]
    tm = _tile(t_rows, ROW_TILE, CHUNK)
    nct = tm // CHUNK
    n_tiles = t_rows // tm

    def body(q_ref, f_ref, i_ref, lb_ref, sall_ref, do_ref, dqfi_ref, dlb_ref, dst_ref):
        @pl.when(pl.program_id(0) == 0)
        def _():
            dst_ref[...] = jnp.zeros_like(dst_ref)
            dlb_ref[...] = jnp.zeros_like(dlb_ref)

        lbs = _layer_lower_bound(lb_ref, layer)
        tri_lo, tri_up, diag, off = _chunk_masks()
        is_last_row = lax.broadcasted_iota(jnp.int32, (CHUNK, 1), 0) == CHUNK - 1

        group = HGRN_UNROLL if nct % HGRN_UNROLL == 0 else 1

        def chunk_group(gi, carry):
            pairs = [(u, hh) for u in range(group) for hh in range(HEADS)]
            cidx = [nct - 1 - (gi * group + u) for u in range(group)]
            rows = [pl.ds(pl.multiple_of(c * CHUNK, SUB), CHUNK) for c in cidx]
            cols = [slice(hh * HEAD_DIM, (hh + 1) * HEAD_DIM) for hh in range(HEADS)]
            lb = {p: lbs[:, cols[p[1]]] for p in pairs}
            q = {p: q_ref[rows[p[0]], cols[p[1]]] for p in pairs}
            vb = {p: i_ref[rows[p[0]], cols[p[1]]].astype(BF) for p in pairs}
            dob = {p: do_ref[rows[p[0]], cols[p[1]]].astype(BF) for p in pairs}
            gates = {p: _chunk_gates(f_ref[rows[p[0]], cols[p[1]]], lb[p]) for p in pairs}
            k = {p: gates[p][4] for p in pairs}
            b = {p: _tri_mm(tri_lo, gates[p][3]) for p in pairs}
            fac = {p: _chunk_factors(q[p], k[p], b[p]) for p in pairs}
            e_b = {p: jnp.exp(b[p]) for p in pairs}
            e_last = {p: jnp.exp(b[p][CHUNK - 1:CHUNK]) for p in pairs}
            e_kl = {p: jnp.exp(b[p][CHUNK - 1:CHUNK] - b[p]) for p in pairs}
            qc = {p: (q[p] * e_b[p]).astype(BF) for p in pairs}
            kdec = {p: (k[p] * e_kl[p]).astype(BF) for p in pairs}
            a = {p: _chunk_scores(*fac[p][4:], diag, off) for p in pairs}
            da_full = {p: _dot_nt(dob[p], vb[p]) for p in pairs}
            dv, dq, dk, upd, dupd = {}, {}, {}, {}, {}
            for p in pairs:
                e_qs, e_qm, e_km, e_ks, qs, qm, km, ks = fac[p]
                da_d = jnp.where(diag, da_full[p], 0.0).astype(BF)
                dv[p] = _dot_tn(a[p].astype(BF), dob[p])
                dq[p] = e_qm * _dot(da_d, km)
                dk[p] = e_km * _dot_tn(da_d, qm)
                for m, kk, ek in zip(off, ks, e_ks):
                    da_i = jnp.where(m, da_full[p], 0.0).astype(BF)
                    dq[p] = dq[p] + e_qs * _dot(da_i, kk)
                    dk[p] = dk[p] + ek * _dot_tn(da_i, qs)
                upd[p] = _dot_tn(vb[p], kdec[p])
                dupd[p] = _dot_tn(dob[p], qc[p])
            db = {}
            for hh in range(HEADS):
                dst = dst_ref[hh]
                for u in range(group):
                    p = (u, hh)
                    st = sall_ref[cidx[u], hh]
                    dstb = dst.astype(BF)
                    dv[p] = dv[p] + _dot_nt(kdec[p], dstb)
                    dq[p] = dq[p] + e_b[p] * _dot(dob[p], st.astype(BF))
                    dk[p] = dk[p] + e_kl[p] * _dot(vb[p], dstb)
                    st_new = st * e_last[p] + upd[p]
                    db[p] = (q[p] * dq[p] - k[p] * dk[p]
                             + jnp.where(is_last_row, jnp.sum(st_new * dst, axis=0, keepdims=True), 0.0))
                    dst = dst * e_last[p] + dupd[p]
                dst_ref[hh] = dst
            for p in pairs:
                u, hh = p
                sg, s2, f = gates[p][:3]
                dlf = _tri_mm(tri_up, db[p])
                df = jnp.where(f > LOG_FLOOR, dlf / f, 0.0)
                dfl = df * (1.0 - lb[p]) * sg * (1.0 - sg) - dk[p] * (1.0 - lb[p]) * s2 * (1.0 - s2)
                dlb_ref[:, cols[hh]] += jnp.sum(df * (1.0 - sg) - dk[p] * s2, axis=0, keepdims=True)
                dqfi_ref[rows[u], cols[hh]] = dq[p].astype(BF)
                dqfi_ref[rows[u], pl.ds(HGRN_W + hh * HEAD_DIM, HEAD_DIM)] = dfl.astype(BF)
                dqfi_ref[rows[u], pl.ds(2 * HGRN_W + hh * HEAD_DIM, HEAD_DIM)] = dv[p].astype(BF)
            return carry

        lax.fori_loop(0, nct // group, chunk_group, 0)

    rev = lambda width, j: pl.BlockSpec((tm, width), lambda i, j=j: (n_tiles - 1 - i, j))
    outs, exchanged = _hosted_call(
        body, comm, name, (n_tiles,),
        [rev(512, 0), rev(512, 1), rev(512, 2), _whole((DEPTH, HGRN_W)),
         pl.BlockSpec((nct, HEADS, HEAD_DIM, HEAD_DIM), lambda i: (n_tiles - 1 - i, 0, 0, 0)), rev(HGRN_W, 0)],
        (proj, proj, proj, lower_bounds, states, do),
        (jax.ShapeDtypeStruct((t_rows, 3 * HGRN_W), BF), jax.ShapeDtypeStruct((1, HGRN_W), F32)),
        (rev(3 * HGRN_W, 0), _whole((1, HGRN_W))),
        [pltpu.VMEM((HEADS, HEAD_DIM, HEAD_DIM), F32)])
    return (*outs, exchanged)


def _in_bwd(dqfi, dgv, dgab, slabs, h, gain_pre, dh_out, name, comm=None):
    t_rows = h.shape[0]
    tm = _tile(t_rows, ROW_TILE)
    c1 = 3 * HGRN_W
    c2 = c1 + HGRN_W + POOL_W

    def body(d1_ref, d2_ref, d3_ref, slabs_ref, h_ref, gp_ref, dho_ref, dh_ref, dg_ref, w_ref, sems):
        _fetch_weights(slabs_ref, ['w_in'], (w_ref,), sems, slabs[1])

        @pl.when(pl.program_id(0) == 0)
        def _():
            dg_ref[...] = jnp.zeros_like(dg_ref)

        du = (_dot(d1_ref[...], w_ref[0:c1, :]) + _dot(d2_ref[...], w_ref[c1:c2, :])
              + _dot(d3_ref[...], w_ref[c2:IN_COLS, :]))
        dx, dgain = _rms_bwd(du, h_ref[...], gp_ref[...])
        dh_ref[...] = dho_ref[...] + dx
        dg_ref[...] += dgain

    rows = lambda width: pl.BlockSpec((tm, width), lambda i: (i, 0))
    outs, exchanged = _hosted_call(
        body, comm, name, (t_rows // tm,),
        [rows(c1), rows(c2 - c1), rows(IN_COLS - c2), HBM_SPEC, rows(D_MODEL), _whole((1, D_MODEL)), rows(D_MODEL)],
        (dqfi, dgv, dgab, slabs[0], h, gain_pre, dh_out),
        (jax.ShapeDtypeStruct((t_rows, D_MODEL), F32), jax.ShapeDtypeStruct((1, D_MODEL), F32)),
        (rows(D_MODEL), _whole((1, D_MODEL))),
        _weight_scratch(['w_in']))
    return (*outs, exchanged)


def _lower_bound_grad(lower_bounds, dlbs, name):
    assert DEPTH == 2 and lower_bounds.shape[0] == DEPTH, "the softmax over layers is written out for two layers"
    def body(lb_ref, d_ref, o_ref):
        g0, g1 = _softmax2(lb_ref)
        bound = (g0 + g1) - g0
        inside = (bound > 0.0) & (bound < 1.0)
        dg1 = jnp.where(inside, d_ref[1:2, :], 0.0)
        inner = g1 * dg1
        o_ref[0:1, :] = g0 * (0.0 - inner)
        o_ref[1:2, :] = g1 * (dg1 - inner)

    return pl.pallas_call(body, name=name, out_shape=jax.ShapeDtypeStruct(lower_bounds.shape, F32))(lower_bounds, dlbs)


def _adamw(w, g, m, v, name):
    r, c = w.shape
    tr = r if (r % 8 or r <= 512) else _tile(r, 512, 8)
    c1 = 1.0 - ADAM_B1 ** ADAM_STEP
    c2 = 1.0 - ADAM_B2 ** ADAM_STEP

    def body(w_ref, g_ref, m_ref, v_ref, d_ref, nm_ref, nv_ref):
        gg = g_ref[...]
        nm = ADAM_B1 * m_ref[...] + (1.0 - ADAM_B1) * gg
        nv = ADAM_B2 * v_ref[...] + (1.0 - ADAM_B2) * (gg * gg)
        d_ref[...] = -ADAM_LR * ((nm / c1) / (jnp.sqrt(nv / c2) + ADAM_EPS) + ADAM_WD * w_ref[...])
        nm_ref[...] = nm
        nv_ref[...] = nv

    blk = pl.BlockSpec((tr, c), lambda i: (i, 0))
    shp = jax.ShapeDtypeStruct((r, c), F32)
    return pl.pallas_call(
        body, name=name, out_shape=(shp, shp, shp), grid=(r // tr,),
        in_specs=[blk, blk, blk, blk], out_specs=(blk, blk, blk),
        compiler_params=_params(("parallel",)),
    )(w, g, m, v)


def _as2d(a):
    return a.reshape(-1, a.shape[-1])


def _layer_slab(w, l):
    t = lambda a: jnp.swapaxes(a, 0, 1)
    parts = [t(w['w_in'][l]), jnp.concatenate([t(w['w_branch_hgrn'][l]), t(w['w_branch_pool'][l])], axis=1),
             w['w_out'][l], t(w['ffn_w_gate'][l]), t(w['ffn_w_up'][l]), w['ffn_w_down'][l]]
    return jnp.concatenate(parts, axis=0).astype(BF)


def _slab_grads(sums):
    t = lambda a: jnp.swapaxes(a, 0, 1)
    wb = sums['wb']
    return {'w_in': t(sums['w_in']), 'w_branch_hgrn': t(wb[:, :HGRN_W]), 'w_branch_pool': t(wb[:, HGRN_W:]),
            'w_out': sums['w_out'], 'ffn_w_gate': t(sums['wg']), 'ffn_w_up': t(sums['wu']), 'ffn_w_down': sums['wd']}


def _train_step(x, target, w, m, v):
    my_slabs = [_layer_slab(w, l) for l in range(DEPTH)]
    n_in = W_ROWS['w_in'][1]
    slabs = [{} for _ in range(DEPTH)]
    slabs[0]['w_in'] = (_all_gather(my_slabs[0][:n_in], "gather_w_in_0"), 0)
    f32_shapes = [w[n].shape for n in F32_GATHERED]
    gathered32 = _all_gather(_pack([w[n] for n in F32_GATHERED], F32, 8), "gather_meta_conv")
    full = {n: _blocks_to_full(blk, SHARD_AXIS[n]) for n, blk in zip(F32_GATHERED, _unpack(gathered32, f32_shapes))}
    row = lambda name, l: w[name][l][None]

    h = jnp.concatenate([full['meta_tokens'], x], axis=0)
    saved = []
    for l in range(DEPTH):
        s = {'h_in': h}
        nxt = l + 1 if l + 1 < DEPTH else None
        plan = {'in_proj': [(0, ['wb', 'w_out', 'wg', 'wu'])] if l == 0 else [],
                'hgrn_fwd': ([(0, ['wd'])] if l == 0 else []) + ([(nxt, ['w_in'])] if nxt else []),
                'mix_merge': [(nxt, ['wb', 'w_out']), (nxt, ['wd'])] if nxt else [],
                'ffn_proj': [(nxt, ['wg', 'wu'])] if nxt else []}

        def hosted(kernel):
            comms = []
            for layer, keys in plan[kernel]:
                first, last = W_ROWS[keys[0]], W_ROWS[keys[-1]]
                comms.append(_SlabGather(my_slabs[layer], first[0], last[0] + last[1] - first[0]))
            return None if not comms else comms[0] if len(comms) == 1 else _Both(*comms)

        def landed(kernel, moved):
            for (layer, keys), arr in zip(plan[kernel], moved or ()):
                slabs[layer].update({k: (arr, W_ROWS[keys[0]][0]) for k in keys})

        s['proj'], moved = _norm_matmul(h, row('mix_norm_pre', l), slabs[l]['w_in'], ['w_in'], F32, f"in_proj_{l}",
                                        comm=hosted('in_proj'))
        landed('in_proj', moved)
        s['o'], s['states'], moved = _hgrn_fwd(s['proj'], w['hgrn_lower_bounds'], l, f"hgrn_fwd_{l}",
                                               comm=hosted('hgrn_fwd'))
        landed('hgrn_fwd', moved)
        h, s['r'], s['a'], s['p'], moved = _mix_merge(
            s['proj'], s['o'], h, row('hgrn_out_norm', l), w['pool_proj'][l], row('pool_scale', l), slabs[l]['wb'],
            row('mix_norm_post', l), f"mix_merge_{l}", comm=hosted('mix_merge'))
        landed('mix_merge', moved)
        s['h_mid'] = h
        s['gu'], moved = _norm_matmul(h, row('ffn_norm_pre', l), slabs[l]['wg'], ['wg', 'wu'], BF, f"ffn_proj_{l}",
                                      comm=hosted('ffn_proj'))
        landed('ffn_proj', moved)
        h, s['y'], s['gl'], s['gg'] = _ffn_down(s['gu'], h, full['ffn_conv_w'][l], row('ffn_conv_b', l),
                                                slabs[l]['wd'], row('ffn_norm_post', l), f"ffn_down_{l}")
        saved.append(s)

    dh, loss_cols = _loss_grad(h, jnp.pad(target, ((N_META, 0), (0, 0))), "loss_grad")
    loss = jnp.sum(loss_cols)

    small = {n: [None] * DEPTH for n in REPLICATED + ['ffn_conv_w']}
    exchanged = [{} for _ in range(DEPTH)]
    pending = None
    for l in reversed(range(DEPTH)):
        s = saved[l]
        g = {}
        (dh, dy, act, dgu, small['ffn_norm_post'][l], small['ffn_conv_w'][l], small['ffn_conv_b'][l],
         small['ffn_norm_pre'][l], done) = _ffn_bwd(
            dh, s['y'], row('ffn_norm_post', l), s['gu'], s['gl'], s['gg'], full['ffn_conv_w'][l], slabs[l]['wd'],
            slabs[l]['wg'], s['h_mid'], row('ffn_norm_pre', l), f"ffn_bwd_{l}", comm=pending)
        if pending is not None:
            exchanged[l + 1]['in'] = done
        g['d_down'], _ = _dw([act], [dy], f"dw_down_{l}")
        g['d_gu'], exchanged[l]['wd'] = _dw([dgu], [s['h_mid']], f"dw_gate_up_{l}", gain=row('ffn_norm_pre', l),
                                            comm=_GradExchange(EXCHANGE_GROUPS['wd'], g))

        dr, z, dya, dyp, da, dp, dgab, small['mix_norm_post'][l], exchanged[l]['wg'] = _mix_bwd_a(
            dh, s['r'], row('mix_norm_post', l), s['proj'], s['a'], s['p'], slabs[l]['wb'], f"mix_bwd_a_{l}",
            comm=_GradExchange(EXCHANGE_GROUPS['wg'], g))
        g['d_out'], _ = _dw([z], [dr], f"dw_out_{l}")
        g['d_b'], _ = _dw([dya, dyp], [s['a'], s['p']], f"dw_branch_{l}")
        do, dgv, small['hgrn_out_norm'][l], small['pool_proj'][l], small['pool_scale'][l] = _mix_bwd_b(
            da, dp, s['proj'], s['o'], row('hgrn_out_norm', l), w['pool_proj'][l], row('pool_scale', l),
            f"mix_bwd_b_{l}")
        dqfi, small['hgrn_lower_bounds'][l], exchanged[l]['rest'] = _hgrn_bwd(
            s['proj'], w['hgrn_lower_bounds'], l, s['states'], do, f"hgrn_bwd_{l}",
            comm=_GradExchange(EXCHANGE_GROUPS['rest'], g))
        g['d_in'], _ = _dw([dqfi, dgv, dgab], [s['h_in']], f"dw_in_{l}", gain=row('mix_norm_pre', l))
        pending = _GradExchange(EXCHANGE_GROUPS['in'], g)
        if l == 0:
            dh, small['mix_norm_pre'][l], exchanged[l]['in'] = _in_bwd(
                dqfi, dgv, dgab, slabs[l]['w_in'], s['h_in'], row('mix_norm_pre', l), dh, f"in_bwd_{l}", comm=pending)
        else:
            dh, small['mix_norm_pre'][l], _ = _in_bwd(
                dqfi, dgv, dgab, slabs[l]['w_in'], s['h_in'], row('mix_norm_pre', l), dh, f"in_bwd_{l}")

    grad_x = dh[N_META:]

    per_layer = []
    for l in range(DEPTH):
        sums = {}
        for part, keys in EXCHANGE_GROUPS.items():
            recv, own = exchanged[l][part]
            total = _sum_parts(own, recv, f"sum_grads_{part}_{l}")
            off = 0
            for k in keys:
                sums[k] = total[off:off + W_ROWS[k][1]]
                off += W_ROWS[k][1]
        per_layer.append(_slab_grads(sums))
    grads = {n: jnp.stack([per_layer[l][n] for l in range(DEPTH)]) for n in per_layer[0]}

    stack = lambda n: jnp.stack(small[n]) if small[n][0].shape[0] != 1 else jnp.concatenate(small[n], axis=0)
    names = list(small) + ['meta_tokens']
    partial = {n: stack(n) for n in small}
    partial['meta_tokens'] = dh[:N_META]
    gathered = _all_gather(_pack([partial[n] for n in names], F32, 8), "gather_small_grads")
    rep = dict(zip(names, _unpack(_sum8(gathered, "sum_small_grads"), [partial[n].shape for n in names])))
    rep['hgrn_lower_bounds'] = _lower_bound_grad(w['hgrn_lower_bounds'], rep['hgrn_lower_bounds'], "lower_bound_grad")
    me = 4 * lax.axis_index("x") + 2 * lax.axis_index("y") + lax.axis_index("c")
    for n in F32_GATHERED:
        width = w[n].shape[SHARD_AXIS[n]]
        rep[n] = lax.dynamic_slice_in_dim(rep[n], me * width, width, axis=SHARD_AXIS[n])
    grads.update(rep)

    delta, new_m, new_v = {}, {}, {}
    for n in WEIGHT_NAMES:
        shape = w[n].shape
        d2, m2, v2 = _adamw(_as2d(w[n]), _as2d(grads[n]), _as2d(m[n]), _as2d(v[n]), f"adamw_{n}")
        delta[n], new_m[n], new_v[n] = d2.reshape(shape), m2.reshape(shape), v2.reshape(shape)
    return loss, grad_x, grads, delta, new_m, new_v


def kernel(x, meta_tokens, mix_norm_pre, mix_norm_post, w_in, hgrn_lower_bounds, hgrn_out_norm, w_branch_hgrn, pool_proj, pool_scale, w_branch_pool, w_out, ffn_norm_pre, ffn_norm_post, ffn_w_gate, ffn_w_up, ffn_conv_w, ffn_conv_b, ffn_w_down, loss_target, m_meta_tokens, m_mix_norm_pre, m_mix_norm_post, m_w_in, m_hgrn_lower_bounds, m_hgrn_out_norm, m_w_branch_hgrn, m_pool_proj, m_pool_scale, m_w_branch_pool, m_w_out, m_ffn_norm_pre, m_ffn_norm_post, m_ffn_w_gate, m_ffn_w_up, m_ffn_conv_w, m_ffn_conv_b, m_ffn_w_down, v_meta_tokens, v_mix_norm_pre, v_mix_norm_post, v_w_in, v_hgrn_lower_bounds, v_hgrn_out_norm, v_w_branch_hgrn, v_pool_proj, v_pool_scale, v_w_branch_pool, v_w_out, v_ffn_norm_pre, v_ffn_norm_post, v_ffn_w_gate, v_ffn_w_up, v_ffn_conv_w, v_ffn_conv_b, v_ffn_w_down):
    w = dict(zip(WEIGHT_NAMES, (meta_tokens, mix_norm_pre, mix_norm_post, w_in, hgrn_lower_bounds, hgrn_out_norm,
                                w_branch_hgrn, pool_proj, pool_scale, w_branch_pool, w_out, ffn_norm_pre,
                                ffn_norm_post, ffn_w_gate, ffn_w_up, ffn_conv_w, ffn_conv_b, ffn_w_down)))
    m = dict(zip(WEIGHT_NAMES, (m_meta_tokens, m_mix_norm_pre, m_mix_norm_post, m_w_in, m_hgrn_lower_bounds,
                                m_hgrn_out_norm, m_w_branch_hgrn, m_pool_proj, m_pool_scale, m_w_branch_pool, m_w_out,
                                m_ffn_norm_pre, m_ffn_norm_post, m_ffn_w_gate, m_ffn_w_up, m_ffn_conv_w,
                                m_ffn_conv_b, m_ffn_w_down)))
    v = dict(zip(WEIGHT_NAMES, (v_meta_tokens, v_mix_norm_pre, v_mix_norm_post, v_w_in, v_hgrn_lower_bounds,
                                v_hgrn_out_norm, v_w_branch_hgrn, v_pool_proj, v_pool_scale, v_w_branch_pool, v_w_out,
                                v_ffn_norm_pre, v_ffn_norm_post, v_ffn_w_gate, v_ffn_w_up, v_ffn_conv_w,
                                v_ffn_conv_b, v_ffn_w_down)))
    loss_local, grad_x, grads, delta, new_m, new_v = _train_step(x[0], loss_target[0], w, m, v)
    loss = lax.psum(loss_local, ("x", "y", "c"))
    return (loss, grad_x[None], *[grads[n] for n in WEIGHT_NAMES], *[delta[n] for n in WEIGHT_NAMES],
            *[new_m[n] for n in WEIGHT_NAMES], *[new_v[n] for n in WEIGHT_NAMES])
```

```python
import jax
import jax.numpy as jnp
from jax import lax
from jax.experimental import pallas as pl
from jax.experimental.pallas import tpu as pltpu

F32 = jnp.float32
BF = jnp.bfloat16

D_MODEL = 1024
N_META = 16
DEPTH = 2
HEADS = 4
HEAD_DIM = 128
HGRN_W = 512
POOL_W = 512
POOL_WINDOWS = (2, 4, 8, 16)
D_FF = 2816
IN_COLS = 4608
EPS = 1e-6
LOG_FLOOR = 1e-30
N_DEV = 8

ADAM_LR = 0.001
ADAM_B1 = 0.9
ADAM_B2 = 0.999
ADAM_EPS = 1e-08
ADAM_WD = 0.01
ADAM_STEP = 10

SUB = 16
CHUNK = 48
HGRN_UNROLL = 9
EXP_CLAMP = 80.0
ROW_TILE = 432
FFN_TILE = 144
HALO = 16
ADVANCE_STEPS = 3
VMEM_LIMIT = 56 * 1024 * 1024
DW_BUFFER_BYTES = 44 * 1024 * 1024
MESH = pl.DeviceIdType.MESH

WEIGHT_NAMES = ['meta_tokens', 'mix_norm_pre', 'mix_norm_post', 'w_in', 'hgrn_lower_bounds', 'hgrn_out_norm',
                'w_branch_hgrn', 'pool_proj', 'pool_scale', 'w_branch_pool', 'w_out', 'ffn_norm_pre', 'ffn_norm_post',
                'ffn_w_gate', 'ffn_w_up', 'ffn_conv_w', 'ffn_conv_b', 'ffn_w_down']
SHARD_AXIS = {'meta_tokens': 1, 'w_in': 2, 'w_branch_hgrn': 2, 'w_branch_pool': 2, 'w_out': 1,
              'ffn_w_gate': 2, 'ffn_w_up': 2, 'ffn_conv_w': 2, 'ffn_w_down': 1}
F32_GATHERED = ['meta_tokens', 'ffn_conv_w']
REPLICATED = [n for n in WEIGHT_NAMES if n not in SHARD_AXIS]

W_ROWS = {'w_in': (0, 576),
          'wb': (576, 128),
          'w_out': (704, 128),
          'wg': (832, 352),
          'wu': (1184, 352),
          'wd': (1536, 352)}
SLAB_ROWS = 1888
EXCHANGE_GROUPS = {'wd': ['wd'], 'wg': ['wg'], 'rest': ['wu', 'wb', 'w_out'], 'in': ['w_in']}


def _params(sem=None):
    return pltpu.CompilerParams(dimension_semantics=sem, vmem_limit_bytes=VMEM_LIMIT)


def _tile(total, pref, mult=16):
    best = None
    for t in range(mult, min(total, pref) + 1, mult):
        if total % t == 0:
            best = t
    assert best is not None, (total, pref, mult)
    return best


def _whole(shape):
    return pl.BlockSpec(shape, lambda *_: (0,) * len(shape))


def _cols(tm, width, j):
    return pl.BlockSpec((tm, width), lambda i, j=j: (i, j))


def _dot(a, b):
    return jnp.dot(a, b, preferred_element_type=F32)


def _dot_nt(a, b):
    return lax.dot_general(a, b, (((1,), (1,)), ((), ())), preferred_element_type=F32)


def _dot_tn(a, b):
    return lax.dot_general(a, b, (((0,), (0,)), ((), ())), preferred_element_type=F32)


def _rms_fwd(x, g):
    r = lax.rsqrt(jnp.mean(x * x, axis=-1, keepdims=True) + EPS)
    return x * r * g


def _rms_bwd(dy, x, g):
    r = lax.rsqrt(jnp.mean(x * x, axis=-1, keepdims=True) + EPS)
    xh = x * r
    dyg = dy * g
    dx = r * (dyg - xh * jnp.mean(dyg * xh, axis=-1, keepdims=True))
    return dx, jnp.sum(dy * xh, axis=0, keepdims=True)


_GELU_C = 0.7978845608028654
_GELU_A = 0.044715


def _gelu_and_grad(x):
    x2 = x * x
    t = jnp.tanh(x * (_GELU_C + (_GELU_C * _GELU_A) * x2))
    u = 1.0 + t
    hx = 0.5 * x
    return hx * u, 0.5 * u + (hx * (1.0 - t * t)) * (_GELU_C + (3.0 * _GELU_C * _GELU_A) * x2)


def _split3(x):
    x1 = x.astype(BF)
    r1 = x - x1.astype(F32)
    x2 = r1.astype(BF)
    x3 = (r1 - x2.astype(F32)).astype(BF)
    return x1, x2, x3


def _tri_mm(tri, x):
    x1, x2, x3 = _split3(x)
    return _dot(tri, x1) + _dot(tri, x2) + _dot(tri, x3)


def _softmax2(lb_ref):
    l0 = lb_ref[0:1, :]
    l1 = lb_ref[1:2, :]
    m = jnp.maximum(l0, l1)
    e0 = jnp.exp(l0 - m)
    e1 = jnp.exp(l1 - m)
    return e0 / (e0 + e1), e1 / (e0 + e1)


def _layer_lower_bound(lb_ref, layer):
    g0, g1 = _softmax2(lb_ref)
    if layer == 0:
        return jnp.clip(g0 - g0, 0.0, 1.0)
    return jnp.clip((g0 + g1) - g0, 0.0, 1.0)


def _all_gather(x, name):
    n, w = x.shape

    def body(x_ref, out_ref, send_sems, recv_sems, local_sem):
        mx, my, mc = lax.axis_index("x"), lax.axis_index("y"), lax.axis_index("c")
        me, sibling = (mx, my, mc), (mx, my, 1 - mc)
        chips = [(1 - mx, my), (mx, 1 - my), (1 - mx, 1 - my)]

        def rows(px, py, pc):
            return out_ref.at[4 * px + 2 * py + pc]

        def copy(k, block, to, src=None):
            return pltpu.make_async_remote_copy(
                src_ref=rows(*block) if src is None else src, dst_ref=rows(*block),
                send_sem=send_sems.at[k], recv_sem=recv_sems.at[k], device_id=to, device_id_type=MESH)

        mine = pltpu.make_async_copy(x_ref, rows(*me), local_sem)
        mine.start()
        first = [copy(0, me, sibling, src=x_ref)]
        first += [copy(1 + j, me, (*chip, mc), src=x_ref) for j, chip in enumerate(chips)]
        for cp in first:
            cp.start()
        passed = [copy(4 + j, (*chip, mc), sibling) for j, chip in enumerate(chips)]
        for j, chip in enumerate(chips):
            copy(1 + j, (*chip, mc), me).wait_recv()
            passed[j].start()
        copy(0, sibling, me).wait_recv()
        for j, chip in enumerate(chips):
            copy(4 + j, (*chip, 1 - mc), me).wait_recv()
        for cp in first + passed:
            cp.wait_send()
        mine.wait()

    return pl.pallas_call(
        body, name=name,
        out_shape=jax.ShapeDtypeStruct((N_DEV, n, w), x.dtype),
        in_specs=[pl.BlockSpec(memory_space=pl.ANY)],
        out_specs=pl.BlockSpec(memory_space=pl.ANY),
        scratch_shapes=[pltpu.SemaphoreType.DMA((7,)), pltpu.SemaphoreType.DMA((7,)), pltpu.SemaphoreType.DMA],
    )(x)


GRAD_SRC = {'w_in': ('d_in', 0), 'wb': ('d_b', 0), 'w_out': ('d_out', 0), 'wg': ('d_gu', 0), 'wu': ('d_gu', D_FF),
            'wd': ('d_down', 0)}


class _GradExchange:
    def __init__(self, keys, grads):
        self.names = sorted({GRAD_SRC[k][0] for k in keys})
        self.arrays = [grads[n][0] for n in self.names] + [grads[n][1] for n in self.names]
        self.pieces, off = [], 0
        for k in keys:
            name, base = GRAD_SRC[k]
            rows = W_ROWS[k][1]
            self.pieces.append((self.names.index(name), base, rows, off))
            off += rows
        self.keys, self.rows = keys, off
        self.out_shape = (jax.ShapeDtypeStruct((N_DEV - 1, off, D_MODEL), BF), jax.ShapeDtypeStruct((off, D_MODEL), F32))
        self.scratch = [pltpu.SemaphoreType.DMA((N_DEV - 1,)), pltpu.SemaphoreType.DMA((N_DEV - 1,)),
                        pltpu.SemaphoreType.DMA((len(self.pieces),))]

    def _local(self, ins, outs, scr):
        mx, my, mc = lax.axis_index("x"), lax.axis_index("y"), lax.axis_index("c")
        me = 4 * mx + 2 * my + mc
        return [pltpu.make_async_copy(ins[ai].at[pl.ds(pl.multiple_of(base + rows * me, 8), rows)],
                                      outs[1].at[pl.ds(off, rows)], scr[2].at[k])
                for k, (ai, base, rows, off) in enumerate(self.pieces)]

    def start(self, ins, outs, scr):
        mx, my, mc = lax.axis_index("x"), lax.axis_index("y"), lax.axis_index("c")
        bf16 = ins[len(self.names):]
        for cp in self._local(ins, outs, scr):
            cp.start()
        for d in range(1, N_DEV):
            px = (1 - mx) if (d >> 2) & 1 else mx
            py = (1 - my) if (d >> 1) & 1 else my
            pc = (1 - mc) if d & 1 else mc
            peer = 4 * px + 2 * py + pc
            for ai, base, rows, off in self.pieces:
                pltpu.make_async_remote_copy(
                    src_ref=bf16[ai].at[pl.ds(pl.multiple_of(base + rows * peer, 16), rows)],
                    dst_ref=outs[0].at[d - 1, pl.ds(off, rows)], send_sem=scr[0].at[d - 1], recv_sem=scr[1].at[d - 1],
                    device_id=(px, py, pc), device_id_type=MESH).start()

    def advance(self, ins, outs, scr):
        pass

    def wait(self, ins, outs, scr):
        me = (lax.axis_index("x"), lax.axis_index("y"), lax.axis_index("c"))
        for d in range(1, N_DEV):
            slot = pltpu.make_async_remote_copy(
                src_ref=outs[0].at[d - 1], dst_ref=outs[0].at[d - 1], send_sem=scr[0].at[d - 1],
                recv_sem=scr[1].at[d - 1], device_id=me, device_id_type=MESH)
            slot.wait_recv()
            slot.wait_send()
        for cp in self._local(ins, outs, scr):
            cp.wait()


class _SlabGather:
    def __init__(self, slab, row0, rows):
        self.arrays = [slab]
        self.row0, self.rows = row0, rows
        self.out_shape = (jax.ShapeDtypeStruct((N_DEV, rows, D_MODEL), slab.dtype),)
        self.scratch = [pltpu.SemaphoreType.DMA((7,)), pltpu.SemaphoreType.DMA((7,)), pltpu.SemaphoreType.DMA]

    def _parts(self, ins, outs, scr):
        mx, my, mc = lax.axis_index("x"), lax.axis_index("y"), lax.axis_index("c")
        me, sibling = (mx, my, mc), (mx, my, 1 - mc)
        chips = [(1 - mx, my), (mx, 1 - my), (1 - mx, 1 - my)]
        mine_ref = ins[0].at[pl.ds(self.row0, self.rows)]
        block = lambda px, py, pc: outs[0].at[4 * px + 2 * py + pc]

        def copy(k, blk, to, src=None):
            return pltpu.make_async_remote_copy(
                src_ref=block(*blk) if src is None else src, dst_ref=block(*blk),
                send_sem=scr[0].at[k], recv_sem=scr[1].at[k], device_id=to, device_id_type=MESH)

        local = pltpu.make_async_copy(mine_ref, block(*me), scr[2])
        first = [copy(0, me, sibling, src=mine_ref)]
        first += [copy(1 + j, me, (*chip, mc), src=mine_ref) for j, chip in enumerate(chips)]
        passed = [copy(4 + j, (*chip, mc), sibling) for j, chip in enumerate(chips)]
        landed = [copy(1 + j, (*chip, mc), me) for j, chip in enumerate(chips)]
        from_sibling = [copy(0, sibling, me)] + [copy(4 + j, (*chip, 1 - mc), me) for j, chip in enumerate(chips)]
        return local, first, passed, landed, from_sibling

    def start(self, ins, outs, scr):
        local, first, _, _, _ = self._parts(ins, outs, scr)
        local.start()
        for cp in first:
            cp.start()

    def advance(self, ins, outs, scr):
        _, _, passed, landed, _ = self._parts(ins, outs, scr)
        for arrived, forward in zip(landed, passed):
            arrived.wait_recv()
            forward.start()

    def wait(self, ins, outs, scr):
        local, first, passed, _, from_sibling = self._parts(ins, outs, scr)
        for cp in from_sibling:
            cp.wait_recv()
        for cp in first + passed:
            cp.wait_send()
        local.wait()


class _Both:
    def __init__(self, first, second):
        self.parts = (first, second)
        self.arrays = list(first.arrays) + list(second.arrays)
        self.out_shape = tuple(first.out_shape) + tuple(second.out_shape)
        self.scratch = list(first.scratch) + list(second.scratch)

    def _split(self, ins, outs, scr):
        a, b = self.parts
        na, oa, sa = len(a.arrays), len(a.out_shape), len(a.scratch)
        return (a, ins[:na], outs[:oa], scr[:sa]), (b, ins[na:], outs[oa:], scr[sa:])

    def start(self, ins, outs, scr):
        for part, i, o, s in self._split(ins, outs, scr):
            part.start(i, o, s)

    def advance(self, ins, outs, scr):
        for part, i, o, s in self._split(ins, outs, scr):
            part.advance(i, o, s)

    def wait(self, ins, outs, scr):
        for part, i, o, s in self._split(ins, outs, scr):
            part.wait(i, o, s)


def _hosted_call(body, comm, name, grid, in_specs, args, out_shape, out_specs, scratch_shapes):
    if comm is None:
        return pl.pallas_call(body, name=name, grid=grid, in_specs=in_specs, out_specs=out_specs, out_shape=out_shape,
                              scratch_shapes=scratch_shapes, compiler_params=_params(("arbitrary",)))(*args), None
    n_in, n_out, n_scr, n_c = len(in_specs), len(out_shape), len(scratch_shapes), len(comm.arrays)
    n_co = len(comm.out_shape)
    steps = grid[0]

    def hosted(*refs):
        ins, cins = refs[:n_in], refs[n_in:n_in + n_c]
        outs = refs[n_in + n_c:n_in + n_c + n_out]
        couts = refs[n_in + n_c + n_out:n_in + n_c + n_out + n_co]
        scr = refs[n_in + n_c + n_out + n_co:n_in + n_c + n_out + n_co + n_scr]
        cscr = refs[n_in + n_c + n_out + n_co + n_scr:]

        @pl.when(pl.program_id(0) == 0)
        def _():
            comm.start(cins, couts, cscr)

        body(*ins, *outs, *scr)

        @pl.when(pl.program_id(0) == max(steps - 1 - ADVANCE_STEPS, 0))
        def _():
            comm.advance(cins, couts, cscr)

        @pl.when(pl.program_id(0) == steps - 1)
        def _():
            comm.wait(cins, couts, cscr)

    res = pl.pallas_call(
        hosted, name=name, grid=grid, in_specs=list(in_specs) + [HBM_SPEC] * n_c,
        out_specs=tuple(out_specs) + (HBM_SPEC,) * n_co, out_shape=tuple(out_shape) + tuple(comm.out_shape),
        scratch_shapes=list(scratch_shapes) + comm.scratch, compiler_params=_params(("arbitrary",)),
    )(*args, *comm.arrays)
    return res[:n_out], res[n_out:]


def _sum_parts(own, recv, name):
    n, w = own.shape
    tn = _tile(n, 256, 16)

    def body(own_ref, p_ref, o_ref):
        acc = own_ref[...]
        for d in range(N_DEV - 1):
            acc = acc + p_ref[d].astype(F32)
        o_ref[...] = acc

    return pl.pallas_call(
        body, name=name, out_shape=jax.ShapeDtypeStruct((n, w), F32), grid=(n // tn,),
        in_specs=[pl.BlockSpec((tn, w), lambda i: (i, 0)), pl.BlockSpec((N_DEV - 1, tn, w), lambda i: (0, i, 0))],
        out_specs=pl.BlockSpec((tn, w), lambda i: (i, 0)),
        compiler_params=_params(("parallel",)),
    )(own, recv)


def _sum8(parts, name):
    _, n, w = parts.shape
    tn = _tile(n, 256 if w > 128 else 1024, 8)

    def body(p_ref, o_ref):
        acc = p_ref[0]
        for d in range(1, N_DEV):
            acc = acc + p_ref[d]
        o_ref[...] = acc

    return pl.pallas_call(
        body, name=name, out_shape=jax.ShapeDtypeStruct((n, w), F32), grid=(n // tn,),
        in_specs=[pl.BlockSpec((N_DEV, tn, w), lambda i: (0, i, 0))],
        out_specs=pl.BlockSpec((tn, w), lambda i: (i, 0)),
        compiler_params=_params(("parallel",)),
    )(parts)


def _pack(arrays, dtype, row_mult):
    flat = jnp.concatenate([a.astype(dtype).reshape(-1) for a in arrays])
    pad = (-flat.shape[0]) % (128 * row_mult)
    if pad:
        flat = jnp.concatenate([flat, jnp.zeros((pad,), dtype)])
    return flat.reshape(-1, 128)


def _unpack(flat2d, shapes):
    lead = flat2d.shape[:-2]
    flat = flat2d.reshape(lead + (-1,))
    out, off = [], 0
    for s in shapes:
        size = 1
        for d in s:
            size *= d
        out.append(flat[..., off:off + size].reshape(lead + tuple(s)))
        off += size
    return out


def _blocks_to_full(blocks, axis):
    moved = jnp.moveaxis(blocks, 0, axis)
    shape = list(moved.shape)
    shape[axis:axis + 2] = [shape[axis] * shape[axis + 1]]
    return moved.reshape(shape)


def _weight_scratch(keys):
    return ([pltpu.VMEM((N_DEV * W_ROWS[k][1], D_MODEL), BF) for k in keys]
            + [pltpu.SemaphoreType.DMA((N_DEV * len(keys),))])


def _fetch_weights(slabs_ref, keys, bufs, sems, base=0):
    @pl.when(pl.program_id(0) == 0)
    def _():
        copies = []
        for ki, (key, buf) in enumerate(zip(keys, bufs)):
            row0, rows = W_ROWS[key][0] - base, W_ROWS[key][1]
            for j in range(N_DEV):
                copies.append(pltpu.make_async_copy(slabs_ref.at[j, pl.ds(row0, rows)],
                                                    buf.at[pl.ds(j * rows, rows)], sems.at[ki * N_DEV + j]))
        for cp in copies:
            cp.start()
        for cp in copies:
            cp.wait()


HBM_SPEC = pl.BlockSpec(memory_space=pl.ANY)


def _norm_matmul(h, gain, slabs, keys, out_dtype, name, comm=None):
    t_rows, d = h.shape
    widths = [N_DEV * W_ROWS[k][1] for k in keys]
    tm = _tile(t_rows, ROW_TILE)

    def body(h_ref, g_ref, slabs_ref, o_ref, *scratch):
        bufs, sems = scratch[:-1], scratch[-1]
        _fetch_weights(slabs_ref, keys, bufs, sems, slabs[1])
        u = _rms_fwd(h_ref[...], g_ref[...]).astype(BF)
        off = 0
        for buf, n in zip(bufs, widths):
            o_ref[:, off:off + n] = _dot_nt(u, buf[...]).astype(out_dtype)
            off += n

    outs, moved = _hosted_call(
        body, comm, name, (t_rows // tm,),
        [pl.BlockSpec((tm, d), lambda i: (i, 0)), _whole((1, d)), HBM_SPEC], (h, gain, slabs[0]),
        (jax.ShapeDtypeStruct((t_rows, sum(widths)), out_dtype),),
        (pl.BlockSpec((tm, sum(widths)), lambda i: (i, 0)),), _weight_scratch(keys))
    return outs[0], moved


def _chunk_masks():
    row = lax.broadcasted_iota(jnp.int32, (CHUNK, CHUNK), 0)
    col = lax.broadcasted_iota(jnp.int32, (CHUNK, CHUNK), 1)
    tri_lo = (col <= row).astype(BF)
    tri_up = (col >= row).astype(BF)
    rb = jnp.right_shift(row, 4)
    cb = jnp.right_shift(col, 4)
    diag = (rb == cb) & (col <= row)
    off = [(rb == i) & (col < SUB * i) for i in range(1, CHUNK // SUB)]
    return tri_lo, tri_up, diag, off


def _chunk_gates(fl, lb):
    sg = jax.nn.sigmoid(fl)
    s2 = jax.nn.sigmoid(-fl)
    f = lb + (1.0 - lb) * sg
    lf = jnp.log(jnp.maximum(f, LOG_FLOOR))
    k = (1.0 - lb) * s2
    return sg, s2, f, lf, k


def _chunk_factors(q, k, b):
    nb = CHUNK // SUB
    rq = jnp.concatenate([jnp.broadcast_to(b[SUB * i:SUB * i + 1], (SUB, HEAD_DIM)) for i in range(nb)], axis=0)
    mq = jnp.concatenate([jnp.broadcast_to(b[SUB * i + SUB // 2:SUB * i + SUB // 2 + 1], (SUB, HEAD_DIM))
                          for i in range(nb)], axis=0)
    e_qs = jnp.exp(b - rq)
    e_qm = jnp.exp(jnp.minimum(b - mq, EXP_CLAMP))
    e_km = jnp.exp(jnp.minimum(mq - b, EXP_CLAMP))
    e_ks = [jnp.exp(jnp.minimum(b[SUB * i:SUB * i + 1] - b, 0.0)) for i in range(1, nb)]
    qs = (q * e_qs).astype(BF)
    qm = (q * e_qm).astype(BF)
    km = (k * e_km).astype(BF)
    ks = [(k * e).astype(BF) for e in e_ks]
    return e_qs, e_qm, e_km, e_ks, qs, qm, km, ks


def _chunk_scores(qs, qm, km, ks, diag, off):
    a = jnp.where(diag, _dot_nt(qm, km), 0.0)
    for m, kk in zip(off, ks):
        a = jnp.where(m, _dot_nt(qs, kk), a)
    return a


def _hgrn_fwd(proj, lower_bounds, layer, name, comm=None):
    t_rows = proj.shape[0]
    tm = _tile(t_rows, ROW_TILE, CHUNK)
    nct = tm // CHUNK

    def body(q_ref, f_ref, i_ref, lb_ref, o_ref, sall_ref, st_ref):
        @pl.when(pl.program_id(0) == 0)
        def _():
            st_ref[...] = jnp.zeros_like(st_ref)

        lbs = _layer_lower_bound(lb_ref, layer)
        tri_lo, _, diag, off = _chunk_masks()

        group = HGRN_UNROLL if nct % HGRN_UNROLL == 0 else 1

        def chunk_group(gi, carry):
            pairs = [(u, hh) for u in range(group) for hh in range(HEADS)]
            rows = [pl.ds(pl.multiple_of((gi * group + u) * CHUNK, SUB), CHUNK) for u in range(group)]
            cols = [slice(hh * HEAD_DIM, (hh + 1) * HEAD_DIM) for hh in range(HEADS)]
            q = {p: q_ref[rows[p[0]], cols[p[1]]] for p in pairs}
            vb = {p: i_ref[rows[p[0]], cols[p[1]]].astype(BF) for p in pairs}
            gates = {p: _chunk_gates(f_ref[rows[p[0]], cols[p[1]]], lbs[:, cols[p[1]]]) for p in pairs}
            b = {p: _tri_mm(tri_lo, gates[p][3]) for p in pairs}
            fac = {p: _chunk_factors(q[p], gates[p][4], b[p]) for p in pairs}
            a = {p: _chunk_scores(*fac[p][4:], diag, off) for p in pairs}
            o_intra = {p: _dot(a[p].astype(BF), vb[p]) for p in pairs}
            upd = {p: _dot_tn(vb[p], (gates[p][4] * jnp.exp(b[p][CHUNK - 1:CHUNK] - b[p])).astype(BF)) for p in pairs}
            qc = {p: (q[p] * jnp.exp(b[p])).astype(BF) for p in pairs}
            for hh in range(HEADS):
                st = st_ref[hh]
                for u in range(group):
                    p = (u, hh)
                    sall_ref[gi * group + u, hh] = st
                    o_ref[rows[u], cols[hh]] = _dot_nt(qc[p], st.astype(BF)) + o_intra[p]
                    st = st * jnp.exp(b[p][CHUNK - 1:CHUNK]) + upd[p]
                st_ref[hh] = st
            return carry

        lax.fori_loop(0, nct // group, chunk_group, 0)

    outs, moved = _hosted_call(
        body, comm, name, (t_rows // tm,),
        [_cols(tm, 512, 0), _cols(tm, 512, 1), _cols(tm, 512, 2), _whole((DEPTH, HGRN_W))],
        (proj, proj, proj, lower_bounds),
        (jax.ShapeDtypeStruct((t_rows, HGRN_W), F32),
         jax.ShapeDtypeStruct((t_rows // CHUNK, HEADS, HEAD_DIM, HEAD_DIM), F32)),
        (pl.BlockSpec((tm, HGRN_W), lambda i: (i, 0)),
         pl.BlockSpec((nct, HEADS, HEAD_DIM, HEAD_DIM), lambda i: (i, 0, 0, 0))),
        [pltpu.VMEM((HEADS, HEAD_DIM, HEAD_DIM), F32)])
    return (*outs, moved)


def _head_norm(o):
    outs, rs = [], []
    for hh in range(HEADS):
        oh = o[:, hh * HEAD_DIM:(hh + 1) * HEAD_DIM]
        r = lax.rsqrt(jnp.mean(oh * oh, axis=-1, keepdims=True) + EPS)
        outs.append(oh * r)
        rs.append(r)
    return outs, rs


def _window_counts(row0, rows):
    t1 = (row0 + lax.broadcasted_iota(jnp.int32, (rows, 1), 0) + 1).astype(F32)
    return [1.0 / jnp.minimum(t1, float(w)) for w in POOL_WINDOWS]


def _pool_fwd(v, halo, row0):
    rows = v.shape[0]
    inv = _window_counts(row0, rows)
    outs = []
    for gi, w in enumerate(POOL_WINDOWS):
        cs = slice(gi * HEAD_DIM, (gi + 1) * HEAD_DIM)
        s = jnp.concatenate([halo[:, cs], v[:, cs]], axis=0)
        step = 1
        while step < w:
            s = s + pltpu.roll(s, step, 0)
            step *= 2
        outs.append(s[HALO:] * inv[gi] - v[:, cs])
    return outs


def _pool_bwd(dpooled, halo, row0):
    rows = dpooled[0].shape[0]
    inv = _window_counts(row0, rows)
    inv_h = _window_counts(row0 + rows, HALO)
    outs = []
    for gi, w in enumerate(POOL_WINDOWS):
        s = jnp.concatenate([dpooled[gi] * inv[gi], halo[gi] * inv_h[gi]], axis=0)
        n_ext = s.shape[0]
        step = 1
        while step < w:
            s = s + pltpu.roll(s, n_ext - step, 0)
            step *= 2
        outs.append(s[:rows] - dpooled[gi])
    return outs


def _pool_project(pooled, pp_ref, scale):
    y = jnp.concatenate([_dot(pooled[gi].astype(BF), pp_ref[gi].astype(BF)) for gi in range(4)], axis=1)
    return y, y * scale


def _mix_merge(proj, o, h, out_gain, pool_proj, pool_scale, slabs, gain_post, name, comm=None):
    t_rows = h.shape[0]
    tm = _tile(t_rows, ROW_TILE)
    hpt = tm // HALO
    keys = ['wb', 'w_out']

    def body(g_ref, v_ref, vh_ref, ga0_ref, ga1_ref, gb0_ref, gb1_ref, o_ref, h_ref, og_ref, pp_ref, ps_ref,
             slabs_ref, gp_ref, hn_ref, r_ref, a_ref, p_ref, wb_ref, wo_ref, sems):
        _fetch_weights(slabs_ref, keys, (wb_ref, wo_ref), sems, slabs[1])
        i = pl.program_id(0)
        on, _ = _head_norm(o_ref[...])
        a = jnp.concatenate(on, axis=1) * og_ref[...] * jax.nn.sigmoid(g_ref[...])
        halo = jnp.where(i == 0, 0.0, vh_ref[...])
        pooled = _pool_fwd(v_ref[...], halo, i * tm)
        _, p = _pool_project(pooled, pp_ref, ps_ref[...])
        ab, pb = a.astype(BF), p.astype(BF)
        sa = jax.nn.sigmoid(jnp.concatenate([ga0_ref[...], ga1_ref[...]], axis=1))
        sb = jax.nn.sigmoid(jnp.concatenate([gb0_ref[...], gb1_ref[...]], axis=1))
        z = sa * _dot_nt(ab, wb_ref[:, 0:HGRN_W]) + sb * _dot_nt(pb, wb_ref[:, HGRN_W:HGRN_W + POOL_W])
        r = _dot(z.astype(BF), wo_ref[...])
        hn_ref[...] = h_ref[...] + _rms_fwd(r, gp_ref[...])
        r_ref[...] = r
        a_ref[...] = ab
        p_ref[...] = pb

    rows = lambda width: pl.BlockSpec((tm, width), lambda i: (i, 0))
    outs, moved = _hosted_call(
        body, comm, name, (t_rows // tm,),
        in_specs=[_cols(tm, 512, 3), _cols(tm, 512, 4),
                  pl.BlockSpec((HALO, 512), lambda i: (jnp.maximum(i * hpt - 1, 0), 4)),
                  _cols(tm, 512, 5), _cols(tm, 512, 6), _cols(tm, 512, 7), _cols(tm, 512, 8),
                  rows(HGRN_W), rows(D_MODEL), _whole((1, HGRN_W)), _whole((4, HEAD_DIM, HEAD_DIM)),
                  _whole((1, POOL_W)), HBM_SPEC, _whole((1, D_MODEL))],
        args=(proj, proj, proj, proj, proj, proj, proj, o, h, out_gain, pool_proj, pool_scale, slabs[0], gain_post),
        out_shape=(jax.ShapeDtypeStruct((t_rows, D_MODEL), F32), jax.ShapeDtypeStruct((t_rows, D_MODEL), F32),
                   jax.ShapeDtypeStruct((t_rows, HGRN_W), BF), jax.ShapeDtypeStruct((t_rows, POOL_W), BF)),
        out_specs=(rows(D_MODEL), rows(D_MODEL), rows(HGRN_W), rows(POOL_W)),
        scratch_shapes=_weight_scratch(keys))
    return (*outs, moved)


def _conv_fwd(g, halo, cw):
    ext = jnp.concatenate([halo, g], axis=0)
    return (cw[0:1] * pltpu.roll(ext, 2, 0)[8:] + cw[1:2] * pltpu.roll(ext, 1, 0)[8:] + cw[2:3] * g)


def _ffn_down(gu, h, conv_w, conv_b, slabs, gain_post, name):
    t_rows = h.shape[0]
    tm = _tile(t_rows, FFN_TILE)
    hpt = tm // HALO

    def body(g_ref, gh_ref, up_ref, h_ref, cw_ref, cb_ref, slabs_ref, gp_ref, hn_ref, y_ref, gl_ref, gg_ref,
             wd_ref, sems):
        _fetch_weights(slabs_ref, ['wd'], (wd_ref,), sems, slabs[1])
        i = pl.program_id(0)
        halo = jnp.where(i == 0, 0.0, gh_ref[8:HALO, :].astype(F32))
        gt = _conv_fwd(g_ref[...].astype(F32), halo, cw_ref[...]) + cb_ref[...]
        gl, gg = _gelu_and_grad(gt)
        gl_ref[...] = gl.astype(BF)
        gg_ref[...] = gg.astype(BF)
        act = gl * up_ref[...].astype(F32)
        y = _dot(act.astype(BF), wd_ref[...])
        hn_ref[...] = h_ref[...] + _rms_fwd(y, gp_ref[...])
        y_ref[...] = y

    rows = lambda width: pl.BlockSpec((tm, width), lambda i: (i, 0))
    return pl.pallas_call(
        body, name=name,
        out_shape=(jax.ShapeDtypeStruct((t_rows, D_MODEL), F32), jax.ShapeDtypeStruct((t_rows, D_MODEL), F32),
                   jax.ShapeDtypeStruct((t_rows, D_FF), BF), jax.ShapeDtypeStruct((t_rows, D_FF), BF)),
        grid=(t_rows // tm,),
        in_specs=[_cols(tm, D_FF, 0), pl.BlockSpec((HALO, D_FF), lambda i: (jnp.maximum(i * hpt - 1, 0), 0)),
                  _cols(tm, D_FF, 1), rows(D_MODEL), _whole((3, D_FF)), _whole((1, D_FF)),
                  HBM_SPEC, _whole((1, D_MODEL))],
        out_specs=(rows(D_MODEL), rows(D_MODEL), rows(D_FF), rows(D_FF)),
        scratch_shapes=_weight_scratch(['wd']),
        compiler_params=_params(("arbitrary",)),
    )(gu, gu, gu, h, conv_w, conv_b, slabs[0], gain_post)


def _loss_grad(h, target, name):
    t_rows, d = h.shape
    tm = _tile(t_rows, ROW_TILE)

    def body(h_ref, t_ref, dh_ref, l_ref):
        i = pl.program_id(0)

        @pl.when(i == 0)
        def _():
            l_ref[...] = jnp.zeros_like(l_ref)

        row = i * tm + lax.broadcasted_iota(jnp.int32, (tm, 1), 0)
        err = jnp.where(row >= N_META, h_ref[...] - t_ref[...], 0.0)
        dh_ref[...] = err * (1.0 / d)
        l_ref[...] += jnp.sum(err * err, axis=0, keepdims=True) * (0.5 / d)

    rows = pl.BlockSpec((tm, d), lambda i: (i, 0))
    return pl.pallas_call(
        body, name=name,
        out_shape=(jax.ShapeDtypeStruct((t_rows, d), F32), jax.ShapeDtypeStruct((1, d), F32)),
        grid=(t_rows // tm,), in_specs=[rows, rows], out_specs=(rows, _whole((1, d))),
        compiler_params=_params(("arbitrary",)),
    )(h, target)


def _dw(lhs, rhs, name, gain=None, comm=None):
    t_rows = lhs[0].shape[0]
    paired = len(rhs) > 1
    n_rows = lhs[0].shape[1] if paired else sum(x.shape[1] for x in lhs)
    n_cols = sum(x.shape[1] for x in rhs)
    row_bytes = 2 * sum(x.shape[1] * x.dtype.itemsize for x in list(lhs) + list(rhs))
    tm = _tile(t_rows, max(ROW_TILE, (DW_BUFFER_BYTES - 4 * n_rows * n_cols) // row_bytes))
    last = t_rows // tm - 1

    def body(*refs):
        lhs_refs = refs[:len(lhs)]
        rhs_refs = refs[len(lhs):len(lhs) + len(rhs)]
        rest = refs[len(lhs) + len(rhs):]
        if gain is not None:
            g_ref, o_ref, o16_ref, acc, stage = rest
            rv = [_rms_fwd(rhs_refs[0][...], g_ref[...]).astype(BF)]
        else:
            o_ref, o16_ref, acc, stage = rest
            rv = [r[...] for r in rhs_refs]
        i = pl.program_id(0)

        @pl.when(i == 0)
        def _():
            acc[...] = jnp.zeros_like(acc)

        r0, c0 = 0, 0
        for p, l_ref in enumerate(lhs_refs):
            r = rv[p] if paired else rv[0]
            n = l_ref.shape[1]
            step = 512 if n % 512 == 0 else 256
            for s in range(0, n, step):
                acc[r0 + s:r0 + s + step, c0:c0 + r.shape[1]] += _dot_tn(l_ref[:, s:s + step], r)
            if paired:
                c0 += r.shape[1]
            else:
                r0 += n

        @pl.when(i == last)
        def _():
            pltpu.sync_copy(acc, o_ref)
            for s in range(0, n_rows, 256):
                stage[...] = acc[s:s + 256, :].astype(BF)
                pltpu.sync_copy(stage, o16_ref.at[pl.ds(s, 256)])

    rows = lambda x: pl.BlockSpec((tm, x.shape[1]), lambda i: (i, 0))
    in_specs = [rows(x) for x in lhs] + [rows(x) for x in rhs]
    args = list(lhs) + list(rhs)
    if gain is not None:
        in_specs.append(_whole(gain.shape))
        args.append(gain)
    outs, moved = _hosted_call(
        body, comm, name, (t_rows // tm,), in_specs, args,
        (jax.ShapeDtypeStruct((n_rows, n_cols), F32), jax.ShapeDtypeStruct((n_rows, n_cols), BF)),
        (HBM_SPEC, HBM_SPEC), [pltpu.VMEM((n_rows, n_cols), F32), pltpu.VMEM((256, n_cols), BF)])
    return (outs[0], outs[1]), moved


def _ffn_bwd(dh, y, gain_post, gu, gl, gg, conv_w, slabs_down, slabs_up, h, gain_pre, name, comm=None):
    t_rows = dh.shape[0]
    tm = _tile(t_rows, FFN_TILE)
    hpt = tm // HALO
    n_tiles = t_rows // tm

    def body(dh_ref, y_ref, gp_ref, g_ref, gh_ref, up_ref, gl_ref, gg_ref, cw_ref, h_ref, gpre_ref, sd_ref, su_ref,
             dhm_ref, dy_ref, act_ref, dgu_ref, dgp_ref, dcw_ref, dcb_ref, dgpre_ref,
             wd_ref, sems_d, wg_ref, wu_ref, sems_u, carry_ref):
        _fetch_weights(sd_ref, ['wd'], (wd_ref,), sems_d, slabs_down[1])
        _fetch_weights(su_ref, ['wg', 'wu'], (wg_ref, wu_ref), sems_u, slabs_up[1])
        i = pl.program_id(0)

        @pl.when(i == 0)
        def _():
            dgp_ref[...] = jnp.zeros_like(dgp_ref)
            dcw_ref[...] = jnp.zeros_like(dcw_ref)
            dcb_ref[...] = jnp.zeros_like(dcb_ref)
            dgpre_ref[...] = jnp.zeros_like(dgpre_ref)
            carry_ref[...] = jnp.zeros_like(carry_ref)

        dh = dh_ref[...]
        dy, dgp = _rms_bwd(dh, y_ref[...], gp_ref[...])
        dgp_ref[...] += dgp
        dyb = dy.astype(BF)
        dy_ref[...] = dyb
        g = g_ref[...].astype(F32)
        up = up_ref[...].astype(F32)
        first_tile = i == n_tiles - 1
        ext = jnp.concatenate([jnp.where(first_tile, 0.0, gh_ref[8:HALO, :].astype(F32)), g], axis=0)
        g2 = pltpu.roll(ext, 2, 0)[8:]
        g1 = pltpu.roll(ext, 1, 0)[8:]
        gl = gl_ref[...].astype(F32)
        act_ref[...] = (gl * up).astype(BF)
        dact = _dot_nt(dyb, wd_ref[...])
        dub = (dact * gl).astype(BF)
        dgt = dact * up * gg_ref[...].astype(F32)
        dcb_ref[...] += jnp.sum(dgt, axis=0, keepdims=True)
        dcw_ref[...] += jnp.concatenate([jnp.sum(dgt * g2, axis=0, keepdims=True),
                                         jnp.sum(dgt * g1, axis=0, keepdims=True),
                                         jnp.sum(dgt * g, axis=0, keepdims=True)], axis=0)
        after = jnp.concatenate([dgt, carry_ref[...]], axis=0)
        carry_ref[...] = dgt[0:8, :]
        cw = cw_ref[...]
        dg = (cw[2:3] * dgt + cw[1:2] * pltpu.roll(after, tm + 8 - 1, 0)[:tm]
              + cw[0:1] * pltpu.roll(after, tm + 8 - 2, 0)[:tm])
        dgb = dg.astype(BF)
        dgu_ref[:, 0:D_FF] = dgb
        dgu_ref[:, D_FF:2 * D_FF] = dub
        du = _dot(dgb, wg_ref[...]) + _dot(dub, wu_ref[...])
        dx, dgain = _rms_bwd(du, h_ref[...], gpre_ref[...])
        dhm_ref[...] = dh + dx
        dgpre_ref[...] += dgain

    rev = lambda width, j=0: pl.BlockSpec((tm, width), lambda i, j=j: (n_tiles - 1 - i, j))
    bf = lambda width: jax.ShapeDtypeStruct((t_rows, width), BF)
    outs, moved = _hosted_call(
        body, comm, name, (n_tiles,),
        [rev(D_MODEL), rev(D_MODEL), _whole((1, D_MODEL)), rev(D_FF, 0),
         pl.BlockSpec((HALO, D_FF), lambda i: (jnp.maximum((n_tiles - 1 - i) * hpt - 1, 0), 0)), rev(D_FF, 1),
         rev(D_FF), rev(D_FF), _whole((3, D_FF)), rev(D_MODEL), _whole((1, D_MODEL)), HBM_SPEC, HBM_SPEC],
        (dh, y, gain_post, gu, gu, gu, gl, gg, conv_w, h, gain_pre, slabs_down[0], slabs_up[0]),
        (jax.ShapeDtypeStruct((t_rows, D_MODEL), F32), bf(D_MODEL), bf(D_FF), bf(2 * D_FF),
         jax.ShapeDtypeStruct((1, D_MODEL), F32), jax.ShapeDtypeStruct((3, D_FF), F32),
         jax.ShapeDtypeStruct((1, D_FF), F32), jax.ShapeDtypeStruct((1, D_MODEL), F32)),
        (rev(D_MODEL), rev(D_MODEL), rev(D_FF), rev(2 * D_FF), _whole((1, D_MODEL)), _whole((3, D_FF)),
         _whole((1, D_FF)), _whole((1, D_MODEL))),
        _weight_scratch(['wd']) + _weight_scratch(['wg', 'wu']) + [pltpu.VMEM((8, D_FF), F32)])
    return (*outs, moved)


def _mix_bwd_a(dh, r, gain_post, proj, a, p, slabs, name, comm=None):
    t_rows = dh.shape[0]
    tm = _tile(t_rows, ROW_TILE)

    def body(dh_ref, r_ref, gp_ref, ga0_ref, ga1_ref, gb0_ref, gb1_ref, a_ref, p_ref, slabs_ref,
             dr_ref, z_ref, dya_ref, dyp_ref, da_ref, dp_ref, dgab_ref, dg_ref, wb_ref, wo_ref, sems):
        _fetch_weights(slabs_ref, ['wb', 'w_out'], (wb_ref, wo_ref), sems, slabs[1])

        @pl.when(pl.program_id(0) == 0)
        def _():
            dg_ref[...] = jnp.zeros_like(dg_ref)

        dr, dgain = _rms_bwd(dh_ref[...], r_ref[...], gp_ref[...])
        dg_ref[...] += dgain
        drb = dr.astype(BF)
        dr_ref[...] = drb
        dz = _dot_nt(drb, wo_ref[...])
        ya = _dot_nt(a_ref[...], wb_ref[:, 0:HGRN_W])
        yp = _dot_nt(p_ref[...], wb_ref[:, HGRN_W:HGRN_W + POOL_W])
        sa = jax.nn.sigmoid(jnp.concatenate([ga0_ref[...], ga1_ref[...]], axis=1))
        sb = jax.nn.sigmoid(jnp.concatenate([gb0_ref[...], gb1_ref[...]], axis=1))
        z_ref[...] = (sa * ya + sb * yp).astype(BF)
        dgab_ref[:, 0:D_MODEL] = (dz * ya * sa * (1.0 - sa)).astype(BF)
        dgab_ref[:, D_MODEL:2 * D_MODEL] = (dz * yp * sb * (1.0 - sb)).astype(BF)
        dya = (dz * sa).astype(BF)
        dyp = (dz * sb).astype(BF)
        dya_ref[...] = dya
        dyp_ref[...] = dyp
        da_ref[...] = _dot(dya, wb_ref[:, 0:HGRN_W])
        dp_ref[...] = _dot(dyp, wb_ref[:, HGRN_W:HGRN_W + POOL_W])

    rows = lambda width: pl.BlockSpec((tm, width), lambda i: (i, 0))
    bf = lambda width: jax.ShapeDtypeStruct((t_rows, width), BF)
    f32 = lambda width: jax.ShapeDtypeStruct((t_rows, width), F32)
    outs, moved = _hosted_call(
        body, comm, name, (t_rows // tm,),
        [rows(D_MODEL), rows(D_MODEL), _whole((1, D_MODEL)),
         _cols(tm, 512, 5), _cols(tm, 512, 6), _cols(tm, 512, 7), _cols(tm, 512, 8),
         rows(HGRN_W), rows(POOL_W), HBM_SPEC],
        (dh, r, gain_post, proj, proj, proj, proj, a, p, slabs[0]),
        (bf(D_MODEL), bf(D_MODEL), bf(D_MODEL), bf(D_MODEL), f32(HGRN_W), f32(POOL_W), bf(2 * D_MODEL),
         jax.ShapeDtypeStruct((1, D_MODEL), F32)),
        (rows(D_MODEL), rows(D_MODEL), rows(D_MODEL), rows(D_MODEL), rows(HGRN_W), rows(POOL_W),
         rows(2 * D_MODEL), _whole((1, D_MODEL))),
        _weight_scratch(['wb', 'w_out']))
    return (*outs, moved)


def _mix_bwd_b(da, dp, proj, o, out_gain, pool_proj, pool_scale, name):
    t_rows = da.shape[0]
    tm = _tile(t_rows, ROW_TILE)
    hpt = tm // HALO
    last = t_rows // tm - 1

    def body(da_ref, dp_ref, dpn_ref, g_ref, v_ref, vh_ref, o_ref, og_ref, pp_ref, ps_ref,
             do_ref, dgv_ref, dog_ref, dpp_ref, dps_ref):
        i = pl.program_id(0)

        @pl.when(i == 0)
        def _():
            dog_ref[...] = jnp.zeros_like(dog_ref)
            dpp_ref[...] = jnp.zeros_like(dpp_ref)
            dps_ref[...] = jnp.zeros_like(dps_ref)

        da = da_ref[...]
        og = og_ref[...]
        o = o_ref[...]
        on, rs = _head_norm(o)
        onc = jnp.concatenate(on, axis=1)
        sg = jax.nn.sigmoid(g_ref[...])
        dog_ref[...] += jnp.sum(da * onc * sg, axis=0, keepdims=True)
        dgv_ref[:, 0:HGRN_W] = (da * onc * og * sg * (1.0 - sg)).astype(BF)
        don = da * og * sg
        for hh in range(HEADS):
            cs = slice(hh * HEAD_DIM, (hh + 1) * HEAD_DIM)
            d = don[:, cs]
            do_ref[:, cs] = rs[hh] * (d - on[hh] * jnp.mean(d * on[hh], axis=-1, keepdims=True))

        scale = ps_ref[...]
        halo = jnp.where(i == 0, 0.0, vh_ref[...])
        pooled = _pool_fwd(v_ref[...], halo, i * tm)
        y, _ = _pool_project(pooled, pp_ref, scale)
        dp = dp_ref[...]
        dps_ref[...] += jnp.sum(dp * y, axis=0, keepdims=True)
        dy = dp * scale
        dyn = jnp.where(i == last, 0.0, dpn_ref[...]) * scale
        dpooled, dhalo = [], []
        for gi in range(4):
            cs = slice(gi * HEAD_DIM, (gi + 1) * HEAD_DIM)
            ppb = pp_ref[gi].astype(BF)
            dyb = dy[:, cs].astype(BF)
            dpooled.append(_dot_nt(dyb, ppb))
            dhalo.append(_dot_nt(dyn[:, cs].astype(BF), ppb))
            dpp_ref[gi] += _dot_tn(pooled[gi].astype(BF), dyb)
        dv = _pool_bwd(dpooled, dhalo, i * tm)
        dgv_ref[:, HGRN_W:HGRN_W + POOL_W] = jnp.concatenate(dv, axis=1).astype(BF)

    rows = lambda width: pl.BlockSpec((tm, width), lambda i: (i, 0))
    return pl.pallas_call(
        body, name=name,
        out_shape=(jax.ShapeDtypeStruct((t_rows, HGRN_W), F32), jax.ShapeDtypeStruct((t_rows, HGRN_W + POOL_W), BF),
                   jax.ShapeDtypeStruct((1, HGRN_W), F32), jax.ShapeDtypeStruct((4, HEAD_DIM, HEAD_DIM), F32),
                   jax.ShapeDtypeStruct((1, POOL_W), F32)),
        grid=(t_rows // tm,),
        in_specs=[rows(HGRN_W), rows(POOL_W),
                  pl.BlockSpec((HALO, POOL_W), lambda i: (jnp.minimum((i + 1) * hpt, (last + 1) * hpt - 1), 0)),
                  _cols(tm, 512, 3), _cols(tm, 512, 4),
                  pl.BlockSpec((HALO, 512), lambda i: (jnp.maximum(i * hpt - 1, 0), 4)),
                  rows(HGRN_W), _whole((1, HGRN_W)), _whole((4, HEAD_DIM, HEAD_DIM)), _whole((1, POOL_W))],
        out_specs=(rows(HGRN_W), rows(HGRN_W + POOL_W), _whole((1, HGRN_W)), _whole((4, HEAD_DIM, HEAD_DIM)),
                   _whole((1, POOL_W))),
        compiler_params=_params(("arbitrary",)),
    )(da, dp, dp, proj, proj, proj, o, out_gain, pool_proj, pool_scale)


def _hgrn_bwd(proj, lower_bounds, layer, states, do, name, comm=None):
    t_rows = proj.shape[0]
    tm = _tile(t_rows, ROW_TILE, CHUNK)
    nct = tm // CHUNK
    n_tiles = t_rows // tm

    def body(q_ref, f_ref, i_ref, lb_ref, sall_ref, do_ref, dqfi_ref, dlb_ref, dst_ref):
        @pl.when(pl.program_id(0) == 0)
        def _():
            dst_ref[...] = jnp.zeros_like(dst_ref)
            dlb_ref[...] = jnp.zeros_like(dlb_ref)

        lbs = _layer_lower_bound(lb_ref, layer)
        tri_lo, tri_up, diag, off = _chunk_masks()
        is_last_row = lax.broadcasted_iota(jnp.int32, (CHUNK, 1), 0) == CHUNK - 1

        group = HGRN_UNROLL if nct % HGRN_UNROLL == 0 else 1

        def chunk_group(gi, carry):
            pairs = [(u, hh) for u in range(group) for hh in range(HEADS)]
            cidx = [nct - 1 - (gi * group + u) for u in range(group)]
            rows = [pl.ds(pl.multiple_of(c * CHUNK, SUB), CHUNK) for c in cidx]
            cols = [slice(hh * HEAD_DIM, (hh + 1) * HEAD_DIM) for hh in range(HEADS)]
            lb = {p: lbs[:, cols[p[1]]] for p in pairs}
            q = {p: q_ref[rows[p[0]], cols[p[1]]] for p in pairs}
            vb = {p: i_ref[rows[p[0]], cols[p[1]]].astype(BF) for p in pairs}
            dob = {p: do_ref[rows[p[0]], cols[p[1]]].astype(BF) for p in pairs}
            gates = {p: _chunk_gates(f_ref[rows[p[0]], cols[p[1]]], lb[p]) for p in pairs}
            k = {p: gates[p][4] for p in pairs}
            b = {p: _tri_mm(tri_lo, gates[p][3]) for p in pairs}
            fac = {p: _chunk_factors(q[p], k[p], b[p]) for p in pairs}
            e_b = {p: jnp.exp(b[p]) for p in pairs}
            e_last = {p: jnp.exp(b[p][CHUNK - 1:CHUNK]) for p in pairs}
            e_kl = {p: jnp.exp(b[p][CHUNK - 1:CHUNK] - b[p]) for p in pairs}
            qc = {p: (q[p] * e_b[p]).astype(BF) for p in pairs}
            kdec = {p: (k[p] * e_kl[p]).astype(BF) for p in pairs}
            a = {p: _chunk_scores(*fac[p][4:], diag, off) for p in pairs}
            da_full = {p: _dot_nt(dob[p], vb[p]) for p in pairs}
            dv, dq, dk, upd, dupd = {}, {}, {}, {}, {}
            for p in pairs:
                e_qs, e_qm, e_km, e_ks, qs, qm, km, ks = fac[p]
                da_d = jnp.where(diag, da_full[p], 0.0).astype(BF)
                dv[p] = _dot_tn(a[p].astype(BF), dob[p])
                dq[p] = e_qm * _dot(da_d, km)
                dk[p] = e_km * _dot_tn(da_d, qm)
                for m, kk, ek in zip(off, ks, e_ks):
                    da_i = jnp.where(m, da_full[p], 0.0).astype(BF)
                    dq[p] = dq[p] + e_qs * _dot(da_i, kk)
                    dk[p] = dk[p] + ek * _dot_tn(da_i, qs)
                upd[p] = _dot_tn(vb[p], kdec[p])
                dupd[p] = _dot_tn(dob[p], qc[p])
            db = {}
            for hh in range(HEADS):
                dst = dst_ref[hh]
                for u in range(group):
                    p = (u, hh)
                    st = sall_ref[cidx[u], hh]
                    dstb = dst.astype(BF)
                    dv[p] = dv[p] + _dot_nt(kdec[p], dstb)
                    dq[p] = dq[p] + e_b[p] * _dot(dob[p], st.astype(BF))
                    dk[p] = dk[p] + e_kl[p] * _dot(vb[p], dstb)
                    st_new = st * e_last[p] + upd[p]
                    db[p] = (q[p] * dq[p] - k[p] * dk[p]
                             + jnp.where(is_last_row, jnp.sum(st_new * dst, axis=0, keepdims=True), 0.0))
                    dst = dst * e_last[p] + dupd[p]
                dst_ref[hh] = dst
            for p in pairs:
                u, hh = p
                sg, s2, f = gates[p][:3]
                dlf = _tri_mm(tri_up, db[p])
                df = jnp.where(f > LOG_FLOOR, dlf / f, 0.0)
                dfl = df * (1.0 - lb[p]) * sg * (1.0 - sg) - dk[p] * (1.0 - lb[p]) * s2 * (1.0 - s2)
                dlb_ref[:, cols[hh]] += jnp.sum(df * (1.0 - sg) - dk[p] * s2, axis=0, keepdims=True)
                dqfi_ref[rows[u], cols[hh]] = dq[p].astype(BF)
                dqfi_ref[rows[u], pl.ds(HGRN_W + hh * HEAD_DIM, HEAD_DIM)] = dfl.astype(BF)
                dqfi_ref[rows[u], pl.ds(2 * HGRN_W + hh * HEAD_DIM, HEAD_DIM)] = dv[p].astype(BF)
            return carry

        lax.fori_loop(0, nct // group, chunk_group, 0)

    rev = lambda width, j: pl.BlockSpec((tm, width), lambda i, j=j: (n_tiles - 1 - i, j))
    outs, exchanged = _hosted_call(
        body, comm, name, (n_tiles,),
        [rev(512, 0), rev(512, 1), rev(512, 2), _whole((DEPTH, HGRN_W)),
         pl.BlockSpec((nct, HEADS, HEAD_DIM, HEAD_DIM), lambda i: (n_tiles - 1 - i, 0, 0, 0)), rev(HGRN_W, 0)],
        (proj, proj, proj, lower_bounds, states, do),
        (jax.ShapeDtypeStruct((t_rows, 3 * HGRN_W), BF), jax.ShapeDtypeStruct((1, HGRN_W), F32)),
        (rev(3 * HGRN_W, 0), _whole((1, HGRN_W))),
        [pltpu.VMEM((HEADS, HEAD_DIM, HEAD_DIM), F32)])
    return (*outs, exchanged)


def _in_bwd(dqfi, dgv, dgab, slabs, h, gain_pre, dh_out, name, comm=None):
    t_rows = h.shape[0]
    tm = _tile(t_rows, ROW_TILE)
    c1 = 3 * HGRN_W
    c2 = c1 + HGRN_W + POOL_W

    def body(d1_ref, d2_ref, d3_ref, slabs_ref, h_ref, gp_ref, dho_ref, dh_ref, dg_ref, w_ref, sems):
        _fetch_weights(slabs_ref, ['w_in'], (w_ref,), sems, slabs[1])

        @pl.when(pl.program_id(0) == 0)
        def _():
            dg_ref[...] = jnp.zeros_like(dg_ref)

        du = (_dot(d1_ref[...], w_ref[0:c1, :]) + _dot(d2_ref[...], w_ref[c1:c2, :])
              + _dot(d3_ref[...], w_ref[c2:IN_COLS, :]))
        dx, dgain = _rms_bwd(du, h_ref[...], gp_ref[...])
        dh_ref[...] = dho_ref[...] + dx
        dg_ref[...] += dgain

    rows = lambda width: pl.BlockSpec((tm, width), lambda i: (i, 0))
    outs, exchanged = _hosted_call(
        body, comm, name, (t_rows // tm,),
        [rows(c1), rows(c2 - c1), rows(IN_COLS - c2), HBM_SPEC, rows(D_MODEL), _whole((1, D_MODEL)), rows(D_MODEL)],
        (dqfi, dgv, dgab, slabs[0], h, gain_pre, dh_out),
        (jax.ShapeDtypeStruct((t_rows, D_MODEL), F32), jax.ShapeDtypeStruct((1, D_MODEL), F32)),
        (rows(D_MODEL), _whole((1, D_MODEL))),
        _weight_scratch(['w_in']))
    return (*outs, exchanged)


def _lower_bound_grad(lower_bounds, dlbs, name):
    assert DEPTH == 2 and lower_bounds.shape[0] == DEPTH, "the softmax over layers is written out for two layers"
    def body(lb_ref, d_ref, o_ref):
        g0, g1 = _softmax2(lb_ref)
        bound = (g0 + g1) - g0
        inside = (bound > 0.0) & (bound < 1.0)
        dg1 = jnp.where(inside, d_ref[1:2, :], 0.0)
        inner = g1 * dg1
        o_ref[0:1, :] = g0 * (0.0 - inner)
        o_ref[1:2, :] = g1 * (dg1 - inner)

    return pl.pallas_call(body, name=name, out_shape=jax.ShapeDtypeStruct(lower_bounds.shape, F32))(lower_bounds, dlbs)


def _adamw(w, g, m, v, name):
    r, c = w.shape
    tr = r if (r % 8 or r <= 512) else _tile(r, 512, 8)
    c1 = 1.0 - ADAM_B1 ** ADAM_STEP
    c2 = 1.0 - ADAM_B2 ** ADAM_STEP

    def body(w_ref, g_ref, m_ref, v_ref, d_ref, nm_ref, nv_ref):
        gg = g_ref[...]
        nm = ADAM_B1 * m_ref[...] + (1.0 - ADAM_B1) * gg
        nv = ADAM_B2 * v_ref[...] + (1.0 - ADAM_B2) * (gg * gg)
        d_ref[...] = -ADAM_LR * ((nm / c1) / (jnp.sqrt(nv / c2) + ADAM_EPS) + ADAM_WD * w_ref[...])
        nm_ref[...] = nm
        nv_ref[...] = nv

    blk = pl.BlockSpec((tr, c), lambda i: (i, 0))
    shp = jax.ShapeDtypeStruct((r, c), F32)
    return pl.pallas_call(
        body, name=name, out_shape=(shp, shp, shp), grid=(r // tr,),
        in_specs=[blk, blk, blk, blk], out_specs=(blk, blk, blk),
        compiler_params=_params(("parallel",)),
    )(w, g, m, v)


def _as2d(a):
    return a.reshape(-1, a.shape[-1])


def _layer_slab(w, l):
    t = lambda a: jnp.swapaxes(a, 0, 1)
    parts = [t(w['w_in'][l]), jnp.concatenate([t(w['w_branch_hgrn'][l]), t(w['w_branch_pool'][l])], axis=1),
             w['w_out'][l], t(w['ffn_w_gate'][l]), t(w['ffn_w_up'][l]), w['ffn_w_down'][l]]
    return jnp.concatenate(parts, axis=0).astype(BF)


def _slab_grads(sums):
    t = lambda a: jnp.swapaxes(a, 0, 1)
    wb = sums['wb']
    return {'w_in': t(sums['w_in']), 'w_branch_hgrn': t(wb[:, :HGRN_W]), 'w_branch_pool': t(wb[:, HGRN_W:]),
            'w_out': sums['w_out'], 'ffn_w_gate': t(sums['wg']), 'ffn_w_up': t(sums['wu']), 'ffn_w_down': sums['wd']}


def _train_step(x, target, w, m, v):
    my_slabs = [_layer_slab(w, l) for l in range(DEPTH)]
    n_in = W_ROWS['w_in'][1]
    slabs = [{} for _ in range(DEPTH)]
    slabs[0]['w_in'] = (_all_gather(my_slabs[0][:n_in], "gather_w_in_0"), 0)
    f32_shapes = [w[n].shape for n in F32_GATHERED]
    gathered32 = _all_gather(_pack([w[n] for n in F32_GATHERED], F32, 8), "gather_meta_conv")
    full = {n: _blocks_to_full(blk, SHARD_AXIS[n]) for n, blk in zip(F32_GATHERED, _unpack(gathered32, f32_shapes))}
    row = lambda name, l: w[name][l][None]

    h = jnp.concatenate([full['meta_tokens'], x], axis=0)
    saved = []
    for l in range(DEPTH):
        s = {'h_in': h}
        nxt = l + 1 if l + 1 < DEPTH else None
        plan = {'in_proj': [(0, ['wb', 'w_out', 'wg', 'wu'])] if l == 0 else [],
                'hgrn_fwd': ([(0, ['wd'])] if l == 0 else []) + ([(nxt, ['w_in'])] if nxt else []),
                'mix_merge': [(nxt, ['wb', 'w_out']), (nxt, ['wd'])] if nxt else [],
                'ffn_proj': [(nxt, ['wg', 'wu'])] if nxt else []}

        def hosted(kernel):
            comms = []
            for layer, keys in plan[kernel]:
                first, last = W_ROWS[keys[0]], W_ROWS[keys[-1]]
                comms.append(_SlabGather(my_slabs[layer], first[0], last[0] + last[1] - first[0]))
            return None if not comms else comms[0] if len(comms) == 1 else _Both(*comms)

        def landed(kernel, moved):
            for (layer, keys), arr in zip(plan[kernel], moved or ()):
                slabs[layer].update({k: (arr, W_ROWS[keys[0]][0]) for k in keys})

        s['proj'], moved = _norm_matmul(h, row('mix_norm_pre', l), slabs[l]['w_in'], ['w_in'], F32, f"in_proj_{l}",
                                        comm=hosted('in_proj'))
        landed('in_proj', moved)
        s['o'], s['states'], moved = _hgrn_fwd(s['proj'], w['hgrn_lower_bounds'], l, f"hgrn_fwd_{l}",
                                               comm=hosted('hgrn_fwd'))
        landed('hgrn_fwd', moved)
        h, s['r'], s['a'], s['p'], moved = _mix_merge(
            s['proj'], s['o'], h, row('hgrn_out_norm', l), w['pool_proj'][l], row('pool_scale', l), slabs[l]['wb'],
            row('mix_norm_post', l), f"mix_merge_{l}", comm=hosted('mix_merge'))
        landed('mix_merge', moved)
        s['h_mid'] = h
        s['gu'], moved = _norm_matmul(h, row('ffn_norm_pre', l), slabs[l]['wg'], ['wg', 'wu'], BF, f"ffn_proj_{l}",
                                      comm=hosted('ffn_proj'))
        landed('ffn_proj', moved)
        h, s['y'], s['gl'], s['gg'] = _ffn_down(s['gu'], h, full['ffn_conv_w'][l], row('ffn_conv_b', l),
                                                slabs[l]['wd'], row('ffn_norm_post', l), f"ffn_down_{l}")
        saved.append(s)

    dh, loss_cols = _loss_grad(h, jnp.pad(target, ((N_META, 0), (0, 0))), "loss_grad")
    loss = jnp.sum(loss_cols)

    small = {n: [None] * DEPTH for n in REPLICATED + ['ffn_conv_w']}
    exchanged = [{} for _ in range(DEPTH)]
    pending = None
    for l in reversed(range(DEPTH)):
        s = saved[l]
        g = {}
        (dh, dy, act, dgu, small['ffn_norm_post'][l], small['ffn_conv_w'][l], small['ffn_conv_b'][l],
         small['ffn_norm_pre'][l], done) = _ffn_bwd(
            dh, s['y'], row('ffn_norm_post', l), s['gu'], s['gl'], s['gg'], full['ffn_conv_w'][l], slabs[l]['wd'],
            slabs[l]['wg'], s['h_mid'], row('ffn_norm_pre', l), f"ffn_bwd_{l}", comm=pending)
        if pending is not None:
            exchanged[l + 1]['in'] = done
        g['d_down'], _ = _dw([act], [dy], f"dw_down_{l}")
        g['d_gu'], exchanged[l]['wd'] = _dw([dgu], [s['h_mid']], f"dw_gate_up_{l}", gain=row('ffn_norm_pre', l),
                                            comm=_GradExchange(EXCHANGE_GROUPS['wd'], g))

        dr, z, dya, dyp, da, dp, dgab, small['mix_norm_post'][l], exchanged[l]['wg'] = _mix_bwd_a(
            dh, s['r'], row('mix_norm_post', l), s['proj'], s['a'], s['p'], slabs[l]['wb'], f"mix_bwd_a_{l}",
            comm=_GradExchange(EXCHANGE_GROUPS['wg'], g))
        g['d_out'], _ = _dw([z], [dr], f"dw_out_{l}")
        g['d_b'], _ = _dw([dya, dyp], [s['a'], s['p']], f"dw_branch_{l}")
        do, dgv, small['hgrn_out_norm'][l], small['pool_proj'][l], small['pool_scale'][l] = _mix_bwd_b(
            da, dp, s['proj'], s['o'], row('hgrn_out_norm', l), w['pool_proj'][l], row('pool_scale', l),
            f"mix_bwd_b_{l}")
        dqfi, small['hgrn_lower_bounds'][l], exchanged[l]['rest'] = _hgrn_bwd(
            s['proj'], w['hgrn_lower_bounds'], l, s['states'], do, f"hgrn_bwd_{l}",
            comm=_GradExchange(EXCHANGE_GROUPS['rest'], g))
        g['d_in'], _ = _dw([dqfi, dgv, dgab], [s['h_in']], f"dw_in_{l}", gain=row('mix_norm_pre', l))
        pending = _GradExchange(EXCHANGE_GROUPS['in'], g)
        if l == 0:
            dh, small['mix_norm_pre'][l], exchanged[l]['in'] = _in_bwd(
                dqfi, dgv, dgab, slabs[l]['w_in'], s['h_in'], row('mix_norm_pre', l), dh, f"in_bwd_{l}", comm=pending)
        else:
            dh, small['mix_norm_pre'][l], _ = _in_bwd(
                dqfi, dgv, dgab, slabs[l]['w_in'], s['h_in'], row('mix_norm_pre', l), dh, f"in_bwd_{l}")

    grad_x = dh[N_META:]

    per_layer = []
    for l in range(DEPTH):
        sums = {}
        for part, keys in EXCHANGE_GROUPS.items():
            recv, own = exchanged[l][part]
            total = _sum_parts(own, recv, f"sum_grads_{part}_{l}")
            off = 0
            for k in keys:
                sums[k] = total[off:off + W_ROWS[k][1]]
                off += W_ROWS[k][1]
        per_layer.append(_slab_grads(sums))
    grads = {n: jnp.stack([per_layer[l][n] for l in range(DEPTH)]) for n in per_layer[0]}

    stack = lambda n: jnp.stack(small[n]) if small[n][0].shape[0] != 1 else jnp.concatenate(small[n], axis=0)
    names = list(small) + ['meta_tokens']
    partial = {n: stack(n) for n in small}
    partial['meta_tokens'] = dh[:N_META]
    gathered = _all_gather(_pack([partial[n] for n in names], F32, 8), "gather_small_grads")
    rep = dict(zip(names, _unpack(_sum8(gathered, "sum_small_grads"), [partial[n].shape for n in names])))
    rep['hgrn_lower_bounds'] = _lower_bound_grad(w['hgrn_lower_bounds'], rep['hgrn_lower_bounds'], "lower_bound_grad")
    me = 4 * lax.axis_index("x") + 2 * lax.axis_index("y") + lax.axis_index("c")
    for n in F32_GATHERED:
        width = w[n].shape[SHARD_AXIS[n]]
        rep[n] = lax.dynamic_slice_in_dim(rep[n], me * width, width, axis=SHARD_AXIS[n])
    grads.update(rep)

    delta, new_m, new_v = {}, {}, {}
    for n in WEIGHT_NAMES:
        shape = w[n].shape
        d2, m2, v2 = _adamw(_as2d(w[n]), _as2d(grads[n]), _as2d(m[n]), _as2d(v[n]), f"adamw_{n}")
        delta[n], new_m[n], new_v[n] = d2.reshape(shape), m2.reshape(shape), v2.reshape(shape)
    return loss, grad_x, grads, delta, new_m, new_v


def kernel(x, meta_tokens, mix_norm_pre, mix_norm_post, w_in, hgrn_lower_bounds, hgrn_out_norm, w_branch_hgrn, pool_proj, pool_scale, w_branch_pool, w_out, ffn_norm_pre, ffn_norm_post, ffn_w_gate, ffn_w_up, ffn_conv_w, ffn_conv_b, ffn_w_down, loss_target, m_meta_tokens, m_mix_norm_pre, m_mix_norm_post, m_w_in, m_hgrn_lower_bounds, m_hgrn_out_norm, m_w_branch_hgrn, m_pool_proj, m_pool_scale, m_w_branch_pool, m_w_out, m_ffn_norm_pre, m_ffn_norm_post, m_ffn_w_gate, m_ffn_w_up, m_ffn_conv_w, m_ffn_conv_b, m_ffn_w_down, v_meta_tokens, v_mix_norm_pre, v_mix_norm_post, v_w_in, v_hgrn_lower_bounds, v_hgrn_out_norm, v_w_branch_hgrn, v_pool_proj, v_pool_scale, v_w_branch_pool, v_w_out, v_ffn_norm_pre, v_ffn_norm_post, v_ffn_w_gate, v_ffn_w_up, v_ffn_conv_w, v_ffn_conv_b, v_ffn_w_down):
    w = dict(zip(WEIGHT_NAMES, (meta_tokens, mix_norm_pre, mix_norm_post, w_in, hgrn_lower_bounds, hgrn_out_norm,
                                w_branch_hgrn, pool_proj, pool_scale, w_branch_pool, w_out, ffn_norm_pre,
                                ffn_norm_post, ffn_w_gate, ffn_w_up, ffn_conv_w, ffn_conv_b, ffn_w_down)))
    m = dict(zip(WEIGHT_NAMES, (m_meta_tokens, m_mix_norm_pre, m_mix_norm_post, m_w_in, m_hgrn_lower_bounds,
                                m_hgrn_out_norm, m_w_branch_hgrn, m_pool_proj, m_pool_scale, m_w_branch_pool, m_w_out,
                                m_ffn_norm_pre, m_ffn_norm_post, m_ffn_w_gate, m_ffn_w_up, m_ffn_conv_w,
                                m_ffn_conv_b, m_ffn_w_down)))
    v = dict(zip(WEIGHT_NAMES, (v_meta_tokens, v_mix_norm_pre, v_mix_norm_post, v_w_in, v_hgrn_lower_bounds,
                                v_hgrn_out_norm, v_w_branch_hgrn, v_pool_proj, v_pool_scale, v_w_branch_pool, v_w_out,
                                v_ffn_norm_pre, v_ffn_norm_post, v_ffn_w_gate, v_ffn_w_up, v_ffn_conv_w,
                                v_ffn_conv_b, v_ffn_w_down)))
    loss_local, grad_x, grads, delta, new_m, new_v = _train_step(x[0], loss_target[0], w, m, v)
    loss = lax.psum(loss_local, ("x", "y", "c"))
    return (loss, grad_x[None], *[grads[n] for n in WEIGHT_NAMES], *[delta[n] for n in WEIGHT_NAMES],
            *[new_m[n] for n in WEIGHT_NAMES], *[new_v[n] for n in WEIGHT_NAMES])
```

```python
import jax
import jax.numpy as jnp
from jax import lax
from jax.experimental import pallas as pl
from jax.experimental.pallas import tpu as pltpu

F32 = jnp.float32
BF = jnp.bfloat16

D_MODEL = 1024
N_META = 16
DEPTH = 2
HEADS = 4
HEAD_DIM = 128
HGRN_W = 512
POOL_W = 512
POOL_WINDOWS = (2, 4, 8, 16)
D_FF = 2816
IN_COLS = 4608
EPS = 1e-6
LOG_FLOOR = 1e-30
N_DEV = 8

ADAM_LR = 0.001
ADAM_B1 = 0.9
ADAM_B2 = 0.999
ADAM_EPS = 1e-08
ADAM_WD = 0.01
ADAM_STEP = 10

SUB = 16
CHUNK = 48
HGRN_UNROLL = 9
EXP_CLAMP = 80.0
ROW_TILE = 432
FFN_TILE = 144
HALO = 16
ADVANCE_STEPS = 3
VMEM_LIMIT = 56 * 1024 * 1024
DW_BUFFER_BYTES = 44 * 1024 * 1024
MESH = pl.DeviceIdType.MESH

WEIGHT_NAMES = ['meta_tokens', 'mix_norm_pre', 'mix_norm_post', 'w_in', 'hgrn_lower_bounds', 'hgrn_out_norm',
                'w_branch_hgrn', 'pool_proj', 'pool_scale', 'w_branch_pool', 'w_out', 'ffn_norm_pre', 'ffn_norm_post',
                'ffn_w_gate', 'ffn_w_up', 'ffn_conv_w', 'ffn_conv_b', 'ffn_w_down']
SHARD_AXIS = {'meta_tokens': 1, 'w_in': 2, 'w_branch_hgrn': 2, 'w_branch_pool': 2, 'w_out': 1,
              'ffn_w_gate': 2, 'ffn_w_up': 2, 'ffn_conv_w': 2, 'ffn_w_down': 1}
F32_GATHERED = ['meta_tokens', 'ffn_conv_w']
REPLICATED = [n for n in WEIGHT_NAMES if n not in SHARD_AXIS]

W_ROWS = {'w_in': (0, 576),
          'wb': (576, 128),
          'w_out': (704, 128),
          'wg': (832, 352),
          'wu': (1184, 352),
          'wd': (1536, 352)}
SLAB_ROWS = 1888
EXCHANGE_GROUPS = {'wd': ['wd'], 'wg': ['wg'], 'rest': ['wu', 'wb', 'w_out'], 'in': ['w_in']}


def _params(sem=None):
    return pltpu.CompilerParams(dimension_semantics=sem, vmem_limit_bytes=VMEM_LIMIT)


def _tile(total, pref, mult=16):
    best = None
    for t in range(mult, min(total, pref) + 1, mult):
        if total % t == 0:
            best = t
    assert best is not None, (total, pref, mult)
    return best


def _whole(shape):
    return pl.BlockSpec(shape, lambda *_: (0,) * len(shape))


def _cols(tm, width, j):
    return pl.BlockSpec((tm, width), lambda i, j=j: (i, j))


def _dot(a, b):
    return jnp.dot(a, b, preferred_element_type=F32)


def _dot_nt(a, b):
    return lax.dot_general(a, b, (((1,), (1,)), ((), ())), preferred_element_type=F32)


def _dot_tn(a, b):
    return lax.dot_general(a, b, (((0,), (0,)), ((), ())), preferred_element_type=F32)


def _rms_fwd(x, g):
    r = lax.rsqrt(jnp.mean(x * x, axis=-1, keepdims=True) + EPS)
    return x * r * g


def _rms_bwd(dy, x, g):
    r = lax.rsqrt(jnp.mean(x * x, axis=-1, keepdims=True) + EPS)
    xh = x * r
    dyg = dy * g
    dx = r * (dyg - xh * jnp.mean(dyg * xh, axis=-1, keepdims=True))
    return dx, jnp.sum(dy * xh, axis=0, keepdims=True)


_GELU_C = 0.7978845608028654
_GELU_A = 0.044715


def _gelu_and_grad(x):
    x2 = x * x
    t = jnp.tanh(x * (_GELU_C + (_GELU_C * _GELU_A) * x2))
    u = 1.0 + t
    hx = 0.5 * x
    return hx * u, 0.5 * u + (hx * (1.0 - t * t)) * (_GELU_C + (3.0 * _GELU_C * _GELU_A) * x2)


def _split3(x):
    x1 = x.astype(BF)
    r1 = x - x1.astype(F32)
    x2 = r1.astype(BF)
    x3 = (r1 - x2.astype(F32)).astype(BF)
    return x1, x2, x3


def _tri_mm(tri, x):
    x1, x2, x3 = _split3(x)
    return _dot(tri, x1) + _dot(tri, x2) + _dot(tri, x3)


def _softmax2(lb_ref):
    l0 = lb_ref[0:1, :]
    l1 = lb_ref[1:2, :]
    m = jnp.maximum(l0, l1)
    e0 = jnp.exp(l0 - m)
    e1 = jnp.exp(l1 - m)
    return e0 / (e0 + e1), e1 / (e0 + e1)


def _layer_lower_bound(lb_ref, layer):
    g0, g1 = _softmax2(lb_ref)
    if layer == 0:
        return jnp.clip(g0 - g0, 0.0, 1.0)
    return jnp.clip((g0 + g1) - g0, 0.0, 1.0)


def _all_gather(x, name):
    n, w = x.shape

    def body(x_ref, out_ref, send_sems, recv_sems, local_sem):
        mx, my, mc = lax.axis_index("x"), lax.axis_index("y"), lax.axis_index("c")
        me, sibling = (mx, my, mc), (mx, my, 1 - mc)
        chips = [(1 - mx, my), (mx, 1 - my), (1 - mx, 1 - my)]

        def rows(px, py, pc):
            return out_ref.at[4 * px + 2 * py + pc]

        def copy(k, block, to, src=None):
            return pltpu.make_async_remote_copy(
                src_ref=rows(*block) if src is None else src, dst_ref=rows(*block),
                send_sem=send_sems.at[k], recv_sem=recv_sems.at[k], device_id=to, device_id_type=MESH)

        mine = pltpu.make_async_copy(x_ref, rows(*me), local_sem)
        mine.start()
        first = [copy(0, me, sibling, src=x_ref)]
        first += [copy(1 + j, me, (*chip, mc), src=x_ref) for j, chip in enumerate(chips)]
        for cp in first:
            cp.start()
        passed = [copy(4 + j, (*chip, mc), sibling) for j, chip in enumerate(chips)]
        for j, chip in enumerate(chips):
            copy(1 + j, (*chip, mc), me).wait_recv()
            passed[j].start()
        copy(0, sibling, me).wait_recv()
        for j, chip in enumerate(chips):
            copy(4 + j, (*chip, 1 - mc), me).wait_recv()
        for cp in first + passed:
            cp.wait_send()
        mine.wait()

    return pl.pallas_call(
        body, name=name,
        out_shape=jax.ShapeDtypeStruct((N_DEV, n, w), x.dtype),
        in_specs=[pl.BlockSpec(memory_space=pl.ANY)],
        out_specs=pl.BlockSpec(memory_space=pl.ANY),
        scratch_shapes=[pltpu.SemaphoreType.DMA((7,)), pltpu.SemaphoreType.DMA((7,)), pltpu.SemaphoreType.DMA],
    )(x)


GRAD_SRC = {'w_in': ('d_in', 0), 'wb': ('d_b', 0), 'w_out': ('d_out', 0), 'wg': ('d_gu', 0), 'wu': ('d_gu', D_FF),
            'wd': ('d_down', 0)}


class _GradExchange:
    def __init__(self, keys, grads):
        self.names = sorted({GRAD_SRC[k][0] for k in keys})
        self.arrays = [grads[n][0] for n in self.names] + [grads[n][1] for n in self.names]
        self.pieces, off = [], 0
        for k in keys:
            name, base = GRAD_SRC[k]
            rows = W_ROWS[k][1]
            self.pieces.append((self.names.index(name), base, rows, off))
            off += rows
        self.keys, self.rows = keys, off
        self.out_shape = (jax.ShapeDtypeStruct((N_DEV - 1, off, D_MODEL), BF), jax.ShapeDtypeStruct((off, D_MODEL), F32))
        self.scratch = [pltpu.SemaphoreType.DMA((N_DEV - 1,)), pltpu.SemaphoreType.DMA((N_DEV - 1,)),
                        pltpu.SemaphoreType.DMA((len(self.pieces),))]

    def _local(self, ins, outs, scr):
        mx, my, mc = lax.axis_index("x"), lax.axis_index("y"), lax.axis_index("c")
        me = 4 * mx + 2 * my + mc
        return [pltpu.make_async_copy(ins[ai].at[pl.ds(pl.multiple_of(base + rows * me, 8), rows)],
                                      outs[1].at[pl.ds(off, rows)], scr[2].at[k])
                for k, (ai, base, rows, off) in enumerate(self.pieces)]

    def start(self, ins, outs, scr):
        mx, my, mc = lax.axis_index("x"), lax.axis_index("y"), lax.axis_index("c")
        bf16 = ins[len(self.names):]
        for cp in self._local(ins, outs, scr):
            cp.start()
        for d in range(1, N_DEV):
            px = (1 - mx) if (d >> 2) & 1 else mx
            py = (1 - my) if (d >> 1) & 1 else my
            pc = (1 - mc) if d & 1 else mc
            peer = 4 * px + 2 * py + pc
            for ai, base, rows, off in self.pieces:
                pltpu.make_async_remote_copy(
                    src_ref=bf16[ai].at[pl.ds(pl.multiple_of(base + rows * peer, 16), rows)],
                    dst_ref=outs[0].at[d - 1, pl.ds(off, rows)], send_sem=scr[0].at[d - 1], recv_sem=scr[1].at[d - 1],
                    device_id=(px, py, pc), device_id_type=MESH).start()

    def advance(self, ins, outs, scr):
        pass

    def wait(self, ins, outs, scr):
        me = (lax.axis_index("x"), lax.axis_index("y"), lax.axis_index("c"))
        for d in range(1, N_DEV):
            slot = pltpu.make_async_remote_copy(
                src_ref=outs[0].at[d - 1], dst_ref=outs[0].at[d - 1], send_sem=scr[0].at[d - 1],
                recv_sem=scr[1].at[d - 1], device_id=me, device_id_type=MESH)
            slot.wait_recv()
            slot.wait_send()
        for cp in self._local(ins, outs, scr):
            cp.wait()


class _SlabGather:
    def __init__(self, slab, row0, rows):
        self.arrays = [slab]
        self.row0, self.rows = row0, rows
        self.out_shape = (jax.ShapeDtypeStruct((N_DEV, rows, D_MODEL), slab.dtype),)
        self.scratch = [pltpu.SemaphoreType.DMA((7,)), pltpu.SemaphoreType.DMA((7,)), pltpu.SemaphoreType.DMA]

    def _parts(self, ins, outs, scr):
        mx, my, mc = lax.axis_index("x"), lax.axis_index("y"), lax.axis_index("c")
        me, sibling = (mx, my, mc), (mx, my, 1 - mc)
        chips = [(1 - mx, my), (mx, 1 - my), (1 - mx, 1 - my)]
        mine_ref = ins[0].at[pl.ds(self.row0, self.rows)]
        block = lambda px, py, pc: outs[0].at[4 * px + 2 * py + pc]

        def copy(k, blk, to, src=None):
            return pltpu.make_async_remote_copy(
                src_ref=block(*blk) if src is None else src, dst_ref=block(*blk),
                send_sem=scr[0].at[k], recv_sem=scr[1].at[k], device_id=to, device_id_type=MESH)

        local = pltpu.make_async_copy(mine_ref, block(*me), scr[2])
        first = [copy(0, me, sibling, src=mine_ref)]
        first += [copy(1 + j, me, (*chip, mc), src=mine_ref) for j, chip in enumerate(chips)]
        passed = [copy(4 + j, (*chip, mc), sibling) for j, chip in enumerate(chips)]
        landed = [copy(1 + j, (*chip, mc), me) for j, chip in enumerate(chips)]
        from_sibling = [copy(0, sibling, me)] + [copy(4 + j, (*chip, 1 - mc), me) for j, chip in enumerate(chips)]
        return local, first, passed, landed, from_sibling

    def start(self, ins, outs, scr):
        local, first, _, _, _ = self._parts(ins, outs, scr)
        local.start()
        for cp in first:
            cp.start()

    def advance(self, ins, outs, scr):
        _, _, passed, landed, _ = self._parts(ins, outs, scr)
        for arrived, forward in zip(landed, passed):
            arrived.wait_recv()
            forward.start()

    def wait(self, ins, outs, scr):
        local, first, passed, _, from_sibling = self._parts(ins, outs, scr)
        for cp in from_sibling:
            cp.wait_recv()
        for cp in first + passed:
            cp.wait_send()
        local.wait()


class _Both:
    def __init__(self, first, second):
        self.parts = (first, second)
        self.arrays = list(first.arrays) + list(second.arrays)
        self.out_shape = tuple(first.out_shape) + tuple(second.out_shape)
        self.scratch = list(first.scratch) + list(second.scratch)

    def _split(self, ins, outs, scr):
        a, b = self.parts
        na, oa, sa = len(a.arrays), len(a.out_shape), len(a.scratch)
        return (a, ins[:na], outs[:oa], scr[:sa]), (b, ins[na:], outs[oa:], scr[sa:])

    def start(self, ins, outs, scr):
        for part, i, o, s in self._split(ins, outs, scr):
            part.start(i, o, s)

    def advance(self, ins, outs, scr):
        for part, i, o, s in self._split(ins, outs, scr):
            part.advance(i, o, s)

    def wait(self, ins, outs, scr):
        for part, i, o, s in self._split(ins, outs, scr):
            part.wait(i, o, s)


def _hosted_call(body, comm, name, grid, in_specs, args, out_shape, out_specs, scratch_shapes):
    if comm is None:
        return pl.pallas_call(body, name=name, grid=grid, in_specs=in_specs, out_specs=out_specs, out_shape=out_shape,
                              scratch_shapes=scratch_shapes, compiler_params=_params(("arbitrary",)))(*args), None
    n_in, n_out, n_scr, n_c = len(in_specs), len(out_shape), len(scratch_shapes), len(comm.arrays)
    n_co = len(comm.out_shape)
    steps = grid[0]

    def hosted(*refs):
        ins, cins = refs[:n_in], refs[n_in:n_in + n_c]
        outs = refs[n_in + n_c:n_in + n_c + n_out]
        couts = refs[n_in + n_c + n_out:n_in + n_c + n_out + n_co]
        scr = refs[n_in + n_c + n_out + n_co:n_in + n_c + n_out + n_co + n_scr]
        cscr = refs[n_in + n_c + n_out + n_co + n_scr:]

        @pl.when(pl.program_id(0) == 0)
        def _():
            comm.start(cins, couts, cscr)

        body(*ins, *outs, *scr)

        @pl.when(pl.program_id(0) == max(steps - 1 - ADVANCE_STEPS, 0))
        def _():
            comm.advance(cins, couts, cscr)

        @pl.when(pl.program_id(0) == steps - 1)
        def _():
            comm.wait(cins, couts, cscr)

    res = pl.pallas_call(
        hosted, name=name, grid=grid, in_specs=list(in_specs) + [HBM_SPEC] * n_c,
        out_specs=tuple(out_specs) + (HBM_SPEC,) * n_co, out_shape=tuple(out_shape) + tuple(comm.out_shape),
        scratch_shapes=list(scratch_shapes) + comm.scratch, compiler_params=_params(("arbitrary",)),
    )(*args, *comm.arrays)
    return res[:n_out], res[n_out:]


def _sum_parts(own, recv, name):
    n, w = own.shape
    tn = _tile(n, 256, 16)

    def body(own_ref, p_ref, o_ref):
        acc = own_ref[...]
        for d in range(N_DEV - 1):
            acc = acc + p_ref[d].astype(F32)
        o_ref[...] = acc

    return pl.pallas_call(
        body, name=name, out_shape=jax.ShapeDtypeStruct((n, w), F32), grid=(n // tn,),
        in_specs=[pl.BlockSpec((tn, w), lambda i: (i, 0)), pl.BlockSpec((N_DEV - 1, tn, w), lambda i: (0, i, 0))],
        out_specs=pl.BlockSpec((tn, w), lambda i: (i, 0)),
        compiler_params=_params(("parallel",)),
    )(own, recv)


def _sum8(parts, name):
    _, n, w = parts.shape
    tn = _tile(n, 256 if w > 128 else 1024, 8)

    def body(p_ref, o_ref):
        acc = p_ref[0]
        for d in range(1, N_DEV):
            acc = acc + p_ref[d]
        o_ref[...] = acc

    return pl.pallas_call(
        body, name=name, out_shape=jax.ShapeDtypeStruct((n, w), F32), grid=(n // tn,),
        in_specs=[pl.BlockSpec((N_DEV, tn, w), lambda i: (0, i, 0))],
        out_specs=pl.BlockSpec((tn, w), lambda i: (i, 0)),
        compiler_params=_params(("parallel",)),
    )(parts)


def _pack(arrays, dtype, row_mult):
    flat = jnp.concatenate([a.astype(dtype).reshape(-1) for a in arrays])
    pad = (-flat.shape[0]) % (128 * row_mult)
    if pad:
        flat = jnp.concatenate([flat, jnp.zeros((pad,), dtype)])
    return flat.reshape(-1, 128)


def _unpack(flat2d, shapes):
    lead = flat2d.shape[:-2]
    flat = flat2d.reshape(lead + (-1,))
    out, off = [], 0
    for s in shapes:
        size = 1
        for d in s:
            size *= d
        out.append(flat[..., off:off + size].reshape(lead + tuple(s)))
        off += size
    return out


def _blocks_to_full(blocks, axis):
    moved = jnp.moveaxis(blocks, 0, axis)
    shape = list(moved.shape)
    shape[axis:axis + 2] = [shape[axis] * shape[axis + 1]]
    return moved.reshape(shape)


def _weight_scratch(keys):
    return ([pltpu.VMEM((N_DEV * W_ROWS[k][1], D_MODEL), BF) for k in keys]
            + [pltpu.SemaphoreType.DMA((N_DEV * len(keys),))])


def _fetch_weights(slabs_ref, keys, bufs, sems, base=0):
    @pl.when(pl.program_id(0) == 0)
    def _():
        copies = []
        for ki, (key, buf) in enumerate(zip(keys, bufs)):
            row0, rows = W_ROWS[key][0] - base, W_ROWS[key][1]
            for j in range(N_DEV):
                copies.append(pltpu.make_async_copy(slabs_ref.at[j, pl.ds(row0, rows)],
                                                    buf.at[pl.ds(j * rows, rows)], sems.at[ki * N_DEV + j]))
        for cp in copies:
            cp.start()
        for cp in copies:
            cp.wait()


HBM_SPEC = pl.BlockSpec(memory_space=pl.ANY)


def _norm_matmul(h, gain, slabs, keys, out_dtype, name, comm=None):
    t_rows, d = h.shape
    widths = [N_DEV * W_ROWS[k][1] for k in keys]
    tm = _tile(t_rows, ROW_TILE)

    def body(h_ref, g_ref, slabs_ref, o_ref, *scratch):
        bufs, sems = scratch[:-1], scratch[-1]
        _fetch_weights(slabs_ref, keys, bufs, sems, slabs[1])
        u = _rms_fwd(h_ref[...], g_ref[...]).astype(BF)
        off = 0
        for buf, n in zip(bufs, widths):
            o_ref[:, off:off + n] = _dot_nt(u, buf[...]).astype(out_dtype)
            off += n

    outs, moved = _hosted_call(
        body, comm, name, (t_rows // tm,),
        [pl.BlockSpec((tm, d), lambda i: (i, 0)), _whole((1, d)), HBM_SPEC], (h, gain, slabs[0]),
        (jax.ShapeDtypeStruct((t_rows, sum(widths)), out_dtype),),
        (pl.BlockSpec((tm, sum(widths)), lambda i: (i, 0)),), _weight_scratch(keys))
    return outs[0], moved


def _chunk_masks():
    row = lax.broadcasted_iota(jnp.int32, (CHUNK, CHUNK), 0)
    col = lax.broadcasted_iota(jnp.int32, (CHUNK, CHUNK), 1)
    tri_lo = (col <= row).astype(BF)
    tri_up = (col >= row).astype(BF)
    rb = jnp.right_shift(row, 4)
    cb = jnp.right_shift(col, 4)
    diag = (rb == cb) & (col <= row)
    off = [(rb == i) & (col < SUB * i) for i in range(1, CHUNK // SUB)]
    return tri_lo, tri_up, diag, off


def _chunk_gates(fl, lb):
    sg = jax.nn.sigmoid(fl)
    s2 = jax.nn.sigmoid(-fl)
    f = lb + (1.0 - lb) * sg
    lf = jnp.log(jnp.maximum(f, LOG_FLOOR))
    k = (1.0 - lb) * s2
    return sg, s2, f, lf, k


def _chunk_factors(q, k, b):
    nb = CHUNK // SUB
    rq = jnp.concatenate([jnp.broadcast_to(b[SUB * i:SUB * i + 1], (SUB, HEAD_DIM)) for i in range(nb)], axis=0)
    mq = jnp.concatenate([jnp.broadcast_to(b[SUB * i + SUB // 2:SUB * i + SUB // 2 + 1], (SUB, HEAD_DIM))
                          for i in range(nb)], axis=0)
    e_qs = jnp.exp(b - rq)
    e_qm = jnp.exp(jnp.minimum(b - mq, EXP_CLAMP))
    e_km = jnp.exp(jnp.minimum(mq - b, EXP_CLAMP))
    e_ks = [jnp.exp(jnp.minimum(b[SUB * i:SUB * i + 1] - b, 0.0)) for i in range(1, nb)]
    qs = (q * e_qs).astype(BF)
    qm = (q * e_qm).astype(BF)
    km = (k * e_km).astype(BF)
    ks = [(k * e).astype(BF) for e in e_ks]
    return e_qs, e_qm, e_km, e_ks, qs, qm, km, ks


def _chunk_scores(qs, qm, km, ks, diag, off):
    a = jnp.where(diag, _dot_nt(qm, km), 0.0)
    for m, kk in zip(off, ks):
        a = jnp.where(m, _dot_nt(qs, kk), a)
    return a


def _hgrn_fwd(proj, lower_bounds, layer, name, comm=None):
    t_rows = proj.shape[0]
    tm = _tile(t_rows, ROW_TILE, CHUNK)
    nct = tm // CHUNK

    def body(q_ref, f_ref, i_ref, lb_ref, o_ref, sall_ref, st_ref):
        @pl.when(pl.program_id(0) == 0)
        def _():
            st_ref[...] = jnp.zeros_like(st_ref)

        lbs = _layer_lower_bound(lb_ref, layer)
        tri_lo, _, diag, off = _chunk_masks()

        group = HGRN_UNROLL if nct % HGRN_UNROLL == 0 else 1

        def chunk_group(gi, carry):
            pairs = [(u, hh) for u in range(group) for hh in range(HEADS)]
            rows = [pl.ds(pl.multiple_of((gi * group + u) * CHUNK, SUB), CHUNK) for u in range(group)]
            cols = [slice(hh * HEAD_DIM, (hh + 1) * HEAD_DIM) for hh in range(HEADS)]
            q = {p: q_ref[rows[p[0]], cols[p[1]]] for p in pairs}
            vb = {p: i_ref[rows[p[0]], cols[p[1]]].astype(BF) for p in pairs}
            gates = {p: _chunk_gates(f_ref[rows[p[0]], cols[p[1]]], lbs[:, cols[p[1]]]) for p in pairs}
            b = {p: _tri_mm(tri_lo, gates[p][3]) for p in pairs}
            fac = {p: _chunk_factors(q[p], gates[p][4], b[p]) for p in pairs}
            a = {p: _chunk_scores(*fac[p][4:], diag, off) for p in pairs}
            o_intra = {p: _dot(a[p].astype(BF), vb[p]) for p in pairs}
            upd = {p: _dot_tn(vb[p], (gates[p][4] * jnp.exp(b[p][CHUNK - 1:CHUNK] - b[p])).astype(BF)) for p in pairs}
            qc = {p: (q[p] * jnp.exp(b[p])).astype(BF) for p in pairs}
            for hh in range(HEADS):
                st = st_ref[hh]
                for u in range(group):
                    p = (u, hh)
                    sall_ref[gi * group + u, hh] = st
                    o_ref[rows[u], cols[hh]] = _dot_nt(qc[p], st.astype(BF)) + o_intra[p]
                    st = st * jnp.exp(b[p][CHUNK - 1:CHUNK]) + upd[p]
                st_ref[hh] = st
            return carry

        lax.fori_loop(0, nct // group, chunk_group, 0)

    outs, moved = _hosted_call(
        body, comm, name, (t_rows // tm,),
        [_cols(tm, 512, 0), _cols(tm, 512, 1), _cols(tm, 512, 2), _whole((DEPTH, HGRN_W))],
        (proj, proj, proj, lower_bounds),
        (jax.ShapeDtypeStruct((t_rows, HGRN_W), F32),
         jax.ShapeDtypeStruct((t_rows // CHUNK, HEADS, HEAD_DIM, HEAD_DIM), F32)),
        (pl.BlockSpec((tm, HGRN_W), lambda i: (i, 0)),
         pl.BlockSpec((nct, HEADS, HEAD_DIM, HEAD_DIM), lambda i: (i, 0, 0, 0))),
        [pltpu.VMEM((HEADS, HEAD_DIM, HEAD_DIM), F32)])
    return (*outs, moved)


def _head_norm(o):
    outs, rs = [], []
    for hh in range(HEADS):
        oh = o[:, hh * HEAD_DIM:(hh + 1) * HEAD_DIM]
        r = lax.rsqrt(jnp.mean(oh * oh, axis=-1, keepdims=True) + EPS)
        outs.append(oh * r)
        rs.append(r)
    return outs, rs


def _window_counts(row0, rows):
    t1 = (row0 + lax.broadcasted_iota(jnp.int32, (rows, 1), 0) + 1).astype(F32)
    return [1.0 / jnp.minimum(t1, float(w)) for w in POOL_WINDOWS]


def _pool_fwd(v, halo, row0):
    rows = v.shape[0]
    inv = _window_counts(row0, rows)
    outs = []
    for gi, w in enumerate(POOL_WINDOWS):
        cs = slice(gi * HEAD_DIM, (gi + 1) * HEAD_DIM)
        s = jnp.concatenate([halo[:, cs], v[:, cs]], axis=0)
        step = 1
        while step < w:
            s = s + pltpu.roll(s, step, 0)
            step *= 2
        outs.append(s[HALO:] * inv[gi] - v[:, cs])
    return outs


def _pool_bwd(dpooled, halo, row0):
    rows = dpooled[0].shape[0]
    inv = _window_counts(row0, rows)
    inv_h = _window_counts(row0 + rows, HALO)
    outs = []
    for gi, w in enumerate(POOL_WINDOWS):
        s = jnp.concatenate([dpooled[gi] * inv[gi], halo[gi] * inv_h[gi]], axis=0)
        n_ext = s.shape[0]
        step = 1
        while step < w:
            s = s + pltpu.roll(s, n_ext - step, 0)
            step *= 2
        outs.append(s[:rows] - dpooled[gi])
    return outs


def _pool_project(pooled, pp_ref, scale):
    y = jnp.concatenate([_dot(pooled[gi].astype(BF), pp_ref[gi].astype(BF)) for gi in range(4)], axis=1)
    return y, y * scale


def _mix_merge(proj, o, h, out_gain, pool_proj, pool_scale, slabs, gain_post, name, comm=None):
    t_rows = h.shape[0]
    tm = _tile(t_rows, ROW_TILE)
    hpt = tm // HALO
    keys = ['wb', 'w_out']

    def body(g_ref, v_ref, vh_ref, ga0_ref, ga1_ref, gb0_ref, gb1_ref, o_ref, h_ref, og_ref, pp_ref, ps_ref,
             slabs_ref, gp_ref, hn_ref, r_ref, a_ref, p_ref, wb_ref, wo_ref, sems):
        _fetch_weights(slabs_ref, keys, (wb_ref, wo_ref), sems, slabs[1])
        i = pl.program_id(0)
        on, _ = _head_norm(o_ref[...])
        a = jnp.concatenate(on, axis=1) * og_ref[...] * jax.nn.sigmoid(g_ref[...])
        halo = jnp.where(i == 0, 0.0, vh_ref[...])
        pooled = _pool_fwd(v_ref[...], halo, i * tm)
        _, p = _pool_project(pooled, pp_ref, ps_ref[...])
        ab, pb = a.astype(BF), p.astype(BF)
        sa = jax.nn.sigmoid(jnp.concatenate([ga0_ref[...], ga1_ref[...]], axis=1))
        sb = jax.nn.sigmoid(jnp.concatenate([gb0_ref[...], gb1_ref[...]], axis=1))
        z = sa * _dot_nt(ab, wb_ref[:, 0:HGRN_W]) + sb * _dot_nt(pb, wb_ref[:, HGRN_W:HGRN_W + POOL_W])
        r = _dot(z.astype(BF), wo_ref[...])
        hn_ref[...] = h_ref[...] + _rms_fwd(r, gp_ref[...])
        r_ref[...] = r
        a_ref[...] = ab
        p_ref[...] = pb

    rows = lambda width: pl.BlockSpec((tm, width), lambda i: (i, 0))
    outs, moved = _hosted_call(
        body, comm, name, (t_rows // tm,),
        in_specs=[_cols(tm, 512, 3), _cols(tm, 512, 4),
                  pl.BlockSpec((HALO, 512), lambda i: (jnp.maximum(i * hpt - 1, 0), 4)),
                  _cols(tm, 512, 5), _cols(tm, 512, 6), _cols(tm, 512, 7), _cols(tm, 512, 8),
                  rows(HGRN_W), rows(D_MODEL), _whole((1, HGRN_W)), _whole((4, HEAD_DIM, HEAD_DIM)),
                  _whole((1, POOL_W)), HBM_SPEC, _whole((1, D_MODEL))],
        args=(proj, proj, proj, proj, proj, proj, proj, o, h, out_gain, pool_proj, pool_scale, slabs[0], gain_post),
        out_shape=(jax.ShapeDtypeStruct((t_rows, D_MODEL), F32), jax.ShapeDtypeStruct((t_rows, D_MODEL), F32),
                   jax.ShapeDtypeStruct((t_rows, HGRN_W), BF), jax.ShapeDtypeStruct((t_rows, POOL_W), BF)),
        out_specs=(rows(D_MODEL), rows(D_MODEL), rows(HGRN_W), rows(POOL_W)),
        scratch_shapes=_weight_scratch(keys))
    return (*outs, moved)


def _conv_fwd(g, halo, cw):
    ext = jnp.concatenate([halo, g], axis=0)
    return (cw[0:1] * pltpu.roll(ext, 2, 0)[8:] + cw[1:2] * pltpu.roll(ext, 1, 0)[8:] + cw[2:3] * g)


def _ffn_down(gu, h, conv_w, conv_b, slabs, gain_post, name):
    t_rows = h.shape[0]
    tm = _tile(t_rows, FFN_TILE)
    hpt = tm // HALO

    def body(g_ref, gh_ref, up_ref, h_ref, cw_ref, cb_ref, slabs_ref, gp_ref, hn_ref, y_ref, gl_ref, gg_ref,
             wd_ref, sems):
        _fetch_weights(slabs_ref, ['wd'], (wd_ref,), sems, slabs[1])
        i = pl.program_id(0)
        halo = jnp.where(i == 0, 0.0, gh_ref[8:HALO, :].astype(F32))
        gt = _conv_fwd(g_ref[...].astype(F32), halo, cw_ref[...]) + cb_ref[...]
        gl, gg = _gelu_and_grad(gt)
        gl_ref[...] = gl.astype(BF)
        gg_ref[...] = gg.astype(BF)
        act = gl * up_ref[...].astype(F32)
        y = _dot(act.astype(BF), wd_ref[...])
        hn_ref[...] = h_ref[...] + _rms_fwd(y, gp_ref[...])
        y_ref[...] = y

    rows = lambda width: pl.BlockSpec((tm, width), lambda i: (i, 0))
    return pl.pallas_call(
        body, name=name,
        out_shape=(jax.ShapeDtypeStruct((t_rows, D_MODEL), F32), jax.ShapeDtypeStruct((t_rows, D_MODEL), F32),
                   jax.ShapeDtypeStruct((t_rows, D_FF), BF), jax.ShapeDtypeStruct((t_rows, D_FF), BF)),
        grid=(t_rows // tm,),
        in_specs=[_cols(tm, D_FF, 0), pl.BlockSpec((HALO, D_FF), lambda i: (jnp.maximum(i * hpt - 1, 0), 0)),
                  _cols(tm, D_FF, 1), rows(D_MODEL), _whole((3, D_FF)), _whole((1, D_FF)),
                  HBM_SPEC, _whole((1, D_MODEL))],
        out_specs=(rows(D_MODEL), rows(D_MODEL), rows(D_FF), rows(D_FF)),
        scratch_shapes=_weight_scratch(['wd']),
        compiler_params=_params(("arbitrary",)),
    )(gu, gu, gu, h, conv_w, conv_b, slabs[0], gain_post)


def _loss_grad(h, target, name):
    t_rows, d = h.shape
    tm = _tile(t_rows, ROW_TILE)

    def body(h_ref, t_ref, dh_ref, l_ref):
        i = pl.program_id(0)

        @pl.when(i == 0)
        def _():
            l_ref[...] = jnp.zeros_like(l_ref)

        row = i * tm + lax.broadcasted_iota(jnp.int32, (tm, 1), 0)
        err = jnp.where(row >= N_META, h_ref[...] - t_ref[...], 0.0)
        dh_ref[...] = err * (1.0 / d)
        l_ref[...] += jnp.sum(err * err, axis=0, keepdims=True) * (0.5 / d)

    rows = pl.BlockSpec((tm, d), lambda i: (i, 0))
    return pl.pallas_call(
        body, name=name,
        out_shape=(jax.ShapeDtypeStruct((t_rows, d), F32), jax.ShapeDtypeStruct((1, d), F32)),
        grid=(t_rows // tm,), in_specs=[rows, rows], out_specs=(rows, _whole((1, d))),
        compiler_params=_params(("arbitrary",)),
    )(h, target)


def _dw(lhs, rhs, name, gain=None, comm=None):
    t_rows = lhs[0].shape[0]
    paired = len(rhs) > 1
    n_rows = lhs[0].shape[1] if paired else sum(x.shape[1] for x in lhs)
    n_cols = sum(x.shape[1] for x in rhs)
    row_bytes = 2 * sum(x.shape[1] * x.dtype.itemsize for x in list(lhs) + list(rhs))
    tm = _tile(t_rows, max(ROW_TILE, (DW_BUFFER_BYTES - 4 * n_rows * n_cols) // row_bytes))
    last = t_rows // tm - 1

    def body(*refs):
        lhs_refs = refs[:len(lhs)]
        rhs_refs = refs[len(lhs):len(lhs) + len(rhs)]
        rest = refs[len(lhs) + len(rhs):]
        if gain is not None:
            g_ref, o_ref, o16_ref, acc, stage = rest
            rv = [_rms_fwd(rhs_refs[0][...], g_ref[...]).astype(BF)]
        else:
            o_ref, o16_ref, acc, stage = rest
            rv = [r[...] for r in rhs_refs]
        i = pl.program_id(0)

        @pl.when(i == 0)
        def _():
            acc[...] = jnp.zeros_like(acc)

        r0, c0 = 0, 0
        for p, l_ref in enumerate(lhs_refs):
            r = rv[p] if paired else rv[0]
            n = l_ref.shape[1]
            step = 512 if n % 512 == 0 else 256
            for s in range(0, n, step):
                acc[r0 + s:r0 + s + step, c0:c0 + r.shape[1]] += _dot_tn(l_ref[:, s:s + step], r)
            if paired:
                c0 += r.shape[1]
            else:
                r0 += n

        @pl.when(i == last)
        def _():
            pltpu.sync_copy(acc, o_ref)
            for s in range(0, n_rows, 256):
                stage[...] = acc[s:s + 256, :].astype(BF)
                pltpu.sync_copy(stage, o16_ref.at[pl.ds(s, 256)])

    rows = lambda x: pl.BlockSpec((tm, x.shape[1]), lambda i: (i, 0))
    in_specs = [rows(x) for x in lhs] + [rows(x) for x in rhs]
    args = list(lhs) + list(rhs)
    if gain is not None:
        in_specs.append(_whole(gain.shape))
        args.append(gain)
    outs, moved = _hosted_call(
        body, comm, name, (t_rows // tm,), in_specs, args,
        (jax.ShapeDtypeStruct((n_rows, n_cols), F32), jax.ShapeDtypeStruct((n_rows, n_cols), BF)),
        (HBM_SPEC, HBM_SPEC), [pltpu.VMEM((n_rows, n_cols), F32), pltpu.VMEM((256, n_cols), BF)])
    return (outs[0], outs[1]), moved


def _ffn_bwd(dh, y, gain_post, gu, gl, gg, conv_w, slabs_down, slabs_up, h, gain_pre, name, comm=None):
    t_rows = dh.shape[0]
    tm = _tile(t_rows, FFN_TILE)
    hpt = tm // HALO
    n_tiles = t_rows // tm

    def body(dh_ref, y_ref, gp_ref, g_ref, gh_ref, up_ref, gl_ref, gg_ref, cw_ref, h_ref, gpre_ref, sd_ref, su_ref,
             dhm_ref, dy_ref, act_ref, dgu_ref, dgp_ref, dcw_ref, dcb_ref, dgpre_ref,
             wd_ref, sems_d, wg_ref, wu_ref, sems_u, carry_ref):
        _fetch_weights(sd_ref, ['wd'], (wd_ref,), sems_d, slabs_down[1])
        _fetch_weights(su_ref, ['wg', 'wu'], (wg_ref, wu_ref), sems_u, slabs_up[1])
        i = pl.program_id(0)

        @pl.when(i == 0)
        def _():
            dgp_ref[...] = jnp.zeros_like(dgp_ref)
            dcw_ref[...] = jnp.zeros_like(dcw_ref)
            dcb_ref[...] = jnp.zeros_like(dcb_ref)
            dgpre_ref[...] = jnp.zeros_like(dgpre_ref)
            carry_ref[...] = jnp.zeros_like(carry_ref)

        dh = dh_ref[...]
        dy, dgp = _rms_bwd(dh, y_ref[...], gp_ref[...])
        dgp_ref[...] += dgp
        dyb = dy.astype(BF)
        dy_ref[...] = dyb
        g = g_ref[...].astype(F32)
        up = up_ref[...].astype(F32)
        first_tile = i == n_tiles - 1
        ext = jnp.concatenate([jnp.where(first_tile, 0.0, gh_ref[8:HALO, :].astype(F32)), g], axis=0)
        g2 = pltpu.roll(ext, 2, 0)[8:]
        g1 = pltpu.roll(ext, 1, 0)[8:]
        gl = gl_ref[...].astype(F32)
        act_ref[...] = (gl * up).astype(BF)
        dact = _dot_nt(dyb, wd_ref[...])
        dub = (dact * gl).astype(BF)
        dgt = dact * up * gg_ref[...].astype(F32)
        dcb_ref[...] += jnp.sum(dgt, axis=0, keepdims=True)
        dcw_ref[...] += jnp.concatenate([jnp.sum(dgt * g2, axis=0, keepdims=True),
                                         jnp.sum(dgt * g1, axis=0, keepdims=True),
                                         jnp.sum(dgt * g, axis=0, keepdims=True)], axis=0)
        after = jnp.concatenate([dgt, carry_ref[...]], axis=0)
        carry_ref[...] = dgt[0:8, :]
        cw = cw_ref[...]
        dg = (cw[2:3] * dgt + cw[1:2] * pltpu.roll(after, tm + 8 - 1, 0)[:tm]
              + cw[0:1] * pltpu.roll(after, tm + 8 - 2, 0)[:tm])
        dgb = dg.astype(BF)
        dgu_ref[:, 0:D_FF] = dgb
        dgu_ref[:, D_FF:2 * D_FF] = dub
        du = _dot(dgb, wg_ref[...]) + _dot(dub, wu_ref[...])
        dx, dgain = _rms_bwd(du, h_ref[...], gpre_ref[...])
        dhm_ref[...] = dh + dx
        dgpre_ref[...] += dgain

    rev = lambda width, j=0: pl.BlockSpec((tm, width), lambda i, j=j: (n_tiles - 1 - i, j))
    bf = lambda width: jax.ShapeDtypeStruct((t_rows, width), BF)
    outs, moved = _hosted_call(
        body, comm, name, (n_tiles,),
        [rev(D_MODEL), rev(D_MODEL), _whole((1, D_MODEL)), rev(D_FF, 0),
         pl.BlockSpec((HALO, D_FF), lambda i: (jnp.maximum((n_tiles - 1 - i) * hpt - 1, 0), 0)), rev(D_FF, 1),
         rev(D_FF), rev(D_FF), _whole((3, D_FF)), rev(D_MODEL), _whole((1, D_MODEL)), HBM_SPEC, HBM_SPEC],
        (dh, y, gain_post, gu, gu, gu, gl, gg, conv_w, h, gain_pre, slabs_down[0], slabs_up[0]),
        (jax.ShapeDtypeStruct((t_rows, D_MODEL), F32), bf(D_MODEL), bf(D_FF), bf(2 * D_FF),
         jax.ShapeDtypeStruct((1, D_MODEL), F32), jax.ShapeDtypeStruct((3, D_FF), F32),
         jax.ShapeDtypeStruct((1, D_FF), F32), jax.ShapeDtypeStruct((1, D_MODEL), F32)),
        (rev(D_MODEL), rev(D_MODEL), rev(D_FF), rev(2 * D_FF), _whole((1, D_MODEL)), _whole((3, D_FF)),
         _whole((1, D_FF)), _whole((1, D_MODEL))),
        _weight_scratch(['wd']) + _weight_scratch(['wg', 'wu']) + [pltpu.VMEM((8, D_FF), F32)])
    return (*outs, moved)


def _mix_bwd_a(dh, r, gain_post, proj, a, p, slabs, name, comm=None):
    t_rows = dh.shape[0]
    tm = _tile(t_rows, ROW_TILE)

    def body(dh_ref, r_ref, gp_ref, ga0_ref, ga1_ref, gb0_ref, gb1_ref, a_ref, p_ref, slabs_ref,
             dr_ref, z_ref, dya_ref, dyp_ref, da_ref, dp_ref, dgab_ref, dg_ref, wb_ref, wo_ref, sems):
        _fetch_weights(slabs_ref, ['wb', 'w_out'], (wb_ref, wo_ref), sems, slabs[1])

        @pl.when(pl.program_id(0) == 0)
        def _():
            dg_ref[...] = jnp.zeros_like(dg_ref)

        dr, dgain = _rms_bwd(dh_ref[...], r_ref[...], gp_ref[...])
        dg_ref[...] += dgain
        drb = dr.astype(BF)
        dr_ref[...] = drb
        dz = _dot_nt(drb, wo_ref[...])
        ya = _dot_nt(a_ref[...], wb_ref[:, 0:HGRN_W])
        yp = _dot_nt(p_ref[...], wb_ref[:, HGRN_W:HGRN_W + POOL_W])
        sa = jax.nn.sigmoid(jnp.concatenate([ga0_ref[...], ga1_ref[...]], axis=1))
        sb = jax.nn.sigmoid(jnp.concatenate([gb0_ref[...], gb1_ref[...]], axis=1))
        z_ref[...] = (sa * ya + sb * yp).astype(BF)
        dgab_ref[:, 0:D_MODEL] = (dz * ya * sa * (1.0 - sa)).astype(BF)
        dgab_ref[:, D_MODEL:2 * D_MODEL] = (dz * yp * sb * (1.0 - sb)).astype(BF)
        dya = (dz * sa).astype(BF)
        dyp = (dz * sb).astype(BF)
        dya_ref[...] = dya
        dyp_ref[...] = dyp
        da_ref[...] = _dot(dya, wb_ref[:, 0:HGRN_W])
        dp_ref[...] = _dot(dyp, wb_ref[:, HGRN_W:HGRN_W + POOL_W])

    rows = lambda width: pl.BlockSpec((tm, width), lambda i: (i, 0))
    bf = lambda width: jax.ShapeDtypeStruct((t_rows, width), BF)
    f32 = lambda width: jax.ShapeDtypeStruct((t_rows, width), F32)
    outs, moved = _hosted_call(
        body, comm, name, (t_rows // tm,),
        [rows(D_MODEL), rows(D_MODEL), _whole((1, D_MODEL)),
         _cols(tm, 512, 5), _cols(tm, 512, 6), _cols(tm, 512, 7), _cols(tm, 512, 8),
         rows(HGRN_W), rows(POOL_W), HBM_SPEC],
        (dh, r, gain_post, proj, proj, proj, proj, a, p, slabs[0]),
        (bf(D_MODEL), bf(D_MODEL), bf(D_MODEL), bf(D_MODEL), f32(HGRN_W), f32(POOL_W), bf(2 * D_MODEL),
         jax.ShapeDtypeStruct((1, D_MODEL), F32)),
        (rows(D_MODEL), rows(D_MODEL), rows(D_MODEL), rows(D_MODEL), rows(HGRN_W), rows(POOL_W),
         rows(2 * D_MODEL), _whole((1, D_MODEL))),
        _weight_scratch(['wb', 'w_out']))
    return (*outs, moved)


def _mix_bwd_b(da, dp, proj, o, out_gain, pool_proj, pool_scale, name):
    t_rows = da.shape[0]
    tm = _tile(t_rows, ROW_TILE)
    hpt = tm // HALO
    last = t_rows // tm - 1

    def body(da_ref, dp_ref, dpn_ref, g_ref, v_ref, vh_ref, o_ref, og_ref, pp_ref, ps_ref,
             do_ref, dgv_ref, dog_ref, dpp_ref, dps_ref):
        i = pl.program_id(0)

        @pl.when(i == 0)
        def _():
            dog_ref[...] = jnp.zeros_like(dog_ref)
            dpp_ref[...] = jnp.zeros_like(dpp_ref)
            dps_ref[...] = jnp.zeros_like(dps_ref)

        da = da_ref[...]
        og = og_ref[...]
        o = o_ref[...]
        on, rs = _head_norm(o)
        onc = jnp.concatenate(on, axis=1)
        sg = jax.nn.sigmoid(g_ref[...])
        dog_ref[...] += jnp.sum(da * onc * sg, axis=0, keepdims=True)
        dgv_ref[:, 0:HGRN_W] = (da * onc * og * sg * (1.0 - sg)).astype(BF)
        don = da * og * sg
        for hh in range(HEADS):
            cs = slice(hh * HEAD_DIM, (hh + 1) * HEAD_DIM)
            d = don[:, cs]
            do_ref[:, cs] = rs[hh] * (d - on[hh] * jnp.mean(d * on[hh], axis=-1, keepdims=True))

        scale = ps_ref[...]
        halo = jnp.where(i == 0, 0.0, vh_ref[...])
        pooled = _pool_fwd(v_ref[...], halo, i * tm)
        y, _ = _pool_project(pooled, pp_ref, scale)
        dp = dp_ref[...]
        dps_ref[...] += jnp.sum(dp * y, axis=0, keepdims=True)
        dy = dp * scale
        dyn = jnp.where(i == last, 0.0, dpn_ref[...]) * scale
        dpooled, dhalo = [], []
        for gi in range(4):
            cs = slice(gi * HEAD_DIM, (gi + 1) * HEAD_DIM)
            ppb = pp_ref[gi].astype(BF)
            dyb = dy[:, cs].astype(BF)
            dpooled.append(_dot_nt(dyb, ppb))
            dhalo.append(_dot_nt(dyn[:, cs].astype(BF), ppb))
            dpp_ref[gi] += _dot_tn(pooled[gi].astype(BF), dyb)
        dv = _pool_bwd(dpooled, dhalo, i * tm)
        dgv_ref[:, HGRN_W:HGRN_W + POOL_W] = jnp.concatenate(dv, axis=1).astype(BF)

    rows = lambda width: pl.BlockSpec((tm, width), lambda i: (i, 0))
    return pl.pallas_call(
        body, name=name,
        out_shape=(jax.ShapeDtypeStruct((t_rows, HGRN_W), F32), jax.ShapeDtypeStruct((t_rows, HGRN_W + POOL_W), BF),
                   jax.ShapeDtypeStruct((1, HGRN_W), F32), jax.ShapeDtypeStruct((4, HEAD_DIM, HEAD_DIM), F32),
                   jax.ShapeDtypeStruct((1, POOL_W), F32)),
        grid=(t_rows // tm,),
        in_specs=[rows(HGRN_W), rows(POOL_W),
                  pl.BlockSpec((HALO, POOL_W), lambda i: (jnp.minimum((i + 1) * hpt, (last + 1) * hpt - 1), 0)),
                  _cols(tm, 512, 3), _cols(tm, 512, 4),
                  pl.BlockSpec((HALO, 512), lambda i: (jnp.maximum(i * hpt - 1, 0), 4)),
                  rows(HGRN_W), _whole((1, HGRN_W)), _whole((4, HEAD_DIM, HEAD_DIM)), _whole((1, POOL_W))],
        out_specs=(rows(HGRN_W), rows(HGRN_W + POOL_W), _whole((1, HGRN_W)), _whole((4, HEAD_DIM, HEAD_DIM)),
                   _whole((1, POOL_W))),
        compiler_params=_params(("arbitrary",)),
    )(da, dp, dp, proj, proj, proj, o, out_gain, pool_proj, pool_scale)


def _hgrn_bwd(proj, lower_bounds, layer, states, do, name, comm=None):
    t_rows = proj.shape[0]
    tm = _tile(t_rows, ROW_TILE, CHUNK)
    nct = tm // CHUNK
    n_tiles = t_rows // tm

    def body(q_ref, f_ref, i_ref, lb_ref, sall_ref, do_ref, dqfi_ref, dlb_ref, dst_ref):
        @pl.when(pl.program_id(0) == 0)
        def _():
            dst_ref[...] = jnp.zeros_like(dst_ref)
            dlb_ref[...] = jnp.zeros_like(dlb_ref)

        lbs = _layer_lower_bound(lb_ref, layer)
        tri_lo, tri_up, diag, off = _chunk_masks()
        is_last_row = lax.broadcasted_iota(jnp.int32, (CHUNK, 1), 0) == CHUNK - 1

        group = HGRN_UNROLL if nct % HGRN_UNROLL == 0 else 1

        def chunk_group(gi, carry):
            pairs = [(u, hh) for u in range(group) for hh in range(HEADS)]
            cidx = [nct - 1 - (gi * group + u) for u in range(group)]
            rows = [pl.ds(pl.multiple_of(c * CHUNK, SUB), CHUNK) for c in cidx]
            cols = [slice(hh * HEAD_DIM, (hh + 1) * HEAD_DIM) for hh in range(HEADS)]
            lb = {p: lbs[:, cols[p[1]]] for p in pairs}
            q = {p: q_ref[rows[p[0]], cols[p[1]]] for p in pairs}
            vb = {p: i_ref[rows[p[0]], cols[p[1]]].astype(BF) for p in pairs}
            dob = {p: do_ref[rows[p[0]], cols[p[1]]].astype(BF) for p in pairs}
            gates = {p: _chunk_gates(f_ref[rows[p[0]], cols[p[1]]], lb[p]) for p in pairs}
            k = {p: gates[p][4] for p in pairs}
            b = {p: _tri_mm(tri_lo, gates[p][3]) for p in pairs}
            fac = {p: _chunk_factors(q[p], k[p], b[p]) for p in pairs}
            e_b = {p: jnp.exp(b[p]) for p in pairs}
            e_last = {p: jnp.exp(b[p][CHUNK - 1:CHUNK]) for p in pairs}
            e_kl = {p: jnp.exp(b[p][CHUNK - 1:CHUNK] - b[p]) for p in pairs}
            qc = {p: (q[p] * e_b[p]).astype(BF) for p in pairs}
            kdec = {p: (k[p] * e_kl[p]).astype(BF) for p in pairs}
            a = {p: _chunk_scores(*fac[p][4:], diag, off) for p in pairs}
            da_full = {p: _dot_nt(dob[p], vb[p]) for p in pairs}
            dv, dq, dk, upd, dupd = {}, {}, {}, {}, {}
            for p in pairs:
                e_qs, e_qm, e_km, e_ks, qs, qm, km, ks = fac[p]
                da_d = jnp.where(diag, da_full[p], 0.0).astype(BF)
                dv[p] = _dot_tn(a[p].astype(BF), dob[p])
                dq[p] = e_qm * _dot(da_d, km)
                dk[p] = e_km * _dot_tn(da_d, qm)
                for m, kk, ek in zip(off, ks, e_ks):
                    da_i = jnp.where(m, da_full[p], 0.0).astype(BF)
                    dq[p] = dq[p] + e_qs * _dot(da_i, kk)
                    dk[p] = dk[p] + ek * _dot_tn(da_i, qs)
                upd[p] = _dot_tn(vb[p], kdec[p])
                dupd[p] = _dot_tn(dob[p], qc[p])
            db = {}
            for hh in range(HEADS):
                dst = dst_ref[hh]
                for u in range(group):
                    p = (u, hh)
                    st = sall_ref[cidx[u], hh]
                    dstb = dst.astype(BF)
                    dv[p] = dv[p] + _dot_nt(kdec[p], dstb)
                    dq[p] = dq[p] + e_b[p] * _dot(dob[p], st.astype(BF))
                    dk[p] = dk[p] + e_kl[p] * _dot(vb[p], dstb)
                    st_new = st * e_last[p] + upd[p]
                    db[p] = (q[p] * dq[p] - k[p] * dk[p]
                             + jnp.where(is_last_row, jnp.sum(st_new * dst, axis=0, keepdims=True), 0.0))
                    dst = dst * e_last[p] + dupd[p]
                dst_ref[hh] = dst
            for p in pairs:
                u, hh = p
                sg, s2, f = gates[p][:3]
                dlf = _tri_mm(tri_up, db[p])
                df = jnp.where(f > LOG_FLOOR, dlf / f, 0.0)
                dfl = df * (1.0 - lb[p]) * sg * (1.0 - sg) - dk[p] * (1.0 - lb[p]) * s2 * (1.0 - s2)
                dlb_ref[:, cols[hh]] += jnp.sum(df * (1.0 - sg) - dk[p] * s2, axis=0, keepdims=True)
                dqfi_ref[rows[u], cols[hh]] = dq[p].astype(BF)
                dqfi_ref[rows[u], pl.ds(HGRN_W + hh * HEAD_DIM, HEAD_DIM)] = dfl.astype(BF)
                dqfi_ref[rows[u], pl.ds(2 * HGRN_W + hh * HEAD_DIM, HEAD_DIM)] = dv[p].astype(BF)
            return carry

        lax.fori_loop(0, nct // group, chunk_group, 0)

    rev = lambda width, j: pl.BlockSpec((tm, width), lambda i, j=j: (n_tiles - 1 - i, j))
    outs, exchanged = _hosted_call(
        body, comm, name, (n_tiles,),
        [rev(512, 0), rev(512, 1), rev(512, 2), _whole((DEPTH, HGRN_W)),
         pl.BlockSpec((nct, HEADS, HEAD_DIM, HEAD_DIM), lambda i: (n_tiles - 1 - i, 0, 0, 0)), rev(HGRN_W, 0)],
        (proj, proj, proj, lower_bounds, states, do),
        (jax.ShapeDtypeStruct((t_rows, 3 * HGRN_W), BF), jax.ShapeDtypeStruct((1, HGRN_W), F32)),
        (rev(3 * HGRN_W, 0), _whole((1, HGRN_W))),
        [pltpu.VMEM((HEADS, HEAD_DIM, HEAD_DIM), F32)])
    return (*outs, exchanged)


def _in_bwd(dqfi, dgv, dgab, slabs, h, gain_pre, dh_out, name, comm=None):
    t_rows = h.shape[0]
    tm = _tile(t_rows, ROW_TILE)
    c1 = 3 * HGRN_W
    c2 = c1 + HGRN_W + POOL_W

    def body(d1_ref, d2_ref, d3_ref, slabs_ref, h_ref, gp_ref, dho_ref, dh_ref, dg_ref, w_ref, sems):
        _fetch_weights(slabs_ref, ['w_in'], (w_ref,), sems, slabs[1])

        @pl.when(pl.program_id(0) == 0)
        def _():
            dg_ref[...] = jnp.zeros_like(dg_ref)

        du = (_dot(d1_ref[...], w_ref[0:c1, :]) + _dot(d2_ref[...], w_ref[c1:c2, :])
              + _dot(d3_ref[...], w_ref[c2:IN_COLS, :]))
        dx, dgain = _rms_bwd(du, h_ref[...], gp_ref[...])
        dh_ref[...] = dho_ref[...] + dx
        dg_ref[...] += dgain

    rows = lambda width: pl.BlockSpec((tm, width), lambda i: (i, 0))
    outs, exchanged = _hosted_call(
        body, comm, name, (t_rows // tm,),
        [rows(c1), rows(c2 - c1), rows(IN_COLS - c2), HBM_SPEC, rows(D_MODEL), _whole((1, D_MODEL)), rows(D_MODEL)],
        (dqfi, dgv, dgab, slabs[0], h, gain_pre, dh_out),
        (jax.ShapeDtypeStruct((t_rows, D_MODEL), F32), jax.ShapeDtypeStruct((1, D_MODEL), F32)),
        (rows(D_MODEL), _whole((1, D_MODEL))),
        _weight_scratch(['w_in']))
    return (*outs, exchanged)


def _lower_bound_grad(lower_bounds, dlbs, name):
    assert DEPTH == 2 and lower_bounds.shape[0] == DEPTH, "the softmax over layers is written out for two layers"
    def body(lb_ref, d_ref, o_ref):
        g0, g1 = _softmax2(lb_ref)
        bound = (g0 + g1) - g0
        inside = (bound > 0.0) & (bound < 1.0)
        dg1 = jnp.where(inside, d_ref[1:2, :], 0.0)
        inner = g1 * dg1
        o_ref[0:1, :] = g0 * (0.0 - inner)
        o_ref[1:2, :] = g1 * (dg1 - inner)

    return pl.pallas_call(body, name=name, out_shape=jax.ShapeDtypeStruct(lower_bounds.shape, F32))(lower_bounds, dlbs)


def _adamw(w, g, m, v, name):
    r, c = w.shape
    tr = r if (r % 8 or r <= 512) else _tile(r, 512, 8)
    c1 = 1.0 - ADAM_B1 ** ADAM_STEP
    c2 = 1.0 - ADAM_B2 ** ADAM_STEP

    def body(w_ref, g_ref, m_ref, v_ref, d_ref, nm_ref, nv_ref):
        gg = g_ref[...]
        nm = ADAM_B1 * m_ref[...] + (1.0 - ADAM_B1) * gg
        nv = ADAM_B2 * v_ref[...] + (1.0 - ADAM_B2) * (gg * gg)
        d_ref[...] = -ADAM_LR * ((nm / c1) / (jnp.sqrt(nv / c2) + ADAM_EPS) + ADAM_WD * w_ref[...])
        nm_ref[...] = nm
        nv_ref[...] = nv

    blk = pl.BlockSpec((tr, c), lambda i: (i, 0))
    shp = jax.ShapeDtypeStruct((r, c), F32)
    return pl.pallas_call(
        body, name=name, out_shape=(shp, shp, shp), grid=(r // tr,),
        in_specs=[blk, blk, blk, blk], out_specs=(blk, blk, blk),
        compiler_params=_params(("parallel",)),
    )(w, g, m, v)


def _as2d(a):
    return a.reshape(-1, a.shape[-1])


def _layer_slab(w, l):
    t = lambda a: jnp.swapaxes(a, 0, 1)
    parts = [t(w['w_in'][l]), jnp.concatenate([t(w['w_branch_hgrn'][l]), t(w['w_branch_pool'][l])], axis=1),
             w['w_out'][l], t(w['ffn_w_gate'][l]), t(w['ffn_w_up'][l]), w['ffn_w_down'][l]]
    return jnp.concatenate(parts, axis=0).astype(BF)


def _slab_grads(sums):
    t = lambda a: jnp.swapaxes(a, 0, 1)
    wb = sums['wb']
    return {'w_in': t(sums['w_in']), 'w_branch_hgrn': t(wb[:, :HGRN_W]), 'w_branch_pool': t(wb[:, HGRN_W:]),
            'w_out': sums['w_out'], 'ffn_w_gate': t(sums['wg']), 'ffn_w_up': t(sums['wu']), 'ffn_w_down': sums['wd']}


def _train_step(x, target, w, m, v):
    my_slabs = [_layer_slab(w, l) for l in range(DEPTH)]
    n_in = W_ROWS['w_in'][1]
    slabs = [{} for _ in range(DEPTH)]
    slabs[0]['w_in'] = (_all_gather(my_slabs[0][:n_in], "gather_w_in_0"), 0)
    f32_shapes = [w[n].shape for n in F32_GATHERED]
    gathered32 = _all_gather(_pack([w[n] for n in F32_GATHERED], F32, 8), "gather_meta_conv")
    full = {n: _blocks_to_full(blk, SHARD_AXIS[n]) for n, blk in zip(F32_GATHERED, _unpack(gathered32, f32_shapes))}
    row = lambda name, l: w[name][l][None]

    h = jnp.concatenate([full['meta_tokens'], x], axis=0)
    saved = []
    for l in range(DEPTH):
        s = {'h_in': h}
        nxt = l + 1 if l + 1 < DEPTH else None
        plan = {'in_proj': [(0, ['wb', 'w_out', 'wg', 'wu'])] if l == 0 else [],
                'hgrn_fwd': [(nxt, ['w_in'])] if nxt else [],
                'mix_merge': ([(0, ['wd'])] if l == 0 else []) + ([(nxt, ['wb', 'w_out']), (nxt, ['wd'])] if nxt else []),
                'ffn_proj': [(nxt, ['wg', 'wu'])] if nxt else []}

        def hosted(kernel):
            comms = []
            for layer, keys in plan[kernel]:
                first, last = W_ROWS[keys[0]], W_ROWS[keys[-1]]
                comms.append(_SlabGather(my_slabs[layer], first[0], last[0] + last[1] - first[0]))
            combined = None
            for c in comms:
                combined = c if combined is None else _Both(combined, c)
            return combined

        def landed(kernel, moved):
            for (layer, keys), arr in zip(plan[kernel], moved or ()):
                slabs[layer].update({k: (arr, W_ROWS[keys[0]][0]) for k in keys})

        s['proj'], moved = _norm_matmul(h, row('mix_norm_pre', l), slabs[l]['w_in'], ['w_in'], F32, f"in_proj_{l}",
                                        comm=hosted('in_proj'))
        landed('in_proj', moved)
        s['o'], s['states'], moved = _hgrn_fwd(s['proj'], w['hgrn_lower_bounds'], l, f"hgrn_fwd_{l}",
                                               comm=hosted('hgrn_fwd'))
        landed('hgrn_fwd', moved)
        h, s['r'], s['a'], s['p'], moved = _mix_merge(
            s['proj'], s['o'], h, row('hgrn_out_norm', l), w['pool_proj'][l], row('pool_scale', l), slabs[l]['wb'],
            row('mix_norm_post', l), f"mix_merge_{l}", comm=hosted('mix_merge'))
        landed('mix_merge', moved)
        s['h_mid'] = h
        s['gu'], moved = _norm_matmul(h, row('ffn_norm_pre', l), slabs[l]['wg'], ['wg', 'wu'], BF, f"ffn_proj_{l}",
                                      comm=hosted('ffn_proj'))
        landed('ffn_proj', moved)
        h, s['y'], s['gl'], s['gg'] = _ffn_down(s['gu'], h, full['ffn_conv_w'][l], row('ffn_conv_b', l),
                                                slabs[l]['wd'], row('ffn_norm_post', l), f"ffn_down_{l}")
        saved.append(s)

    dh, loss_cols = _loss_grad(h, jnp.pad(target, ((N_META, 0), (0, 0))), "loss_grad")
    loss = jnp.sum(loss_cols)

    small = {n: [None] * DEPTH for n in REPLICATED + ['ffn_conv_w']}
    exchanged = [{} for _ in range(DEPTH)]
    pending = None
    for l in reversed(range(DEPTH)):
        s = saved[l]
        g = {}
        (dh, dy, act, dgu, small['ffn_norm_post'][l], small['ffn_conv_w'][l], small['ffn_conv_b'][l],
         small['ffn_norm_pre'][l], done) = _ffn_bwd(
            dh, s['y'], row('ffn_norm_post', l), s['gu'], s['gl'], s['gg'], full['ffn_conv_w'][l], slabs[l]['wd'],
            slabs[l]['wg'], s['h_mid'], row('ffn_norm_pre', l), f"ffn_bwd_{l}", comm=pending)
        if pending is not None:
            exchanged[l + 1]['in'] = done
        g['d_down'], _ = _dw([act], [dy], f"dw_down_{l}")
        g['d_gu'], exchanged[l]['wd'] = _dw([dgu], [s['h_mid']], f"dw_gate_up_{l}", gain=row('ffn_norm_pre', l),
                                            comm=_GradExchange(EXCHANGE_GROUPS['wd'], g))

        dr, z, dya, dyp, da, dp, dgab, small['mix_norm_post'][l], exchanged[l]['wg'] = _mix_bwd_a(
            dh, s['r'], row('mix_norm_post', l), s['proj'], s['a'], s['p'], slabs[l]['wb'], f"mix_bwd_a_{l}",
            comm=_GradExchange(EXCHANGE_GROUPS['wg'], g))
        g['d_out'], _ = _dw([z], [dr], f"dw_out_{l}")
        g['d_b'], _ = _dw([dya, dyp], [s['a'], s['p']], f"dw_branch_{l}")
        do, dgv, small['hgrn_out_norm'][l], small['pool_proj'][l], small['pool_scale'][l] = _mix_bwd_b(
            da, dp, s['proj'], s['o'], row('hgrn_out_norm', l), w['pool_proj'][l], row('pool_scale', l),
            f"mix_bwd_b_{l}")
        dqfi, small['hgrn_lower_bounds'][l], exchanged[l]['rest'] = _hgrn_bwd(
            s['proj'], w['hgrn_lower_bounds'], l, s['states'], do, f"hgrn_bwd_{l}",
            comm=_GradExchange(EXCHANGE_GROUPS['rest'], g))
        g['d_in'], _ = _dw([dqfi, dgv, dgab], [s['h_in']], f"dw_in_{l}", gain=row('mix_norm_pre', l))
        pending = _GradExchange(EXCHANGE_GROUPS['in'], g)
        if l == 0:
            dh, small['mix_norm_pre'][l], exchanged[l]['in'] = _in_bwd(
                dqfi, dgv, dgab, slabs[l]['w_in'], s['h_in'], row('mix_norm_pre', l), dh, f"in_bwd_{l}", comm=pending)
        else:
            dh, small['mix_norm_pre'][l], _ = _in_bwd(
                dqfi, dgv, dgab, slabs[l]['w_in'], s['h_in'], row('mix_norm_pre', l), dh, f"in_bwd_{l}")

    grad_x = dh[N_META:]

    per_layer = []
    for l in range(DEPTH):
        sums = {}
        for part, keys in EXCHANGE_GROUPS.items():
            recv, own = exchanged[l][part]
            total = _sum_parts(own, recv, f"sum_grads_{part}_{l}")
            off = 0
            for k in keys:
                sums[k] = total[off:off + W_ROWS[k][1]]
                off += W_ROWS[k][1]
        per_layer.append(_slab_grads(sums))
    grads = {n: jnp.stack([per_layer[l][n] for l in range(DEPTH)]) for n in per_layer[0]}

    stack = lambda n: jnp.stack(small[n]) if small[n][0].shape[0] != 1 else jnp.concatenate(small[n], axis=0)
    names = list(small) + ['meta_tokens']
    partial = {n: stack(n) for n in small}
    partial['meta_tokens'] = dh[:N_META]
    gathered = _all_gather(_pack([partial[n] for n in names], F32, 8), "gather_small_grads")
    rep = dict(zip(names, _unpack(_sum8(gathered, "sum_small_grads"), [partial[n].shape for n in names])))
    rep['hgrn_lower_bounds'] = _lower_bound_grad(w['hgrn_lower_bounds'], rep['hgrn_lower_bounds'], "lower_bound_grad")
    me = 4 * lax.axis_index("x") + 2 * lax.axis_index("y") + lax.axis_index("c")
    for n in F32_GATHERED:
        width = w[n].shape[SHARD_AXIS[n]]
        rep[n] = lax.dynamic_slice_in_dim(rep[n], me * width, width, axis=SHARD_AXIS[n])
    grads.update(rep)

    delta, new_m, new_v = {}, {}, {}
    for n in WEIGHT_NAMES:
        shape = w[n].shape
        d2, m2, v2 = _adamw(_as2d(w[n]), _as2d(grads[n]), _as2d(m[n]), _as2d(v[n]), f"adamw_{n}")
        delta[n], new_m[n], new_v[n] = d2.reshape(shape), m2.reshape(shape), v2.reshape(shape)
    return loss, grad_x, grads, delta, new_m, new_v


def kernel(x, meta_tokens, mix_norm_pre, mix_norm_post, w_in, hgrn_lower_bounds, hgrn_out_norm, w_branch_hgrn, pool_proj, pool_scale, w_branch_pool, w_out, ffn_norm_pre, ffn_norm_post, ffn_w_gate, ffn_w_up, ffn_conv_w, ffn_conv_b, ffn_w_down, loss_target, m_meta_tokens, m_mix_norm_pre, m_mix_norm_post, m_w_in, m_hgrn_lower_bounds, m_hgrn_out_norm, m_w_branch_hgrn, m_pool_proj, m_pool_scale, m_w_branch_pool, m_w_out, m_ffn_norm_pre, m_ffn_norm_post, m_ffn_w_gate, m_ffn_w_up, m_ffn_conv_w, m_ffn_conv_b, m_ffn_w_down, v_meta_tokens, v_mix_norm_pre, v_mix_norm_post, v_w_in, v_hgrn_lower_bounds, v_hgrn_out_norm, v_w_branch_hgrn, v_pool_proj, v_pool_scale, v_w_branch_pool, v_w_out, v_ffn_norm_pre, v_ffn_norm_post, v_ffn_w_gate, v_ffn_w_up, v_ffn_conv_w, v_ffn_conv_b, v_ffn_w_down):
    w = dict(zip(WEIGHT_NAMES, (meta_tokens, mix_norm_pre, mix_norm_post, w_in, hgrn_lower_bounds, hgrn_out_norm,
                                w_branch_hgrn, pool_proj, pool_scale, w_branch_pool, w_out, ffn_norm_pre,
                                ffn_norm_post, ffn_w_gate, ffn_w_up, ffn_conv_w, ffn_conv_b, ffn_w_down)))
    m = dict(zip(WEIGHT_NAMES, (m_meta_tokens, m_mix_norm_pre, m_mix_norm_post, m_w_in, m_hgrn_lower_bounds,
                                m_hgrn_out_norm, m_w_branch_hgrn, m_pool_proj, m_pool_scale, m_w_branch_pool, m_w_out,
                                m_ffn_norm_pre, m_ffn_norm_post, m_ffn_w_gate, m_ffn_w_up, m_ffn_conv_w,
                                m_ffn_conv_b, m_ffn_w_down)))
    v = dict(zip(WEIGHT_NAMES, (v_meta_tokens, v_mix_norm_pre, v_mix_norm_post, v_w_in, v_hgrn_lower_bounds,
                                v_hgrn_out_norm, v_w_branch_hgrn, v_pool_proj, v_pool_scale, v_w_branch_pool, v_w_out,
                                v_ffn_norm_pre, v_ffn_norm_post, v_ffn_w_gate, v_ffn_w_up, v_ffn_conv_w,
                                v_ffn_conv_b, v_ffn_w_down)))
    loss_local, grad_x, grads, delta, new_m, new_v = _train_step(x[0], loss_target[0], w, m, v)
    loss = lax.psum(loss_local, ("x", "y", "c"))
    return (loss, grad_x[None], *[grads[n] for n in WEIGHT_NAMES], *[delta[n] for n in WEIGHT_NAMES],
            *[new_m[n] for n in WEIGHT_NAMES], *[new_v[n] for n in WEIGHT_NAMES])
```

```python
import jax
import jax.numpy as jnp
from jax import lax
from jax.experimental import pallas as pl
from jax.experimental.pallas import tpu as pltpu

F32 = jnp.float32
BF = jnp.bfloat16

D_MODEL = 1024
N_META = 16
DEPTH = 2
HEADS = 4
HEAD_DIM = 128
HGRN_W = 512
POOL_W = 512
POOL_WINDOWS = (2, 4, 8, 16)
D_FF = 2816
IN_COLS = 4608
EPS = 1e-6
LOG_FLOOR = 1e-30
N_DEV = 8

ADAM_LR = 0.001
ADAM_B1 = 0.9
ADAM_B2 = 0.999
ADAM_EPS = 1e-08
ADAM_WD = 0.01
ADAM_STEP = 10

SUB = 16
CHUNK = 48
HGRN_UNROLL = 9
EXP_CLAMP = 80.0
ROW_TILE = 432
FFN_TILE = 144
HALO = 16
ADVANCE_STEPS = 2
VMEM_LIMIT = 56 * 1024 * 1024
DW_BUFFER_BYTES = 44 * 1024 * 1024
MESH = pl.DeviceIdType.MESH

WEIGHT_NAMES = ['meta_tokens', 'mix_norm_pre', 'mix_norm_post', 'w_in', 'hgrn_lower_bounds', 'hgrn_out_norm',
                'w_branch_hgrn', 'pool_proj', 'pool_scale', 'w_branch_pool', 'w_out', 'ffn_norm_pre', 'ffn_norm_post',
                'ffn_w_gate', 'ffn_w_up', 'ffn_conv_w', 'ffn_conv_b', 'ffn_w_down']
SHARD_AXIS = {'meta_tokens': 1, 'w_in': 2, 'w_branch_hgrn': 2, 'w_branch_pool': 2, 'w_out': 1,
              'ffn_w_gate': 2, 'ffn_w_up': 2, 'ffn_conv_w': 2, 'ffn_w_down': 1}
F32_GATHERED = ['meta_tokens', 'ffn_conv_w']
REPLICATED = [n for n in WEIGHT_NAMES if n not in SHARD_AXIS]

W_ROWS = {'w_in': (0, 576),
          'wb': (576, 128),
          'w_out': (704, 128),
          'wg': (832, 352),
          'wu': (1184, 352),
          'wd': (1536, 352)}
SLAB_ROWS = 1888
EXCHANGE_GROUPS = {'wd': ['wd'], 'wg': ['wg'], 'rest': ['wu', 'wb', 'w_out'], 'in': ['w_in']}


def _params(sem=None):
    return pltpu.CompilerParams(dimension_semantics=sem, vmem_limit_bytes=VMEM_LIMIT)


def _tile(total, pref, mult=16):
    best = None
    for t in range(mult, min(total, pref) + 1, mult):
        if total % t == 0:
            best = t
    assert best is not None, (total, pref, mult)
    return best


def _whole(shape):
    return pl.BlockSpec(shape, lambda *_: (0,) * len(shape))


def _cols(tm, width, j):
    return pl.BlockSpec((tm, width), lambda i, j=j: (i, j))


def _dot(a, b):
    return jnp.dot(a, b, preferred_element_type=F32)


def _dot_nt(a, b):
    return lax.dot_general(a, b, (((1,), (1,)), ((), ())), preferred_element_type=F32)


def _dot_tn(a, b):
    return lax.dot_general(a, b, (((0,), (0,)), ((), ())), preferred_element_type=F32)


def _rms_fwd(x, g):
    r = lax.rsqrt(jnp.mean(x * x, axis=-1, keepdims=True) + EPS)
    return x * r * g


def _rms_bwd(dy, x, g):
    r = lax.rsqrt(jnp.mean(x * x, axis=-1, keepdims=True) + EPS)
    xh = x * r
    dyg = dy * g
    dx = r * (dyg - xh * jnp.mean(dyg * xh, axis=-1, keepdims=True))
    return dx, jnp.sum(dy * xh, axis=0, keepdims=True)


_GELU_C = 0.7978845608028654
_GELU_A = 0.044715


def _gelu_and_grad(x):
    x2 = x * x
    t = jnp.tanh(x * (_GELU_C + (_GELU_C * _GELU_A) * x2))
    u = 1.0 + t
    hx = 0.5 * x
    return hx * u, 0.5 * u + (hx * (1.0 - t * t)) * (_GELU_C + (3.0 * _GELU_C * _GELU_A) * x2)


def _split3(x):
    x1 = x.astype(BF)
    r1 = x - x1.astype(F32)
    x2 = r1.astype(BF)
    x3 = (r1 - x2.astype(F32)).astype(BF)
    return x1, x2, x3


def _tri_mm(tri, x):
    x1, x2, x3 = _split3(x)
    return _dot(tri, x1) + _dot(tri, x2) + _dot(tri, x3)


def _softmax2(lb_ref):
    l0 = lb_ref[0:1, :]
    l1 = lb_ref[1:2, :]
    m = jnp.maximum(l0, l1)
    e0 = jnp.exp(l0 - m)
    e1 = jnp.exp(l1 - m)
    return e0 / (e0 + e1), e1 / (e0 + e1)


def _layer_lower_bound(lb_ref, layer):
    g0, g1 = _softmax2(lb_ref)
    if layer == 0:
        return jnp.clip(g0 - g0, 0.0, 1.0)
    return jnp.clip((g0 + g1) - g0, 0.0, 1.0)


def _all_gather(x, name):
    n, w = x.shape

    def body(x_ref, out_ref, send_sems, recv_sems, local_sem):
        mx, my, mc = lax.axis_index("x"), lax.axis_index("y"), lax.axis_index("c")
        me, sibling = (mx, my, mc), (mx, my, 1 - mc)
        chips = [(1 - mx, my), (mx, 1 - my), (1 - mx, 1 - my)]

        def rows(px, py, pc):
            return out_ref.at[4 * px + 2 * py + pc]

        def copy(k, block, to, src=None):
            return pltpu.make_async_remote_copy(
                src_ref=rows(*block) if src is None else src, dst_ref=rows(*block),
                send_sem=send_sems.at[k], recv_sem=recv_sems.at[k], device_id=to, device_id_type=MESH)

        mine = pltpu.make_async_copy(x_ref, rows(*me), local_sem)
        mine.start()
        first = [copy(0, me, sibling, src=x_ref)]
        first += [copy(1 + j, me, (*chip, mc), src=x_ref) for j, chip in enumerate(chips)]
        for cp in first:
            cp.start()
        passed = [copy(4 + j, (*chip, mc), sibling) for j, chip in enumerate(chips)]
        for j, chip in enumerate(chips):
            copy(1 + j, (*chip, mc), me).wait_recv()
            passed[j].start()
        copy(0, sibling, me).wait_recv()
        for j, chip in enumerate(chips):
            copy(4 + j, (*chip, 1 - mc), me).wait_recv()
        for cp in first + passed:
            cp.wait_send()
        mine.wait()

    return pl.pallas_call(
        body, name=name,
        out_shape=jax.ShapeDtypeStruct((N_DEV, n, w), x.dtype),
        in_specs=[pl.BlockSpec(memory_space=pl.ANY)],
        out_specs=pl.BlockSpec(memory_space=pl.ANY),
        scratch_shapes=[pltpu.SemaphoreType.DMA((7,)), pltpu.SemaphoreType.DMA((7,)), pltpu.SemaphoreType.DMA],
    )(x)


GRAD_SRC = {'w_in': ('d_in', 0), 'wb': ('d_b', 0), 'w_out': ('d_out', 0), 'wg': ('d_gu', 0), 'wu': ('d_gu', D_FF),
            'wd': ('d_down', 0)}


class _GradExchange:
    def __init__(self, keys, grads):
        self.names = sorted({GRAD_SRC[k][0] for k in keys})
        self.arrays = [grads[n][0] for n in self.names] + [grads[n][1] for n in self.names]
        self.pieces, off = [], 0
        for k in keys:
            name, base = GRAD_SRC[k]
            rows = W_ROWS[k][1]
            self.pieces.append((self.names.index(name), base, rows, off))
            off += rows
        self.keys, self.rows = keys, off
        self.out_shape = (jax.ShapeDtypeStruct((N_DEV - 1, off, D_MODEL), BF), jax.ShapeDtypeStruct((off, D_MODEL), F32))
        self.scratch = [pltpu.SemaphoreType.DMA((N_DEV - 1,)), pltpu.SemaphoreType.DMA((N_DEV - 1,)),
                        pltpu.SemaphoreType.DMA((len(self.pieces),))]

    def _local(self, ins, outs, scr):
        mx, my, mc = lax.axis_index("x"), lax.axis_index("y"), lax.axis_index("c")
        me = 4 * mx + 2 * my + mc
        return [pltpu.make_async_copy(ins[ai].at[pl.ds(pl.multiple_of(base + rows * me, 8), rows)],
                                      outs[1].at[pl.ds(off, rows)], scr[2].at[k])
                for k, (ai, base, rows, off) in enumerate(self.pieces)]

    def start(self, ins, outs, scr):
        mx, my, mc = lax.axis_index("x"), lax.axis_index("y"), lax.axis_index("c")
        bf16 = ins[len(self.names):]
        for cp in self._local(ins, outs, scr):
            cp.start()
        for d in range(1, N_DEV):
            px = (1 - mx) if (d >> 2) & 1 else mx
            py = (1 - my) if (d >> 1) & 1 else my
            pc = (1 - mc) if d & 1 else mc
            peer = 4 * px + 2 * py + pc
            for ai, base, rows, off in self.pieces:
                pltpu.make_async_remote_copy(
                    src_ref=bf16[ai].at[pl.ds(pl.multiple_of(base + rows * peer, 16), rows)],
                    dst_ref=outs[0].at[d - 1, pl.ds(off, rows)], send_sem=scr[0].at[d - 1], recv_sem=scr[1].at[d - 1],
                    device_id=(px, py, pc), device_id_type=MESH).start()

    def advance(self, ins, outs, scr):
        pass

    def wait(self, ins, outs, scr):
        me = (lax.axis_index("x"), lax.axis_index("y"), lax.axis_index("c"))
        for d in range(1, N_DEV):
            slot = pltpu.make_async_remote_copy(
                src_ref=outs[0].at[d - 1], dst_ref=outs[0].at[d - 1], send_sem=scr[0].at[d - 1],
                recv_sem=scr[1].at[d - 1], device_id=me, device_id_type=MESH)
            slot.wait_recv()
            slot.wait_send()
        for cp in self._local(ins, outs, scr):
            cp.wait()


class _SlabGather:
    def __init__(self, slab, row0, rows):
        self.arrays = [slab]
        self.row0, self.rows = row0, rows
        self.out_shape = (jax.ShapeDtypeStruct((N_DEV, rows, D_MODEL), slab.dtype),)
        self.scratch = [pltpu.SemaphoreType.DMA((7,)), pltpu.SemaphoreType.DMA((7,)), pltpu.SemaphoreType.DMA]

    def _parts(self, ins, outs, scr):
        mx, my, mc = lax.axis_index("x"), lax.axis_index("y"), lax.axis_index("c")
        me, sibling = (mx, my, mc), (mx, my, 1 - mc)
        chips = [(1 - mx, my), (mx, 1 - my), (1 - mx, 1 - my)]
        mine_ref = ins[0].at[pl.ds(self.row0, self.rows)]
        block = lambda px, py, pc: outs[0].at[4 * px + 2 * py + pc]

        def copy(k, blk, to, src=None):
            return pltpu.make_async_remote_copy(
                src_ref=block(*blk) if src is None else src, dst_ref=block(*blk),
                send_sem=scr[0].at[k], recv_sem=scr[1].at[k], device_id=to, device_id_type=MESH)

        local = pltpu.make_async_copy(mine_ref, block(*me), scr[2])
        first = [copy(0, me, sibling, src=mine_ref)]
        first += [copy(1 + j, me, (*chip, mc), src=mine_ref) for j, chip in enumerate(chips)]
        passed = [copy(4 + j, (*chip, mc), sibling) for j, chip in enumerate(chips)]
        landed = [copy(1 + j, (*chip, mc), me) for j, chip in enumerate(chips)]
        from_sibling = [copy(0, sibling, me)] + [copy(4 + j, (*chip, 1 - mc), me) for j, chip in enumerate(chips)]
        return local, first, passed, landed, from_sibling

    def start(self, ins, outs, scr):
        local, first, _, _, _ = self._parts(ins, outs, scr)
        local.start()
        for cp in first:
            cp.start()

    def advance(self, ins, outs, scr):
        _, _, passed, landed, _ = self._parts(ins, outs, scr)
        for arrived, forward in zip(landed, passed):
            arrived.wait_recv()
            forward.start()

    def wait(self, ins, outs, scr):
        local, first, passed, _, from_sibling = self._parts(ins, outs, scr)
        for cp in from_sibling:
            cp.wait_recv()
        for cp in first + passed:
            cp.wait_send()
        local.wait()


class _Both:
    def __init__(self, first, second):
        self.parts = (first, second)
        self.arrays = list(first.arrays) + list(second.arrays)
        self.out_shape = tuple(first.out_shape) + tuple(second.out_shape)
        self.scratch = list(first.scratch) + list(second.scratch)

    def _split(self, ins, outs, scr):
        a, b = self.parts
        na, oa, sa = len(a.arrays), len(a.out_shape), len(a.scratch)
        return (a, ins[:na], outs[:oa], scr[:sa]), (b, ins[na:], outs[oa:], scr[sa:])

    def start(self, ins, outs, scr):
        for part, i, o, s in self._split(ins, outs, scr):
            part.start(i, o, s)

    def advance(self, ins, outs, scr):
        for part, i, o, s in self._split(ins, outs, scr):
            part.advance(i, o, s)

    def wait(self, ins, outs, scr):
        for part, i, o, s in self._split(ins, outs, scr):
            part.wait(i, o, s)


def _hosted_call(body, comm, name, grid, in_specs, args, out_shape, out_specs, scratch_shapes):
    if comm is None:
        return pl.pallas_call(body, name=name, grid=grid, in_specs=in_specs, out_specs=out_specs, out_shape=out_shape,
                              scratch_shapes=scratch_shapes, compiler_params=_params(("arbitrary",)))(*args), None
    n_in, n_out, n_scr, n_c = len(in_specs), len(out_shape), len(scratch_shapes), len(comm.arrays)
    n_co = len(comm.out_shape)
    steps = grid[0]

    def hosted(*refs):
        ins, cins = refs[:n_in], refs[n_in:n_in + n_c]
        outs = refs[n_in + n_c:n_in + n_c + n_out]
        couts = refs[n_in + n_c + n_out:n_in + n_c + n_out + n_co]
        scr = refs[n_in + n_c + n_out + n_co:n_in + n_c + n_out + n_co + n_scr]
        cscr = refs[n_in + n_c + n_out + n_co + n_scr:]

        @pl.when(pl.program_id(0) == 0)
        def _():
            comm.start(cins, couts, cscr)

        body(*ins, *outs, *scr)

        @pl.when(pl.program_id(0) == max(steps - 1 - ADVANCE_STEPS, 0))
        def _():
            comm.advance(cins, couts, cscr)

        @pl.when(pl.program_id(0) == steps - 1)
        def _():
            comm.wait(cins, couts, cscr)

    res = pl.pallas_call(
        hosted, name=name, grid=grid, in_specs=list(in_specs) + [HBM_SPEC] * n_c,
        out_specs=tuple(out_specs) + (HBM_SPEC,) * n_co, out_shape=tuple(out_shape) + tuple(comm.out_shape),
        scratch_shapes=list(scratch_shapes) + comm.scratch, compiler_params=_params(("arbitrary",)),
    )(*args, *comm.arrays)
    return res[:n_out], res[n_out:]


def _sum_parts(own, recv, name):
    n, w = own.shape
    tn = _tile(n, 256, 16)

    def body(own_ref, p_ref, o_ref):
        acc = own_ref[...]
        for d in range(N_DEV - 1):
            acc = acc + p_ref[d].astype(F32)
        o_ref[...] = acc

    return pl.pallas_call(
        body, name=name, out_shape=jax.ShapeDtypeStruct((n, w), F32), grid=(n // tn,),
        in_specs=[pl.BlockSpec((tn, w), lambda i: (i, 0)), pl.BlockSpec((N_DEV - 1, tn, w), lambda i: (0, i, 0))],
        out_specs=pl.BlockSpec((tn, w), lambda i: (i, 0)),
        compiler_params=_params(("parallel",)),
    )(own, recv)


def _sum8(parts, name):
    _, n, w = parts.shape
    tn = _tile(n, 256 if w > 128 else 1024, 8)

    def body(p_ref, o_ref):
        acc = p_ref[0]
        for d in range(1, N_DEV):
            acc = acc + p_ref[d]
        o_ref[...] = acc

    return pl.pallas_call(
        body, name=name, out_shape=jax.ShapeDtypeStruct((n, w), F32), grid=(n // tn,),
        in_specs=[pl.BlockSpec((N_DEV, tn, w), lambda i: (0, i, 0))],
        out_specs=pl.BlockSpec((tn, w), lambda i: (i, 0)),
        compiler_params=_params(("parallel",)),
    )(parts)


def _pack(arrays, dtype, row_mult):
    flat = jnp.concatenate([a.astype(dtype).reshape(-1) for a in arrays])
    pad = (-flat.shape[0]) % (128 * row_mult)
    if pad:
        flat = jnp.concatenate([flat, jnp.zeros((pad,), dtype)])
    return flat.reshape(-1, 128)


def _unpack(flat2d, shapes):
    lead = flat2d.shape[:-2]
    flat = flat2d.reshape(lead + (-1,))
    out, off = [], 0
    for s in shapes:
        size = 1
        for d in s:
            size *= d
        out.append(flat[..., off:off + size].reshape(lead + tuple(s)))
        off += size
    return out


def _blocks_to_full(blocks, axis):
    moved = jnp.moveaxis(blocks, 0, axis)
    shape = list(moved.shape)
    shape[axis:axis + 2] = [shape[axis] * shape[axis + 1]]
    return moved.reshape(shape)


def _weight_scratch(keys):
    return ([pltpu.VMEM((N_DEV * W_ROWS[k][1], D_MODEL), BF) for k in keys]
            + [pltpu.SemaphoreType.DMA((N_DEV * len(keys),))])


def _fetch_weights(slabs_ref, keys, bufs, sems, base=0):
    @pl.when(pl.program_id(0) == 0)
    def _():
        copies = []
        for ki, (key, buf) in enumerate(zip(keys, bufs)):
            row0, rows = W_ROWS[key][0] - base, W_ROWS[key][1]
            for j in range(N_DEV):
                copies.append(pltpu.make_async_copy(slabs_ref.at[j, pl.ds(row0, rows)],
                                                    buf.at[pl.ds(j * rows, rows)], sems.at[ki * N_DEV + j]))
        for cp in copies:
            cp.start()
        for cp in copies:
            cp.wait()


HBM_SPEC = pl.BlockSpec(memory_space=pl.ANY)


def _norm_matmul(h, gain, slabs, keys, out_dtype, name, comm=None):
    t_rows, d = h.shape
    widths = [N_DEV * W_ROWS[k][1] for k in keys]
    tm = _tile(t_rows, ROW_TILE)

    def body(h_ref, g_ref, slabs_ref, o_ref, *scratch):
        bufs, sems = scratch[:-1], scratch[-1]
        _fetch_weights(slabs_ref, keys, bufs, sems, slabs[1])
        u = _rms_fwd(h_ref[...], g_ref[...]).astype(BF)
        off = 0
        for buf, n in zip(bufs, widths):
            o_ref[:, off:off + n] = _dot_nt(u, buf[...]).astype(out_dtype)
            off += n

    outs, moved = _hosted_call(
        body, comm, name, (t_rows // tm,),
        [pl.BlockSpec((tm, d), lambda i: (i, 0)), _whole((1, d)), HBM_SPEC], (h, gain, slabs[0]),
        (jax.ShapeDtypeStruct((t_rows, sum(widths)), out_dtype),),
        (pl.BlockSpec((tm, sum(widths)), lambda i: (i, 0)),), _weight_scratch(keys))
    return outs[0], moved


def _chunk_masks():
    row = lax.broadcasted_iota(jnp.int32, (CHUNK, CHUNK), 0)
    col = lax.broadcasted_iota(jnp.int32, (CHUNK, CHUNK), 1)
    tri_lo = (col <= row).astype(BF)
    tri_up = (col >= row).astype(BF)
    rb = jnp.right_shift(row, 4)
    cb = jnp.right_shift(col, 4)
    diag = (rb == cb) & (col <= row)
    off = [(rb == i) & (col < SUB * i) for i in range(1, CHUNK // SUB)]
    return tri_lo, tri_up, diag, off


def _chunk_gates(fl, lb):
    sg = jax.nn.sigmoid(fl)
    s2 = jax.nn.sigmoid(-fl)
    f = lb + (1.0 - lb) * sg
    lf = jnp.log(jnp.maximum(f, LOG_FLOOR))
    k = (1.0 - lb) * s2
    return sg, s2, f, lf, k


def _chunk_factors(q, k, b):
    nb = CHUNK // SUB
    rq = jnp.concatenate([jnp.broadcast_to(b[SUB * i:SUB * i + 1], (SUB, HEAD_DIM)) for i in range(nb)], axis=0)
    mq = jnp.concatenate([jnp.broadcast_to(b[SUB * i + SUB // 2:SUB * i + SUB // 2 + 1], (SUB, HEAD_DIM))
                          for i in range(nb)], axis=0)
    e_qs = jnp.exp(b - rq)
    e_qm = jnp.exp(jnp.minimum(b - mq, EXP_CLAMP))
    e_km = jnp.exp(jnp.minimum(mq - b, EXP_CLAMP))
    e_ks = [jnp.exp(jnp.minimum(b[SUB * i:SUB * i + 1] - b, 0.0)) for i in range(1, nb)]
    qs = (q * e_qs).astype(BF)
    qm = (q * e_qm).astype(BF)
    km = (k * e_km).astype(BF)
    ks = [(k * e).astype(BF) for e in e_ks]
    return e_qs, e_qm, e_km, e_ks, qs, qm, km, ks


def _chunk_scores(qs, qm, km, ks, diag, off):
    a = jnp.where(diag, _dot_nt(qm, km), 0.0)
    for m, kk in zip(off, ks):
        a = jnp.where(m, _dot_nt(qs, kk), a)
    return a


def _hgrn_fwd(proj, lower_bounds, layer, name, comm=None):
    t_rows = proj.shape[0]
    tm = _tile(t_rows, ROW_TILE, CHUNK)
    nct = tm // CHUNK

    def body(q_ref, f_ref, i_ref, lb_ref, o_ref, sall_ref, st_ref):
        @pl.when(pl.program_id(0) == 0)
        def _():
            st_ref[...] = jnp.zeros_like(st_ref)

        lbs = _layer_lower_bound(lb_ref, layer)
        tri_lo, _, diag, off = _chunk_masks()

        group = HGRN_UNROLL if nct % HGRN_UNROLL == 0 else 1

        def chunk_group(gi, carry):
            pairs = [(u, hh) for u in range(group) for hh in range(HEADS)]
            rows = [pl.ds(pl.multiple_of((gi * group + u) * CHUNK, SUB), CHUNK) for u in range(group)]
            cols = [slice(hh * HEAD_DIM, (hh + 1) * HEAD_DIM) for hh in range(HEADS)]
            q = {p: q_ref[rows[p[0]], cols[p[1]]] for p in pairs}
            vb = {p: i_ref[rows[p[0]], cols[p[1]]].astype(BF) for p in pairs}
            gates = {p: _chunk_gates(f_ref[rows[p[0]], cols[p[1]]], lbs[:, cols[p[1]]]) for p in pairs}
            b = {p: _tri_mm(tri_lo, gates[p][3]) for p in pairs}
            fac = {p: _chunk_factors(q[p], gates[p][4], b[p]) for p in pairs}
            a = {p: _chunk_scores(*fac[p][4:], diag, off) for p in pairs}
            o_intra = {p: _dot(a[p].astype(BF), vb[p]) for p in pairs}
            upd = {p: _dot_tn(vb[p], (gates[p][4] * jnp.exp(b[p][CHUNK - 1:CHUNK] - b[p])).astype(BF)) for p in pairs}
            qc = {p: (q[p] * jnp.exp(b[p])).astype(BF) for p in pairs}
            for hh in range(HEADS):
                st = st_ref[hh]
                for u in range(group):
                    p = (u, hh)
                    sall_ref[gi * group + u, hh] = st
                    o_ref[rows[u], cols[hh]] = _dot_nt(qc[p], st.astype(BF)) + o_intra[p]
                    st = st * jnp.exp(b[p][CHUNK - 1:CHUNK]) + upd[p]
                st_ref[hh] = st
            return carry

        lax.fori_loop(0, nct // group, chunk_group, 0)

    outs, moved = _hosted_call(
        body, comm, name, (t_rows // tm,),
        [_cols(tm, 512, 0), _cols(tm, 512, 1), _cols(tm, 512, 2), _whole((DEPTH, HGRN_W))],
        (proj, proj, proj, lower_bounds),
        (jax.ShapeDtypeStruct((t_rows, HGRN_W), F32),
         jax.ShapeDtypeStruct((t_rows // CHUNK, HEADS, HEAD_DIM, HEAD_DIM), F32)),
        (pl.BlockSpec((tm, HGRN_W), lambda i: (i, 0)),
         pl.BlockSpec((nct, HEADS, HEAD_DIM, HEAD_DIM), lambda i: (i, 0, 0, 0))),
        [pltpu.VMEM((HEADS, HEAD_DIM, HEAD_DIM), F32)])
    return (*outs, moved)


def _head_norm(o):
    outs, rs = [], []
    for hh in range(HEADS):
        oh = o[:, hh * HEAD_DIM:(hh + 1) * HEAD_DIM]
        r = lax.rsqrt(jnp.mean(oh * oh, axis=-1, keepdims=True) + EPS)
        outs.append(oh * r)
        rs.append(r)
    return outs, rs


def _window_counts(row0, rows):
    t1 = (row0 + lax.broadcasted_iota(jnp.int32, (rows, 1), 0) + 1).astype(F32)
    return [1.0 / jnp.minimum(t1, float(w)) for w in POOL_WINDOWS]


def _pool_fwd(v, halo, row0):
    rows = v.shape[0]
    inv = _window_counts(row0, rows)
    outs = []
    for gi, w in enumerate(POOL_WINDOWS):
        cs = slice(gi * HEAD_DIM, (gi + 1) * HEAD_DIM)
        s = jnp.concatenate([halo[:, cs], v[:, cs]], axis=0)
        step = 1
        while step < w:
            s = s + pltpu.roll(s, step, 0)
            step *= 2
        outs.append(s[HALO:] * inv[gi] - v[:, cs])
    return outs


def _pool_bwd(dpooled, halo, row0):
    rows = dpooled[0].shape[0]
    inv = _window_counts(row0, rows)
    inv_h = _window_counts(row0 + rows, HALO)
    outs = []
    for gi, w in enumerate(POOL_WINDOWS):
        s = jnp.concatenate([dpooled[gi] * inv[gi], halo[gi] * inv_h[gi]], axis=0)
        n_ext = s.shape[0]
        step = 1
        while step < w:
            s = s + pltpu.roll(s, n_ext - step, 0)
            step *= 2
        outs.append(s[:rows] - dpooled[gi])
    return outs


def _pool_project(pooled, pp_ref, scale):
    y = jnp.concatenate([_dot(pooled[gi].astype(BF), pp_ref[gi].astype(BF)) for gi in range(4)], axis=1)
    return y, y * scale


def _mix_merge(proj, o, h, out_gain, pool_proj, pool_scale, slabs, gain_post, name, comm=None):
    t_rows = h.shape[0]
    tm = _tile(t_rows, ROW_TILE)
    hpt = tm // HALO
    keys = ['wb', 'w_out']

    def body(g_ref, v_ref, vh_ref, ga0_ref, ga1_ref, gb0_ref, gb1_ref, o_ref, h_ref, og_ref, pp_ref, ps_ref,
             slabs_ref, gp_ref, hn_ref, r_ref, a_ref, p_ref, wb_ref, wo_ref, sems):
        _fetch_weights(slabs_ref, keys, (wb_ref, wo_ref), sems, slabs[1])
        i = pl.program_id(0)
        on, _ = _head_norm(o_ref[...])
        a = jnp.concatenate(on, axis=1) * og_ref[...] * jax.nn.sigmoid(g_ref[...])
        halo = jnp.where(i == 0, 0.0, vh_ref[...])
        pooled = _pool_fwd(v_ref[...], halo, i * tm)
        _, p = _pool_project(pooled, pp_ref, ps_ref[...])
        ab, pb = a.astype(BF), p.astype(BF)
        sa = jax.nn.sigmoid(jnp.concatenate([ga0_ref[...], ga1_ref[...]], axis=1))
        sb = jax.nn.sigmoid(jnp.concatenate([gb0_ref[...], gb1_ref[...]], axis=1))
        z = sa * _dot_nt(ab, wb_ref[:, 0:HGRN_W]) + sb * _dot_nt(pb, wb_ref[:, HGRN_W:HGRN_W + POOL_W])
        r = _dot(z.astype(BF), wo_ref[...])
        hn_ref[...] = h_ref[...] + _rms_fwd(r, gp_ref[...])
        r_ref[...] = r
        a_ref[...] = ab
        p_ref[...] = pb

    rows = lambda width: pl.BlockSpec((tm, width), lambda i: (i, 0))
    outs, moved = _hosted_call(
        body, comm, name, (t_rows // tm,),
        in_specs=[_cols(tm, 512, 3), _cols(tm, 512, 4),
                  pl.BlockSpec((HALO, 512), lambda i: (jnp.maximum(i * hpt - 1, 0), 4)),
                  _cols(tm, 512, 5), _cols(tm, 512, 6), _cols(tm, 512, 7), _cols(tm, 512, 8),
                  rows(HGRN_W), rows(D_MODEL), _whole((1, HGRN_W)), _whole((4, HEAD_DIM, HEAD_DIM)),
                  _whole((1, POOL_W)), HBM_SPEC, _whole((1, D_MODEL))],
        args=(proj, proj, proj, proj, proj, proj, proj, o, h, out_gain, pool_proj, pool_scale, slabs[0], gain_post),
        out_shape=(jax.ShapeDtypeStruct((t_rows, D_MODEL), F32), jax.ShapeDtypeStruct((t_rows, D_MODEL), F32),
                   jax.ShapeDtypeStruct((t_rows, HGRN_W), BF), jax.ShapeDtypeStruct((t_rows, POOL_W), BF)),
        out_specs=(rows(D_MODEL), rows(D_MODEL), rows(HGRN_W), rows(POOL_W)),
        scratch_shapes=_weight_scratch(keys))
    return (*outs, moved)


def _conv_fwd(g, halo, cw):
    ext = jnp.concatenate([halo, g], axis=0)
    return (cw[0:1] * pltpu.roll(ext, 2, 0)[8:] + cw[1:2] * pltpu.roll(ext, 1, 0)[8:] + cw[2:3] * g)


def _ffn_down(gu, h, conv_w, conv_b, slabs, gain_post, name):
    t_rows = h.shape[0]
    tm = _tile(t_rows, FFN_TILE)
    hpt = tm // HALO

    def body(g_ref, gh_ref, up_ref, h_ref, cw_ref, cb_ref, slabs_ref, gp_ref, hn_ref, y_ref, gl_ref, gg_ref,
             wd_ref, sems):
        _fetch_weights(slabs_ref, ['wd'], (wd_ref,), sems, slabs[1])
        i = pl.program_id(0)
        halo = jnp.where(i == 0, 0.0, gh_ref[8:HALO, :].astype(F32))
        gt = _conv_fwd(g_ref[...].astype(F32), halo, cw_ref[...]) + cb_ref[...]
        gl, gg = _gelu_and_grad(gt)
        gl_ref[...] = gl.astype(BF)
        gg_ref[...] = gg.astype(BF)
        act = gl * up_ref[...].astype(F32)
        y = _dot(act.astype(BF), wd_ref[...])
        hn_ref[...] = h_ref[...] + _rms_fwd(y, gp_ref[...])
        y_ref[...] = y

    rows = lambda width: pl.BlockSpec((tm, width), lambda i: (i, 0))
    return pl.pallas_call(
        body, name=name,
        out_shape=(jax.ShapeDtypeStruct((t_rows, D_MODEL), F32), jax.ShapeDtypeStruct((t_rows, D_MODEL), F32),
                   jax.ShapeDtypeStruct((t_rows, D_FF), BF), jax.ShapeDtypeStruct((t_rows, D_FF), BF)),
        grid=(t_rows // tm,),
        in_specs=[_cols(tm, D_FF, 0), pl.BlockSpec((HALO, D_FF), lambda i: (jnp.maximum(i * hpt - 1, 0), 0)),
                  _cols(tm, D_FF, 1), rows(D_MODEL), _whole((3, D_FF)), _whole((1, D_FF)),
                  HBM_SPEC, _whole((1, D_MODEL))],
        out_specs=(rows(D_MODEL), rows(D_MODEL), rows(D_FF), rows(D_FF)),
        scratch_shapes=_weight_scratch(['wd']),
        compiler_params=_params(("arbitrary",)),
    )(gu, gu, gu, h, conv_w, conv_b, slabs[0], gain_post)


def _loss_grad(h, target, name):
    t_rows, d = h.shape
    tm = _tile(t_rows, ROW_TILE)

    def body(h_ref, t_ref, dh_ref, l_ref):
        i = pl.program_id(0)

        @pl.when(i == 0)
        def _():
            l_ref[...] = jnp.zeros_like(l_ref)

        row = i * tm + lax.broadcasted_iota(jnp.int32, (tm, 1), 0)
        err = jnp.where(row >= N_META, h_ref[...] - t_ref[...], 0.0)
        dh_ref[...] = err * (1.0 / d)
        l_ref[...] += jnp.sum(err * err, axis=0, keepdims=True) * (0.5 / d)

    rows = pl.BlockSpec((tm, d), lambda i: (i, 0))
    return pl.pallas_call(
        body, name=name,
        out_shape=(jax.ShapeDtypeStruct((t_rows, d), F32), jax.ShapeDtypeStruct((1, d), F32)),
        grid=(t_rows // tm,), in_specs=[rows, rows], out_specs=(rows, _whole((1, d))),
        compiler_params=_params(("arbitrary",)),
    )(h, target)


def _dw(lhs, rhs, name, gain=None, comm=None):
    t_rows = lhs[0].shape[0]
    paired = len(rhs) > 1
    n_rows = lhs[0].shape[1] if paired else sum(x.shape[1] for x in lhs)
    n_cols = sum(x.shape[1] for x in rhs)
    row_bytes = 2 * sum(x.shape[1] * x.dtype.itemsize for x in list(lhs) + list(rhs))
    tm = _tile(t_rows, max(ROW_TILE, (DW_BUFFER_BYTES - 4 * n_rows * n_cols) // row_bytes))
    last = t_rows // tm - 1

    def body(*refs):
        lhs_refs = refs[:len(lhs)]
        rhs_refs = refs[len(lhs):len(lhs) + len(rhs)]
        rest = refs[len(lhs) + len(rhs):]
        if gain is not None:
            g_ref, o_ref, o16_ref, acc, stage = rest
            rv = [_rms_fwd(rhs_refs[0][...], g_ref[...]).astype(BF)]
        else:
            o_ref, o16_ref, acc, stage = rest
            rv = [r[...] for r in rhs_refs]
        i = pl.program_id(0)

        @pl.when(i == 0)
        def _():
            acc[...] = jnp.zeros_like(acc)

        r0, c0 = 0, 0
        for p, l_ref in enumerate(lhs_refs):
            r = rv[p] if paired else rv[0]
            n = l_ref.shape[1]
            step = 512 if n % 512 == 0 else 256
            for s in range(0, n, step):
                acc[r0 + s:r0 + s + step, c0:c0 + r.shape[1]] += _dot_tn(l_ref[:, s:s + step], r)
            if paired:
                c0 += r.shape[1]
            else:
                r0 += n

        @pl.when(i == last)
        def _():
            pltpu.sync_copy(acc, o_ref)
            for s in range(0, n_rows, 256):
                stage[...] = acc[s:s + 256, :].astype(BF)
                pltpu.sync_copy(stage, o16_ref.at[pl.ds(s, 256)])

    rows = lambda x: pl.BlockSpec((tm, x.shape[1]), lambda i: (i, 0))
    in_specs = [rows(x) for x in lhs] + [rows(x) for x in rhs]
    args = list(lhs) + list(rhs)
    if gain is not None:
        in_specs.append(_whole(gain.shape))
        args.append(gain)
    outs, moved = _hosted_call(
        body, comm, name, (t_rows // tm,), in_specs, args,
        (jax.ShapeDtypeStruct((n_rows, n_cols), F32), jax.ShapeDtypeStruct((n_rows, n_cols), BF)),
        (HBM_SPEC, HBM_SPEC), [pltpu.VMEM((n_rows, n_cols), F32), pltpu.VMEM((256, n_cols), BF)])
    return (outs[0], outs[1]), moved


def _ffn_bwd(dh, y, gain_post, gu, gl, gg, conv_w, slabs_down, slabs_up, h, gain_pre, name, comm=None):
    t_rows = dh.shape[0]
    tm = _tile(t_rows, FFN_TILE)
    hpt = tm // HALO
    n_tiles = t_rows // tm

    def body(dh_ref, y_ref, gp_ref, g_ref, gh_ref, up_ref, gl_ref, gg_ref, cw_ref, h_ref, gpre_ref, sd_ref, su_ref,
             dhm_ref, dy_ref, act_ref, dgu_ref, dgp_ref, dcw_ref, dcb_ref, dgpre_ref,
             wd_ref, sems_d, wg_ref, wu_ref, sems_u, carry_ref):
        _fetch_weights(sd_ref, ['wd'], (wd_ref,), sems_d, slabs_down[1])
        _fetch_weights(su_ref, ['wg', 'wu'], (wg_ref, wu_ref), sems_u, slabs_up[1])
        i = pl.program_id(0)

        @pl.when(i == 0)
        def _():
            dgp_ref[...] = jnp.zeros_like(dgp_ref)
            dcw_ref[...] = jnp.zeros_like(dcw_ref)
            dcb_ref[...] = jnp.zeros_like(dcb_ref)
            dgpre_ref[...] = jnp.zeros_like(dgpre_ref)
            carry_ref[...] = jnp.zeros_like(carry_ref)

        dh = dh_ref[...]
        dy, dgp = _rms_bwd(dh, y_ref[...], gp_ref[...])
        dgp_ref[...] += dgp
        dyb = dy.astype(BF)
        dy_ref[...] = dyb
        g = g_ref[...].astype(F32)
        up = up_ref[...].astype(F32)
        first_tile = i == n_tiles - 1
        ext = jnp.concatenate([jnp.where(first_tile, 0.0, gh_ref[8:HALO, :].astype(F32)), g], axis=0)
        g2 = pltpu.roll(ext, 2, 0)[8:]
        g1 = pltpu.roll(ext, 1, 0)[8:]
        gl = gl_ref[...].astype(F32)
        act_ref[...] = (gl * up).astype(BF)
        dact = _dot_nt(dyb, wd_ref[...])
        dub = (dact * gl).astype(BF)
        dgt = dact * up * gg_ref[...].astype(F32)
        dcb_ref[...] += jnp.sum(dgt, axis=0, keepdims=True)
        dcw_ref[...] += jnp.concatenate([jnp.sum(dgt * g2, axis=0, keepdims=True),
                                         jnp.sum(dgt * g1, axis=0, keepdims=True),
                                         jnp.sum(dgt * g, axis=0, keepdims=True)], axis=0)
        after = jnp.concatenate([dgt, carry_ref[...]], axis=0)
        carry_ref[...] = dgt[0:8, :]
        cw = cw_ref[...]
        dg = (cw[2:3] * dgt + cw[1:2] * pltpu.roll(after, tm + 8 - 1, 0)[:tm]
              + cw[0:1] * pltpu.roll(after, tm + 8 - 2, 0)[:tm])
        dgb = dg.astype(BF)
        dgu_ref[:, 0:D_FF] = dgb
        dgu_ref[:, D_FF:2 * D_FF] = dub
        du = _dot(dgb, wg_ref[...]) + _dot(dub, wu_ref[...])
        dx, dgain = _rms_bwd(du, h_ref[...], gpre_ref[...])
        dhm_ref[...] = dh + dx
        dgpre_ref[...] += dgain

    rev = lambda width, j=0: pl.BlockSpec((tm, width), lambda i, j=j: (n_tiles - 1 - i, j))
    bf = lambda width: jax.ShapeDtypeStruct((t_rows, width), BF)
    outs, moved = _hosted_call(
        body, comm, name, (n_tiles,),
        [rev(D_MODEL), rev(D_MODEL), _whole((1, D_MODEL)), rev(D_FF, 0),
         pl.BlockSpec((HALO, D_FF), lambda i: (jnp.maximum((n_tiles - 1 - i) * hpt - 1, 0), 0)), rev(D_FF, 1),
         rev(D_FF), rev(D_FF), _whole((3, D_FF)), rev(D_MODEL), _whole((1, D_MODEL)), HBM_SPEC, HBM_SPEC],
        (dh, y, gain_post, gu, gu, gu, gl, gg, conv_w, h, gain_pre, slabs_down[0], slabs_up[0]),
        (jax.ShapeDtypeStruct((t_rows, D_MODEL), F32), bf(D_MODEL), bf(D_FF), bf(2 * D_FF),
         jax.ShapeDtypeStruct((1, D_MODEL), F32), jax.ShapeDtypeStruct((3, D_FF), F32),
         jax.ShapeDtypeStruct((1, D_FF), F32), jax.ShapeDtypeStruct((1, D_MODEL), F32)),
        (rev(D_MODEL), rev(D_MODEL), rev(D_FF), rev(2 * D_FF), _whole((1, D_MODEL)), _whole((3, D_FF)),
         _whole((1, D_FF)), _whole((1, D_MODEL))),
        _weight_scratch(['wd']) + _weight_scratch(['wg', 'wu']) + [pltpu.VMEM((8, D_FF), F32)])
    return (*outs, moved)


def _mix_bwd_a(dh, r, gain_post, proj, a, p, slabs, name, comm=None):
    t_rows = dh.shape[0]
    tm = _tile(t_rows, ROW_TILE)

    def body(dh_ref, r_ref, gp_ref, ga0_ref, ga1_ref, gb0_ref, gb1_ref, a_ref, p_ref, slabs_ref,
             dr_ref, z_ref, dya_ref, dyp_ref, da_ref, dp_ref, dgab_ref, dg_ref, wb_ref, wo_ref, sems):
        _fetch_weights(slabs_ref, ['wb', 'w_out'], (wb_ref, wo_ref), sems, slabs[1])

        @pl.when(pl.program_id(0) == 0)
        def _():
            dg_ref[...] = jnp.zeros_like(dg_ref)

        dr, dgain = _rms_bwd(dh_ref[...], r_ref[...], gp_ref[...])
        dg_ref[...] += dgain
        drb = dr.astype(BF)
        dr_ref[...] = drb
        dz = _dot_nt(drb, wo_ref[...])
        ya = _dot_nt(a_ref[...], wb_ref[:, 0:HGRN_W])
        yp = _dot_nt(p_ref[...], wb_ref[:, HGRN_W:HGRN_W + POOL_W])
        sa = jax.nn.sigmoid(jnp.concatenate([ga0_ref[...], ga1_ref[...]], axis=1))
        sb = jax.nn.sigmoid(jnp.concatenate([gb0_ref[...], gb1_ref[...]], axis=1))
        z_ref[...] = (sa * ya + sb * yp).astype(BF)
        dgab_ref[:, 0:D_MODEL] = (dz * ya * sa * (1.0 - sa)).astype(BF)
        dgab_ref[:, D_MODEL:2 * D_MODEL] = (dz * yp * sb * (1.0 - sb)).astype(BF)
        dya = (dz * sa).astype(BF)
        dyp = (dz * sb).astype(BF)
        dya_ref[...] = dya
        dyp_ref[...] = dyp
        da_ref[...] = _dot(dya, wb_ref[:, 0:HGRN_W])
        dp_ref[...] = _dot(dyp, wb_ref[:, HGRN_W:HGRN_W + POOL_W])

    rows = lambda width: pl.BlockSpec((tm, width), lambda i: (i, 0))
    bf = lambda width: jax.ShapeDtypeStruct((t_rows, width), BF)
    f32 = lambda width: jax.ShapeDtypeStruct((t_rows, width), F32)
    outs, moved = _hosted_call(
        body, comm, name, (t_rows // tm,),
        [rows(D_MODEL), rows(D_MODEL), _whole((1, D_MODEL)),
         _cols(tm, 512, 5), _cols(tm, 512, 6), _cols(tm, 512, 7), _cols(tm, 512, 8),
         rows(HGRN_W), rows(POOL_W), HBM_SPEC],
        (dh, r, gain_post, proj, proj, proj, proj, a, p, slabs[0]),
        (bf(D_MODEL), bf(D_MODEL), bf(D_MODEL), bf(D_MODEL), f32(HGRN_W), f32(POOL_W), bf(2 * D_MODEL),
         jax.ShapeDtypeStruct((1, D_MODEL), F32)),
        (rows(D_MODEL), rows(D_MODEL), rows(D_MODEL), rows(D_MODEL), rows(HGRN_W), rows(POOL_W),
         rows(2 * D_MODEL), _whole((1, D_MODEL))),
        _weight_scratch(['wb', 'w_out']))
    return (*outs, moved)


def _mix_bwd_b(da, dp, proj, o, out_gain, pool_proj, pool_scale, name):
    t_rows = da.shape[0]
    tm = _tile(t_rows, ROW_TILE)
    hpt = tm // HALO
    last = t_rows // tm - 1

    def body(da_ref, dp_ref, dpn_ref, g_ref, v_ref, vh_ref, o_ref, og_ref, pp_ref, ps_ref,
             do_ref, dgv_ref, dog_ref, dpp_ref, dps_ref):
        i = pl.program_id(0)

        @pl.when(i == 0)
        def _():
            dog_ref[...] = jnp.zeros_like(dog_ref)
            dpp_ref[...] = jnp.zeros_like(dpp_ref)
            dps_ref[...] = jnp.zeros_like(dps_ref)

        da = da_ref[...]
        og = og_ref[...]
        o = o_ref[...]
        on, rs = _head_norm(o)
        onc = jnp.concatenate(on, axis=1)
        sg = jax.nn.sigmoid(g_ref[...])
        dog_ref[...] += jnp.sum(da * onc * sg, axis=0, keepdims=True)
        dgv_ref[:, 0:HGRN_W] = (da * onc * og * sg * (1.0 - sg)).astype(BF)
        don = da * og * sg
        for hh in range(HEADS):
            cs = slice(hh * HEAD_DIM, (hh + 1) * HEAD_DIM)
            d = don[:, cs]
            do_ref[:, cs] = rs[hh] * (d - on[hh] * jnp.mean(d * on[hh], axis=-1, keepdims=True))

        scale = ps_ref[...]
        halo = jnp.where(i == 0, 0.0, vh_ref[...])
        pooled = _pool_fwd(v_ref[...], halo, i * tm)
        y, _ = _pool_project(pooled, pp_ref, scale)
        dp = dp_ref[...]
        dps_ref[...] += jnp.sum(dp * y, axis=0, keepdims=True)
        dy = dp * scale
        dyn = jnp.where(i == last, 0.0, dpn_ref[...]) * scale
        dpooled, dhalo = [], []
        for gi in range(4):
            cs = slice(gi * HEAD_DIM, (gi + 1) * HEAD_DIM)
            ppb = pp_ref[gi].astype(BF)
            dyb = dy[:, cs].astype(BF)
            dpooled.append(_dot_nt(dyb, ppb))
            dhalo.append(_dot_nt(dyn[:, cs].astype(BF), ppb))
            dpp_ref[gi] += _dot_tn(pooled[gi].astype(BF), dyb)
        dv = _pool_bwd(dpooled, dhalo, i * tm)
        dgv_ref[:, HGRN_W:HGRN_W + POOL_W] = jnp.concatenate(dv, axis=1).astype(BF)

    rows = lambda width: pl.BlockSpec((tm, width), lambda i: (i, 0))
    return pl.pallas_call(
        body, name=name,
        out_shape=(jax.ShapeDtypeStruct((t_rows, HGRN_W), F32), jax.ShapeDtypeStruct((t_rows, HGRN_W + POOL_W), BF),
                   jax.ShapeDtypeStruct((1, HGRN_W), F32), jax.ShapeDtypeStruct((4, HEAD_DIM, HEAD_DIM), F32),
                   jax.ShapeDtypeStruct((1, POOL_W), F32)),
        grid=(t_rows // tm,),
        in_specs=[rows(HGRN_W), rows(POOL_W),
                  pl.BlockSpec((HALO, POOL_W), lambda i: (jnp.minimum((i + 1) * hpt, (last + 1) * hpt - 1), 0)),
                  _cols(tm, 512, 3), _cols(tm, 512, 4),
                  pl.BlockSpec((HALO, 512), lambda i: (jnp.maximum(i * hpt - 1, 0), 4)),
                  rows(HGRN_W), _whole((1, HGRN_W)), _whole((4, HEAD_DIM, HEAD_DIM)), _whole((1, POOL_W))],
        out_specs=(rows(HGRN_W), rows(HGRN_W + POOL_W), _whole((1, HGRN_W)), _whole((4, HEAD_DIM, HEAD_DIM)),
                   _whole((1, POOL_W))),
        compiler_params=_params(("arbitrary",)),
    )(da, dp, dp, proj, proj, proj, o, out_gain, pool_proj, pool_scale)


def _hgrn_bwd(proj, lower_bounds, layer, states, do, name, comm=None):
    t_rows = proj.shape[0]
    tm = _tile(t_rows, ROW_TILE, CHUNK)
    nct = tm // CHUNK
    n_tiles = t_rows // tm

    def body(q_ref, f_ref, i_ref, lb_ref, sall_ref, do_ref, dqfi_ref, dlb_ref, dst_ref):
        @pl.when(pl.program_id(0) == 0)
        def _():
            dst_ref[...] = jnp.zeros_like(dst_ref)
            dlb_ref[...] = jnp.zeros_like(dlb_ref)

        lbs = _layer_lower_bound(lb_ref, layer)
        tri_lo, tri_up, diag, off = _chunk_masks()
        is_last_row = lax.broadcasted_iota(jnp.int32, (CHUNK, 1), 0) == CHUNK - 1

        group = HGRN_UNROLL if nct % HGRN_UNROLL == 0 else 1

        def chunk_group(gi, carry):
            pairs = [(u, hh) for u in range(group) for hh in range(HEADS)]
            cidx = [nct - 1 - (gi * group + u) for u in range(group)]
            rows = [pl.ds(pl.multiple_of(c * CHUNK, SUB), CHUNK) for c in cidx]
            cols = [slice(hh * HEAD_DIM, (hh + 1) * HEAD_DIM) for hh in range(HEADS)]
            lb = {p: lbs[:, cols[p[1]]] for p in pairs}
            q = {p: q_ref[rows[p[0]], cols[p[1]]] for p in pairs}
            vb = {p: i_ref[rows[p[0]], cols[p[1]]].astype(BF) for p in pairs}
            dob = {p: do_ref[rows[p[0]], cols[p[1]]].astype(BF) for p in pairs}
            gates = {p: _chunk_gates(f_ref[rows[p[0]], cols[p[1]]], lb[p]) for p in pairs}
            k = {p: gates[p][4] for p in pairs}
            b = {p: _tri_mm(tri_lo, gates[p][3]) for p in pairs}
            fac = {p: _chunk_factors(q[p], k[p], b[p]) for p in pairs}
            e_b = {p: jnp.exp(b[p]) for p in pairs}
            e_last = {p: jnp.exp(b[p][CHUNK - 1:CHUNK]) for p in pairs}
            e_kl = {p: jnp.exp(b[p][CHUNK - 1:CHUNK] - b[p]) for p in pairs}
            qc = {p: (q[p] * e_b[p]).astype(BF) for p in pairs}
            kdec = {p: (k[p] * e_kl[p]).astype(BF) for p in pairs}
            a = {p: _chunk_scores(*fac[p][4:], diag, off) for p in pairs}
            da_full = {p: _dot_nt(dob[p], vb[p]) for p in pairs}
            dv, dq, dk, upd, dupd = {}, {}, {}, {}, {}
            for p in pairs:
                e_qs, e_qm, e_km, e_ks, qs, qm, km, ks = fac[p]
                da_d = jnp.where(diag, da_full[p], 0.0).astype(BF)
                dv[p] = _dot_tn(a[p].astype(BF), dob[p])
                dq[p] = e_qm * _dot(da_d, km)
                dk[p] = e_km * _dot_tn(da_d, qm)
                for m, kk, ek in zip(off, ks, e_ks):
                    da_i = jnp.where(m, da_full[p], 0.0).astype(BF)
                    dq[p] = dq[p] + e_qs * _dot(da_i, kk)
                    dk[p] = dk[p] + ek * _dot_tn(da_i, qs)
                upd[p] = _dot_tn(vb[p], kdec[p])
                dupd[p] = _dot_tn(dob[p], qc[p])
            db = {}
            for hh in range(HEADS):
                dst = dst_ref[hh]
                for u in range(group):
                    p = (u, hh)
                    st = sall_ref[cidx[u], hh]
                    dstb = dst.astype(BF)
                    dv[p] = dv[p] + _dot_nt(kdec[p], dstb)
                    dq[p] = dq[p] + e_b[p] * _dot(dob[p], st.astype(BF))
                    dk[p] = dk[p] + e_kl[p] * _dot(vb[p], dstb)
                    st_new = st * e_last[p] + upd[p]
                    db[p] = (q[p] * dq[p] - k[p] * dk[p]
                             + jnp.where(is_last_row, jnp.sum(st_new * dst, axis=0, keepdims=True), 0.0))
                    dst = dst * e_last[p] + dupd[p]
                dst_ref[hh] = dst
            for p in pairs:
                u, hh = p
                sg, s2, f = gates[p][:3]
                dlf = _tri_mm(tri_up, db[p])
                df = jnp.where(f > LOG_FLOOR, dlf / f, 0.0)
                dfl = df * (1.0 - lb[p]) * sg * (1.0 - sg) - dk[p] * (1.0 - lb[p]) * s2 * (1.0 - s2)
                dlb_ref[:, cols[hh]] += jnp.sum(df * (1.0 - sg) - dk[p] * s2, axis=0, keepdims=True)
                dqfi_ref[rows[u], cols[hh]] = dq[p].astype(BF)
                dqfi_ref[rows[u], pl.ds(HGRN_W + hh * HEAD_DIM, HEAD_DIM)] = dfl.astype(BF)
                dqfi_ref[rows[u], pl.ds(2 * HGRN_W + hh * HEAD_DIM, HEAD_DIM)] = dv[p].astype(BF)
            return carry

        lax.fori_loop(0, nct // group, chunk_group, 0)

    rev = lambda width, j: pl.BlockSpec((tm, width), lambda i, j=j: (n_tiles - 1 - i, j))
    outs, exchanged = _hosted_call(
        body, comm, name, (n_tiles,),
        [rev(512, 0), rev(512, 1), rev(512, 2), _whole((DEPTH, HGRN_W)),
         pl.BlockSpec((nct, HEADS, HEAD_DIM, HEAD_DIM), lambda i: (n_tiles - 1 - i, 0, 0, 0)), rev(HGRN_W, 0)],
        (proj, proj, proj, lower_bounds, states, do),
        (jax.ShapeDtypeStruct((t_rows, 3 * HGRN_W), BF), jax.ShapeDtypeStruct((1, HGRN_W), F32)),
        (rev(3 * HGRN_W, 0), _whole((1, HGRN_W))),
        [pltpu.VMEM((HEADS, HEAD_DIM, HEAD_DIM), F32)])
    return (*outs, exchanged)


def _in_bwd(dqfi, dgv, dgab, slabs, h, gain_pre, dh_out, name, comm=None):
    t_rows = h.shape[0]
    tm = _tile(t_rows, ROW_TILE)
    c1 = 3 * HGRN_W
    c2 = c1 + HGRN_W + POOL_W

    def body(d1_ref, d2_ref, d3_ref, slabs_ref, h_ref, gp_ref, dho_ref, dh_ref, dg_ref, w_ref, sems):
        _fetch_weights(slabs_ref, ['w_in'], (w_ref,), sems, slabs[1])

        @pl.when(pl.program_id(0) == 0)
        def _():
            dg_ref[...] = jnp.zeros_like(dg_ref)

        du = (_dot(d1_ref[...], w_ref[0:c1, :]) + _dot(d2_ref[...], w_ref[c1:c2, :])
              + _dot(d3_ref[...], w_ref[c2:IN_COLS, :]))
        dx, dgain = _rms_bwd(du, h_ref[...], gp_ref[...])
        dh_ref[...] = dho_ref[...] + dx
        dg_ref[...] += dgain

    rows = lambda width: pl.BlockSpec((tm, width), lambda i: (i, 0))
    outs, exchanged = _hosted_call(
        body, comm, name, (t_rows // tm,),
        [rows(c1), rows(c2 - c1), rows(IN_COLS - c2), HBM_SPEC, rows(D_MODEL), _whole((1, D_MODEL)), rows(D_MODEL)],
        (dqfi, dgv, dgab, slabs[0], h, gain_pre, dh_out),
        (jax.ShapeDtypeStruct((t_rows, D_MODEL), F32), jax.ShapeDtypeStruct((1, D_MODEL), F32)),
        (rows(D_MODEL), _whole((1, D_MODEL))),
        _weight_scratch(['w_in']))
    return (*outs, exchanged)


def _lower_bound_grad(lower_bounds, dlbs, name):
    assert DEPTH == 2 and lower_bounds.shape[0] == DEPTH, "the softmax over layers is written out for two layers"
    def body(lb_ref, d_ref, o_ref):
        g0, g1 = _softmax2(lb_ref)
        bound = (g0 + g1) - g0
        inside = (bound > 0.0) & (bound < 1.0)
        dg1 = jnp.where(inside, d_ref[1:2, :], 0.0)
        inner = g1 * dg1
        o_ref[0:1, :] = g0 * (0.0 - inner)
        o_ref[1:2, :] = g1 * (dg1 - inner)

    return pl.pallas_call(body, name=name, out_shape=jax.ShapeDtypeStruct(lower_bounds.shape, F32))(lower_bounds, dlbs)


def _adamw(w, g, m, v, name):
    r, c = w.shape
    tr = r if (r % 8 or r <= 512) else _tile(r, 512, 8)
    c1 = 1.0 - ADAM_B1 ** ADAM_STEP
    c2 = 1.0 - ADAM_B2 ** ADAM_STEP

    def body(w_ref, g_ref, m_ref, v_ref, d_ref, nm_ref, nv_ref):
        gg = g_ref[...]
        nm = ADAM_B1 * m_ref[...] + (1.0 - ADAM_B1) * gg
        nv = ADAM_B2 * v_ref[...] + (1.0 - ADAM_B2) * (gg * gg)
        d_ref[...] = -ADAM_LR * ((nm / c1) / (jnp.sqrt(nv / c2) + ADAM_EPS) + ADAM_WD * w_ref[...])
        nm_ref[...] = nm
        nv_ref[...] = nv

    blk = pl.BlockSpec((tr, c), lambda i: (i, 0))
    shp = jax.ShapeDtypeStruct((r, c), F32)
    return pl.pallas_call(
        body, name=name, out_shape=(shp, shp, shp), grid=(r // tr,),
        in_specs=[blk, blk, blk, blk], out_specs=(blk, blk, blk),
        compiler_params=_params(("parallel",)),
    )(w, g, m, v)


def _as2d(a):
    return a.reshape(-1, a.shape[-1])


def _layer_slab(w, l):
    t = lambda a: jnp.swapaxes(a, 0, 1)
    parts = [t(w['w_in'][l]), jnp.concatenate([t(w['w_branch_hgrn'][l]), t(w['w_branch_pool'][l])], axis=1),
             w['w_out'][l], t(w['ffn_w_gate'][l]), t(w['ffn_w_up'][l]), w['ffn_w_down'][l]]
    return jnp.concatenate(parts, axis=0).astype(BF)


def _slab_grads(sums):
    t = lambda a: jnp.swapaxes(a, 0, 1)
    wb = sums['wb']
    return {'w_in': t(sums['w_in']), 'w_branch_hgrn': t(wb[:, :HGRN_W]), 'w_branch_pool': t(wb[:, HGRN_W:]),
            'w_out': sums['w_out'], 'ffn_w_gate': t(sums['wg']), 'ffn_w_up': t(sums['wu']), 'ffn_w_down': sums['wd']}


def _train_step(x, target, w, m, v):
    my_slabs = [_layer_slab(w, l) for l in range(DEPTH)]
    n_in = W_ROWS['w_in'][1]
    slabs = [{} for _ in range(DEPTH)]
    slabs[0]['w_in'] = (_all_gather(my_slabs[0][:n_in], "gather_w_in_0"), 0)
    f32_shapes = [w[n].shape for n in F32_GATHERED]
    gathered32 = _all_gather(_pack([w[n] for n in F32_GATHERED], F32, 8), "gather_meta_conv")
    full = {n: _blocks_to_full(blk, SHARD_AXIS[n]) for n, blk in zip(F32_GATHERED, _unpack(gathered32, f32_shapes))}
    row = lambda name, l: w[name][l][None]

    h = jnp.concatenate([full['meta_tokens'], x], axis=0)
    saved = []
    for l in range(DEPTH):
        s = {'h_in': h}
        nxt = l + 1 if l + 1 < DEPTH else None
        plan = {'in_proj': [(0, ['wb', 'w_out', 'wg', 'wu'])] if l == 0 else [],
                'hgrn_fwd': [(nxt, ['w_in'])] if nxt else [],
                'mix_merge': ([(0, ['wd'])] if l == 0 else []) + ([(nxt, ['wb', 'w_out']), (nxt, ['wd'])] if nxt else []),
                'ffn_proj': [(nxt, ['wg', 'wu'])] if nxt else []}

        def hosted(kernel):
            comms = []
            for layer, keys in plan[kernel]:
                first, last = W_ROWS[keys[0]], W_ROWS[keys[-1]]
                comms.append(_SlabGather(my_slabs[layer], first[0], last[0] + last[1] - first[0]))
            combined = None
            for c in comms:
                combined = c if combined is None else _Both(combined, c)
            return combined

        def landed(kernel, moved):
            for (layer, keys), arr in zip(plan[kernel], moved or ()):
                slabs[layer].update({k: (arr, W_ROWS[keys[0]][0]) for k in keys})

        s['proj'], moved = _norm_matmul(h, row('mix_norm_pre', l), slabs[l]['w_in'], ['w_in'], F32, f"in_proj_{l}",
                                        comm=hosted('in_proj'))
        landed('in_proj', moved)
        s['o'], s['states'], moved = _hgrn_fwd(s['proj'], w['hgrn_lower_bounds'], l, f"hgrn_fwd_{l}",
                                               comm=hosted('hgrn_fwd'))
        landed('hgrn_fwd', moved)
        h, s['r'], s['a'], s['p'], moved = _mix_merge(
            s['proj'], s['o'], h, row('hgrn_out_norm', l), w['pool_proj'][l], row('pool_scale', l), slabs[l]['wb'],
            row('mix_norm_post', l), f"mix_merge_{l}", comm=hosted('mix_merge'))
        landed('mix_merge', moved)
        s['h_mid'] = h
        s['gu'], moved = _norm_matmul(h, row('ffn_norm_pre', l), slabs[l]['wg'], ['wg', 'wu'], BF, f"ffn_proj_{l}",
                                      comm=hosted('ffn_proj'))
        landed('ffn_proj', moved)
        h, s['y'], s['gl'], s['gg'] = _ffn_down(s['gu'], h, full['ffn_conv_w'][l], row('ffn_conv_b', l),
                                                slabs[l]['wd'], row('ffn_norm_post', l), f"ffn_down_{l}")
        saved.append(s)

    dh, loss_cols = _loss_grad(h, jnp.pad(target, ((N_META, 0), (0, 0))), "loss_grad")
    loss = jnp.sum(loss_cols)

    small = {n: [None] * DEPTH for n in REPLICATED + ['ffn_conv_w']}
    exchanged = [{} for _ in range(DEPTH)]
    pending = None
    for l in reversed(range(DEPTH)):
        s = saved[l]
        g = {}
        (dh, dy, act, dgu, small['ffn_norm_post'][l], small['ffn_conv_w'][l], small['ffn_conv_b'][l],
         small['ffn_norm_pre'][l], done) = _ffn_bwd(
            dh, s['y'], row('ffn_norm_post', l), s['gu'], s['gl'], s['gg'], full['ffn_conv_w'][l], slabs[l]['wd'],
            slabs[l]['wg'], s['h_mid'], row('ffn_norm_pre', l), f"ffn_bwd_{l}", comm=pending)
        if pending is not None:
            exchanged[l + 1]['in'] = done
        g['d_down'], _ = _dw([act], [dy], f"dw_down_{l}")
        g['d_gu'], exchanged[l]['wd'] = _dw([dgu], [s['h_mid']], f"dw_gate_up_{l}", gain=row('ffn_norm_pre', l),
                                            comm=_GradExchange(EXCHANGE_GROUPS['wd'], g))

        dr, z, dya, dyp, da, dp, dgab, small['mix_norm_post'][l], exchanged[l]['wg'] = _mix_bwd_a(
            dh, s['r'], row('mix_norm_post', l), s['proj'], s['a'], s['p'], slabs[l]['wb'], f"mix_bwd_a_{l}",
            comm=_GradExchange(EXCHANGE_GROUPS['wg'], g))
        g['d_out'], _ = _dw([z], [dr], f"dw_out_{l}")
        g['d_b'], _ = _dw([dya, dyp], [s['a'], s['p']], f"dw_branch_{l}")
        do, dgv, small['hgrn_out_norm'][l], small['pool_proj'][l], small['pool_scale'][l] = _mix_bwd_b(
            da, dp, s['proj'], s['o'], row('hgrn_out_norm', l), w['pool_proj'][l], row('pool_scale', l),
            f"mix_bwd_b_{l}")
        dqfi, small['hgrn_lower_bounds'][l], exchanged[l]['rest'] = _hgrn_bwd(
            s['proj'], w['hgrn_lower_bounds'], l, s['states'], do, f"hgrn_bwd_{l}",
            comm=_GradExchange(EXCHANGE_GROUPS['rest'], g))
        g['d_in'], _ = _dw([dqfi, dgv, dgab], [s['h_in']], f"dw_in_{l}", gain=row('mix_norm_pre', l))
        pending = _GradExchange(EXCHANGE_GROUPS['in'], g)
        if l == 0:
            dh, small['mix_norm_pre'][l], exchanged[l]['in'] = _in_bwd(
                dqfi, dgv, dgab, slabs[l]['w_in'], s['h_in'], row('mix_norm_pre', l), dh, f"in_bwd_{l}", comm=pending)
        else:
            dh, small['mix_norm_pre'][l], _ = _in_bwd(
                dqfi, dgv, dgab, slabs[l]['w_in'], s['h_in'], row('mix_norm_pre', l), dh, f"in_bwd_{l}")

    grad_x = dh[N_META:]

    per_layer = []
    for l in range(DEPTH):
        sums = {}
        for part, keys in EXCHANGE_GROUPS.items():
            recv, own = exchanged[l][part]
            total = _sum_parts(own, recv, f"sum_grads_{part}_{l}")
            off = 0
            for k in keys:
                sums[k] = total[off:off + W_ROWS[k][1]]
                off += W_ROWS[k][1]
        per_layer.append(_slab_grads(sums))
    grads = {n: jnp.stack([per_layer[l][n] for l in range(DEPTH)]) for n in per_layer[0]}

    stack = lambda n: jnp.stack(small[n]) if small[n][0].shape[0] != 1 else jnp.concatenate(small[n], axis=0)
    names = list(small) + ['meta_tokens']
    partial = {n: stack(n) for n in small}
    partial['meta_tokens'] = dh[:N_META]
    gathered = _all_gather(_pack([partial[n] for n in names], F32, 8), "gather_small_grads")
    rep = dict(zip(names, _unpack(_sum8(gathered, "sum_small_grads"), [partial[n].shape for n in names])))
    rep['hgrn_lower_bounds'] = _lower_bound_grad(w['hgrn_lower_bounds'], rep['hgrn_lower_bounds'], "lower_bound_grad")
    me = 4 * lax.axis_index("x") + 2 * lax.axis_index("y") + lax.axis_index("c")
    for n in F32_GATHERED:
        width = w[n].shape[SHARD_AXIS[n]]
        rep[n] = lax.dynamic_slice_in_dim(rep[n], me * width, width, axis=SHARD_AXIS[n])
    grads.update(rep)

    delta, new_m, new_v = {}, {}, {}
    for n in WEIGHT_NAMES:
        shape = w[n].shape
        d2, m2, v2 = _adamw(_as2d(w[n]), _as2d(grads[n]), _as2d(m[n]), _as2d(v[n]), f"adamw_{n}")
        delta[n], new_m[n], new_v[n] = d2.reshape(shape), m2.reshape(shape), v2.reshape(shape)
    return loss, grad_x, grads, delta, new_m, new_v


def kernel(x, meta_tokens, mix_norm_pre, mix_norm_post, w_in, hgrn_lower_bounds, hgrn_out_norm, w_branch_hgrn, pool_proj, pool_scale, w_branch_pool, w_out, ffn_norm_pre, ffn_norm_post, ffn_w_gate, ffn_w_up, ffn_conv_w, ffn_conv_b, ffn_w_down, loss_target, m_meta_tokens, m_mix_norm_pre, m_mix_norm_post, m_w_in, m_hgrn_lower_bounds, m_hgrn_out_norm, m_w_branch_hgrn, m_pool_proj, m_pool_scale, m_w_branch_pool, m_w_out, m_ffn_norm_pre, m_ffn_norm_post, m_ffn_w_gate, m_ffn_w_up, m_ffn_conv_w, m_ffn_conv_b, m_ffn_w_down, v_meta_tokens, v_mix_norm_pre, v_mix_norm_post, v_w_in, v_hgrn_lower_bounds, v_hgrn_out_norm, v_w_branch_hgrn, v_pool_proj, v_pool_scale, v_w_branch_pool, v_w_out, v_ffn_norm_pre, v_ffn_norm_post, v_ffn_w_gate, v_ffn_w_up, v_ffn_conv_w, v_ffn_conv_b, v_ffn_w_down):
    w = dict(zip(WEIGHT_NAMES, (meta_tokens, mix_norm_pre, mix_norm_post, w_in, hgrn_lower_bounds, hgrn_out_norm,
                                w_branch_hgrn, pool_proj, pool_scale, w_branch_pool, w_out, ffn_norm_pre,
                                ffn_norm_post, ffn_w_gate, ffn_w_up, ffn_conv_w, ffn_conv_b, ffn_w_down)))
    m = dict(zip(WEIGHT_NAMES, (m_meta_tokens, m_mix_norm_pre, m_mix_norm_post, m_w_in, m_hgrn_lower_bounds,
                                m_hgrn_out_norm, m_w_branch_hgrn, m_pool_proj, m_pool_scale, m_w_branch_pool, m_w_out,
                                m_ffn_norm_pre, m_ffn_norm_post, m_ffn_w_gate, m_ffn_w_up, m_ffn_conv_w,
                                m_ffn_conv_b, m_ffn_w_down)))
    v = dict(zip(WEIGHT_NAMES, (v_meta_tokens, v_mix_norm_pre, v_mix_norm_post, v_w_in, v_hgrn_lower_bounds,
                                v_hgrn_out_norm, v_w_branch_hgrn, v_pool_proj, v_pool_scale, v_w_branch_pool, v_w_out,
                                v_ffn_norm_pre, v_ffn_norm_post, v_ffn_w_gate, v_ffn_w_up, v_ffn_conv_w,
                                v_ffn_conv_b, v_ffn_w_down)))
    loss_local, grad_x, grads, delta, new_m, new_v = _train_step(x[0], loss_target[0], w, m, v)
    loss = lax.psum(loss_local, ("x", "y", "c"))
    return (loss, grad_x[None], *[grads[n] for n in WEIGHT_NAMES], *[delta[n] for n in WEIGHT_NAMES],
            *[new_m[n] for n in WEIGHT_NAMES], *[new_v[n] for n in WEIGHT_NAMES])
```
